```python
import jax
import jax.numpy as jnp
from jax import lax
import numpy as np

D_MODEL = 1024
BATCH = 2
SEQ = 8192
DEPTH = 1

CTX_LEN = 256
GRID_W = 64
MIX_W = D_MODEL
A_W = MIX_W // 2
A_GROUPS = 4
A_GROUP_DIM = A_W // A_GROUPS
A_CHUNK = 128
ROWS_PER_CHUNK = A_CHUNK // GRID_W
B_W = MIX_W - A_W
B_HEADS = 4
B_HEAD_DIM = B_W // B_HEADS
CONV_K = 3
GDN_CHUNK = 64
N_EXPERTS = 16
EC_CAPACITY = 2
EXPERT_FF = 1024
N_MOD = 6
NORM_EPS = 1e-6
COL_QKV = 3 * B_W
N_STATE_COLS = COL_QKV + 4 * B_HEADS
COL_Z_END = N_STATE_COLS + B_W
IN_COLS = COL_Z_END + 2 * A_W

kernel_name = "hybrid_gmlp_gdn_ec_dit_layer"


def rms_norm(x, g):
    xf = x.astype(jnp.float32)
    y = xf * lax.rsqrt(jnp.mean(xf * xf, axis=-1, keepdims=True) + NORM_EPS)
    return (y * g.astype(jnp.float32)).astype(x.dtype)


def l2_normalize(x):
    return x * lax.rsqrt(jnp.sum(x * x, axis=-1, keepdims=True) + NORM_EPS)


def short_conv(x, w):
    pad = (CONV_K - 1) // 2
    y = lax.conv_general_dilated(
        x, w[:, None, :].astype(x.dtype), window_strides=(1,), padding=[(pad, pad)],
        dimension_numbers=("NWC", "WIO", "NWC"), feature_group_count=x.shape[-1])
    return jax.nn.silu(y)


def chunk_mlp(uv, n_chunks, norm_g, ws, bs):
    B, T, _ = uv.shape
    u, v = jnp.split(jax.nn.gelu(uv, approximate=False), 2, axis=-1)
    v = rms_norm(v.reshape(B, T, A_GROUPS, A_GROUP_DIM), norm_g.reshape(A_GROUPS, A_GROUP_DIM))
    vc = v.reshape(B, n_chunks, A_CHUNK, A_GROUPS, A_GROUP_DIM)
    s = jnp.einsum("gts,bnsgd->bntgd", ws, vc) + jnp.swapaxes(bs, 0, 1)[None, None, :, :, None]
    return u * s.reshape(B, T, A_W)


def gdn_streams(p, conv_w, a_log, dt_bias):
    B, T, _ = p.shape
    qkv = short_conv(p[..., :COL_QKV], conv_w).astype(jnp.float32)
    qkv = qkv.reshape(B, T, 3, B_HEADS, B_HEAD_DIM)
    q = l2_normalize(qkv[:, :, 0]) * (B_HEAD_DIM ** -0.5)
    k = l2_normalize(qkv[:, :, 1])
    v = qkv[:, :, 2]
    ab = p[..., COL_QKV:N_STATE_COLS].astype(jnp.float32).reshape(B, T, 2, 2, B_HEADS)
    g = -jnp.exp(a_log.astype(jnp.float32)) * jax.nn.softplus(ab[:, :, 0] + dt_bias.astype(jnp.float32))
    beta = jax.nn.sigmoid(ab[:, :, 1])
    return q, k, v, g, beta


def gdn_chunked(q, k, v, g, beta, s0):
    B, T, H, Dk = q.shape
    C = GDN_CHUNK
    N = T // C

    def chunks(t):
        t = t.reshape(B, N, C, H, *t.shape[3:])
        return jnp.moveaxis(t, (1, 3), (0, 2))

    qc, kc, vc = chunks(q), chunks(k), chunks(v)
    gcum = jnp.cumsum(chunks(g), axis=-1)
    bc = chunks(beta)
    tri = jnp.tril(jnp.ones((C, C), dtype=bool))
    strict = jnp.tril(jnp.ones((C, C), dtype=bool), -1)
    diff = gcum[..., :, None] - gcum[..., None, :]
    decay = jnp.where(tri, jnp.exp(jnp.where(tri, diff, 0.0)), 0.0)
    kb = kc * bc[..., None]
    a_mat = jnp.where(strict, jnp.einsum("nbhid,nbhjd->nbhij", kb, kc) * decay, 0.0)
    eye = jnp.eye(C, dtype=jnp.float32)
    t_inv = lax.linalg.triangular_solve(eye + a_mat, jnp.broadcast_to(eye, a_mat.shape),
                                        left_side=True, lower=True, unit_diagonal=True)
    u = t_inv @ (vc * bc[..., None])
    w = t_inv @ (kb * jnp.exp(gcum)[..., None])
    attn = jnp.einsum("nbhid,nbhjd->nbhij", qc, kc) * decay
    q_g = qc * jnp.exp(gcum)[..., None]
    g_last = gcum[..., -1]
    k_g = kc * jnp.exp(g_last[..., None] - gcum)[..., None]

    def step(state, xs):
        q_i, k_i, u_i, w_i, a_i, gl_i = xs
        v_new = u_i - w_i @ state
        o_i = q_i @ state + a_i @ v_new
        state = state * jnp.exp(gl_i)[..., None, None] + jnp.swapaxes(k_i, -1, -2) @ v_new
        return state, o_i

    s_final, o = lax.scan(step, s0, (q_g, k_g, u, w, attn, g_last))
    o = jnp.moveaxis(o, (0, 2), (1, 3)).reshape(B, T, H, v.shape[-1])
    return o, s_final


def gdn_two_dirs(q, k, v, g, beta, s0_f, s0_b):
    o_f, s_f = gdn_chunked(q, k, v, g[:, :, 0], beta[:, :, 0], s0_f)
    flip = lambda t: jnp.flip(t, axis=1)
    o_b, s_b = gdn_chunked(flip(q), flip(k), flip(v), flip(g[:, :, 1]), flip(beta[:, :, 1]), s0_b)
    return o_f + flip(o_b), s_f, s_b


def gdn_out(o, z, norm_g):
    B, T = o.shape[:2]
    zh = z.reshape(B, T, B_HEADS, B_HEAD_DIM).astype(jnp.float32)
    y = o * lax.rsqrt(jnp.mean(o * o, axis=-1, keepdims=True) + NORM_EPS)
    y = y * norm_g.astype(jnp.float32) * jax.nn.silu(zh)
    return y.reshape(B, T, B_W)


def mix_out(p, o, n_chunks, gdn_norm_g, gm_norm_g, gm_ws, gm_bs, w_out):
    y_a = chunk_mlp(p[..., COL_Z_END:], n_chunks, gm_norm_g, gm_ws, gm_bs)
    y_b = gdn_out(o, p[..., N_STATE_COLS:COL_Z_END], gdn_norm_g).astype(p.dtype)
    return jnp.concatenate([y_a, y_b], axis=-1) @ w_out


def expert_choice_ffn(h, w_router, b_router, w_gate, w_up, w_down):
    B, N, D = h.shape
    cap = EC_CAPACITY * N // N_EXPERTS
    aff = jax.nn.softmax((h @ w_router + b_router).astype(jnp.float32), axis=-1)
    gate, idx = lax.top_k(jnp.swapaxes(aff, 1, 2), cap)
    xe = jax.vmap(lambda hb, ib: hb[ib])(h, idx)
    hid = jax.nn.silu(jnp.einsum("becd,edf->becf", xe, w_gate)) * jnp.einsum("becd,edf->becf", xe, w_up)
    ye = jnp.einsum("becf,efd->becd", hid, w_down) * gate[..., None].astype(h.dtype)
    return jax.vmap(lambda ib, yb: jnp.zeros((N, D), yb.dtype).at[ib.reshape(-1)].add(yb.reshape(-1, D)))(idx, ye)


def setup_inputs(seed: int = 0) -> dict:
    key = jax.random.key(seed)
    ks = jax.random.split(key, 24)
    f32 = jnp.float32
    nrm = lambda k, shape, s: jax.random.normal(k, shape, f32) * s
    dt = jnp.exp(jax.random.uniform(ks[10], (DEPTH, 2, B_HEADS), f32, np.log(1e-3), np.log(1e-1)))
    return {
        "x": nrm(ks[0], (BATCH, SEQ, D_MODEL), 1.0),
        "c": nrm(ks[1], (BATCH, D_MODEL), 1.0),
        "ctx": nrm(ks[2], (BATCH, CTX_LEN, D_MODEL), 1.0),
        "c_ctx": nrm(ks[3], (D_MODEL,), 1.0),
        "w_mod": nrm(ks[4], (DEPTH, D_MODEL, N_MOD * D_MODEL), 0.5 * D_MODEL ** -0.5),
        "b_mod": nrm(ks[5], (DEPTH, N_MOD * D_MODEL), 0.02),
        "norm1_g": 1.0 + nrm(ks[6], (DEPTH, D_MODEL), 0.05),
        "norm2_g": 1.0 + nrm(ks[7], (DEPTH, D_MODEL), 0.05),
        "w_in": nrm(ks[8], (DEPTH, D_MODEL, IN_COLS), D_MODEL ** -0.5),
        "conv_w": nrm(ks[9], (DEPTH, CONV_K, COL_QKV), CONV_K ** -0.5),
        "a_log": jnp.log(jax.random.uniform(ks[11], (DEPTH, 2, B_HEADS), f32, 1.0, 16.0)),
        "dt_bias": dt + jnp.log(-jnp.expm1(-dt)),
        "gdn_norm_g": 1.0 + nrm(ks[12], (DEPTH, B_HEAD_DIM), 0.05),
        "gm_norm_g": 1.0 + nrm(ks[13], (DEPTH, A_W), 0.05),
        "gm_ws": nrm(ks[14], (DEPTH, A_GROUPS, A_CHUNK, A_CHUNK), A_CHUNK ** -0.5),
        "gm_bs": 1.0 + nrm(ks[15], (DEPTH, A_GROUPS, A_CHUNK), 0.1),
        "w_out": nrm(ks[16], (DEPTH, MIX_W, D_MODEL), MIX_W ** -0.5),
        "w_router": nrm(ks[17], (DEPTH, D_MODEL, N_EXPERTS), D_MODEL ** -0.5),
        "b_router": nrm(ks[18], (DEPTH, N_EXPERTS), 0.01),
        "w_gate": nrm(ks[19], (DEPTH, N_EXPERTS, D_MODEL, EXPERT_FF), D_MODEL ** -0.5),
        "w_up": nrm(ks[20], (DEPTH, N_EXPERTS, D_MODEL, EXPERT_FF), D_MODEL ** -0.5),
        "w_down": nrm(ks[21], (DEPTH, N_EXPERTS, EXPERT_FF, D_MODEL), EXPERT_FF ** -0.5),
        "final_norm_g": 1.0 + nrm(ks[22], (D_MODEL,), 0.05),
    }


def reference(x, c, ctx, c_ctx, w_mod, b_mod, norm1_g, norm2_g, w_in, conv_w, a_log, dt_bias,
              gdn_norm_g, gm_norm_g, gm_ws, gm_bs, w_out, w_router, b_router, w_gate, w_up, w_down,
              final_norm_g):
    B, T, _ = x.shape
    rows = T // GRID_W
    n_lat_chunks = rows // ROWS_PER_CHUNK
    n_ctx_chunks = ctx.shape[1] // A_CHUNK
    h_lat, h_ctx = x, ctx
    for layer in range(DEPTH):
        last = layer == DEPTH - 1
        mod = jax.nn.silu(c) @ w_mod[layer] + b_mod[layer]
        mod_c = jax.nn.silu(c_ctx) @ w_mod[layer] + b_mod[layer]
        sh1, sc1, gt1, sh2, sc2, gt2 = jnp.split(mod[:, None, :], N_MOD, axis=-1)
        csh1, csc1, cgt1, csh2, csc2, cgt2 = jnp.split(mod_c, N_MOD, axis=-1)
        wl = w_in[layer]

        c_in = rms_norm(h_ctx, norm1_g[layer]) * (1.0 + csc1) + csh1
        p_ctx = c_in @ (wl[:, :N_STATE_COLS] if last else wl)
        qc, kc, vc, gc, bc = gdn_streams(p_ctx[..., :N_STATE_COLS], conv_w[layer], a_log[layer], dt_bias[layer])
        zero_state = jnp.zeros((B, B_HEADS, B_HEAD_DIM, B_HEAD_DIM), jnp.float32)
        o_ctx, s_f, s_b = gdn_two_dirs(qc, kc, vc, gc, bc, zero_state, zero_state)

        a_in = rms_norm(h_lat, norm1_g[layer]) * (1.0 + sc1) + sh1
        p_lat = a_in @ wl
        ql, kl, vl, gl, bl = gdn_streams(p_lat[..., :N_STATE_COLS], conv_w[layer], a_log[layer], dt_bias[layer])
        o_lat, _, _ = gdn_two_dirs(ql, kl, vl, gl, bl, s_f, s_b)
        h_lat = h_lat + gt1 * mix_out(p_lat, o_lat, n_lat_chunks, gdn_norm_g[layer], gm_norm_g[layer],
                                      gm_ws[layer], gm_bs[layer], w_out[layer])
        f_in = rms_norm(h_lat, norm2_g[layer]) * (1.0 + sc2) + sh2
        h_lat = h_lat + gt2 * expert_choice_ffn(f_in, w_router[layer], b_router[layer],
                                                w_gate[layer], w_up[layer], w_down[layer])

        if not last:
            h_ctx = h_ctx + cgt1 * mix_out(p_ctx, o_ctx, n_ctx_chunks, gdn_norm_g[layer], gm_norm_g[layer],
                                           gm_ws[layer], gm_bs[layer], w_out[layer])
            cf_in = rms_norm(h_ctx, norm2_g[layer]) * (1.0 + csc2) + csh2
            h_ctx = h_ctx + cgt2 * expert_choice_ffn(cf_in, w_router[layer], b_router[layer],
                                                     w_gate[layer], w_up[layer], w_down[layer])
    return rms_norm(h_lat, final_norm_g)
```

```python
import functools

import jax
import jax.numpy as jnp
from jax import lax
from jax.experimental import pallas as pl
from jax.experimental.pallas import tpu as pltpu

F32 = jnp.float32
BF16 = jnp.bfloat16
I32 = jnp.int32

D_MODEL = 1024
N_MOD = 6
N_HEADS = 4
HEAD_DIM = 128
B_W = N_HEADS * HEAD_DIM
QKV_W = 3 * B_W
A_W = 512
A_GROUPS = 4
A_CHUNK = 128
GDN_CHUNK = 64
N_EXPERTS = 16
EC_CAPACITY = 2
EXPERT_FF = 1024
NORM_EPS = 1e-6
LANES = 128
STATE_COLS = 4 * N_HEADS

ROUTE_CHUNK = 256
SLOT_WIN = 64
SLOT_ALIGN = 16
WIN_GROUP = 4
VMEM_LIMIT = 56 * 1024 * 1024


def _cparams(sem):
    return pltpu.CompilerParams(dimension_semantics=sem, vmem_limit_bytes=VMEM_LIMIT)


def _dot(a, b):
    return jnp.dot(a, b, preferred_element_type=F32)


def _dot_nt(a, b):
    return lax.dot_general(a, b, (((1,), (1,)), ((), ())), preferred_element_type=F32)


def _dot_tn(a, b):
    return lax.dot_general(a, b, (((0,), (0,)), ((), ())), preferred_element_type=F32)


def _silu(x):
    return x * jax.nn.sigmoid(x)


def _mod_body(c_ref, w_ref, b_ref, o_ref):
    s = _silu(c_ref[...])
    o_ref[...] = _dot(s.astype(BF16), w_ref[...].astype(BF16)) + b_ref[...]


def _mod_call(cs, w_mod, b_mod):
    n = w_mod.shape[1] // D_MODEL
    return pl.pallas_call(
        _mod_body,
        out_shape=jax.ShapeDtypeStruct((8, w_mod.shape[1]), F32),
        grid=(n,),
        in_specs=[pl.BlockSpec((8, D_MODEL), lambda j: (0, 0)),
                  pl.BlockSpec((D_MODEL, D_MODEL), lambda j: (0, j)),
                  pl.BlockSpec((1, D_MODEL), lambda j: (0, j))],
        out_specs=pl.BlockSpec((8, D_MODEL), lambda j: (0, j)),
        compiler_params=_cparams(("arbitrary",)),
        name="mod",
    )(cs, w_mod, b_mod)


def _norm_mod(x, g, shift, scale):
    ms = jnp.mean(x * x, axis=-1, keepdims=True)
    return (x * lax.rsqrt(ms + NORM_EPS) * g) * (1.0 + scale) + shift


def _gate_streams(st, gp_ref):
    lane = lax.broadcasted_iota(I32, st.shape, 1)
    g = -jnp.exp(gp_ref[0:1, :]) * jax.nn.softplus(st + gp_ref[1:2, :])
    beta = jax.nn.sigmoid(st)
    return jnp.where(lane < 2 * N_HEADS, g, jnp.where(lane < STATE_COLS, beta, 0.0))


def _inproj_lat_body(x_ref, mod_ref, g1_ref, w_ref, gp_ref, gmg_ref, ws_ref, bst_ref,
                     qkv_ref, gb_ref, z_ref, ya_ref, *, tm):
    a = _norm_mod(x_ref[0], g1_ref[...], mod_ref[0, 0:1, :], mod_ref[0, 1:2, :]).astype(BF16)
    qkv_ref[0] = _dot(a, w_ref[:, 0:QKV_W])
    z_ref[0] = _dot(a, w_ref[:, QKV_W:QKV_W + B_W])
    c_uv = QKV_W + B_W
    gb_ref[0] = _gate_streams(_dot(a, w_ref[:, c_uv + 2 * A_W:c_uv + 2 * A_W + LANES]), gp_ref)
    uv = _dot(a, w_ref[:, c_uv:c_uv + 2 * A_W])
    uv = 0.5 * uv * (1.0 + lax.erf(uv * 0.7071067811865476))
    gd = A_W // A_GROUPS
    for grp in range(A_GROUPS):
        v = uv[:, A_W + grp * gd:A_W + (grp + 1) * gd]
        vn = v * lax.rsqrt(jnp.mean(v * v, axis=-1, keepdims=True) + NORM_EPS) * gmg_ref[:, grp * gd:(grp + 1) * gd]
        vn = vn.astype(BF16)
        bias = bst_ref[:, grp:grp + 1]
        for c in range(tm // A_CHUNK):
            rows = slice(c * A_CHUNK, (c + 1) * A_CHUNK)
            s = _dot(ws_ref[grp], vn[rows]) + bias
            ya_ref[0, rows, grp * gd:(grp + 1) * gd] = (uv[rows, grp * gd:(grp + 1) * gd] * s).astype(BF16)


def _inproj_ctx_body(x_ref, mod_ref, g1_ref, w_ref, gp_ref, qkv_ref, gb_ref):
    a = _norm_mod(x_ref[0], g1_ref[...], mod_ref[0, 0:1, :], mod_ref[0, 1:2, :]).astype(BF16)
    qkv_ref[0] = _dot(a, w_ref[:, 0:QKV_W])
    gb_ref[0] = _gate_streams(_dot(a, w_ref[:, QKV_W:QKV_W + LANES]), gp_ref)


def _inproj_lat_call(x, mod3, g1, w_lat, gp, gmg, ws16, bst, tm):
    bsz, t, _ = x.shape
    full = lambda shape: pl.BlockSpec(shape, lambda b, i: (0,) * len(shape))
    tok = lambda w: pl.BlockSpec((1, tm, w), lambda b, i: (b, i, 0))
    return pl.pallas_call(
        functools.partial(_inproj_lat_body, tm=tm),
        out_shape=(jax.ShapeDtypeStruct((bsz, t, QKV_W), F32),
                   jax.ShapeDtypeStruct((bsz, t, LANES), F32),
                   jax.ShapeDtypeStruct((bsz, t, B_W), F32),
                   jax.ShapeDtypeStruct((bsz, t, A_W), BF16)),
        grid=(bsz, t // tm),
        in_specs=[tok(D_MODEL),
                  pl.BlockSpec((1, N_MOD, D_MODEL), lambda b, i: (b, 0, 0)),
                  full((1, D_MODEL)), full(w_lat.shape), full(gp.shape), full(gmg.shape),
                  full(ws16.shape), full(bst.shape)],
        out_specs=(tok(QKV_W), tok(LANES), tok(B_W), tok(A_W)),
        compiler_params=_cparams(("parallel", "parallel")),
        name="inproj_lat",
    )(x, mod3, g1, w_lat, gp, gmg, ws16, bst)


def _inproj_ctx_call(ctx, mod3, ctx_row, g1, w_ctx, gp, tm):
    bsz, t, _ = ctx.shape
    full = lambda shape: pl.BlockSpec(shape, lambda b, i: (0,) * len(shape))
    tok = lambda w: pl.BlockSpec((1, tm, w), lambda b, i: (b, i, 0))
    return pl.pallas_call(
        _inproj_ctx_body,
        out_shape=(jax.ShapeDtypeStruct((bsz, t, QKV_W), F32),
                   jax.ShapeDtypeStruct((bsz, t, LANES), F32)),
        grid=(bsz, t // tm),
        in_specs=[tok(D_MODEL),
                  pl.BlockSpec((1, N_MOD, D_MODEL), lambda b, i: (ctx_row, 0, 0)),
                  full((1, D_MODEL)), full(w_ctx.shape), full(gp.shape)],
        out_specs=(tok(QKV_W), tok(LANES)),
        compiler_params=_cparams(("parallel", "parallel")),
        name="inproj_ctx",
    )(ctx, mod3, g1, w_ctx, gp)


def _conv_norm(main_ref, prev_ref, next_ref, first, last, cw_ref, out_ref, tb):
    cs = GDN_CHUNK
    nsub = tb // cs
    w0, w1, w2 = cw_ref[0:1, :], cw_ref[1:2, :], cw_ref[2:3, :]
    row = lax.broadcasted_iota(I32, (cs, 1), 0)
    edge_prev = jnp.where(first, 0.0, prev_ref[0, 7:8, :])
    edge_next = jnp.where(last, 0.0, next_ref[0, 0:1, :])

    def body(c, carry):
        r0 = pl.multiple_of(c * cs, cs)
        x = main_ref[0, pl.ds(r0, cs), :]
        p8 = main_ref[0, pl.ds(pl.multiple_of(jnp.maximum(r0 - 8, 0), 8), 8), :]
        n8 = main_ref[0, pl.ds(pl.multiple_of(jnp.minimum(r0 + cs, tb - 8), 8), 8), :]
        prow = jnp.where(c == 0, edge_prev, p8[7:8, :])
        nrow = jnp.where(c == nsub - 1, edge_next, n8[0:1, :])
        xp = jnp.where(row == 0, prow, pltpu.roll(x, 1, 0))
        xn = jnp.where(row == cs - 1, nrow, pltpu.roll(x, cs - 1, 0))
        y = _silu(xp * w0 + x * w1 + xn * w2)
        for h in range(N_HEADS):
            cq = slice(h * HEAD_DIM, (h + 1) * HEAD_DIM)
            ck = slice(B_W + h * HEAD_DIM, B_W + (h + 1) * HEAD_DIM)
            q = y[:, cq]
            k = y[:, ck]
            out_ref[pl.ds(r0, cs), cq] = q * (lax.rsqrt(jnp.sum(q * q, axis=-1, keepdims=True) + NORM_EPS)
                                              * (HEAD_DIM ** -0.5))
            out_ref[pl.ds(r0, cs), ck] = k * lax.rsqrt(jnp.sum(k * k, axis=-1, keepdims=True) + NORM_EPS)
        out_ref[pl.ds(r0, cs), 2 * B_W:3 * B_W] = y[:, 2 * B_W:3 * B_W]
        return carry

    lax.fori_loop(0, nsub, body, 0)


def _tri_inverse_minus_eye(a, level_masks):
    m = -jnp.where(level_masks[0], a, 0.0)
    for lm in level_masks[1:]:
        cm = jnp.where(lm, a, 0.0)
        x = cm + _dot(m.astype(BF16), cm.astype(BF16))
        y = x + _dot(x.astype(BF16), m.astype(BF16))
        m = m - y
    return m


def _gdn_chain(q, k, v, gc, gc_row, beta, egc, glast, incl, strict, level_masks, s_ref, h):
    decay = jnp.where(incl, jnp.exp(jnp.where(incl, gc - gc_row, 0.0)), 0.0)
    kb = k * beta
    k16 = k.astype(BF16)
    a = jnp.where(strict, _dot_nt(kb.astype(BF16), k16) * decay, 0.0)
    attn = _dot_nt(q.astype(BF16), k16) * decay
    m = _tri_inverse_minus_eye(a, level_masks)
    rhs = jnp.concatenate([v * beta, kb * egc], axis=1)
    uw = rhs + _dot(m.astype(BF16), rhs.astype(BF16))
    u = uw[:, :HEAD_DIM]
    w = uw[:, HEAD_DIM:]
    s = s_ref[h]
    wq = _dot(jnp.concatenate([w, q * egc], axis=0).astype(BF16), s.astype(BF16))
    v_new = (u - wq[:GDN_CHUNK]).astype(BF16)
    o = wq[GDN_CHUNK:] + _dot(attn.astype(BF16), v_new)
    kg = k * jnp.exp(glast - gc)
    s_ref[h] = s * jnp.exp(glast) + _dot_tn(kg.astype(BF16), v_new)
    return o


def _gdn_body(qf_ref, qfp_ref, qfn_ref, qb_ref, qbp_ref, qbn_ref, gbf_ref, gbb_ref, cw_ref, s0f_ref, s0b_ref,
              of_ref, ob_ref, sff_ref, sfb_ref, sf_scr, sb_scr, nf_scr, nb_scr, *, tb, nt):
    t = pl.program_id(1)
    cs = GDN_CHUNK
    nch = tb // cs

    @pl.when(t == 0)
    def _():
        sf_scr[...] = s0f_ref[0]
        sb_scr[...] = s0b_ref[0]

    _conv_norm(qf_ref, qfp_ref, qfn_ref, t == 0, t == nt - 1, cw_ref, nf_scr, tb)
    _conv_norm(qb_ref, qbp_ref, qbn_ref, t == nt - 1, t == 0, cw_ref, nb_scr, tb)

    ii = lax.broadcasted_iota(I32, (cs, cs), 0)
    jj = lax.broadcasted_iota(I32, (cs, cs), 1)
    incl_f, strict_f = jj <= ii, jj < ii
    incl_b, strict_b = jj >= ii, jj > ii
    levels = []
    sh = 0
    while (1 << sh) < cs:
        levels.append(((ii >> (sh + 1)) == (jj >> (sh + 1))) & ((ii >> sh) != (jj >> sh)))
        sh += 1
    lv_f = [lm & strict_f for lm in levels]
    lv_b = [lm & strict_b for lm in levels]
    row = lax.broadcasted_iota(I32, (cs, LANES), 0)

    def step(n, carry):
        rf = pl.multiple_of(n * cs, cs)
        rb = pl.multiple_of((nch - 1 - n) * cs, cs)
        gf = gbf_ref[0, pl.ds(rf, cs), :]
        gb = gbb_ref[0, pl.ds(rb, cs), :]
        cf, cb = gf, gb
        s = 1
        while s < cs:
            cf = cf + jnp.where(row >= s, pltpu.roll(cf, s, 0), 0.0)
            cb = cb + jnp.where(row < cs - s, pltpu.roll(cb, cs - s, 0), 0.0)
            s *= 2
        gt = jnp.concatenate([cf, cb], axis=0).T
        ecf, ecb = jnp.exp(cf), jnp.exp(cb)
        for h in range(N_HEADS):
            cq = slice(h * HEAD_DIM, (h + 1) * HEAD_DIM)
            ck = slice(B_W + h * HEAD_DIM, B_W + (h + 1) * HEAD_DIM)
            cv = slice(2 * B_W + h * HEAD_DIM, 2 * B_W + (h + 1) * HEAD_DIM)
            lf = h
            o = _gdn_chain(nf_scr[pl.ds(rf, cs), cq], nf_scr[pl.ds(rf, cs), ck], nf_scr[pl.ds(rf, cs), cv],
                           cf[:, lf:lf + 1], gt[lf:lf + 1, 0:cs], gf[:, 2 * N_HEADS + h:2 * N_HEADS + h + 1],
                           ecf[:, lf:lf + 1], cf[cs - 1:cs, lf:lf + 1], incl_f, strict_f, lv_f, sf_scr, h)
            of_ref[0, pl.ds(rf, cs), cq] = o
            lb = N_HEADS + h
            o = _gdn_chain(nb_scr[pl.ds(rb, cs), cq], nb_scr[pl.ds(rb, cs), ck], nb_scr[pl.ds(rb, cs), cv],
                           cb[:, lb:lb + 1], gt[lb:lb + 1, cs:2 * cs], gb[:, 3 * N_HEADS + h:3 * N_HEADS + h + 1],
                           ecb[:, lb:lb + 1], cb[0:1, lb:lb + 1], incl_b, strict_b, lv_b, sb_scr, h)
            ob_ref[0, pl.ds(rb, cs), cq] = o
        return carry

    lax.fori_loop(0, nch, step, 0)

    @pl.when(t == nt - 1)
    def _():
        sff_ref[0] = sf_scr[...]
        sfb_ref[0] = sb_scr[...]


def _gdn_call(qkv, gb, cw, s0f, s0b, tb):
    bsz, t, _ = qkv.shape
    nt = t // tb
    hb = tb // 8
    last8 = t // 8 - 1
    main = lambda f: pl.BlockSpec((1, tb, QKV_W), lambda b, i: (b, f(i), 0))
    halo_p = lambda f: pl.BlockSpec((1, 8, QKV_W), lambda b, i: (b, jnp.maximum(f(i) * hb - 1, 0), 0))
    halo_n = lambda f: pl.BlockSpec((1, 8, QKV_W), lambda b, i: (b, jnp.minimum((f(i) + 1) * hb, last8), 0))
    fwd = lambda i: i
    bwd = lambda i: nt - 1 - i
    st = pl.BlockSpec((1, N_HEADS, HEAD_DIM, HEAD_DIM), lambda b, i: (b, 0, 0, 0))
    return pl.pallas_call(
        functools.partial(_gdn_body, tb=tb, nt=nt),
        out_shape=(jax.ShapeDtypeStruct((bsz, t, B_W), F32), jax.ShapeDtypeStruct((bsz, t, B_W), F32),
                   jax.ShapeDtypeStruct((bsz, N_HEADS, HEAD_DIM, HEAD_DIM), F32),
                   jax.ShapeDtypeStruct((bsz, N_HEADS, HEAD_DIM, HEAD_DIM), F32)),
        grid=(bsz, nt),
        in_specs=[main(fwd), halo_p(fwd), halo_n(fwd), main(bwd), halo_p(bwd), halo_n(bwd),
                  pl.BlockSpec((1, tb, LANES), lambda b, i: (b, i, 0)),
                  pl.BlockSpec((1, tb, LANES), lambda b, i: (b, nt - 1 - i, 0)),
                  pl.BlockSpec(cw.shape, lambda b, i: (0, 0)), st, st],
        out_specs=(pl.BlockSpec((1, tb, B_W), lambda b, i: (b, i, 0)),
                   pl.BlockSpec((1, tb, B_W), lambda b, i: (b, nt - 1 - i, 0)), st, st),
        scratch_shapes=[pltpu.VMEM((N_HEADS, HEAD_DIM, HEAD_DIM), F32), pltpu.VMEM((N_HEADS, HEAD_DIM, HEAD_DIM), F32),
                        pltpu.VMEM((tb, QKV_W), F32), pltpu.VMEM((tb, QKV_W), F32)],
        compiler_params=_cparams(("parallel", "arbitrary")),
        name="gdn",
    )(qkv, qkv, qkv, qkv, qkv, qkv, gb, gb, cw, s0f, s0b)


def _mixout_body(x_ref, of_ref, ob_ref, z_ref, ya_ref, mod_ref, gng_ref, wout_ref, n2g_ref, wrh_ref, wrl_ref, br_ref,
                 h_ref, fin_ref, aff_ref):
    o = of_ref[0] + ob_ref[0]
    z = z_ref[0]
    parts = [ya_ref[0]]
    for h in range(N_HEADS):
        c = slice(h * HEAD_DIM, (h + 1) * HEAD_DIM)
        oh = o[:, c]
        y = oh * lax.rsqrt(jnp.mean(oh * oh, axis=-1, keepdims=True) + NORM_EPS)
        parts.append((y * gng_ref[...] * _silu(z[:, c])).astype(BF16))
    mix = _dot(jnp.concatenate(parts, axis=1), wout_ref[...])
    hl = x_ref[0] + mod_ref[0, 2:3, :] * mix
    h_ref[0] = hl
    fin = _norm_mod(hl, n2g_ref[...], mod_ref[0, 3:4, :], mod_ref[0, 4:5, :])
    f_hi = fin.astype(BF16)
    fin_ref[0] = f_hi
    f_lo = (fin - f_hi.astype(F32)).astype(BF16)
    logits = _dot(f_hi, wrh_ref[...]) + _dot(f_lo, wrh_ref[...]) + _dot(f_hi, wrl_ref[...]) + br_ref[...]
    e = jnp.exp(logits - jnp.max(logits, axis=-1, keepdims=True))
    aff_ref[0] = e / jnp.sum(e, axis=-1, keepdims=True)


def _mixout_call(x, o_f, o_b, z, ya, mod3, gng, wout16, n2g, wr_hi, wr_lo, br, tm):
    bsz, t, _ = x.shape
    full = lambda a: pl.BlockSpec(a.shape, lambda b, i: (0,) * a.ndim)
    tok = lambda w: pl.BlockSpec((1, tm, w), lambda b, i: (b, i, 0))
    return pl.pallas_call(
        _mixout_body,
        out_shape=(jax.ShapeDtypeStruct((bsz, t, D_MODEL), F32), jax.ShapeDtypeStruct((bsz, t, D_MODEL), BF16),
                   jax.ShapeDtypeStruct((bsz, t, LANES), F32)),
        grid=(bsz, t // tm),
        in_specs=[tok(D_MODEL), tok(B_W), tok(B_W), tok(B_W), tok(A_W),
                  pl.BlockSpec((1, N_MOD, D_MODEL), lambda b, i: (b, 0, 0)),
                  full(gng), full(wout16), full(n2g), full(wr_hi), full(wr_lo), full(br)],
        out_specs=(tok(D_MODEL), tok(D_MODEL), tok(LANES)),
        compiler_params=_cparams(("parallel", "parallel")),
        name="mixout",
    )(x, o_f, o_b, z, ya, mod3, gng, wout16, n2g, wr_hi, wr_lo, br)


def _route_body(aff_ref, slot_ref, slott_ref, base_ref, *, t, cap):
    rc = ROUTE_CHUNK
    nchunk = t // rc

    def search(i, thr):
        cand = thr | jnp.left_shift(jnp.int32(1), 30 - i)
        cnt = jnp.sum((aff_ref[0] >= pltpu.bitcast(cand, F32)).astype(I32), axis=0, keepdims=True)
        return jnp.where(cnt >= cap, cand, thr)

    thr_bits = lax.fori_loop(0, 31, search, jnp.zeros((1, LANES), I32))
    thr = pltpu.bitcast(thr_bits, F32)
    n_gt = jnp.sum((aff_ref[0] > thr).astype(I32), axis=0, keepdims=True)
    need = (cap - n_gt).astype(F32)

    r = lax.broadcasted_iota(I32, (rc, rc), 0)
    c = lax.broadcasted_iota(I32, (rc, rc), 1)
    tril = (c <= r).astype(BF16)

    def chunk(ci, carry):
        ceq, csel = carry
        r0 = pl.multiple_of(ci * rc, rc)
        xc = aff_ref[0, pl.ds(r0, rc), :]
        gt = xc > thr
        eq = xc == thr
        eqp = _dot(tril, eq.astype(BF16)) + ceq
        sel = gt | (eq & (eqp <= need))
        selp = _dot(tril, sel.astype(BF16)) + csel
        slot = jnp.where(sel, selp - 1.0, -1.0)
        slot_ref[0, pl.ds(r0, rc), :] = slot.astype(I32)
        base_ref[0, pl.ds(ci, 1), :] = csel.astype(I32)
        for hh in range(rc // LANES):
            st = slot[hh * LANES:(hh + 1) * LANES, :].T
            slott_ref[0, ci * (rc // LANES) + hh] = st[0:N_EXPERTS, :].astype(I32)
        return eqp[rc - 1:rc, :], selp[rc - 1:rc, :]

    _, csel = lax.fori_loop(0, nchunk, chunk, (jnp.zeros((1, LANES), F32), jnp.zeros((1, LANES), F32)))
    base_ref[0, nchunk:nchunk + 1, :] = csel.astype(I32)


def _route_call(aff, cap):
    bsz, t, _ = aff.shape
    nchunk = t // ROUTE_CHUNK
    return pl.pallas_call(
        functools.partial(_route_body, t=t, cap=cap),
        out_shape=(jax.ShapeDtypeStruct((bsz, t, LANES), I32),
                   jax.ShapeDtypeStruct((bsz, t // LANES, N_EXPERTS, LANES), I32),
                   jax.ShapeDtypeStruct((bsz, nchunk + 1, LANES), I32)),
        grid=(bsz,),
        in_specs=[pl.BlockSpec((1, t, LANES), lambda b: (b, 0, 0))],
        out_specs=(pl.BlockSpec((1, t, LANES), lambda b: (b, 0, 0)),
                   pl.BlockSpec((1, t // LANES, N_EXPERTS, LANES), lambda b: (b, 0, 0, 0)),
                   pl.BlockSpec((1, nchunk + 1, LANES), lambda b: (b, 0, 0))),
        compiler_params=_cparams(("parallel",)),
        name="route",
    )(aff)


def _window_plan(base_ref, flat0, experts):
    starts, rounds = [], jnp.int32(0)
    for e in experts:
        lo = base_ref[flat0 + e]
        hi = base_ref[flat0 + N_EXPERTS + e]
        lo_al = (lo >> 4) << 4
        starts.append(lo_al)
        rounds = jnp.maximum(rounds, (hi - lo_al + SLOT_WIN - 1) // SLOT_WIN)
    return starts, rounds


def _dispatch_body(base_ref, slott_ref, fin_ref, xe_ref, *, nchunk, sub, eh_n):
    b, eh, ci = pl.program_id(0), pl.program_id(1), pl.program_id(2)
    rc = ROUTE_CHUNK

    @pl.when(ci == 0)
    def _():
        xe_ref[...] = jnp.zeros_like(xe_ref)

    srow = lax.broadcasted_iota(I32, (SLOT_WIN, rc), 0)
    for sc in range(sub):
        cc = ci * sub + sc
        flat0 = (b * (nchunk + 1) + cc) * N_EXPERTS + eh * eh_n
        f = fin_ref[0, sc * rc:(sc + 1) * rc, :]
        for g in range(eh_n // WIN_GROUP):
            experts = [g * WIN_GROUP + el for el in range(WIN_GROUP)]
            starts, rounds = _window_plan(base_ref, flat0, experts)

            def one_round(r, carry, experts=experts, starts=starts, f=f, sc=sc):
                rows = []
                for el, e in enumerate(experts):
                    tok_slot = jnp.concatenate(
                        [slott_ref[0, sc * (rc // LANES) + j, e:e + 1, :] for j in range(rc // LANES)], axis=1)
                    rows.append((tok_slot == srow + (starts[el] + r * SLOT_WIN)).astype(BF16))
                prod = _dot(jnp.concatenate(rows, axis=0), f)
                for el, e in enumerate(experts):
                    win = pl.ds(pl.multiple_of(starts[el] + r * SLOT_WIN, SLOT_ALIGN), SLOT_WIN)
                    xe_ref[0, e, win, :] = xe_ref[0, e, win, :] + prod[el * SLOT_WIN:(el + 1) * SLOT_WIN].astype(BF16)
                return carry

            lax.fori_loop(0, rounds, one_round, 0)


def _dispatch_call(base_flat, slott, fin, cap):
    bsz, t, _ = fin.shape
    nchunk = t // ROUTE_CHUNK
    sub = 2
    eh_n = N_EXPERTS // 2
    sp = cap + SLOT_WIN
    grid_spec = pltpu.PrefetchScalarGridSpec(
        num_scalar_prefetch=1,
        grid=(bsz, N_EXPERTS // eh_n, nchunk // sub),
        in_specs=[pl.BlockSpec((1, sub * ROUTE_CHUNK // LANES, eh_n, LANES), lambda b, eh, ci, base: (b, ci, eh, 0)),
                  pl.BlockSpec((1, sub * ROUTE_CHUNK, D_MODEL), lambda b, eh, ci, base: (b, ci, 0))],
        out_specs=pl.BlockSpec((1, eh_n, sp, D_MODEL), lambda b, eh, ci, base: (b, eh, 0, 0)))
    return pl.pallas_call(
        functools.partial(_dispatch_body, nchunk=nchunk, sub=sub, eh_n=eh_n),
        out_shape=jax.ShapeDtypeStruct((bsz, N_EXPERTS, sp, D_MODEL), BF16),
        grid_spec=grid_spec,
        compiler_params=_cparams(("parallel", "parallel", "arbitrary")),
        name="dispatch",
    )(base_flat, slott, fin)


def _experts_body(xe_ref, wg_ref, wu_ref, wd_ref, y_ref, wg16, wu16, wd16, *, cap):
    @pl.when(pl.program_id(1) == 0)
    def _():
        wg16[...] = wg_ref[0].astype(BF16)
        wu16[...] = wu_ref[0].astype(BF16)
        wd16[...] = wd_ref[0].astype(BF16)

    x = xe_ref[0, 0, 0:cap, :]
    ft = 256
    acc = None
    for f in range(EXPERT_FF // ft):
        cols = slice(f * ft, (f + 1) * ft)
        hid = (_silu(_dot(x, wg16[:, cols])) * _dot(x, wu16[:, cols])).astype(BF16)
        part = _dot(hid, wd16[cols, :])
        acc = part if acc is None else acc + part
    y_ref[0, 0, 0:cap, :] = acc.astype(BF16)
    y_ref[0, 0, cap:, :] = jnp.zeros((y_ref.shape[2] - cap, D_MODEL), BF16)


def _experts_call(xe, w_gate, w_up, w_down, cap):
    bsz, _, sp, _ = xe.shape
    wspec = lambda shape: pl.BlockSpec((1,) + shape, lambda e, b: (e, 0, 0))
    slots = pl.BlockSpec((1, 1, sp, D_MODEL), lambda e, b: (b, e, 0, 0))
    return pl.pallas_call(
        functools.partial(_experts_body, cap=cap),
        out_shape=jax.ShapeDtypeStruct(xe.shape, BF16),
        grid=(N_EXPERTS, bsz),
        in_specs=[slots, wspec((D_MODEL, EXPERT_FF)), wspec((D_MODEL, EXPERT_FF)), wspec((EXPERT_FF, D_MODEL))],
        out_specs=slots,
        scratch_shapes=[pltpu.VMEM((D_MODEL, EXPERT_FF), BF16), pltpu.VMEM((D_MODEL, EXPERT_FF), BF16),
                        pltpu.VMEM((EXPERT_FF, D_MODEL), BF16)],
        compiler_params=_cparams(("arbitrary", "arbitrary")),
        name="experts",
    )(xe, w_gate, w_up, w_down)


def _combine_body(base_ref, slot_ref, aff_ref, h_ref, y_ref, mod_ref, fng_ref, o_ref, acc_ref, *, nchunk):
    b, ci = pl.program_id(0), pl.program_id(1)
    rc = ROUTE_CHUNK
    acc_ref[...] = jnp.zeros_like(acc_ref)
    flat0 = (b * (nchunk + 1) + ci) * N_EXPERTS
    width = WIN_GROUP * SLOT_WIN
    lane = lax.broadcasted_iota(I32, (rc, width), 1)
    lane_el = lane >> 6
    lane_j = lane & (SLOT_WIN - 1)
    slot = slot_ref[0]
    aff = aff_ref[0]
    for g in range(N_EXPERTS // WIN_GROUP):
        experts = [g * WIN_GROUP + el for el in range(WIN_GROUP)]
        starts, rounds = _window_plan(base_ref, flat0, experts)

        def one_round(r, carry, experts=experts, starts=starts):
            ywin = jnp.concatenate(
                [y_ref[0, e, pl.ds(pl.multiple_of(starts[el] + r * SLOT_WIN, SLOT_ALIGN), SLOT_WIN), :]
                 for el, e in enumerate(experts)], axis=0)
            s = jnp.zeros((rc, width), F32)
            for el, e in enumerate(experts):
                rel = slot[:, e:e + 1] - (starts[el] + r * SLOT_WIN)
                s = jnp.where((lane_el == el) & (rel == lane_j), aff[:, e:e + 1], s)
            acc_ref[...] += _dot(s.astype(BF16), ywin)
            return carry

        lax.fori_loop(0, rounds, one_round, 0)
    hl = h_ref[0] + mod_ref[0, 5:6, :] * acc_ref[...]
    ms = jnp.mean(hl * hl, axis=-1, keepdims=True)
    o_ref[0] = hl * lax.rsqrt(ms + NORM_EPS) * fng_ref[...]


def _combine_call(base_flat, slot, aff, h, y, mod3, fng):
    bsz, t, _ = h.shape
    nchunk = t // ROUTE_CHUNK
    rc = ROUTE_CHUNK
    tok = lambda w: pl.BlockSpec((1, rc, w), lambda b, i, base: (b, i, 0))
    grid_spec = pltpu.PrefetchScalarGridSpec(
        num_scalar_prefetch=1,
        grid=(bsz, nchunk),
        in_specs=[tok(LANES), tok(LANES), tok(D_MODEL),
                  pl.BlockSpec((1,) + y.shape[1:], lambda b, i, base: (b, 0, 0, 0), pipeline_mode=pl.Buffered(1)),
                  pl.BlockSpec((1, N_MOD, D_MODEL), lambda b, i, base: (b, 0, 0)),
                  pl.BlockSpec((1, D_MODEL), lambda b, i, base: (0, 0))],
        out_specs=tok(D_MODEL),
        scratch_shapes=[pltpu.VMEM((rc, D_MODEL), F32)])
    return pl.pallas_call(
        functools.partial(_combine_body, nchunk=nchunk),
        out_shape=jax.ShapeDtypeStruct(h.shape, F32),
        grid_spec=grid_spec,
        compiler_params=_cparams(("parallel", "arbitrary")),
        name="combine",
    )(base_flat, slot, aff, h, y, mod3, fng)


def _pad_lanes(a):
    return jnp.pad(a, ((0, 0), (0, LANES - a.shape[1])))


def kernel(x, c, ctx, c_ctx, w_mod, b_mod, norm1_g, norm2_g, w_in, conv_w, a_log, dt_bias, gdn_norm_g, gm_norm_g,
           gm_ws, gm_bs, w_out, w_router, b_router, w_gate, w_up, w_down, final_norm_g):
    bsz, t, _ = x.shape
    ctx_len = ctx.shape[1]
    assert w_mod.shape[0] == 1, "single-layer problem"
    assert t % 512 == 0 and ctx_len % GDN_CHUNK == 0 and bsz < 8
    cap = EC_CAPACITY * t // N_EXPERTS

    cs = jnp.zeros((8, D_MODEL), F32).at[:bsz].set(c).at[bsz].set(c_ctx)
    mod3 = _mod_call(cs, w_mod[0], b_mod[0][None, :]).reshape(8, N_MOD, D_MODEL)

    wl = w_in[0]
    n_state = QKV_W + STATE_COLS
    w_state = _pad_lanes(wl[:, QKV_W:n_state])
    w_lat = jnp.concatenate([wl[:, :QKV_W], wl[:, n_state:n_state + B_W], wl[:, n_state + B_W:], w_state],
                            axis=1).astype(BF16)
    w_ctx = jnp.concatenate([wl[:, :QKV_W], w_state], axis=1).astype(BF16)
    gp = jnp.zeros((8, LANES), F32).at[0, :2 * N_HEADS].set(a_log[0].reshape(-1)).at[1, :2 * N_HEADS].set(
        dt_bias[0].reshape(-1))
    g1 = norm1_g[0][None, :]
    cw = jnp.zeros((8, QKV_W), F32).at[:conv_w.shape[1]].set(conv_w[0])

    qkv_c, gb_c = _inproj_ctx_call(ctx, mod3, bsz, g1, w_ctx, gp, ctx_len)
    zero_state = jnp.zeros((bsz, N_HEADS, HEAD_DIM, HEAD_DIM), F32)
    _, _, s_f, s_b = _gdn_call(qkv_c, gb_c, cw, zero_state, zero_state, ctx_len)

    qkv, gb, z, ya = _inproj_lat_call(x, mod3, g1, w_lat, gp, gm_norm_g[0][None, :], gm_ws[0].astype(BF16),
                                      _pad_lanes(gm_bs[0].T), 512)
    o_f, o_b, _, _ = _gdn_call(qkv, gb, cw, s_f, s_b, 512)
    wr = _pad_lanes(w_router[0])
    wr_hi = wr.astype(BF16)
    wr_lo = (wr - wr_hi.astype(F32)).astype(BF16)
    br = jnp.full((1, LANES), -1e30, F32).at[0, :N_EXPERTS].set(b_router[0])
    h, fin, aff = _mixout_call(x, o_f, o_b, z, ya, mod3, gdn_norm_g[0][None, :], w_out[0].astype(BF16),
                               norm2_g[0][None, :], wr_hi, wr_lo, br, 512)

    slot, slott, base = _route_call(aff, cap)
    base_flat = base[:, :, :N_EXPERTS].reshape(-1)
    xe = _dispatch_call(base_flat, slott, fin, cap)
    y = _experts_call(xe, w_gate[0], w_up[0], w_down[0], cap)
    return _combine_call(base_flat, slot, aff, h, y, mod3, final_norm_g[None, :])
```

```python
import functools

import jax
import jax.numpy as jnp
from jax import lax
from jax.experimental import pallas as pl
from jax.experimental.pallas import tpu as pltpu

F32 = jnp.float32
BF16 = jnp.bfloat16
I32 = jnp.int32

D_MODEL = 1024
N_MOD = 6
N_HEADS = 4
HEAD_DIM = 128
B_W = N_HEADS * HEAD_DIM
QKV_W = 3 * B_W
A_W = 512
A_GROUPS = 4
A_CHUNK = 128
GDN_CHUNK = 64
N_EXPERTS = 16
EC_CAPACITY = 2
EXPERT_FF = 1024
NORM_EPS = 1e-6
LANES = 128
STATE_COLS = 4 * N_HEADS

ROUTE_CHUNK = 256
SLOT_WIN = 64
SLOT_ALIGN = 16
WIN_GROUP = 4
VMEM_LIMIT = 56 * 1024 * 1024


def _cparams(sem):
    return pltpu.CompilerParams(dimension_semantics=sem, vmem_limit_bytes=VMEM_LIMIT)


def _dot(a, b):
    return jnp.dot(a, b, preferred_element_type=F32)


def _dot_nt(a, b):
    return lax.dot_general(a, b, (((1,), (1,)), ((), ())), preferred_element_type=F32)


def _dot_tn(a, b):
    return lax.dot_general(a, b, (((0,), (0,)), ((), ())), preferred_element_type=F32)


def _silu(x):
    return x * jax.nn.sigmoid(x)


def _mod_body(c_ref, w_ref, b_ref, o_ref):
    s = _silu(c_ref[...])
    o_ref[...] = _dot(s.astype(BF16), w_ref[...].astype(BF16)) + b_ref[...]


def _mod_call(cs, w_mod, b_mod):
    n = w_mod.shape[1] // D_MODEL
    return pl.pallas_call(
        _mod_body,
        out_shape=jax.ShapeDtypeStruct((8, w_mod.shape[1]), F32),
        grid=(n,),
        in_specs=[pl.BlockSpec((8, D_MODEL), lambda j: (0, 0)),
                  pl.BlockSpec((D_MODEL, D_MODEL), lambda j: (0, j)),
                  pl.BlockSpec((1, D_MODEL), lambda j: (0, j))],
        out_specs=pl.BlockSpec((8, D_MODEL), lambda j: (0, j)),
        compiler_params=_cparams(("arbitrary",)),
        name="mod",
    )(cs, w_mod, b_mod)


def _norm_mod(x, g, shift, scale):
    ms = jnp.mean(x * x, axis=-1, keepdims=True)
    return (x * lax.rsqrt(ms + NORM_EPS) * g) * (1.0 + scale) + shift


def _gate_streams(st, gp_ref):
    lane = lax.broadcasted_iota(I32, st.shape, 1)
    g = -jnp.exp(gp_ref[0:1, :]) * jax.nn.softplus(st + gp_ref[1:2, :])
    beta = jax.nn.sigmoid(st)
    return jnp.where(lane < 2 * N_HEADS, g, jnp.where(lane < STATE_COLS, beta, 0.0))


def _inproj_lat_body(x_ref, mod_ref, g1_ref, w_ref, gp_ref, gmg_ref, ws_ref, bst_ref,
                     qkv_ref, gb_ref, z_ref, ya_ref, *, tm):
    a = _norm_mod(x_ref[0], g1_ref[...], mod_ref[0, 0:1, :], mod_ref[0, 1:2, :]).astype(BF16)
    qkv_ref[0] = _dot(a, w_ref[:, 0:QKV_W])
    z_ref[0] = _dot(a, w_ref[:, QKV_W:QKV_W + B_W])
    c_uv = QKV_W + B_W
    gb_ref[0] = _gate_streams(_dot(a, w_ref[:, c_uv + 2 * A_W:c_uv + 2 * A_W + LANES]), gp_ref)
    uv = _dot(a, w_ref[:, c_uv:c_uv + 2 * A_W])
    uv = 0.5 * uv * (1.0 + lax.erf(uv * 0.7071067811865476))
    gd = A_W // A_GROUPS
    for grp in range(A_GROUPS):
        v = uv[:, A_W + grp * gd:A_W + (grp + 1) * gd]
        vn = v * lax.rsqrt(jnp.mean(v * v, axis=-1, keepdims=True) + NORM_EPS) * gmg_ref[:, grp * gd:(grp + 1) * gd]
        vn = vn.astype(BF16)
        bias = bst_ref[:, grp:grp + 1]
        for c in range(tm // A_CHUNK):
            rows = slice(c * A_CHUNK, (c + 1) * A_CHUNK)
            s = _dot(ws_ref[grp], vn[rows]) + bias
            ya_ref[0, rows, grp * gd:(grp + 1) * gd] = (uv[rows, grp * gd:(grp + 1) * gd] * s).astype(BF16)


def _inproj_ctx_body(x_ref, mod_ref, g1_ref, w_ref, gp_ref, qkv_ref, gb_ref):
    a = _norm_mod(x_ref[0], g1_ref[...], mod_ref[0, 0:1, :], mod_ref[0, 1:2, :]).astype(BF16)
    qkv_ref[0] = _dot(a, w_ref[:, 0:QKV_W])
    gb_ref[0] = _gate_streams(_dot(a, w_ref[:, QKV_W:QKV_W + LANES]), gp_ref)


def _inproj_lat_call(x, mod3, g1, w_lat, gp, gmg, ws16, bst, tm):
    bsz, t, _ = x.shape
    full = lambda shape: pl.BlockSpec(shape, lambda b, i: (0,) * len(shape))
    tok = lambda w: pl.BlockSpec((1, tm, w), lambda b, i: (b, i, 0))
    return pl.pallas_call(
        functools.partial(_inproj_lat_body, tm=tm),
        out_shape=(jax.ShapeDtypeStruct((bsz, t, QKV_W), F32),
                   jax.ShapeDtypeStruct((bsz, t, LANES), F32),
                   jax.ShapeDtypeStruct((bsz, t, B_W), F32),
                   jax.ShapeDtypeStruct((bsz, t, A_W), BF16)),
        grid=(bsz, t // tm),
        in_specs=[tok(D_MODEL),
                  pl.BlockSpec((1, N_MOD, D_MODEL), lambda b, i: (b, 0, 0)),
                  full((1, D_MODEL)), full(w_lat.shape), full(gp.shape), full(gmg.shape),
                  full(ws16.shape), full(bst.shape)],
        out_specs=(tok(QKV_W), tok(LANES), tok(B_W), tok(A_W)),
        compiler_params=_cparams(("parallel", "parallel")),
        name="inproj_lat",
    )(x, mod3, g1, w_lat, gp, gmg, ws16, bst)


def _inproj_ctx_call(ctx, mod3, ctx_row, g1, w_ctx, gp, tm):
    bsz, t, _ = ctx.shape
    full = lambda shape: pl.BlockSpec(shape, lambda b, i: (0,) * len(shape))
    tok = lambda w: pl.BlockSpec((1, tm, w), lambda b, i: (b, i, 0))
    return pl.pallas_call(
        _inproj_ctx_body,
        out_shape=(jax.ShapeDtypeStruct((bsz, t, QKV_W), F32),
                   jax.ShapeDtypeStruct((bsz, t, LANES), F32)),
        grid=(bsz, t // tm),
        in_specs=[tok(D_MODEL),
                  pl.BlockSpec((1, N_MOD, D_MODEL), lambda b, i: (ctx_row, 0, 0)),
                  full((1, D_MODEL)), full(w_ctx.shape), full(gp.shape)],
        out_specs=(tok(QKV_W), tok(LANES)),
        compiler_params=_cparams(("parallel", "parallel")),
        name="inproj_ctx",
    )(ctx, mod3, g1, w_ctx, gp)


def _conv_norm(main_ref, prev_ref, next_ref, first, last, cw_ref, out_ref, tb):
    cs = GDN_CHUNK
    nsub = tb // cs
    w0, w1, w2 = cw_ref[0:1, :], cw_ref[1:2, :], cw_ref[2:3, :]
    row = lax.broadcasted_iota(I32, (cs, 1), 0)
    edge_prev = jnp.where(first, 0.0, prev_ref[0, 7:8, :])
    edge_next = jnp.where(last, 0.0, next_ref[0, 0:1, :])

    def body(c, carry):
        r0 = pl.multiple_of(c * cs, cs)
        x = main_ref[0, pl.ds(r0, cs), :]
        p8 = main_ref[0, pl.ds(pl.multiple_of(jnp.maximum(r0 - 8, 0), 8), 8), :]
        n8 = main_ref[0, pl.ds(pl.multiple_of(jnp.minimum(r0 + cs, tb - 8), 8), 8), :]
        prow = jnp.where(c == 0, edge_prev, p8[7:8, :])
        nrow = jnp.where(c == nsub - 1, edge_next, n8[0:1, :])
        xp = jnp.where(row == 0, prow, pltpu.roll(x, 1, 0))
        xn = jnp.where(row == cs - 1, nrow, pltpu.roll(x, cs - 1, 0))
        y = _silu(xp * w0 + x * w1 + xn * w2)
        for h in range(N_HEADS):
            cq = slice(h * HEAD_DIM, (h + 1) * HEAD_DIM)
            ck = slice(B_W + h * HEAD_DIM, B_W + (h + 1) * HEAD_DIM)
            q = y[:, cq]
            k = y[:, ck]
            out_ref[pl.ds(r0, cs), cq] = q * (lax.rsqrt(jnp.sum(q * q, axis=-1, keepdims=True) + NORM_EPS)
                                              * (HEAD_DIM ** -0.5))
            out_ref[pl.ds(r0, cs), ck] = k * lax.rsqrt(jnp.sum(k * k, axis=-1, keepdims=True) + NORM_EPS)
        out_ref[pl.ds(r0, cs), 2 * B_W:3 * B_W] = y[:, 2 * B_W:3 * B_W]
        return carry

    lax.fori_loop(0, nsub, body, 0)


def _gdn_chunk_all(chains):
    cs = GDN_CHUNK
    for c in chains:
        c["kb"] = c["k"] * c["beta"]
        c["k16"] = c["k"].astype(BF16)
        c["decay"] = jnp.where(c["incl"], jnp.exp(jnp.where(c["incl"], c["gc"] - c["gc_row"], 0.0)), 0.0)
    for c in chains:
        c["kk"] = _dot_nt(jnp.concatenate([c["kb"], c["q"]], axis=0).astype(BF16), c["k16"])
    for c in chains:
        c["a"] = jnp.where(c["strict"], c["kk"][:cs] * c["decay"], 0.0)
        c["attn"] = (c["kk"][cs:] * c["decay"]).astype(BF16)
        c["m"] = -jnp.where(c["levels"][0], c["a"], 0.0)
    for li in range(1, len(chains[0]["levels"])):
        for c in chains:
            c["m16"] = c["m"].astype(BF16)
            c["cm"] = jnp.where(c["levels"][li], c["a"], 0.0)
        for c in chains:
            c["x"] = c["cm"] + _dot(c["m16"], c["cm"].astype(BF16))
        for c in chains:
            c["y"] = c["x"] + _dot(c["x"].astype(BF16), c["m16"])
        for c in chains:
            c["m"] = c["m"] - c["y"]
    for c in chains:
        c["rhs"] = jnp.concatenate([c["v"] * c["beta"], c["kb"] * c["egc"]], axis=1)
    for c in chains:
        c["uw"] = c["rhs"] + _dot(c["m"].astype(BF16), c["rhs"].astype(BF16))
    for c in chains:
        c["s"] = c["s_ref"][c["h"]]
        lhs = jnp.concatenate([c["uw"][:, HEAD_DIM:], c["q"] * c["egc"]], axis=0).astype(BF16)
        c["wq"] = _dot(lhs, c["s"].astype(BF16))
    for c in chains:
        c["v_new"] = (c["uw"][:, :HEAD_DIM] - c["wq"][:cs]).astype(BF16)
        c["kg"] = (c["k"] * jnp.exp(c["glast"] - c["gc"])).astype(BF16)
    for c in chains:
        c["o"] = c["wq"][cs:] + _dot(c["attn"], c["v_new"])
    for c in chains:
        c["s_ref"][c["h"]] = c["s"] * jnp.exp(c["glast"]) + _dot_tn(c["kg"], c["v_new"])


def _gdn_body(qf_ref, qfp_ref, qfn_ref, qb_ref, qbp_ref, qbn_ref, gbf_ref, gbb_ref, cw_ref, s0f_ref, s0b_ref,
              of_ref, ob_ref, sff_ref, sfb_ref, sf_scr, sb_scr, nf_scr, nb_scr, *, tb, nt):
    t = pl.program_id(1)
    cs = GDN_CHUNK
    nch = tb // cs

    @pl.when(t == 0)
    def _():
        sf_scr[...] = s0f_ref[0]
        sb_scr[...] = s0b_ref[0]

    _conv_norm(qf_ref, qfp_ref, qfn_ref, t == 0, t == nt - 1, cw_ref, nf_scr, tb)
    _conv_norm(qb_ref, qbp_ref, qbn_ref, t == nt - 1, t == 0, cw_ref, nb_scr, tb)

    ii = lax.broadcasted_iota(I32, (cs, cs), 0)
    jj = lax.broadcasted_iota(I32, (cs, cs), 1)
    incl_f, strict_f = jj <= ii, jj < ii
    incl_b, strict_b = jj >= ii, jj > ii
    levels = []
    sh = 0
    while (1 << sh) < cs:
        levels.append(((ii >> (sh + 1)) == (jj >> (sh + 1))) & ((ii >> sh) != (jj >> sh)))
        sh += 1
    lv_f = [lm & strict_f for lm in levels]
    lv_b = [lm & strict_b for lm in levels]
    row = lax.broadcasted_iota(I32, (cs, LANES), 0)

    def step(n, carry):
        rf = pl.multiple_of(n * cs, cs)
        rb = pl.multiple_of((nch - 1 - n) * cs, cs)
        gf = gbf_ref[0, pl.ds(rf, cs), :]
        gb = gbb_ref[0, pl.ds(rb, cs), :]
        cf, cb = gf, gb
        s = 1
        while s < cs:
            cf = cf + jnp.where(row >= s, pltpu.roll(cf, s, 0), 0.0)
            cb = cb + jnp.where(row < cs - s, pltpu.roll(cb, cs - s, 0), 0.0)
            s *= 2
        gt = jnp.concatenate([cf, cb], axis=0).T
        ecf, ecb = jnp.exp(cf), jnp.exp(cb)
        chains = []
        for h in range(N_HEADS):
            cq = slice(h * HEAD_DIM, (h + 1) * HEAD_DIM)
            ck = slice(B_W + h * HEAD_DIM, B_W + (h + 1) * HEAD_DIM)
            cv = slice(2 * B_W + h * HEAD_DIM, 2 * B_W + (h + 1) * HEAD_DIM)
            lf, lb = h, N_HEADS + h
            chains.append(dict(
                q=nf_scr[pl.ds(rf, cs), cq], k=nf_scr[pl.ds(rf, cs), ck], v=nf_scr[pl.ds(rf, cs), cv],
                gc=cf[:, lf:lf + 1], gc_row=gt[lf:lf + 1, 0:cs], beta=gf[:, 2 * N_HEADS + h:2 * N_HEADS + h + 1],
                egc=ecf[:, lf:lf + 1], glast=cf[cs - 1:cs, lf:lf + 1], incl=incl_f, strict=strict_f, levels=lv_f,
                s_ref=sf_scr, h=h, o_ref=of_ref, r0=rf, cols=cq))
            chains.append(dict(
                q=nb_scr[pl.ds(rb, cs), cq], k=nb_scr[pl.ds(rb, cs), ck], v=nb_scr[pl.ds(rb, cs), cv],
                gc=cb[:, lb:lb + 1], gc_row=gt[lb:lb + 1, cs:2 * cs], beta=gb[:, 3 * N_HEADS + h:3 * N_HEADS + h + 1],
                egc=ecb[:, lb:lb + 1], glast=cb[0:1, lb:lb + 1], incl=incl_b, strict=strict_b, levels=lv_b,
                s_ref=sb_scr, h=h, o_ref=ob_ref, r0=rb, cols=cq))
        _gdn_chunk_all(chains)
        for c in chains:
            c["o_ref"][0, pl.ds(c["r0"], cs), c["cols"]] = c["o"]
        return carry

    lax.fori_loop(0, nch, step, 0)

    @pl.when(t == nt - 1)
    def _():
        sff_ref[0] = sf_scr[...]
        sfb_ref[0] = sb_scr[...]


def _gdn_call(qkv, gb, cw, s0f, s0b, tb):
    bsz, t, _ = qkv.shape
    nt = t // tb
    hb = tb // 8
    last8 = t // 8 - 1
    main = lambda f: pl.BlockSpec((1, tb, QKV_W), lambda b, i: (b, f(i), 0))
    halo_p = lambda f: pl.BlockSpec((1, 8, QKV_W), lambda b, i: (b, jnp.maximum(f(i) * hb - 1, 0), 0))
    halo_n = lambda f: pl.BlockSpec((1, 8, QKV_W), lambda b, i: (b, jnp.minimum((f(i) + 1) * hb, last8), 0))
    fwd = lambda i: i
    bwd = lambda i: nt - 1 - i
    st = pl.BlockSpec((1, N_HEADS, HEAD_DIM, HEAD_DIM), lambda b, i: (b, 0, 0, 0))
    return pl.pallas_call(
        functools.partial(_gdn_body, tb=tb, nt=nt),
        out_shape=(jax.ShapeDtypeStruct((bsz, t, B_W), F32), jax.ShapeDtypeStruct((bsz, t, B_W), F32),
                   jax.ShapeDtypeStruct((bsz, N_HEADS, HEAD_DIM, HEAD_DIM), F32),
                   jax.ShapeDtypeStruct((bsz, N_HEADS, HEAD_DIM, HEAD_DIM), F32)),
        grid=(bsz, nt),
        in_specs=[main(fwd), halo_p(fwd), halo_n(fwd), main(bwd), halo_p(bwd), halo_n(bwd),
                  pl.BlockSpec((1, tb, LANES), lambda b, i: (b, i, 0)),
                  pl.BlockSpec((1, tb, LANES), lambda b, i: (b, nt - 1 - i, 0)),
                  pl.BlockSpec(cw.shape, lambda b, i: (0, 0)), st, st],
        out_specs=(pl.BlockSpec((1, tb, B_W), lambda b, i: (b, i, 0)),
                   pl.BlockSpec((1, tb, B_W), lambda b, i: (b, nt - 1 - i, 0)), st, st),
        scratch_shapes=[pltpu.VMEM((N_HEADS, HEAD_DIM, HEAD_DIM), F32), pltpu.VMEM((N_HEADS, HEAD_DIM, HEAD_DIM), F32),
                        pltpu.VMEM((tb, QKV_W), F32), pltpu.VMEM((tb, QKV_W), F32)],
        compiler_params=_cparams(("parallel", "arbitrary")),
        name="gdn",
    )(qkv, qkv, qkv, qkv, qkv, qkv, gb, gb, cw, s0f, s0b)


def _mixout_body(x_ref, of_ref, ob_ref, z_ref, ya_ref, mod_ref, gng_ref, wout_ref, n2g_ref, wrh_ref, wrl_ref, br_ref,
                 h_ref, fin_ref, aff_ref):
    o = of_ref[0] + ob_ref[0]
    z = z_ref[0]
    parts = [ya_ref[0]]
    for h in range(N_HEADS):
        c = slice(h * HEAD_DIM, (h + 1) * HEAD_DIM)
        oh = o[:, c]
        y = oh * lax.rsqrt(jnp.mean(oh * oh, axis=-1, keepdims=True) + NORM_EPS)
        parts.append((y * gng_ref[...] * _silu(z[:, c])).astype(BF16))
    mix = _dot(jnp.concatenate(parts, axis=1), wout_ref[...])
    hl = x_ref[0] + mod_ref[0, 2:3, :] * mix
    h_ref[0] = hl
    fin = _norm_mod(hl, n2g_ref[...], mod_ref[0, 3:4, :], mod_ref[0, 4:5, :])
    f_hi = fin.astype(BF16)
    fin_ref[0] = f_hi
    f_lo = (fin - f_hi.astype(F32)).astype(BF16)
    logits = _dot(f_hi, wrh_ref[...]) + _dot(f_lo, wrh_ref[...]) + _dot(f_hi, wrl_ref[...]) + br_ref[...]
    e = jnp.exp(logits - jnp.max(logits, axis=-1, keepdims=True))
    aff_ref[0] = e / jnp.sum(e, axis=-1, keepdims=True)


def _mixout_call(x, o_f, o_b, z, ya, mod3, gng, wout16, n2g, wr_hi, wr_lo, br, tm):
    bsz, t, _ = x.shape
    full = lambda a: pl.BlockSpec(a.shape, lambda b, i: (0,) * a.ndim)
    tok = lambda w: pl.BlockSpec((1, tm, w), lambda b, i: (b, i, 0))
    return pl.pallas_call(
        _mixout_body,
        out_shape=(jax.ShapeDtypeStruct((bsz, t, D_MODEL), F32), jax.ShapeDtypeStruct((bsz, t, D_MODEL), BF16),
                   jax.ShapeDtypeStruct((bsz, t, LANES), F32)),
        grid=(bsz, t // tm),
        in_specs=[tok(D_MODEL), tok(B_W), tok(B_W), tok(B_W), tok(A_W),
                  pl.BlockSpec((1, N_MOD, D_MODEL), lambda b, i: (b, 0, 0)),
                  full(gng), full(wout16), full(n2g), full(wr_hi), full(wr_lo), full(br)],
        out_specs=(tok(D_MODEL), tok(D_MODEL), tok(LANES)),
        compiler_params=_cparams(("parallel", "parallel")),
        name="mixout",
    )(x, o_f, o_b, z, ya, mod3, gng, wout16, n2g, wr_hi, wr_lo, br)


def _route_body(aff_ref, slot_ref, slott_ref, base_ref, *, t, cap):
    rc = ROUTE_CHUNK
    nchunk = t // rc

    def search(i, thr):
        cand = thr | jnp.left_shift(jnp.int32(1), 30 - i)
        cnt = jnp.sum((aff_ref[0] >= pltpu.bitcast(cand, F32)).astype(I32), axis=0, keepdims=True)
        return jnp.where(cnt >= cap, cand, thr)

    thr_bits = lax.fori_loop(0, 31, search, jnp.zeros((1, LANES), I32))
    thr = pltpu.bitcast(thr_bits, F32)
    n_gt = jnp.sum((aff_ref[0] > thr).astype(I32), axis=0, keepdims=True)
    need = (cap - n_gt).astype(F32)

    r = lax.broadcasted_iota(I32, (rc, rc), 0)
    c = lax.broadcasted_iota(I32, (rc, rc), 1)
    tril = (c <= r).astype(BF16)

    def chunk(ci, carry):
        ceq, csel = carry
        r0 = pl.multiple_of(ci * rc, rc)
        xc = aff_ref[0, pl.ds(r0, rc), :]
        gt = xc > thr
        eq = xc == thr
        eqp = _dot(tril, eq.astype(BF16)) + ceq
        sel = gt | (eq & (eqp <= need))
        selp = _dot(tril, sel.astype(BF16)) + csel
        slot = jnp.where(sel, selp - 1.0, -1.0)
        slot_ref[0, pl.ds(r0, rc), :] = slot.astype(I32)
        base_ref[0, pl.ds(ci, 1), :] = csel.astype(I32)
        for hh in range(rc // LANES):
            st = slot[hh * LANES:(hh + 1) * LANES, :].T
            slott_ref[0, ci * (rc // LANES) + hh] = st[0:N_EXPERTS, :].astype(I32)
        return eqp[rc - 1:rc, :], selp[rc - 1:rc, :]

    _, csel = lax.fori_loop(0, nchunk, chunk, (jnp.zeros((1, LANES), F32), jnp.zeros((1, LANES), F32)))
    base_ref[0, nchunk:nchunk + 1, :] = csel.astype(I32)


def _route_call(aff, cap):
    bsz, t, _ = aff.shape
    nchunk = t // ROUTE_CHUNK
    return pl.pallas_call(
        functools.partial(_route_body, t=t, cap=cap),
        out_shape=(jax.ShapeDtypeStruct((bsz, t, LANES), I32),
                   jax.ShapeDtypeStruct((bsz, t // LANES, N_EXPERTS, LANES), I32),
                   jax.ShapeDtypeStruct((bsz, nchunk + 1, LANES), I32)),
        grid=(bsz,),
        in_specs=[pl.BlockSpec((1, t, LANES), lambda b: (b, 0, 0))],
        out_specs=(pl.BlockSpec((1, t, LANES), lambda b: (b, 0, 0)),
                   pl.BlockSpec((1, t // LANES, N_EXPERTS, LANES), lambda b: (b, 0, 0, 0)),
                   pl.BlockSpec((1, nchunk + 1, LANES), lambda b: (b, 0, 0))),
        compiler_params=_cparams(("parallel",)),
        name="route",
    )(aff)


def _window_plan(base_ref, flat0, experts):
    starts, rounds = [], jnp.int32(0)
    for e in experts:
        lo = base_ref[flat0 + e]
        hi = base_ref[flat0 + N_EXPERTS + e]
        lo_al = (lo >> 4) << 4
        starts.append(lo_al)
        rounds = jnp.maximum(rounds, (hi - lo_al + SLOT_WIN - 1) // SLOT_WIN)
    return starts, rounds


def _dispatch_body(base_ref, slott_ref, fin_ref, xe_ref, *, nchunk, sub, eh_n):
    b, eh, ci = pl.program_id(0), pl.program_id(1), pl.program_id(2)
    rc = ROUTE_CHUNK

    @pl.when(ci == 0)
    def _():
        xe_ref[...] = jnp.zeros_like(xe_ref)

    srow = lax.broadcasted_iota(I32, (SLOT_WIN, rc), 0)
    for sc in range(sub):
        cc = ci * sub + sc
        flat0 = (b * (nchunk + 1) + cc) * N_EXPERTS + eh * eh_n
        f = fin_ref[0, sc * rc:(sc + 1) * rc, :]
        for g in range(eh_n // WIN_GROUP):
            experts = [g * WIN_GROUP + el for el in range(WIN_GROUP)]
            starts, rounds = _window_plan(base_ref, flat0, experts)

            def one_round(r, carry, experts=experts, starts=starts, f=f, sc=sc):
                rows = []
                for el, e in enumerate(experts):
                    tok_slot = jnp.concatenate(
                        [slott_ref[0, sc * (rc // LANES) + j, e:e + 1, :] for j in range(rc // LANES)], axis=1)
                    rows.append((tok_slot == srow + (starts[el] + r * SLOT_WIN)).astype(BF16))
                prod = _dot(jnp.concatenate(rows, axis=0), f)
                for el, e in enumerate(experts):
                    win = pl.ds(pl.multiple_of(starts[el] + r * SLOT_WIN, SLOT_ALIGN), SLOT_WIN)
                    xe_ref[0, e, win, :] = xe_ref[0, e, win, :] + prod[el * SLOT_WIN:(el + 1) * SLOT_WIN].astype(BF16)
                return carry

            lax.fori_loop(0, rounds, one_round, 0)


def _dispatch_call(base_flat, slott, fin, cap):
    bsz, t, _ = fin.shape
    nchunk = t // ROUTE_CHUNK
    sub = 2
    eh_n = N_EXPERTS // 2
    sp = cap + SLOT_WIN
    grid_spec = pltpu.PrefetchScalarGridSpec(
        num_scalar_prefetch=1,
        grid=(bsz, N_EXPERTS // eh_n, nchunk // sub),
        in_specs=[pl.BlockSpec((1, sub * ROUTE_CHUNK // LANES, eh_n, LANES), lambda b, eh, ci, base: (b, ci, eh, 0)),
                  pl.BlockSpec((1, sub * ROUTE_CHUNK, D_MODEL), lambda b, eh, ci, base: (b, ci, 0))],
        out_specs=pl.BlockSpec((1, eh_n, sp, D_MODEL), lambda b, eh, ci, base: (b, eh, 0, 0)))
    return pl.pallas_call(
        functools.partial(_dispatch_body, nchunk=nchunk, sub=sub, eh_n=eh_n),
        out_shape=jax.ShapeDtypeStruct((bsz, N_EXPERTS, sp, D_MODEL), BF16),
        grid_spec=grid_spec,
        compiler_params=_cparams(("parallel", "parallel", "arbitrary")),
        name="dispatch",
    )(base_flat, slott, fin)


def _experts_body(xe_ref, wg_ref, wu_ref, wd_ref, y_ref, wg16, wu16, wd16, *, cap):
    @pl.when(pl.program_id(1) == 0)
    def _():
        wg16[...] = wg_ref[0].astype(BF16)
        wu16[...] = wu_ref[0].astype(BF16)
        wd16[...] = wd_ref[0].astype(BF16)

    x = xe_ref[0, 0, 0:cap, :]
    ft = 256
    acc = None
    for f in range(EXPERT_FF // ft):
        cols = slice(f * ft, (f + 1) * ft)
        hid = (_silu(_dot(x, wg16[:, cols])) * _dot(x, wu16[:, cols])).astype(BF16)
        part = _dot(hid, wd16[cols, :])
        acc = part if acc is None else acc + part
    y_ref[0, 0, 0:cap, :] = acc.astype(BF16)
    y_ref[0, 0, cap:, :] = jnp.zeros((y_ref.shape[2] - cap, D_MODEL), BF16)


def _experts_call(xe, w_gate, w_up, w_down, cap):
    bsz, _, sp, _ = xe.shape
    wspec = lambda shape: pl.BlockSpec((1,) + shape, lambda e, b: (e, 0, 0))
    slots = pl.BlockSpec((1, 1, sp, D_MODEL), lambda e, b: (b, e, 0, 0))
    return pl.pallas_call(
        functools.partial(_experts_body, cap=cap),
        out_shape=jax.ShapeDtypeStruct(xe.shape, BF16),
        grid=(N_EXPERTS, bsz),
        in_specs=[slots, wspec((D_MODEL, EXPERT_FF)), wspec((D_MODEL, EXPERT_FF)), wspec((EXPERT_FF, D_MODEL))],
        out_specs=slots,
        scratch_shapes=[pltpu.VMEM((D_MODEL, EXPERT_FF), BF16), pltpu.VMEM((D_MODEL, EXPERT_FF), BF16),
                        pltpu.VMEM((EXPERT_FF, D_MODEL), BF16)],
        compiler_params=_cparams(("arbitrary", "arbitrary")),
        name="experts",
    )(xe, w_gate, w_up, w_down)


def _combine_body(base_ref, slot_ref, aff_ref, h_ref, y_ref, mod_ref, fng_ref, o_ref, acc_ref, *, nchunk):
    b, ci = pl.program_id(0), pl.program_id(1)
    rc = ROUTE_CHUNK
    acc_ref[...] = jnp.zeros_like(acc_ref)
    flat0 = (b * (nchunk + 1) + ci) * N_EXPERTS
    width = WIN_GROUP * SLOT_WIN
    lane = lax.broadcasted_iota(I32, (rc, width), 1)
    lane_el = lane >> 6
    lane_j = lane & (SLOT_WIN - 1)
    slot = slot_ref[0]
    aff = aff_ref[0]
    for g in range(N_EXPERTS // WIN_GROUP):
        experts = [g * WIN_GROUP + el for el in range(WIN_GROUP)]
        starts, rounds = _window_plan(base_ref, flat0, experts)

        def one_round(r, carry, experts=experts, starts=starts):
            ywin = jnp.concatenate(
                [y_ref[0, e, pl.ds(pl.multiple_of(starts[el] + r * SLOT_WIN, SLOT_ALIGN), SLOT_WIN), :]
                 for el, e in enumerate(experts)], axis=0)
            s = jnp.zeros((rc, width), F32)
            for el, e in enumerate(experts):
                rel = slot[:, e:e + 1] - (starts[el] + r * SLOT_WIN)
                s = jnp.where((lane_el == el) & (rel == lane_j), aff[:, e:e + 1], s)
            acc_ref[...] += _dot(s.astype(BF16), ywin)
            return carry

        lax.fori_loop(0, rounds, one_round, 0)
    hl = h_ref[0] + mod_ref[0, 5:6, :] * acc_ref[...]
    ms = jnp.mean(hl * hl, axis=-1, keepdims=True)
    o_ref[0] = hl * lax.rsqrt(ms + NORM_EPS) * fng_ref[...]


def _combine_call(base_flat, slot, aff, h, y, mod3, fng):
    bsz, t, _ = h.shape
    nchunk = t // ROUTE_CHUNK
    rc = ROUTE_CHUNK
    tok = lambda w: pl.BlockSpec((1, rc, w), lambda b, i, base: (b, i, 0))
    grid_spec = pltpu.PrefetchScalarGridSpec(
        num_scalar_prefetch=1,
        grid=(bsz, nchunk),
        in_specs=[tok(LANES), tok(LANES), tok(D_MODEL),
                  pl.BlockSpec((1,) + y.shape[1:], lambda b, i, base: (b, 0, 0, 0), pipeline_mode=pl.Buffered(1)),
                  pl.BlockSpec((1, N_MOD, D_MODEL), lambda b, i, base: (b, 0, 0)),
                  pl.BlockSpec((1, D_MODEL), lambda b, i, base: (0, 0))],
        out_specs=tok(D_MODEL),
        scratch_shapes=[pltpu.VMEM((rc, D_MODEL), F32)])
    return pl.pallas_call(
        functools.partial(_combine_body, nchunk=nchunk),
        out_shape=jax.ShapeDtypeStruct(h.shape, F32),
        grid_spec=grid_spec,
        compiler_params=_cparams(("parallel", "arbitrary")),
        name="combine",
    )(base_flat, slot, aff, h, y, mod3, fng)


def _pad_lanes(a):
    return jnp.pad(a, ((0, 0), (0, LANES - a.shape[1])))


def kernel(x, c, ctx, c_ctx, w_mod, b_mod, norm1_g, norm2_g, w_in, conv_w, a_log, dt_bias, gdn_norm_g, gm_norm_g,
           gm_ws, gm_bs, w_out, w_router, b_router, w_gate, w_up, w_down, final_norm_g):
    bsz, t, _ = x.shape
    ctx_len = ctx.shape[1]
    assert w_mod.shape[0] == 1, "single-layer problem"
    assert t % 512 == 0 and ctx_len % GDN_CHUNK == 0 and bsz < 8
    cap = EC_CAPACITY * t // N_EXPERTS

    cs = jnp.zeros((8, D_MODEL), F32).at[:bsz].set(c).at[bsz].set(c_ctx)
    mod3 = _mod_call(cs, w_mod[0], b_mod[0][None, :]).reshape(8, N_MOD, D_MODEL)

    wl = w_in[0]
    n_state = QKV_W + STATE_COLS
    w_state = _pad_lanes(wl[:, QKV_W:n_state])
    w_lat = jnp.concatenate([wl[:, :QKV_W], wl[:, n_state:n_state + B_W], wl[:, n_state + B_W:], w_state],
                            axis=1).astype(BF16)
    w_ctx = jnp.concatenate([wl[:, :QKV_W], w_state], axis=1).astype(BF16)
    gp = jnp.zeros((8, LANES), F32).at[0, :2 * N_HEADS].set(a_log[0].reshape(-1)).at[1, :2 * N_HEADS].set(
        dt_bias[0].reshape(-1))
    g1 = norm1_g[0][None, :]
    cw = jnp.zeros((8, QKV_W), F32).at[:conv_w.shape[1]].set(conv_w[0])

    qkv_c, gb_c = _inproj_ctx_call(ctx, mod3, bsz, g1, w_ctx, gp, ctx_len)
    zero_state = jnp.zeros((bsz, N_HEADS, HEAD_DIM, HEAD_DIM), F32)
    _, _, s_f, s_b = _gdn_call(qkv_c, gb_c, cw, zero_state, zero_state, ctx_len)

    qkv, gb, z, ya = _inproj_lat_call(x, mod3, g1, w_lat, gp, gm_norm_g[0][None, :], gm_ws[0].astype(BF16),
                                      _pad_lanes(gm_bs[0].T), 512)
    o_f, o_b, _, _ = _gdn_call(qkv, gb, cw, s_f, s_b, 512)
    wr = _pad_lanes(w_router[0])
    wr_hi = wr.astype(BF16)
    wr_lo = (wr - wr_hi.astype(F32)).astype(BF16)
    br = jnp.full((1, LANES), -1e30, F32).at[0, :N_EXPERTS].set(b_router[0])
    h, fin, aff = _mixout_call(x, o_f, o_b, z, ya, mod3, gdn_norm_g[0][None, :], w_out[0].astype(BF16),
                               norm2_g[0][None, :], wr_hi, wr_lo, br, 512)

    slot, slott, base = _route_call(aff, cap)
    base_flat = base[:, :, :N_EXPERTS].reshape(-1)
    xe = _dispatch_call(base_flat, slott, fin, cap)
    y = _experts_call(xe, w_gate[0], w_up[0], w_down[0], cap)
    return _combine_call(base_flat, slot, aff, h, y, mod3, final_norm_g[None, :])
```

```python
import functools

import jax
import jax.numpy as jnp
from jax import lax
from jax.experimental import pallas as pl
from jax.experimental.pallas import tpu as pltpu

F32 = jnp.float32
BF16 = jnp.bfloat16
I32 = jnp.int32

D_MODEL = 1024
N_MOD = 6
N_HEADS = 4
HEAD_DIM = 128
B_W = N_HEADS * HEAD_DIM
QKV_W = 3 * B_W
A_W = 512
A_GROUPS = 4
A_CHUNK = 128
GDN_CHUNK = 64
N_EXPERTS = 16
EC_CAPACITY = 2
EXPERT_FF = 1024
NORM_EPS = 1e-6
LANES = 128
STATE_COLS = 4 * N_HEADS

ROUTE_CHUNK = 256
SLOT_WIN = 64
SLOT_ALIGN = 16
WIN_GROUP = 4
VMEM_LIMIT = 56 * 1024 * 1024


def _cparams(sem):
    return pltpu.CompilerParams(dimension_semantics=sem, vmem_limit_bytes=VMEM_LIMIT)


def _dot(a, b):
    return jnp.dot(a, b, preferred_element_type=F32)


def _dot_nt(a, b):
    return lax.dot_general(a, b, (((1,), (1,)), ((), ())), preferred_element_type=F32)


def _dot_tn(a, b):
    return lax.dot_general(a, b, (((0,), (0,)), ((), ())), preferred_element_type=F32)


def _silu(x):
    return x * jax.nn.sigmoid(x)


def _mod_body(c_ref, w_ref, b_ref, o_ref):
    s = _silu(c_ref[...])
    o_ref[...] = _dot(s.astype(BF16), w_ref[...].astype(BF16)) + b_ref[...]


def _mod_call(cs, w_mod, b_mod):
    n = w_mod.shape[1] // D_MODEL
    return pl.pallas_call(
        _mod_body,
        out_shape=jax.ShapeDtypeStruct((8, w_mod.shape[1]), F32),
        grid=(n,),
        in_specs=[pl.BlockSpec((8, D_MODEL), lambda j: (0, 0)),
                  pl.BlockSpec((D_MODEL, D_MODEL), lambda j: (0, j)),
                  pl.BlockSpec((1, D_MODEL), lambda j: (0, j))],
        out_specs=pl.BlockSpec((8, D_MODEL), lambda j: (0, j)),
        compiler_params=_cparams(("arbitrary",)),
        name="mod",
    )(cs, w_mod, b_mod)


def _norm_mod(x, g, shift, scale):
    ms = jnp.mean(x * x, axis=-1, keepdims=True)
    return (x * lax.rsqrt(ms + NORM_EPS) * g) * (1.0 + scale) + shift


def _gate_streams(st, gp_ref):
    lane = lax.broadcasted_iota(I32, st.shape, 1)
    g = -jnp.exp(gp_ref[0:1, :]) * jax.nn.softplus(st + gp_ref[1:2, :])
    beta = jax.nn.sigmoid(st)
    return jnp.where(lane < 2 * N_HEADS, g, jnp.where(lane < STATE_COLS, beta, 0.0))


def _conv_qkv(qkv, prev_row, next_row, cw_ref, out_ref, tm):
    cs = GDN_CHUNK
    nsub = tm // cs
    w0, w1, w2 = cw_ref[0:1, :], cw_ref[1:2, :], cw_ref[2:3, :]
    row = lax.broadcasted_iota(I32, (cs, 1), 0)
    for c in range(nsub):
        rows = slice(c * cs, (c + 1) * cs)
        x = qkv[rows]
        prow = prev_row if c == 0 else qkv[c * cs - 1:c * cs]
        nrow = next_row if c == nsub - 1 else qkv[(c + 1) * cs:(c + 1) * cs + 1]
        xp = jnp.where(row == 0, prow, pltpu.roll(x, 1, 0))
        xn = jnp.where(row == cs - 1, nrow, pltpu.roll(x, cs - 1, 0))
        y = _silu(xp * w0 + x * w1 + xn * w2)
        for h in range(N_HEADS):
            cq = slice(h * HEAD_DIM, (h + 1) * HEAD_DIM)
            ck = slice(B_W + h * HEAD_DIM, B_W + (h + 1) * HEAD_DIM)
            q = y[:, cq]
            k = y[:, ck]
            out_ref[0, rows, cq] = (q * (lax.rsqrt(jnp.sum(q * q, axis=-1, keepdims=True) + NORM_EPS)
                                         * (HEAD_DIM ** -0.5))).astype(BF16)
            out_ref[0, rows, ck] = (k * lax.rsqrt(jnp.sum(k * k, axis=-1, keepdims=True) + NORM_EPS)).astype(BF16)
        out_ref[0, rows, 2 * B_W:3 * B_W] = y[:, 2 * B_W:3 * B_W].astype(BF16)


def _inproj_lat_body(x_ref, xp_ref, xn_ref, mod_ref, g1_ref, w_ref, gp_ref, cw_ref, gmg_ref, ws_ref, bst_ref,
                     qkv_ref, gb_ref, z_ref, ya_ref, *, tm):
    i = pl.program_id(1)
    shift, scale = mod_ref[0, 0:1, :], mod_ref[0, 1:2, :]
    a = _norm_mod(x_ref[0], g1_ref[...], shift, scale).astype(BF16)
    xh = jnp.concatenate([xp_ref[0], xn_ref[0]], axis=0)
    halo = _dot(_norm_mod(xh, g1_ref[...], shift, scale).astype(BF16), w_ref[:, 0:QKV_W])
    prev_row = jnp.where(i == 0, 0.0, halo[7:8, :])
    next_row = jnp.where(i == pl.num_programs(1) - 1, 0.0, halo[8:9, :])
    _conv_qkv(_dot(a, w_ref[:, 0:QKV_W]), prev_row, next_row, cw_ref, qkv_ref, tm)
    z_ref[0] = _dot(a, w_ref[:, QKV_W:QKV_W + B_W])
    c_uv = QKV_W + B_W
    gb_ref[0] = _gate_streams(_dot(a, w_ref[:, c_uv + 2 * A_W:c_uv + 2 * A_W + LANES]), gp_ref)
    uv = _dot(a, w_ref[:, c_uv:c_uv + 2 * A_W])
    uv = 0.5 * uv * (1.0 + lax.erf(uv * 0.7071067811865476))
    gd = A_W // A_GROUPS
    for grp in range(A_GROUPS):
        v = uv[:, A_W + grp * gd:A_W + (grp + 1) * gd]
        vn = v * lax.rsqrt(jnp.mean(v * v, axis=-1, keepdims=True) + NORM_EPS) * gmg_ref[:, grp * gd:(grp + 1) * gd]
        vn = vn.astype(BF16)
        bias = bst_ref[:, grp:grp + 1]
        for c in range(tm // A_CHUNK):
            rows = slice(c * A_CHUNK, (c + 1) * A_CHUNK)
            s = _dot(ws_ref[grp], vn[rows]) + bias
            ya_ref[0, rows, grp * gd:(grp + 1) * gd] = (uv[rows, grp * gd:(grp + 1) * gd] * s).astype(BF16)


def _inproj_ctx_body(x_ref, mod_ref, g1_ref, w_ref, gp_ref, cw_ref, qkv_ref, gb_ref, *, tm):
    a = _norm_mod(x_ref[0], g1_ref[...], mod_ref[0, 0:1, :], mod_ref[0, 1:2, :]).astype(BF16)
    edge = jnp.zeros((1, QKV_W), F32)
    _conv_qkv(_dot(a, w_ref[:, 0:QKV_W]), edge, edge, cw_ref, qkv_ref, tm)
    gb_ref[0] = _gate_streams(_dot(a, w_ref[:, QKV_W:QKV_W + LANES]), gp_ref)


def _inproj_lat_call(x, mod3, g1, w_lat, gp, cw, gmg, ws16, bst, tm):
    bsz, t, _ = x.shape
    hb = tm // 8
    last8 = t // 8 - 1
    full = lambda a: pl.BlockSpec(a.shape, lambda b, i: (0,) * a.ndim)
    tok = lambda w: pl.BlockSpec((1, tm, w), lambda b, i: (b, i, 0))
    return pl.pallas_call(
        functools.partial(_inproj_lat_body, tm=tm),
        out_shape=(jax.ShapeDtypeStruct((bsz, t, QKV_W), BF16),
                   jax.ShapeDtypeStruct((bsz, t, LANES), F32),
                   jax.ShapeDtypeStruct((bsz, t, B_W), F32),
                   jax.ShapeDtypeStruct((bsz, t, A_W), BF16)),
        grid=(bsz, t // tm),
        in_specs=[tok(D_MODEL),
                  pl.BlockSpec((1, 8, D_MODEL), lambda b, i: (b, jnp.maximum(i * hb - 1, 0), 0)),
                  pl.BlockSpec((1, 8, D_MODEL), lambda b, i: (b, jnp.minimum((i + 1) * hb, last8), 0)),
                  pl.BlockSpec((1, N_MOD, D_MODEL), lambda b, i: (b, 0, 0)),
                  full(g1), full(w_lat), full(gp), full(cw), full(gmg), full(ws16), full(bst)],
        out_specs=(tok(QKV_W), tok(LANES), tok(B_W), tok(A_W)),
        compiler_params=_cparams(("parallel", "arbitrary")),
        name="inproj_lat",
    )(x, x, x, mod3, g1, w_lat, gp, cw, gmg, ws16, bst)


def _inproj_ctx_call(ctx, mod3, ctx_row, g1, w_ctx, gp, cw):
    bsz, t, _ = ctx.shape
    full = lambda a: pl.BlockSpec(a.shape, lambda b: (0,) * a.ndim)
    tok = lambda w: pl.BlockSpec((1, t, w), lambda b: (b, 0, 0))
    return pl.pallas_call(
        functools.partial(_inproj_ctx_body, tm=t),
        out_shape=(jax.ShapeDtypeStruct((bsz, t, QKV_W), BF16),
                   jax.ShapeDtypeStruct((bsz, t, LANES), F32)),
        grid=(bsz,),
        in_specs=[tok(D_MODEL),
                  pl.BlockSpec((1, N_MOD, D_MODEL), lambda b: (ctx_row, 0, 0)),
                  full(g1), full(w_ctx), full(gp), full(cw)],
        out_specs=(tok(QKV_W), tok(LANES)),
        compiler_params=_cparams(("parallel",)),
        name="inproj_ctx",
    )(ctx, mod3, g1, w_ctx, gp, cw)


GDN_CHUNKS_PER_STEP = 2


def _gdn_local(chains):
    cs = GDN_CHUNK
    for c in chains:
        c["kb"] = c["k"] * c["beta"]
        c["k16"] = c["k"].astype(BF16)
        c["decay"] = jnp.where(c["incl"], jnp.exp(jnp.where(c["incl"], c["gc"] - c["gc_row"], 0.0)), 0.0)
    for c in chains:
        c["kk"] = _dot_nt(jnp.concatenate([c["kb"], c["q"]], axis=0).astype(BF16), c["k16"])
    for c in chains:
        c["a"] = jnp.where(c["strict"], c["kk"][:cs] * c["decay"], 0.0)
        c["attn"] = (c["kk"][cs:] * c["decay"]).astype(BF16)
        c["m"] = -jnp.where(c["levels"][0], c["a"], 0.0)
    for li in range(1, len(chains[0]["levels"])):
        for c in chains:
            c["m16"] = c["m"].astype(BF16)
            c["cm"] = jnp.where(c["levels"][li], c["a"], 0.0)
        for c in chains:
            c["x"] = c["cm"] + _dot(c["m16"], c["cm"].astype(BF16))
        for c in chains:
            c["y"] = c["x"] + _dot(c["x"].astype(BF16), c["m16"])
        for c in chains:
            c["m"] = c["m"] - c["y"]
    for c in chains:
        c["rhs"] = jnp.concatenate([c["v"] * c["beta"], c["kb"] * c["egc"]], axis=1)
    for c in chains:
        uw = c["rhs"] + _dot(c["m"].astype(BF16), c["rhs"].astype(BF16))
        c["u"] = uw[:, :HEAD_DIM]
        c["wq_lhs"] = jnp.concatenate([uw[:, HEAD_DIM:], c["q"] * c["egc"]], axis=0).astype(BF16)
        c["kg"] = (c["k"] * jnp.exp(c["glast"] - c["gc"])).astype(BF16)
        c["eg"] = jnp.exp(c["glast"])


def _gdn_sequential(chains):
    cs = GDN_CHUNK
    for c in chains:
        c["s"] = c["s_ref"][c["h"]]
        c["wq"] = _dot(c["wq_lhs"], c["s"].astype(BF16))
    for c in chains:
        c["v_new"] = (c["u"] - c["wq"][:cs]).astype(BF16)
    for c in chains:
        c["o"] = c["wq"][cs:] + _dot(c["attn"], c["v_new"])
    for c in chains:
        c["s_ref"][c["h"]] = c["s"] * c["eg"] + _dot_tn(c["kg"], c["v_new"])
    for c in chains:
        c["o_ref"][0, pl.ds(c["r0"], cs), c["cols"]] = c["o"]


def _gdn_body(qf_ref, qb_ref, gbf_ref, gbb_ref, s0f_ref, s0b_ref,
              of_ref, ob_ref, sff_ref, sfb_ref, sf_scr, sb_scr, *, tb, nt):
    t = pl.program_id(1)
    cs = GDN_CHUNK
    nch = tb // cs
    cps = GDN_CHUNKS_PER_STEP

    @pl.when(t == 0)
    def _():
        sf_scr[...] = s0f_ref[0]
        sb_scr[...] = s0b_ref[0]

    ii = lax.broadcasted_iota(I32, (cs, cs), 0)
    jj = lax.broadcasted_iota(I32, (cs, cs), 1)
    incl_f, strict_f = jj <= ii, jj < ii
    incl_b, strict_b = jj >= ii, jj > ii
    levels = []
    sh = 0
    while (1 << sh) < cs:
        levels.append(((ii >> (sh + 1)) == (jj >> (sh + 1))) & ((ii >> sh) != (jj >> sh)))
        sh += 1
    lv_f = [lm & strict_f for lm in levels]
    lv_b = [lm & strict_b for lm in levels]
    row = lax.broadcasted_iota(I32, (cs, LANES), 0)

    def step(n, carry):
        groups = []
        for j in range(cps):
            rf = pl.multiple_of((n * cps + j) * cs, cs)
            rb = pl.multiple_of((nch - 1 - n * cps - j) * cs, cs)
            gf = gbf_ref[0, pl.ds(rf, cs), :]
            gb = gbb_ref[0, pl.ds(rb, cs), :]
            cf, cb = gf, gb
            s = 1
            while s < cs:
                cf = cf + jnp.where(row >= s, pltpu.roll(cf, s, 0), 0.0)
                cb = cb + jnp.where(row < cs - s, pltpu.roll(cb, cs - s, 0), 0.0)
                s *= 2
            gt = jnp.concatenate([cf, cb], axis=0).T
            ecf, ecb = jnp.exp(cf), jnp.exp(cb)
            chains = []
            for h in range(N_HEADS):
                cq = slice(h * HEAD_DIM, (h + 1) * HEAD_DIM)
                ck = slice(B_W + h * HEAD_DIM, B_W + (h + 1) * HEAD_DIM)
                cv = slice(2 * B_W + h * HEAD_DIM, 2 * B_W + (h + 1) * HEAD_DIM)
                lf, lb = h, N_HEADS + h
                ld = lambda ref, r0, cols: ref[0, pl.ds(r0, cs), cols].astype(F32)
                chains.append(dict(
                    q=ld(qf_ref, rf, cq), k=ld(qf_ref, rf, ck), v=ld(qf_ref, rf, cv),
                    gc=cf[:, lf:lf + 1], gc_row=gt[lf:lf + 1, 0:cs], beta=gf[:, 2 * N_HEADS + h:2 * N_HEADS + h + 1],
                    egc=ecf[:, lf:lf + 1], glast=cf[cs - 1:cs, lf:lf + 1], incl=incl_f, strict=strict_f, levels=lv_f,
                    s_ref=sf_scr, h=h, o_ref=of_ref, r0=rf, cols=cq))
                chains.append(dict(
                    q=ld(qb_ref, rb, cq), k=ld(qb_ref, rb, ck), v=ld(qb_ref, rb, cv),
                    gc=cb[:, lb:lb + 1], gc_row=gt[lb:lb + 1, cs:2 * cs],
                    beta=gb[:, 3 * N_HEADS + h:3 * N_HEADS + h + 1],
                    egc=ecb[:, lb:lb + 1], glast=cb[0:1, lb:lb + 1], incl=incl_b, strict=strict_b, levels=lv_b,
                    s_ref=sb_scr, h=h, o_ref=ob_ref, r0=rb, cols=cq))
            groups.append(chains)
        _gdn_local([c for chains in groups for c in chains])
        for chains in groups:
            _gdn_sequential(chains)
        return carry

    lax.fori_loop(0, nch // cps, step, 0)

    @pl.when(t == nt - 1)
    def _():
        sff_ref[0] = sf_scr[...]
        sfb_ref[0] = sb_scr[...]


def _gdn_call(qkv, gb, s0f, s0b, tb):
    bsz, t, _ = qkv.shape
    nt = t // tb
    assert (tb // GDN_CHUNK) % GDN_CHUNKS_PER_STEP == 0
    st = pl.BlockSpec((1, N_HEADS, HEAD_DIM, HEAD_DIM), lambda b, i: (b, 0, 0, 0))
    fwd = lambda w: pl.BlockSpec((1, tb, w), lambda b, i: (b, i, 0))
    bwd = lambda w: pl.BlockSpec((1, tb, w), lambda b, i: (b, nt - 1 - i, 0))
    return pl.pallas_call(
        functools.partial(_gdn_body, tb=tb, nt=nt),
        out_shape=(jax.ShapeDtypeStruct((bsz, t, B_W), F32), jax.ShapeDtypeStruct((bsz, t, B_W), F32),
                   jax.ShapeDtypeStruct((bsz, N_HEADS, HEAD_DIM, HEAD_DIM), F32),
                   jax.ShapeDtypeStruct((bsz, N_HEADS, HEAD_DIM, HEAD_DIM), F32)),
        grid=(bsz, nt),
        in_specs=[fwd(QKV_W), bwd(QKV_W), fwd(LANES), bwd(LANES), st, st],
        out_specs=(fwd(B_W), bwd(B_W), st, st),
        scratch_shapes=[pltpu.VMEM((N_HEADS, HEAD_DIM, HEAD_DIM), F32), pltpu.VMEM((N_HEADS, HEAD_DIM, HEAD_DIM), F32)],
        compiler_params=_cparams(("parallel", "arbitrary")),
        name="gdn",
    )(qkv, qkv, gb, gb, s0f, s0b)


def _mixout_body(x_ref, of_ref, ob_ref, z_ref, ya_ref, mod_ref, gng_ref, wout_ref, n2g_ref, wrh_ref, wrl_ref, br_ref,
                 h_ref, fin_ref, aff_ref):
    o = of_ref[0] + ob_ref[0]
    z = z_ref[0]
    parts = [ya_ref[0]]
    for h in range(N_HEADS):
        c = slice(h * HEAD_DIM, (h + 1) * HEAD_DIM)
        oh = o[:, c]
        y = oh * lax.rsqrt(jnp.mean(oh * oh, axis=-1, keepdims=True) + NORM_EPS)
        parts.append((y * gng_ref[...] * _silu(z[:, c])).astype(BF16))
    mix = _dot(jnp.concatenate(parts, axis=1), wout_ref[...])
    hl = x_ref[0] + mod_ref[0, 2:3, :] * mix
    h_ref[0] = hl
    fin = _norm_mod(hl, n2g_ref[...], mod_ref[0, 3:4, :], mod_ref[0, 4:5, :])
    f_hi = fin.astype(BF16)
    fin_ref[0] = f_hi
    f_lo = (fin - f_hi.astype(F32)).astype(BF16)
    logits = _dot(f_hi, wrh_ref[...]) + _dot(f_lo, wrh_ref[...]) + _dot(f_hi, wrl_ref[...]) + br_ref[...]
    e = jnp.exp(logits - jnp.max(logits, axis=-1, keepdims=True))
    aff_ref[0] = e / jnp.sum(e, axis=-1, keepdims=True)


def _mixout_call(x, o_f, o_b, z, ya, mod3, gng, wout16, n2g, wr_hi, wr_lo, br, tm):
    bsz, t, _ = x.shape
    full = lambda a: pl.BlockSpec(a.shape, lambda b, i: (0,) * a.ndim)
    tok = lambda w: pl.BlockSpec((1, tm, w), lambda b, i: (b, i, 0))
    return pl.pallas_call(
        _mixout_body,
        out_shape=(jax.ShapeDtypeStruct((bsz, t, D_MODEL), F32), jax.ShapeDtypeStruct((bsz, t, D_MODEL), BF16),
                   jax.ShapeDtypeStruct((bsz, t, LANES), F32)),
        grid=(bsz, t // tm),
        in_specs=[tok(D_MODEL), tok(B_W), tok(B_W), tok(B_W), tok(A_W),
                  pl.BlockSpec((1, N_MOD, D_MODEL), lambda b, i: (b, 0, 0)),
                  full(gng), full(wout16), full(n2g), full(wr_hi), full(wr_lo), full(br)],
        out_specs=(tok(D_MODEL), tok(D_MODEL), tok(LANES)),
        compiler_params=_cparams(("parallel", "parallel")),
        name="mixout",
    )(x, o_f, o_b, z, ya, mod3, gng, wout16, n2g, wr_hi, wr_lo, br)


def _route_body(aff_ref, slot_ref, slott_ref, base_ref, *, t, cap):
    rc = ROUTE_CHUNK
    nchunk = t // rc

    def search(i, thr):
        cand = thr | jnp.left_shift(jnp.int32(1), 30 - i)
        cnt = jnp.sum((aff_ref[0] >= pltpu.bitcast(cand, F32)).astype(I32), axis=0, keepdims=True)
        return jnp.where(cnt >= cap, cand, thr)

    thr_bits = lax.fori_loop(0, 31, search, jnp.zeros((1, LANES), I32))
    thr = pltpu.bitcast(thr_bits, F32)
    n_gt = jnp.sum((aff_ref[0] > thr).astype(I32), axis=0, keepdims=True)
    need = (cap - n_gt).astype(F32)

    r = lax.broadcasted_iota(I32, (rc, rc), 0)
    c = lax.broadcasted_iota(I32, (rc, rc), 1)
    tril = (c <= r).astype(BF16)

    def chunk(ci, carry):
        ceq, csel = carry
        r0 = pl.multiple_of(ci * rc, rc)
        xc = aff_ref[0, pl.ds(r0, rc), :]
        gt = xc > thr
        eq = xc == thr
        eqp = _dot(tril, eq.astype(BF16)) + ceq
        sel = gt | (eq & (eqp <= need))
        selp = _dot(tril, sel.astype(BF16)) + csel
        slot = jnp.where(sel, selp - 1.0, -1.0)
        slot_ref[0, pl.ds(r0, rc), :] = slot.astype(I32)
        base_ref[0, pl.ds(ci, 1), :] = csel.astype(I32)
        for hh in range(rc // LANES):
            st = slot[hh * LANES:(hh + 1) * LANES, :].T
            slott_ref[0, ci * (rc // LANES) + hh] = st[0:N_EXPERTS, :].astype(I32)
        return eqp[rc - 1:rc, :], selp[rc - 1:rc, :]

    _, csel = lax.fori_loop(0, nchunk, chunk, (jnp.zeros((1, LANES), F32), jnp.zeros((1, LANES), F32)))
    base_ref[0, nchunk:nchunk + 1, :] = csel.astype(I32)


def _route_call(aff, cap):
    bsz, t, _ = aff.shape
    nchunk = t // ROUTE_CHUNK
    return pl.pallas_call(
        functools.partial(_route_body, t=t, cap=cap),
        out_shape=(jax.ShapeDtypeStruct((bsz, t, LANES), I32),
                   jax.ShapeDtypeStruct((bsz, t // LANES, N_EXPERTS, LANES), I32),
                   jax.ShapeDtypeStruct((bsz, nchunk + 1, LANES), I32)),
        grid=(bsz,),
        in_specs=[pl.BlockSpec((1, t, LANES), lambda b: (b, 0, 0))],
        out_specs=(pl.BlockSpec((1, t, LANES), lambda b: (b, 0, 0)),
                   pl.BlockSpec((1, t // LANES, N_EXPERTS, LANES), lambda b: (b, 0, 0, 0)),
                   pl.BlockSpec((1, nchunk + 1, LANES), lambda b: (b, 0, 0))),
        compiler_params=_cparams(("parallel",)),
        name="route",
    )(aff)


def _window_plan(base_ref, flat0, experts):
    starts, rounds = [], jnp.int32(0)
    for e in experts:
        lo = base_ref[flat0 + e]
        hi = base_ref[flat0 + N_EXPERTS + e]
        lo_al = (lo >> 4) << 4
        starts.append(lo_al)
        rounds = jnp.maximum(rounds, (hi - lo_al + SLOT_WIN - 1) // SLOT_WIN)
    return starts, rounds


def _dispatch_body(base_ref, slott_ref, fin_ref, xe_ref, *, nchunk, sub, eh_n):
    b, eh, ci = pl.program_id(0), pl.program_id(1), pl.program_id(2)
    rc = ROUTE_CHUNK

    @pl.when(ci == 0)
    def _():
        xe_ref[...] = jnp.zeros_like(xe_ref)

    srow = lax.broadcasted_iota(I32, (SLOT_WIN, rc), 0)
    for sc in range(sub):
        cc = ci * sub + sc
        flat0 = (b * (nchunk + 1) + cc) * N_EXPERTS + eh * eh_n
        f = fin_ref[0, sc * rc:(sc + 1) * rc, :]
        for g in range(eh_n // WIN_GROUP):
            experts = [g * WIN_GROUP + el for el in range(WIN_GROUP)]
            starts, rounds = _window_plan(base_ref, flat0, experts)

            def one_round(r, carry, experts=experts, starts=starts, f=f, sc=sc):
                rows = []
                for el, e in enumerate(experts):
                    tok_slot = jnp.concatenate(
                        [slott_ref[0, sc * (rc // LANES) + j, e:e + 1, :] for j in range(rc // LANES)], axis=1)
                    rows.append((tok_slot == srow + (starts[el] + r * SLOT_WIN)).astype(BF16))
                prod = _dot(jnp.concatenate(rows, axis=0), f)
                for el, e in enumerate(experts):
                    win = pl.ds(pl.multiple_of(starts[el] + r * SLOT_WIN, SLOT_ALIGN), SLOT_WIN)
                    xe_ref[0, e, win, :] = xe_ref[0, e, win, :] + prod[el * SLOT_WIN:(el + 1) * SLOT_WIN].astype(BF16)
                return carry

            lax.fori_loop(0, rounds, one_round, 0)


def _dispatch_call(base_flat, slott, fin, cap):
    bsz, t, _ = fin.shape
    nchunk = t // ROUTE_CHUNK
    sub = 2
    eh_n = N_EXPERTS // 2
    sp = cap + SLOT_WIN
    grid_spec = pltpu.PrefetchScalarGridSpec(
        num_scalar_prefetch=1,
        grid=(bsz, N_EXPERTS // eh_n, nchunk // sub),
        in_specs=[pl.BlockSpec((1, sub * ROUTE_CHUNK // LANES, eh_n, LANES), lambda b, eh, ci, base: (b, ci, eh, 0)),
                  pl.BlockSpec((1, sub * ROUTE_CHUNK, D_MODEL), lambda b, eh, ci, base: (b, ci, 0))],
        out_specs=pl.BlockSpec((1, eh_n, sp, D_MODEL), lambda b, eh, ci, base: (b, eh, 0, 0)))
    return pl.pallas_call(
        functools.partial(_dispatch_body, nchunk=nchunk, sub=sub, eh_n=eh_n),
        out_shape=jax.ShapeDtypeStruct((bsz, N_EXPERTS, sp, D_MODEL), BF16),
        grid_spec=grid_spec,
        compiler_params=_cparams(("parallel", "parallel", "arbitrary")),
        name="dispatch",
    )(base_flat, slott, fin)


def _experts_body(xe_ref, wg_ref, wu_ref, wd_ref, y_ref, wg16, wu16, wd16, *, cap):
    @pl.when(pl.program_id(1) == 0)
    def _():
        wg16[...] = wg_ref[0].astype(BF16)
        wu16[...] = wu_ref[0].astype(BF16)
        wd16[...] = wd_ref[0].astype(BF16)

    x = xe_ref[0, 0, 0:cap, :]
    ft = 256
    acc = None
    for f in range(EXPERT_FF // ft):
        cols = slice(f * ft, (f + 1) * ft)
        hid = (_silu(_dot(x, wg16[:, cols])) * _dot(x, wu16[:, cols])).astype(BF16)
        part = _dot(hid, wd16[cols, :])
        acc = part if acc is None else acc + part
    y_ref[0, 0, 0:cap, :] = acc.astype(BF16)
    y_ref[0, 0, cap:, :] = jnp.zeros((y_ref.shape[2] - cap, D_MODEL), BF16)


def _experts_call(xe, w_gate, w_up, w_down, cap):
    bsz, _, sp, _ = xe.shape
    wspec = lambda shape: pl.BlockSpec((1,) + shape, lambda e, b: (e, 0, 0))
    slots = pl.BlockSpec((1, 1, sp, D_MODEL), lambda e, b: (b, e, 0, 0))
    return pl.pallas_call(
        functools.partial(_experts_body, cap=cap),
        out_shape=jax.ShapeDtypeStruct(xe.shape, BF16),
        grid=(N_EXPERTS, bsz),
        in_specs=[slots, wspec((D_MODEL, EXPERT_FF)), wspec((D_MODEL, EXPERT_FF)), wspec((EXPERT_FF, D_MODEL))],
        out_specs=slots,
        scratch_shapes=[pltpu.VMEM((D_MODEL, EXPERT_FF), BF16), pltpu.VMEM((D_MODEL, EXPERT_FF), BF16),
                        pltpu.VMEM((EXPERT_FF, D_MODEL), BF16)],
        compiler_params=_cparams(("arbitrary", "arbitrary")),
        name="experts",
    )(xe, w_gate, w_up, w_down)


def _combine_body(base_ref, slot_ref, aff_ref, h_ref, y_ref, mod_ref, fng_ref, o_ref, acc_ref, *, nchunk):
    b, ci = pl.program_id(0), pl.program_id(1)
    rc = ROUTE_CHUNK
    acc_ref[...] = jnp.zeros_like(acc_ref)
    flat0 = (b * (nchunk + 1) + ci) * N_EXPERTS
    width = WIN_GROUP * SLOT_WIN
    lane = lax.broadcasted_iota(I32, (rc, width), 1)
    lane_el = lane >> 6
    lane_j = lane & (SLOT_WIN - 1)
    slot = slot_ref[0]
    aff = aff_ref[0]
    for g in range(N_EXPERTS // WIN_GROUP):
        experts = [g * WIN_GROUP + el for el in range(WIN_GROUP)]
        starts, rounds = _window_plan(base_ref, flat0, experts)

        def one_round(r, carry, experts=experts, starts=starts):
            ywin = jnp.concatenate(
                [y_ref[0, e, pl.ds(pl.multiple_of(starts[el] + r * SLOT_WIN, SLOT_ALIGN), SLOT_WIN), :]
                 for el, e in enumerate(experts)], axis=0)
            s = jnp.zeros((rc, width), F32)
            for el, e in enumerate(experts):
                rel = slot[:, e:e + 1] - (starts[el] + r * SLOT_WIN)
                s = jnp.where((lane_el == el) & (rel == lane_j), aff[:, e:e + 1], s)
            acc_ref[...] += _dot(s.astype(BF16), ywin)
            return carry

        lax.fori_loop(0, rounds, one_round, 0)
    hl = h_ref[0] + mod_ref[0, 5:6, :] * acc_ref[...]
    ms = jnp.mean(hl * hl, axis=-1, keepdims=True)
    o_ref[0] = hl * lax.rsqrt(ms + NORM_EPS) * fng_ref[...]


def _combine_call(base_flat, slot, aff, h, y, mod3, fng):
    bsz, t, _ = h.shape
    nchunk = t // ROUTE_CHUNK
    rc = ROUTE_CHUNK
    tok = lambda w: pl.BlockSpec((1, rc, w), lambda b, i, base: (b, i, 0))
    grid_spec = pltpu.PrefetchScalarGridSpec(
        num_scalar_prefetch=1,
        grid=(bsz, nchunk),
        in_specs=[tok(LANES), tok(LANES), tok(D_MODEL),
                  pl.BlockSpec((1,) + y.shape[1:], lambda b, i, base: (b, 0, 0, 0), pipeline_mode=pl.Buffered(1)),
                  pl.BlockSpec((1, N_MOD, D_MODEL), lambda b, i, base: (b, 0, 0)),
                  pl.BlockSpec((1, D_MODEL), lambda b, i, base: (0, 0))],
        out_specs=tok(D_MODEL),
        scratch_shapes=[pltpu.VMEM((rc, D_MODEL), F32)])
    return pl.pallas_call(
        functools.partial(_combine_body, nchunk=nchunk),
        out_shape=jax.ShapeDtypeStruct(h.shape, F32),
        grid_spec=grid_spec,
        compiler_params=_cparams(("parallel", "arbitrary")),
        name="combine",
    )(base_flat, slot, aff, h, y, mod3, fng)


def _pad_lanes(a):
    return jnp.pad(a, ((0, 0), (0, LANES - a.shape[1])))


def kernel(x, c, ctx, c_ctx, w_mod, b_mod, norm1_g, norm2_g, w_in, conv_w, a_log, dt_bias, gdn_norm_g, gm_norm_g,
           gm_ws, gm_bs, w_out, w_router, b_router, w_gate, w_up, w_down, final_norm_g):
    bsz, t, _ = x.shape
    ctx_len = ctx.shape[1]
    assert w_mod.shape[0] == 1, "single-layer problem"
    assert t % 512 == 0 and ctx_len % GDN_CHUNK == 0 and bsz < 8
    cap = EC_CAPACITY * t // N_EXPERTS

    cs = jnp.zeros((8, D_MODEL), F32).at[:bsz].set(c).at[bsz].set(c_ctx)
    mod3 = _mod_call(cs, w_mod[0], b_mod[0][None, :]).reshape(8, N_MOD, D_MODEL)

    wl = w_in[0]
    n_state = QKV_W + STATE_COLS
    w_state = _pad_lanes(wl[:, QKV_W:n_state])
    w_lat = jnp.concatenate([wl[:, :QKV_W], wl[:, n_state:n_state + B_W], wl[:, n_state + B_W:], w_state],
                            axis=1).astype(BF16)
    w_ctx = jnp.concatenate([wl[:, :QKV_W], w_state], axis=1).astype(BF16)
    gp = jnp.zeros((8, LANES), F32).at[0, :2 * N_HEADS].set(a_log[0].reshape(-1)).at[1, :2 * N_HEADS].set(
        dt_bias[0].reshape(-1))
    g1 = norm1_g[0][None, :]
    cw = jnp.zeros((8, QKV_W), F32).at[:conv_w.shape[1]].set(conv_w[0])

    qkv_c, gb_c = _inproj_ctx_call(ctx, mod3, bsz, g1, w_ctx, gp, cw)
    zero_state = jnp.zeros((bsz, N_HEADS, HEAD_DIM, HEAD_DIM), F32)
    _, _, s_f, s_b = _gdn_call(qkv_c, gb_c, zero_state, zero_state, ctx_len)

    qkv, gb, z, ya = _inproj_lat_call(x, mod3, g1, w_lat, gp, cw, gm_norm_g[0][None, :], gm_ws[0].astype(BF16),
                                      _pad_lanes(gm_bs[0].T), 512)
    o_f, o_b, _, _ = _gdn_call(qkv, gb, s_f, s_b, 512)
    wr = _pad_lanes(w_router[0])
    wr_hi = wr.astype(BF16)
    wr_lo = (wr - wr_hi.astype(F32)).astype(BF16)
    br = jnp.full((1, LANES), -1e30, F32).at[0, :N_EXPERTS].set(b_router[0])
    h, fin, aff = _mixout_call(x, o_f, o_b, z, ya, mod3, gdn_norm_g[0][None, :], w_out[0].astype(BF16),
                               norm2_g[0][None, :], wr_hi, wr_lo, br, 512)

    slot, slott, base = _route_call(aff, cap)
    base_flat = base[:, :, :N_EXPERTS].reshape(-1)
    xe = _dispatch_call(base_flat, slott, fin, cap)
    y = _experts_call(xe, w_gate[0], w_up[0], w_down[0], cap)
    return _combine_call(base_flat, slot, aff, h, y, mod3, final_norm_g[None, :])
```

```python
import functools

import jax
import jax.numpy as jnp
from jax import lax
from jax.experimental import pallas as pl
from jax.experimental.pallas import tpu as pltpu

F32 = jnp.float32
BF16 = jnp.bfloat16
I32 = jnp.int32

D_MODEL = 1024
N_MOD = 6
N_HEADS = 4
HEAD_DIM = 128
B_W = N_HEADS * HEAD_DIM
QKV_W = 3 * B_W
A_W = 512
A_GROUPS = 4
A_CHUNK = 128
GDN_CHUNK = 64
N_EXPERTS = 16
EC_CAPACITY = 2
EXPERT_FF = 1024
NORM_EPS = 1e-6
LANES = 128
STATE_COLS = 4 * N_HEADS

ROUTE_CHUNK = 256
SLOT_WIN = 64
SLOT_ALIGN = 16
WIN_GROUP = 4
VMEM_LIMIT = 56 * 1024 * 1024


def _cparams(sem):
    return pltpu.CompilerParams(dimension_semantics=sem, vmem_limit_bytes=VMEM_LIMIT)


def _dot(a, b):
    return jnp.dot(a, b, preferred_element_type=F32)


def _dot_nt(a, b):
    return lax.dot_general(a, b, (((1,), (1,)), ((), ())), preferred_element_type=F32)


def _dot_tn(a, b):
    return lax.dot_general(a, b, (((0,), (0,)), ((), ())), preferred_element_type=F32)


def _silu(x):
    return x * jax.nn.sigmoid(x)


def _mod_body(c_ref, w_ref, b_ref, o_ref):
    s = _silu(c_ref[...])
    o_ref[...] = _dot(s.astype(BF16), w_ref[...].astype(BF16)) + b_ref[...]


def _mod_call(cs, w_mod, b_mod):
    n = w_mod.shape[1] // D_MODEL
    return pl.pallas_call(
        _mod_body,
        out_shape=jax.ShapeDtypeStruct((8, w_mod.shape[1]), F32),
        grid=(n,),
        in_specs=[pl.BlockSpec((8, D_MODEL), lambda j: (0, 0)),
                  pl.BlockSpec((D_MODEL, D_MODEL), lambda j: (0, j)),
                  pl.BlockSpec((1, D_MODEL), lambda j: (0, j))],
        out_specs=pl.BlockSpec((8, D_MODEL), lambda j: (0, j)),
        compiler_params=_cparams(("arbitrary",)),
        name="mod",
    )(cs, w_mod, b_mod)


def _norm_mod(x, g, shift, scale):
    ms = jnp.mean(x * x, axis=-1, keepdims=True)
    return (x * lax.rsqrt(ms + NORM_EPS) * g) * (1.0 + scale) + shift


def _gate_streams(st, gp_ref):
    lane = lax.broadcasted_iota(I32, st.shape, 1)
    g = -jnp.exp(gp_ref[0:1, :]) * jax.nn.softplus(st + gp_ref[1:2, :])
    beta = jax.nn.sigmoid(st)
    return jnp.where(lane < 2 * N_HEADS, g, jnp.where(lane < STATE_COLS, beta, 0.0))


def _conv_qkv(qkv, prev_row, next_row, cw_ref, out_ref, tm):
    cs = GDN_CHUNK
    nsub = tm // cs
    w0, w1, w2 = cw_ref[0:1, :], cw_ref[1:2, :], cw_ref[2:3, :]
    row = lax.broadcasted_iota(I32, (cs, 1), 0)
    for c in range(nsub):
        rows = slice(c * cs, (c + 1) * cs)
        x = qkv[rows]
        prow = prev_row if c == 0 else qkv[c * cs - 1:c * cs]
        nrow = next_row if c == nsub - 1 else qkv[(c + 1) * cs:(c + 1) * cs + 1]
        xp = jnp.where(row == 0, prow, pltpu.roll(x, 1, 0))
        xn = jnp.where(row == cs - 1, nrow, pltpu.roll(x, cs - 1, 0))
        y = _silu(xp * w0 + x * w1 + xn * w2)
        for h in range(N_HEADS):
            cq = slice(h * HEAD_DIM, (h + 1) * HEAD_DIM)
            ck = slice(B_W + h * HEAD_DIM, B_W + (h + 1) * HEAD_DIM)
            q = y[:, cq]
            k = y[:, ck]
            out_ref[0, rows, cq] = (q * (lax.rsqrt(jnp.sum(q * q, axis=-1, keepdims=True) + NORM_EPS)
                                         * (HEAD_DIM ** -0.5))).astype(BF16)
            out_ref[0, rows, ck] = (k * lax.rsqrt(jnp.sum(k * k, axis=-1, keepdims=True) + NORM_EPS)).astype(BF16)
        out_ref[0, rows, 2 * B_W:3 * B_W] = y[:, 2 * B_W:3 * B_W].astype(BF16)


def _inproj_lat_body(x_ref, xp_ref, xn_ref, mod_ref, g1_ref, w_ref, gp_ref, cw_ref, gmg_ref, ws_ref, bst_ref,
                     qkv_ref, gb_ref, z_ref, ya_ref, *, tm):
    i = pl.program_id(1)
    shift, scale = mod_ref[0, 0:1, :], mod_ref[0, 1:2, :]
    a = _norm_mod(x_ref[0], g1_ref[...], shift, scale).astype(BF16)
    xh = jnp.concatenate([xp_ref[0], xn_ref[0]], axis=0)
    halo = _dot(_norm_mod(xh, g1_ref[...], shift, scale).astype(BF16), w_ref[:, 0:QKV_W])
    prev_row = jnp.where(i == 0, 0.0, halo[7:8, :])
    next_row = jnp.where(i == pl.num_programs(1) - 1, 0.0, halo[8:9, :])
    _conv_qkv(_dot(a, w_ref[:, 0:QKV_W]), prev_row, next_row, cw_ref, qkv_ref, tm)
    z_ref[0] = _dot(a, w_ref[:, QKV_W:QKV_W + B_W])
    c_uv = QKV_W + B_W
    gb_ref[0] = _gate_streams(_dot(a, w_ref[:, c_uv + 2 * A_W:c_uv + 2 * A_W + LANES]), gp_ref)
    uv = _dot(a, w_ref[:, c_uv:c_uv + 2 * A_W])
    uv = 0.5 * uv * (1.0 + lax.erf(uv * 0.7071067811865476))
    gd = A_W // A_GROUPS
    for grp in range(A_GROUPS):
        v = uv[:, A_W + grp * gd:A_W + (grp + 1) * gd]
        vn = v * lax.rsqrt(jnp.mean(v * v, axis=-1, keepdims=True) + NORM_EPS) * gmg_ref[:, grp * gd:(grp + 1) * gd]
        vn = vn.astype(BF16)
        bias = bst_ref[:, grp:grp + 1]
        for c in range(tm // A_CHUNK):
            rows = slice(c * A_CHUNK, (c + 1) * A_CHUNK)
            s = _dot(ws_ref[grp], vn[rows]) + bias
            ya_ref[0, rows, grp * gd:(grp + 1) * gd] = (uv[rows, grp * gd:(grp + 1) * gd] * s).astype(BF16)


def _inproj_ctx_body(x_ref, mod_ref, g1_ref, w_ref, gp_ref, cw_ref, qkv_ref, gb_ref, *, tm):
    a = _norm_mod(x_ref[0], g1_ref[...], mod_ref[0, 0:1, :], mod_ref[0, 1:2, :]).astype(BF16)
    edge = jnp.zeros((1, QKV_W), F32)
    _conv_qkv(_dot(a, w_ref[:, 0:QKV_W]), edge, edge, cw_ref, qkv_ref, tm)
    gb_ref[0] = _gate_streams(_dot(a, w_ref[:, QKV_W:QKV_W + LANES]), gp_ref)


def _inproj_lat_call(x, mod3, g1, w_lat, gp, cw, gmg, ws16, bst, tm):
    bsz, t, _ = x.shape
    hb = tm // 8
    last8 = t // 8 - 1
    full = lambda a: pl.BlockSpec(a.shape, lambda b, i: (0,) * a.ndim)
    tok = lambda w: pl.BlockSpec((1, tm, w), lambda b, i: (b, i, 0))
    return pl.pallas_call(
        functools.partial(_inproj_lat_body, tm=tm),
        out_shape=(jax.ShapeDtypeStruct((bsz, t, QKV_W), BF16),
                   jax.ShapeDtypeStruct((bsz, t, LANES), F32),
                   jax.ShapeDtypeStruct((bsz, t, B_W), F32),
                   jax.ShapeDtypeStruct((bsz, t, A_W), BF16)),
        grid=(bsz, t // tm),
        in_specs=[tok(D_MODEL),
                  pl.BlockSpec((1, 8, D_MODEL), lambda b, i: (b, jnp.maximum(i * hb - 1, 0), 0)),
                  pl.BlockSpec((1, 8, D_MODEL), lambda b, i: (b, jnp.minimum((i + 1) * hb, last8), 0)),
                  pl.BlockSpec((1, N_MOD, D_MODEL), lambda b, i: (b, 0, 0)),
                  full(g1), full(w_lat), full(gp), full(cw), full(gmg), full(ws16), full(bst)],
        out_specs=(tok(QKV_W), tok(LANES), tok(B_W), tok(A_W)),
        compiler_params=_cparams(("parallel", "arbitrary")),
        name="inproj_lat",
    )(x, x, x, mod3, g1, w_lat, gp, cw, gmg, ws16, bst)


def _inproj_ctx_call(ctx, mod3, ctx_row, g1, w_ctx, gp, cw):
    bsz, t, _ = ctx.shape
    full = lambda a: pl.BlockSpec(a.shape, lambda b: (0,) * a.ndim)
    tok = lambda w: pl.BlockSpec((1, t, w), lambda b: (b, 0, 0))
    return pl.pallas_call(
        functools.partial(_inproj_ctx_body, tm=t),
        out_shape=(jax.ShapeDtypeStruct((bsz, t, QKV_W), BF16),
                   jax.ShapeDtypeStruct((bsz, t, LANES), F32)),
        grid=(bsz,),
        in_specs=[tok(D_MODEL),
                  pl.BlockSpec((1, N_MOD, D_MODEL), lambda b: (ctx_row, 0, 0)),
                  full(g1), full(w_ctx), full(gp), full(cw)],
        out_specs=(tok(QKV_W), tok(LANES)),
        compiler_params=_cparams(("parallel",)),
        name="inproj_ctx",
    )(ctx, mod3, g1, w_ctx, gp, cw)


GDN_CHUNKS_PER_STEP = 2


def _gdn_local(chains):
    cs = GDN_CHUNK
    for c in chains:
        c["kb"] = c["k"] * c["beta"]
        c["k16"] = c["k"].astype(BF16)
        c["decay"] = jnp.where(c["incl"], jnp.exp(jnp.where(c["incl"], c["gc"] - c["gc_row"], 0.0)), 0.0)
    for c in chains:
        c["kk"] = _dot_nt(jnp.concatenate([c["kb"], c["q"]], axis=0).astype(BF16), c["k16"])
    for c in chains:
        c["a"] = jnp.where(c["strict"], c["kk"][:cs] * c["decay"], 0.0)
        c["attn"] = (c["kk"][cs:] * c["decay"]).astype(BF16)
        c["m"] = -jnp.where(c["levels"][0], c["a"], 0.0)
    for li in range(1, len(chains[0]["levels"])):
        for c in chains:
            c["m16"] = c["m"].astype(BF16)
            c["cm"] = jnp.where(c["levels"][li], c["a"], 0.0)
        for c in chains:
            c["x"] = c["cm"] + _dot(c["m16"], c["cm"].astype(BF16))
        for c in chains:
            c["y"] = c["x"] + _dot(c["x"].astype(BF16), c["m16"])
        for c in chains:
            c["m"] = c["m"] - c["y"]
    for c in chains:
        c["rhs"] = jnp.concatenate([c["v"] * c["beta"], c["kb"] * c["egc"]], axis=1)
    for c in chains:
        uw = c["rhs"] + _dot(c["m"].astype(BF16), c["rhs"].astype(BF16))
        c["u"] = uw[:, :HEAD_DIM]
        c["wq_lhs"] = jnp.concatenate([uw[:, HEAD_DIM:], c["q"] * c["egc"]], axis=0).astype(BF16)
        c["kg"] = (c["k"] * jnp.exp(c["glast"] - c["gc"])).astype(BF16)
        c["eg"] = jnp.exp(c["glast"])


def _gdn_sequential(chains):
    cs = GDN_CHUNK
    for c in chains:
        c["s"] = c["s_ref"][c["h"]]
        c["wq"] = _dot(c["wq_lhs"], c["s"].astype(BF16))
    for c in chains:
        c["v_new"] = (c["u"] - c["wq"][:cs]).astype(BF16)
    for c in chains:
        c["o"] = c["wq"][cs:] + _dot(c["attn"], c["v_new"])
    for c in chains:
        c["s_ref"][c["h"]] = c["s"] * c["eg"] + _dot_tn(c["kg"], c["v_new"])
    for c in chains:
        c["o_ref"][0, pl.ds(c["r0"], cs), c["cols"]] = c["o"]


def _gdn_body(qf_ref, qb_ref, gbf_ref, gbb_ref, s0f_ref, s0b_ref,
              of_ref, ob_ref, sff_ref, sfb_ref, sf_scr, sb_scr, *, tb, nt):
    t = pl.program_id(1)
    cs = GDN_CHUNK
    nch = tb // cs
    cps = GDN_CHUNKS_PER_STEP

    @pl.when(t == 0)
    def _():
        sf_scr[...] = s0f_ref[0]
        sb_scr[...] = s0b_ref[0]

    ii = lax.broadcasted_iota(I32, (cs, cs), 0)
    jj = lax.broadcasted_iota(I32, (cs, cs), 1)
    incl_f, strict_f = jj <= ii, jj < ii
    incl_b, strict_b = jj >= ii, jj > ii
    levels = []
    sh = 0
    while (1 << sh) < cs:
        levels.append(((ii >> (sh + 1)) == (jj >> (sh + 1))) & ((ii >> sh) != (jj >> sh)))
        sh += 1
    lv_f = [lm & strict_f for lm in levels]
    lv_b = [lm & strict_b for lm in levels]
    row = lax.broadcasted_iota(I32, (cs, LANES), 0)

    def step(n, carry):
        groups = []
        for j in range(cps):
            rf = pl.multiple_of((n * cps + j) * cs, cs)
            rb = pl.multiple_of((nch - 1 - n * cps - j) * cs, cs)
            gf = gbf_ref[0, pl.ds(rf, cs), :]
            gb = gbb_ref[0, pl.ds(rb, cs), :]
            cf, cb = gf, gb
            s = 1
            while s < cs:
                cf = cf + jnp.where(row >= s, pltpu.roll(cf, s, 0), 0.0)
                cb = cb + jnp.where(row < cs - s, pltpu.roll(cb, cs - s, 0), 0.0)
                s *= 2
            gt = jnp.concatenate([cf, cb], axis=0).T
            ecf, ecb = jnp.exp(cf), jnp.exp(cb)
            chains = []
            for h in range(N_HEADS):
                cq = slice(h * HEAD_DIM, (h + 1) * HEAD_DIM)
                ck = slice(B_W + h * HEAD_DIM, B_W + (h + 1) * HEAD_DIM)
                cv = slice(2 * B_W + h * HEAD_DIM, 2 * B_W + (h + 1) * HEAD_DIM)
                lf, lb = h, N_HEADS + h
                ld = lambda ref, r0, cols: ref[0, pl.ds(r0, cs), cols].astype(F32)
                chains.append(dict(
                    q=ld(qf_ref, rf, cq), k=ld(qf_ref, rf, ck), v=ld(qf_ref, rf, cv),
                    gc=cf[:, lf:lf + 1], gc_row=gt[lf:lf + 1, 0:cs], beta=gf[:, 2 * N_HEADS + h:2 * N_HEADS + h + 1],
                    egc=ecf[:, lf:lf + 1], glast=cf[cs - 1:cs, lf:lf + 1], incl=incl_f, strict=strict_f, levels=lv_f,
                    s_ref=sf_scr, h=h, o_ref=of_ref, r0=rf, cols=cq))
                chains.append(dict(
                    q=ld(qb_ref, rb, cq), k=ld(qb_ref, rb, ck), v=ld(qb_ref, rb, cv),
                    gc=cb[:, lb:lb + 1], gc_row=gt[lb:lb + 1, cs:2 * cs],
                    beta=gb[:, 3 * N_HEADS + h:3 * N_HEADS + h + 1],
                    egc=ecb[:, lb:lb + 1], glast=cb[0:1, lb:lb + 1], incl=incl_b, strict=strict_b, levels=lv_b,
                    s_ref=sb_scr, h=h, o_ref=ob_ref, r0=rb, cols=cq))
            groups.append(chains)
        _gdn_local([c for chains in groups for c in chains])
        for chains in groups:
            _gdn_sequential(chains)
        return carry

    lax.fori_loop(0, nch // cps, step, 0)

    @pl.when(t == nt - 1)
    def _():
        sff_ref[0] = sf_scr[...]
        sfb_ref[0] = sb_scr[...]


def _gdn_call(qkv, gb, s0f, s0b, tb):
    bsz, t, _ = qkv.shape
    nt = t // tb
    assert (tb // GDN_CHUNK) % GDN_CHUNKS_PER_STEP == 0
    st = pl.BlockSpec((1, N_HEADS, HEAD_DIM, HEAD_DIM), lambda b, i: (b, 0, 0, 0))
    fwd = lambda w: pl.BlockSpec((1, tb, w), lambda b, i: (b, i, 0))
    bwd = lambda w: pl.BlockSpec((1, tb, w), lambda b, i: (b, nt - 1 - i, 0))
    return pl.pallas_call(
        functools.partial(_gdn_body, tb=tb, nt=nt),
        out_shape=(jax.ShapeDtypeStruct((bsz, t, B_W), F32), jax.ShapeDtypeStruct((bsz, t, B_W), F32),
                   jax.ShapeDtypeStruct((bsz, N_HEADS, HEAD_DIM, HEAD_DIM), F32),
                   jax.ShapeDtypeStruct((bsz, N_HEADS, HEAD_DIM, HEAD_DIM), F32)),
        grid=(bsz, nt),
        in_specs=[fwd(QKV_W), bwd(QKV_W), fwd(LANES), bwd(LANES), st, st],
        out_specs=(fwd(B_W), bwd(B_W), st, st),
        scratch_shapes=[pltpu.VMEM((N_HEADS, HEAD_DIM, HEAD_DIM), F32), pltpu.VMEM((N_HEADS, HEAD_DIM, HEAD_DIM), F32)],
        compiler_params=_cparams(("parallel", "arbitrary")),
        name="gdn",
    )(qkv, qkv, gb, gb, s0f, s0b)


def _mixout_body(x_ref, of_ref, ob_ref, z_ref, ya_ref, mod_ref, gng_ref, wout_ref, n2g_ref, wrh_ref, wrl_ref, br_ref,
                 h_ref, fin_ref, aff_ref, afft_ref, *, tm):
    o = of_ref[0] + ob_ref[0]
    z = z_ref[0]
    parts = [ya_ref[0]]
    for h in range(N_HEADS):
        c = slice(h * HEAD_DIM, (h + 1) * HEAD_DIM)
        oh = o[:, c]
        y = oh * lax.rsqrt(jnp.mean(oh * oh, axis=-1, keepdims=True) + NORM_EPS)
        parts.append((y * gng_ref[...] * _silu(z[:, c])).astype(BF16))
    mix = _dot(jnp.concatenate(parts, axis=1), wout_ref[...])
    hl = x_ref[0] + mod_ref[0, 2:3, :] * mix
    h_ref[0] = hl
    fin = _norm_mod(hl, n2g_ref[...], mod_ref[0, 3:4, :], mod_ref[0, 4:5, :])
    f_hi = fin.astype(BF16)
    fin_ref[0] = f_hi
    f_lo = (fin - f_hi.astype(F32)).astype(BF16)
    logits = _dot(f_hi, wrh_ref[...]) + _dot(f_lo, wrh_ref[...]) + _dot(f_hi, wrl_ref[...]) + br_ref[...]
    e = jnp.exp(logits - jnp.max(logits, axis=-1, keepdims=True))
    aff = e / jnp.sum(e, axis=-1, keepdims=True)
    aff_ref[0] = aff
    for j in range(tm // LANES):
        afft_ref[0, j] = aff[j * LANES:(j + 1) * LANES, :].T[0:N_EXPERTS, :]


def _mixout_call(x, o_f, o_b, z, ya, mod3, gng, wout16, n2g, wr_hi, wr_lo, br, tm):
    bsz, t, _ = x.shape
    full = lambda a: pl.BlockSpec(a.shape, lambda b, i: (0,) * a.ndim)
    tok = lambda w: pl.BlockSpec((1, tm, w), lambda b, i: (b, i, 0))
    return pl.pallas_call(
        functools.partial(_mixout_body, tm=tm),
        out_shape=(jax.ShapeDtypeStruct((bsz, t, D_MODEL), F32), jax.ShapeDtypeStruct((bsz, t, D_MODEL), BF16),
                   jax.ShapeDtypeStruct((bsz, t, LANES), F32),
                   jax.ShapeDtypeStruct((bsz, t // LANES, N_EXPERTS, LANES), F32)),
        grid=(bsz, t // tm),
        in_specs=[tok(D_MODEL), tok(B_W), tok(B_W), tok(B_W), tok(A_W),
                  pl.BlockSpec((1, N_MOD, D_MODEL), lambda b, i: (b, 0, 0)),
                  full(gng), full(wout16), full(n2g), full(wr_hi), full(wr_lo), full(br)],
        out_specs=(tok(D_MODEL), tok(D_MODEL), tok(LANES),
                   pl.BlockSpec((1, tm // LANES, N_EXPERTS, LANES), lambda b, i: (b, i, 0, 0))),
        compiler_params=_cparams(("parallel", "parallel")),
        name="mixout",
    )(x, o_f, o_b, z, ya, mod3, gng, wout16, n2g, wr_hi, wr_lo, br)


def _route_body(afft_ref, slott_ref, slot_ref, off_ref, *, t, cap):
    ne = N_EXPERTS
    npieces = t // LANES
    rows = npieces * ne

    def count(thr_col, strict):
        acc = jnp.zeros((ne, LANES), I32)
        for p in range(npieces):
            piece = afft_ref[0, p * ne:(p + 1) * ne, :]
            acc = acc + (piece > thr_col if strict else piece >= thr_col).astype(I32)
        return jnp.sum(acc, axis=1, keepdims=True)

    def search(i, thr):
        cand = thr | jnp.left_shift(jnp.int32(1), 30 - i)
        return jnp.where(count(pltpu.bitcast(cand, F32), False) >= cap, cand, thr)

    thr_bits = lax.fori_loop(0, 31, search, jnp.zeros((ne, 1), I32))
    thr = pltpu.bitcast(thr_bits, F32)
    need = (cap - count(thr, True)).astype(F32)

    x = afft_ref[0]
    thr_rows = jnp.concatenate([thr] * npieces, axis=0)
    need_rows = jnp.concatenate([need] * npieces, axis=0)
    gt = x > thr_rows
    eq = x == thr_rows
    ti = lax.broadcasted_iota(I32, (LANES, LANES), 0)
    tj = lax.broadcasted_iota(I32, (LANES, LANES), 1)
    triu = (ti <= tj).astype(BF16)
    ri = lax.broadcasted_iota(I32, (rows, rows), 0)
    rj = lax.broadcasted_iota(I32, (rows, rows), 1)
    earlier = (((ri & (ne - 1)) == (rj & (ne - 1))) & (rj < ri)).astype(BF16)

    def prefix(mask):
        inpiece = _dot(mask.astype(BF16), triu)
        total = jnp.broadcast_to(inpiece[:, LANES - 1:LANES], (rows, LANES)).astype(BF16)
        offset = _dot(earlier, total)
        return inpiece + offset, offset

    eq_rank, _ = prefix(eq)
    sel = gt | (eq & (eq_rank <= need_rows))
    sel_rank, sel_off = prefix(sel)
    slot = jnp.where(sel, sel_rank - 1.0, -1.0)
    slott_ref[0] = slot.astype(I32)
    off_ref[0] = sel_off.astype(I32)
    pad = jnp.zeros((LANES - ne, LANES), F32)
    for p in range(npieces):
        piece = jnp.concatenate([slot[p * ne:(p + 1) * ne, :], pad], axis=0)
        slot_ref[0, p * LANES:(p + 1) * LANES, :] = piece.T.astype(I32)


def _route_call(afft, cap):
    bsz, rows, _ = afft.shape
    t = rows // N_EXPERTS * LANES
    spec = lambda r: pl.BlockSpec((1, r, LANES), lambda b: (b, 0, 0))
    return pl.pallas_call(
        functools.partial(_route_body, t=t, cap=cap),
        out_shape=(jax.ShapeDtypeStruct((bsz, rows, LANES), I32),
                   jax.ShapeDtypeStruct((bsz, t, LANES), I32),
                   jax.ShapeDtypeStruct((bsz, rows, LANES), I32)),
        grid=(bsz,),
        in_specs=[spec(rows)],
        out_specs=(spec(rows), spec(t), spec(rows)),
        compiler_params=_cparams(("parallel",)),
        name="route",
    )(afft)


def _window_plan(base_ref, flat0, experts):
    starts, rounds = [], jnp.int32(0)
    for e in experts:
        lo = base_ref[flat0 + e]
        hi = base_ref[flat0 + N_EXPERTS + e]
        lo_al = (lo >> 4) << 4
        starts.append(lo_al)
        rounds = jnp.maximum(rounds, (hi - lo_al + SLOT_WIN - 1) // SLOT_WIN)
    return starts, rounds


def _dispatch_body(base_ref, slott_ref, fin_ref, xe_ref, *, nchunk, sub, eh_n):
    b, eh, ci = pl.program_id(0), pl.program_id(1), pl.program_id(2)
    rc = ROUTE_CHUNK

    @pl.when(ci == 0)
    def _():
        xe_ref[...] = jnp.zeros_like(xe_ref)

    srow = lax.broadcasted_iota(I32, (SLOT_WIN, rc), 0)
    for sc in range(sub):
        cc = ci * sub + sc
        flat0 = (b * (nchunk + 1) + cc) * N_EXPERTS + eh * eh_n
        f = fin_ref[0, sc * rc:(sc + 1) * rc, :]
        for g in range(eh_n // WIN_GROUP):
            experts = [g * WIN_GROUP + el for el in range(WIN_GROUP)]
            starts, rounds = _window_plan(base_ref, flat0, experts)

            def one_round(r, carry, experts=experts, starts=starts, f=f, sc=sc):
                rows = []
                for el, e in enumerate(experts):
                    tok_slot = jnp.concatenate(
                        [slott_ref[0, sc * (rc // LANES) + j, e:e + 1, :] for j in range(rc // LANES)], axis=1)
                    rows.append((tok_slot == srow + (starts[el] + r * SLOT_WIN)).astype(BF16))
                prod = _dot(jnp.concatenate(rows, axis=0), f)
                for el, e in enumerate(experts):
                    win = pl.ds(pl.multiple_of(starts[el] + r * SLOT_WIN, SLOT_ALIGN), SLOT_WIN)
                    xe_ref[0, e, win, :] = xe_ref[0, e, win, :] + prod[el * SLOT_WIN:(el + 1) * SLOT_WIN].astype(BF16)
                return carry

            lax.fori_loop(0, rounds, one_round, 0)


def _dispatch_call(base_flat, slott, fin, cap):
    bsz, t, _ = fin.shape
    nchunk = t // ROUTE_CHUNK
    sub = 2
    eh_n = N_EXPERTS // 2
    sp = cap + SLOT_WIN
    grid_spec = pltpu.PrefetchScalarGridSpec(
        num_scalar_prefetch=1,
        grid=(bsz, N_EXPERTS // eh_n, nchunk // sub),
        in_specs=[pl.BlockSpec((1, sub * ROUTE_CHUNK // LANES, eh_n, LANES), lambda b, eh, ci, base: (b, ci, eh, 0)),
                  pl.BlockSpec((1, sub * ROUTE_CHUNK, D_MODEL), lambda b, eh, ci, base: (b, ci, 0))],
        out_specs=pl.BlockSpec((1, eh_n, sp, D_MODEL), lambda b, eh, ci, base: (b, eh, 0, 0)))
    return pl.pallas_call(
        functools.partial(_dispatch_body, nchunk=nchunk, sub=sub, eh_n=eh_n),
        out_shape=jax.ShapeDtypeStruct((bsz, N_EXPERTS, sp, D_MODEL), BF16),
        grid_spec=grid_spec,
        compiler_params=_cparams(("parallel", "parallel", "arbitrary")),
        name="dispatch",
    )(base_flat, slott, fin)


def _experts_body(xe_ref, wg_ref, wu_ref, wd_ref, y_ref, wg16, wu16, wd16, *, cap):
    @pl.when(pl.program_id(1) == 0)
    def _():
        wg16[...] = wg_ref[0].astype(BF16)
        wu16[...] = wu_ref[0].astype(BF16)
        wd16[...] = wd_ref[0].astype(BF16)

    x = xe_ref[0, 0, 0:cap, :]
    ft = 256
    acc = None
    for f in range(EXPERT_FF // ft):
        cols = slice(f * ft, (f + 1) * ft)
        hid = (_silu(_dot(x, wg16[:, cols])) * _dot(x, wu16[:, cols])).astype(BF16)
        part = _dot(hid, wd16[cols, :])
        acc = part if acc is None else acc + part
    y_ref[0, 0, 0:cap, :] = acc.astype(BF16)
    y_ref[0, 0, cap:, :] = jnp.zeros((y_ref.shape[2] - cap, D_MODEL), BF16)


def _experts_call(xe, w_gate, w_up, w_down, cap):
    bsz, _, sp, _ = xe.shape
    wspec = lambda shape: pl.BlockSpec((1,) + shape, lambda e, b: (e, 0, 0))
    slots = pl.BlockSpec((1, 1, sp, D_MODEL), lambda e, b: (b, e, 0, 0))
    return pl.pallas_call(
        functools.partial(_experts_body, cap=cap),
        out_shape=jax.ShapeDtypeStruct(xe.shape, BF16),
        grid=(N_EXPERTS, bsz),
        in_specs=[slots, wspec((D_MODEL, EXPERT_FF)), wspec((D_MODEL, EXPERT_FF)), wspec((EXPERT_FF, D_MODEL))],
        out_specs=slots,
        scratch_shapes=[pltpu.VMEM((D_MODEL, EXPERT_FF), BF16), pltpu.VMEM((D_MODEL, EXPERT_FF), BF16),
                        pltpu.VMEM((EXPERT_FF, D_MODEL), BF16)],
        compiler_params=_cparams(("arbitrary", "arbitrary")),
        name="experts",
    )(xe, w_gate, w_up, w_down)


SLOT_SPLIT = 32


def _combine_selectors():
    k = jnp.arange(LANES)[:, None]
    e_of_lane = jnp.arange(N_EXPERTS * SLOT_WIN)[None, :] // SLOT_WIN
    sel_gate = (k == e_of_lane).astype(BF16)
    sel_slot = (SLOT_SPLIT * (k == e_of_lane) + (k - N_EXPERTS == e_of_lane)).astype(BF16)
    return sel_slot, sel_gate


def _combine_body(base_ref, slot_ref, aff_ref, h_ref, y_ref, mod_ref, fng_ref, ssel_ref, gsel_ref, o_ref, acc_ref,
                  *, nchunk, sub):
    b, ci = pl.program_id(0), pl.program_id(1)
    rc = ROUTE_CHUNK
    width = WIN_GROUP * SLOT_WIN
    lane = lax.broadcasted_iota(I32, (1, width), 1)
    lane_el = lane >> 6
    lane_j = (lane & (SLOT_WIN - 1)).astype(F32)
    lane128 = lax.broadcasted_iota(I32, (rc, LANES), 1)
    for sc in range(sub):
        rows = slice(sc * rc, (sc + 1) * rc)
        flat0 = (b * (nchunk + 1) + ci * sub + sc) * N_EXPERTS
        s1 = slot_ref[0, rows, :] + 1
        halves = jnp.where(lane128 < N_EXPERTS, s1 >> (SLOT_SPLIT.bit_length() - 1),
                           pltpu.roll(s1 & (SLOT_SPLIT - 1), N_EXPERTS, 1))
        slot1 = _dot(halves.astype(F32).astype(BF16), ssel_ref[...])
        gates = _dot(aff_ref[0, rows, :].astype(BF16), gsel_ref[...])
        acc_ref[...] = jnp.zeros_like(acc_ref)
        for g in range(N_EXPERTS // WIN_GROUP):
            experts = [g * WIN_GROUP + el for el in range(WIN_GROUP)]
            starts, rounds = _window_plan(base_ref, flat0, experts)
            first = jnp.zeros((1, width), I32)
            for el in range(WIN_GROUP):
                first = jnp.where(lane_el == el, starts[el] + 1, first)
            first = first.astype(F32)
            tg = slot1[:, g * width:(g + 1) * width]
            gg = gates[:, g * width:(g + 1) * width]

            def one_round(r, carry, experts=experts, starts=starts, first=first, tg=tg, gg=gg):
                ywin = jnp.concatenate(
                    [y_ref[0, e, pl.ds(pl.multiple_of(starts[el] + r * SLOT_WIN, SLOT_ALIGN), SLOT_WIN), :]
                     for el, e in enumerate(experts)], axis=0)
                held = first + (r * SLOT_WIN).astype(F32) + lane_j
                s = jnp.where(tg == held, gg, 0.0).astype(BF16)
                acc_ref[...] += _dot(s, ywin)
                return carry

            lax.fori_loop(0, rounds, one_round, 0)
        hl = h_ref[0, rows, :] + mod_ref[0, 5:6, :] * acc_ref[...]
        ms = jnp.mean(hl * hl, axis=-1, keepdims=True)
        o_ref[0, rows, :] = hl * lax.rsqrt(ms + NORM_EPS) * fng_ref[...]


def _combine_call(base_flat, slot, aff, h, y, mod3, fng):
    bsz, t, _ = h.shape
    nchunk = t // ROUTE_CHUNK
    sub = 2
    rc = ROUTE_CHUNK
    ssel, gsel = _combine_selectors()
    tok = lambda w: pl.BlockSpec((1, sub * rc, w), lambda b, i, base: (b, i, 0))
    full = lambda a: pl.BlockSpec(a.shape, lambda b, i, base: (0,) * a.ndim)
    grid_spec = pltpu.PrefetchScalarGridSpec(
        num_scalar_prefetch=1,
        grid=(bsz, nchunk // sub),
        in_specs=[tok(LANES), tok(LANES), tok(D_MODEL),
                  pl.BlockSpec((1,) + y.shape[1:], lambda b, i, base: (b, 0, 0, 0), pipeline_mode=pl.Buffered(1)),
                  pl.BlockSpec((1, N_MOD, D_MODEL), lambda b, i, base: (b, 0, 0)),
                  full(fng), full(ssel), full(gsel)],
        out_specs=tok(D_MODEL),
        scratch_shapes=[pltpu.VMEM((rc, D_MODEL), F32)])
    return pl.pallas_call(
        functools.partial(_combine_body, nchunk=nchunk, sub=sub),
        out_shape=jax.ShapeDtypeStruct(h.shape, F32),
        grid_spec=grid_spec,
        compiler_params=_cparams(("parallel", "arbitrary")),
        name="combine",
    )(base_flat, slot, aff, h, y, mod3, fng, ssel, gsel)


def _pad_lanes(a):
    return jnp.pad(a, ((0, 0), (0, LANES - a.shape[1])))


def kernel(x, c, ctx, c_ctx, w_mod, b_mod, norm1_g, norm2_g, w_in, conv_w, a_log, dt_bias, gdn_norm_g, gm_norm_g,
           gm_ws, gm_bs, w_out, w_router, b_router, w_gate, w_up, w_down, final_norm_g):
    bsz, t, _ = x.shape
    ctx_len = ctx.shape[1]
    assert w_mod.shape[0] == 1, "single-layer problem"
    assert t % 512 == 0 and ctx_len % GDN_CHUNK == 0 and bsz < 8
    cap = EC_CAPACITY * t // N_EXPERTS

    cs = jnp.zeros((8, D_MODEL), F32).at[:bsz].set(c).at[bsz].set(c_ctx)
    mod3 = _mod_call(cs, w_mod[0], b_mod[0][None, :]).reshape(8, N_MOD, D_MODEL)

    wl = w_in[0]
    n_state = QKV_W + STATE_COLS
    w_state = _pad_lanes(wl[:, QKV_W:n_state])
    w_lat = jnp.concatenate([wl[:, :QKV_W], wl[:, n_state:n_state + B_W], wl[:, n_state + B_W:], w_state],
                            axis=1).astype(BF16)
    w_ctx = jnp.concatenate([wl[:, :QKV_W], w_state], axis=1).astype(BF16)
    gp = jnp.zeros((8, LANES), F32).at[0, :2 * N_HEADS].set(a_log[0].reshape(-1)).at[1, :2 * N_HEADS].set(
        dt_bias[0].reshape(-1))
    g1 = norm1_g[0][None, :]
    cw = jnp.zeros((8, QKV_W), F32).at[:conv_w.shape[1]].set(conv_w[0])

    qkv_c, gb_c = _inproj_ctx_call(ctx, mod3, bsz, g1, w_ctx, gp, cw)
    zero_state = jnp.zeros((bsz, N_HEADS, HEAD_DIM, HEAD_DIM), F32)
    _, _, s_f, s_b = _gdn_call(qkv_c, gb_c, zero_state, zero_state, ctx_len)

    qkv, gb, z, ya = _inproj_lat_call(x, mod3, g1, w_lat, gp, cw, gm_norm_g[0][None, :], gm_ws[0].astype(BF16),
                                      _pad_lanes(gm_bs[0].T), 512)
    o_f, o_b, _, _ = _gdn_call(qkv, gb, s_f, s_b, 512)
    wr = _pad_lanes(w_router[0])
    wr_hi = wr.astype(BF16)
    wr_lo = (wr - wr_hi.astype(F32)).astype(BF16)
    br = jnp.full((1, LANES), -1e30, F32).at[0, :N_EXPERTS].set(b_router[0])
    h, fin, aff, afft = _mixout_call(x, o_f, o_b, z, ya, mod3, gdn_norm_g[0][None, :], w_out[0].astype(BF16),
                                     norm2_g[0][None, :], wr_hi, wr_lo, br, 512)

    npieces = t // LANES
    slott, slot, off = _route_call(afft.reshape(bsz, npieces * N_EXPERTS, LANES), cap)
    base = off[:, :, 0].reshape(bsz, npieces, N_EXPERTS)[:, ::ROUTE_CHUNK // LANES, :]
    base_flat = jnp.concatenate([base, jnp.full((bsz, 1, N_EXPERTS), cap, I32)], axis=1).reshape(-1)
    xe = _dispatch_call(base_flat, slott.reshape(bsz, npieces, N_EXPERTS, LANES), fin, cap)
    y = _experts_call(xe, w_gate[0], w_up[0], w_down[0], cap)
    return _combine_call(base_flat, slot, aff, h, y, mod3, final_norm_g[None, :])
```

```python
import functools

import jax
import jax.numpy as jnp
from jax import lax
from jax.experimental import pallas as pl
from jax.experimental.pallas import tpu as pltpu

F32 = jnp.float32
BF16 = jnp.bfloat16
I32 = jnp.int32

D_MODEL = 1024
N_MOD = 6
N_HEADS = 4
HEAD_DIM = 128
B_W = N_HEADS * HEAD_DIM
QKV_W = 3 * B_W
A_W = 512
A_GROUPS = 4
A_CHUNK = 128
GDN_CHUNK = 64
N_EXPERTS = 16
EC_CAPACITY = 2
EXPERT_FF = 1024
NORM_EPS = 1e-6
LANES = 128
STATE_COLS = 4 * N_HEADS

ROUTE_CHUNK = 256
SLOT_WIN = 64
SLOT_ALIGN = 16
VMEM_LIMIT = 56 * 1024 * 1024


def _cparams(sem):
    return pltpu.CompilerParams(dimension_semantics=sem, vmem_limit_bytes=VMEM_LIMIT)


def _dot(a, b):
    return jnp.dot(a, b, preferred_element_type=F32)


def _dot_nt(a, b):
    return lax.dot_general(a, b, (((1,), (1,)), ((), ())), preferred_element_type=F32)


def _dot_tn(a, b):
    return lax.dot_general(a, b, (((0,), (0,)), ((), ())), preferred_element_type=F32)


def _silu(x):
    return x * jax.nn.sigmoid(x)


def _mod_body(c_ref, w_ref, b_ref, o_ref):
    s = _silu(c_ref[...])
    o_ref[...] = _dot(s.astype(BF16), w_ref[...].astype(BF16)) + b_ref[...]


def _mod_call(cs, w_mod, b_mod):
    n = w_mod.shape[1] // D_MODEL
    return pl.pallas_call(
        _mod_body,
        out_shape=jax.ShapeDtypeStruct((8, w_mod.shape[1]), F32),
        grid=(n,),
        in_specs=[pl.BlockSpec((8, D_MODEL), lambda j: (0, 0)),
                  pl.BlockSpec((D_MODEL, D_MODEL), lambda j: (0, j)),
                  pl.BlockSpec((1, D_MODEL), lambda j: (0, j))],
        out_specs=pl.BlockSpec((8, D_MODEL), lambda j: (0, j)),
        compiler_params=_cparams(("arbitrary",)),
        name="mod",
    )(cs, w_mod, b_mod)


def _norm_mod(x, g, shift, scale):
    ms = jnp.mean(x * x, axis=-1, keepdims=True)
    return (x * lax.rsqrt(ms + NORM_EPS) * g) * (1.0 + scale) + shift


def _gate_streams(st, gp_ref):
    lane = lax.broadcasted_iota(I32, st.shape, 1)
    g = -jnp.exp(gp_ref[0:1, :]) * jax.nn.softplus(st + gp_ref[1:2, :])
    beta = jax.nn.sigmoid(st)
    return jnp.where(lane < 2 * N_HEADS, g, jnp.where(lane < STATE_COLS, beta, 0.0))


def _conv_qkv(qkv, prev_row, next_row, cw_ref, out_ref, tm):
    cs = GDN_CHUNK
    nsub = tm // cs
    w0, w1, w2 = cw_ref[0:1, :], cw_ref[1:2, :], cw_ref[2:3, :]
    row = lax.broadcasted_iota(I32, (cs, 1), 0)
    for c in range(nsub):
        rows = slice(c * cs, (c + 1) * cs)
        x = qkv[rows]
        prow = prev_row if c == 0 else qkv[c * cs - 1:c * cs]
        nrow = next_row if c == nsub - 1 else qkv[(c + 1) * cs:(c + 1) * cs + 1]
        xp = jnp.where(row == 0, prow, pltpu.roll(x, 1, 0))
        xn = jnp.where(row == cs - 1, nrow, pltpu.roll(x, cs - 1, 0))
        y = _silu(xp * w0 + x * w1 + xn * w2)
        for h in range(N_HEADS):
            cq = slice(h * HEAD_DIM, (h + 1) * HEAD_DIM)
            ck = slice(B_W + h * HEAD_DIM, B_W + (h + 1) * HEAD_DIM)
            q = y[:, cq]
            k = y[:, ck]
            out_ref[0, rows, cq] = (q * (lax.rsqrt(jnp.sum(q * q, axis=-1, keepdims=True) + NORM_EPS)
                                         * (HEAD_DIM ** -0.5))).astype(BF16)
            out_ref[0, rows, ck] = (k * lax.rsqrt(jnp.sum(k * k, axis=-1, keepdims=True) + NORM_EPS)).astype(BF16)
        out_ref[0, rows, 2 * B_W:3 * B_W] = y[:, 2 * B_W:3 * B_W].astype(BF16)


def _inproj_lat_body(x_ref, xp_ref, xn_ref, mod_ref, g1_ref, w_ref, gp_ref, cw_ref, gmg_ref, ws_ref, bst_ref,
                     qkv_ref, gb_ref, z_ref, ya_ref, *, tm):
    i = pl.program_id(1)
    shift, scale = mod_ref[0, 0:1, :], mod_ref[0, 1:2, :]
    a = _norm_mod(x_ref[0], g1_ref[...], shift, scale).astype(BF16)
    xh = jnp.concatenate([xp_ref[0], xn_ref[0]], axis=0)
    halo = _dot(_norm_mod(xh, g1_ref[...], shift, scale).astype(BF16), w_ref[:, 0:QKV_W])
    prev_row = jnp.where(i == 0, 0.0, halo[7:8, :])
    next_row = jnp.where(i == pl.num_programs(1) - 1, 0.0, halo[8:9, :])
    _conv_qkv(_dot(a, w_ref[:, 0:QKV_W]), prev_row, next_row, cw_ref, qkv_ref, tm)
    z_ref[0] = _dot(a, w_ref[:, QKV_W:QKV_W + B_W]).astype(BF16)
    c_uv = QKV_W + B_W
    gb_ref[0] = _gate_streams(_dot(a, w_ref[:, c_uv + 2 * A_W:c_uv + 2 * A_W + LANES]), gp_ref)
    uv = _dot(a, w_ref[:, c_uv:c_uv + 2 * A_W])
    uv = 0.5 * uv * (1.0 + lax.erf(uv * 0.7071067811865476))
    gd = A_W // A_GROUPS
    for grp in range(A_GROUPS):
        v = uv[:, A_W + grp * gd:A_W + (grp + 1) * gd]
        vn = v * lax.rsqrt(jnp.mean(v * v, axis=-1, keepdims=True) + NORM_EPS) * gmg_ref[:, grp * gd:(grp + 1) * gd]
        vn = vn.astype(BF16)
        bias = bst_ref[:, grp:grp + 1]
        for c in range(tm // A_CHUNK):
            rows = slice(c * A_CHUNK, (c + 1) * A_CHUNK)
            s = _dot(ws_ref[grp], vn[rows]) + bias
            ya_ref[0, rows, grp * gd:(grp + 1) * gd] = (uv[rows, grp * gd:(grp + 1) * gd] * s).astype(BF16)


def _inproj_ctx_body(x_ref, mod_ref, g1_ref, w_ref, gp_ref, cw_ref, qkv_ref, gb_ref, *, tm):
    a = _norm_mod(x_ref[0], g1_ref[...], mod_ref[0, 0:1, :], mod_ref[0, 1:2, :]).astype(BF16)
    edge = jnp.zeros((1, QKV_W), F32)
    _conv_qkv(_dot(a, w_ref[:, 0:QKV_W]), edge, edge, cw_ref, qkv_ref, tm)
    gb_ref[0] = _gate_streams(_dot(a, w_ref[:, QKV_W:QKV_W + LANES]), gp_ref)


def _inproj_lat_call(x, mod3, g1, w_lat, gp, cw, gmg, ws16, bst, tm):
    bsz, t, _ = x.shape
    hb = tm // 8
    last8 = t // 8 - 1
    full = lambda a: pl.BlockSpec(a.shape, lambda b, i: (0,) * a.ndim)
    tok = lambda w: pl.BlockSpec((1, tm, w), lambda b, i: (b, i, 0))
    return pl.pallas_call(
        functools.partial(_inproj_lat_body, tm=tm),
        out_shape=(jax.ShapeDtypeStruct((bsz, t, QKV_W), BF16),
                   jax.ShapeDtypeStruct((bsz, t, LANES), F32),
                   jax.ShapeDtypeStruct((bsz, t, B_W), BF16),
                   jax.ShapeDtypeStruct((bsz, t, A_W), BF16)),
        grid=(bsz, t // tm),
        in_specs=[tok(D_MODEL),
                  pl.BlockSpec((1, 8, D_MODEL), lambda b, i: (b, jnp.maximum(i * hb - 1, 0), 0)),
                  pl.BlockSpec((1, 8, D_MODEL), lambda b, i: (b, jnp.minimum((i + 1) * hb, last8), 0)),
                  pl.BlockSpec((1, N_MOD, D_MODEL), lambda b, i: (b, 0, 0)),
                  full(g1), full(w_lat), full(gp), full(cw), full(gmg), full(ws16), full(bst)],
        out_specs=(tok(QKV_W), tok(LANES), tok(B_W), tok(A_W)),
        compiler_params=_cparams(("parallel", "arbitrary")),
        name="inproj_lat",
    )(x, x, x, mod3, g1, w_lat, gp, cw, gmg, ws16, bst)


def _inproj_ctx_call(ctx, mod3, ctx_row, g1, w_ctx, gp, cw):
    bsz, t, _ = ctx.shape
    full = lambda a: pl.BlockSpec(a.shape, lambda b: (0,) * a.ndim)
    tok = lambda w: pl.BlockSpec((1, t, w), lambda b: (b, 0, 0))
    return pl.pallas_call(
        functools.partial(_inproj_ctx_body, tm=t),
        out_shape=(jax.ShapeDtypeStruct((bsz, t, QKV_W), BF16),
                   jax.ShapeDtypeStruct((bsz, t, LANES), F32)),
        grid=(bsz,),
        in_specs=[tok(D_MODEL),
                  pl.BlockSpec((1, N_MOD, D_MODEL), lambda b: (ctx_row, 0, 0)),
                  full(g1), full(w_ctx), full(gp), full(cw)],
        out_specs=(tok(QKV_W), tok(LANES)),
        compiler_params=_cparams(("parallel",)),
        name="inproj_ctx",
    )(ctx, mod3, g1, w_ctx, gp, cw)


GDN_CHUNKS_PER_STEP = 2


def _gdn_local(chains):
    cs = GDN_CHUNK
    for c in chains:
        c["kb"] = c["k"] * c["beta"]
        c["k16"] = c["k"].astype(BF16)
        c["decay"] = jnp.where(c["incl"], jnp.exp(jnp.where(c["incl"], c["gc"] - c["gc_row"], 0.0)), 0.0)
    for c in chains:
        c["kk"] = _dot_nt(jnp.concatenate([c["kb"], c["q"]], axis=0).astype(BF16), c["k16"])
    for c in chains:
        c["a"] = jnp.where(c["strict"], c["kk"][:cs] * c["decay"], 0.0)
        c["attn"] = (c["kk"][cs:] * c["decay"]).astype(BF16)
        c["m"] = -jnp.where(c["levels"][0], c["a"], 0.0)
    for li in range(1, len(chains[0]["levels"])):
        for c in chains:
            c["m16"] = c["m"].astype(BF16)
            c["cm"] = jnp.where(c["levels"][li], c["a"], 0.0)
        for c in chains:
            c["x"] = c["cm"] + _dot(c["m16"], c["cm"].astype(BF16))
        for c in chains:
            c["y"] = c["x"] + _dot(c["x"].astype(BF16), c["m16"])
        for c in chains:
            c["m"] = c["m"] - c["y"]
    for c in chains:
        c["rhs"] = jnp.concatenate([c["v"] * c["beta"], c["kb"] * c["egc"]], axis=1)
    for c in chains:
        uw = c["rhs"] + _dot(c["m"].astype(BF16), c["rhs"].astype(BF16))
        c["u"] = uw[:, :HEAD_DIM]
        c["wq_lhs"] = jnp.concatenate([uw[:, HEAD_DIM:], c["q"] * c["egc"]], axis=0).astype(BF16)
        c["kg"] = (c["k"] * jnp.exp(c["glast"] - c["gc"])).astype(BF16)
        c["eg"] = jnp.exp(c["glast"])


def _gdn_sequential(chains):
    cs = GDN_CHUNK
    for c in chains:
        c["s"] = c["s_ref"][c["h"]]
        c["wq"] = _dot(c["wq_lhs"], c["s"].astype(BF16))
    for c in chains:
        c["v_new"] = (c["u"] - c["wq"][:cs]).astype(BF16)
    for c in chains:
        c["o"] = c["wq"][cs:] + _dot(c["attn"], c["v_new"])
    for c in chains:
        c["s_ref"][c["h"]] = c["s"] * c["eg"] + _dot_tn(c["kg"], c["v_new"])
    for c in chains:
        c["o_ref"][0, pl.ds(c["r0"], cs), c["cols"]] = c["o"].astype(BF16)


def _gdn_body(qf_ref, qb_ref, gbf_ref, gbb_ref, s0f_ref, s0b_ref,
              of_ref, ob_ref, sff_ref, sfb_ref, sf_scr, sb_scr, *, tb, nt):
    t = pl.program_id(1)
    cs = GDN_CHUNK
    nch = tb // cs
    cps = GDN_CHUNKS_PER_STEP

    @pl.when(t == 0)
    def _():
        sf_scr[...] = s0f_ref[0]
        sb_scr[...] = s0b_ref[0]

    ii = lax.broadcasted_iota(I32, (cs, cs), 0)
    jj = lax.broadcasted_iota(I32, (cs, cs), 1)
    incl_f, strict_f = jj <= ii, jj < ii
    incl_b, strict_b = jj >= ii, jj > ii
    levels = []
    sh = 0
    while (1 << sh) < cs:
        levels.append(((ii >> (sh + 1)) == (jj >> (sh + 1))) & ((ii >> sh) != (jj >> sh)))
        sh += 1
    lv_f = [lm & strict_f for lm in levels]
    lv_b = [lm & strict_b for lm in levels]
    row = lax.broadcasted_iota(I32, (cs, LANES), 0)

    def step(n, carry):
        groups = []
        for j in range(cps):
            rf = pl.multiple_of((n * cps + j) * cs, cs)
            rb = pl.multiple_of((nch - 1 - n * cps - j) * cs, cs)
            gf = gbf_ref[0, pl.ds(rf, cs), :]
            gb = gbb_ref[0, pl.ds(rb, cs), :]
            cf, cb = gf, gb
            s = 1
            while s < cs:
                cf = cf + jnp.where(row >= s, pltpu.roll(cf, s, 0), 0.0)
                cb = cb + jnp.where(row < cs - s, pltpu.roll(cb, cs - s, 0), 0.0)
                s *= 2
            gt = jnp.concatenate([cf, cb], axis=0).T
            ecf, ecb = jnp.exp(cf), jnp.exp(cb)
            chains = []
            for h in range(N_HEADS):
                cq = slice(h * HEAD_DIM, (h + 1) * HEAD_DIM)
                ck = slice(B_W + h * HEAD_DIM, B_W + (h + 1) * HEAD_DIM)
                cv = slice(2 * B_W + h * HEAD_DIM, 2 * B_W + (h + 1) * HEAD_DIM)
                lf, lb = h, N_HEADS + h
                ld = lambda ref, r0, cols: ref[0, pl.ds(r0, cs), cols].astype(F32)
                chains.append(dict(
                    q=ld(qf_ref, rf, cq), k=ld(qf_ref, rf, ck), v=ld(qf_ref, rf, cv),
                    gc=cf[:, lf:lf + 1], gc_row=gt[lf:lf + 1, 0:cs], beta=gf[:, 2 * N_HEADS + h:2 * N_HEADS + h + 1],
                    egc=ecf[:, lf:lf + 1], glast=cf[cs - 1:cs, lf:lf + 1], incl=incl_f, strict=strict_f, levels=lv_f,
                    s_ref=sf_scr, h=h, o_ref=of_ref, r0=rf, cols=cq))
                chains.append(dict(
                    q=ld(qb_ref, rb, cq), k=ld(qb_ref, rb, ck), v=ld(qb_ref, rb, cv),
                    gc=cb[:, lb:lb + 1], gc_row=gt[lb:lb + 1, cs:2 * cs],
                    beta=gb[:, 3 * N_HEADS + h:3 * N_HEADS + h + 1],
                    egc=ecb[:, lb:lb + 1], glast=cb[0:1, lb:lb + 1], incl=incl_b, strict=strict_b, levels=lv_b,
                    s_ref=sb_scr, h=h, o_ref=ob_ref, r0=rb, cols=cq))
            groups.append(chains)
        _gdn_local([c for chains in groups for c in chains])
        for chains in groups:
            _gdn_sequential(chains)
        return carry

    lax.fori_loop(0, nch // cps, step, 0)

    @pl.when(t == nt - 1)
    def _():
        sff_ref[0] = sf_scr[...]
        sfb_ref[0] = sb_scr[...]


def _gdn_call(qkv, gb, s0f, s0b, tb):
    bsz, t, _ = qkv.shape
    nt = t // tb
    assert (tb // GDN_CHUNK) % GDN_CHUNKS_PER_STEP == 0
    st = pl.BlockSpec((1, N_HEADS, HEAD_DIM, HEAD_DIM), lambda b, i: (b, 0, 0, 0))
    fwd = lambda w: pl.BlockSpec((1, tb, w), lambda b, i: (b, i, 0))
    bwd = lambda w: pl.BlockSpec((1, tb, w), lambda b, i: (b, nt - 1 - i, 0))
    return pl.pallas_call(
        functools.partial(_gdn_body, tb=tb, nt=nt),
        out_shape=(jax.ShapeDtypeStruct((bsz, t, B_W), BF16), jax.ShapeDtypeStruct((bsz, t, B_W), BF16),
                   jax.ShapeDtypeStruct((bsz, N_HEADS, HEAD_DIM, HEAD_DIM), F32),
                   jax.ShapeDtypeStruct((bsz, N_HEADS, HEAD_DIM, HEAD_DIM), F32)),
        grid=(bsz, nt),
        in_specs=[fwd(QKV_W), bwd(QKV_W), fwd(LANES), bwd(LANES), st, st],
        out_specs=(fwd(B_W), bwd(B_W), st, st),
        scratch_shapes=[pltpu.VMEM((N_HEADS, HEAD_DIM, HEAD_DIM), F32), pltpu.VMEM((N_HEADS, HEAD_DIM, HEAD_DIM), F32)],
        compiler_params=_cparams(("parallel", "arbitrary")),
        name="gdn",
    )(qkv, qkv, gb, gb, s0f, s0b)


def _mixout_body(x_ref, of_ref, ob_ref, z_ref, ya_ref, mod_ref, gng_ref, wout_ref, n2g_ref, wrh_ref, wrl_ref, br_ref,
                 h_ref, fin_ref, aff_ref, afft_ref, *, tm):
    o = of_ref[0].astype(F32) + ob_ref[0].astype(F32)
    z = z_ref[0].astype(F32)
    parts = [ya_ref[0]]
    for h in range(N_HEADS):
        c = slice(h * HEAD_DIM, (h + 1) * HEAD_DIM)
        oh = o[:, c]
        y = oh * lax.rsqrt(jnp.mean(oh * oh, axis=-1, keepdims=True) + NORM_EPS)
        parts.append((y * gng_ref[...] * _silu(z[:, c])).astype(BF16))
    mix = _dot(jnp.concatenate(parts, axis=1), wout_ref[...])
    hl = x_ref[0] + mod_ref[0, 2:3, :] * mix
    h_ref[0] = hl
    fin = _norm_mod(hl, n2g_ref[...], mod_ref[0, 3:4, :], mod_ref[0, 4:5, :])
    f_hi = fin.astype(BF16)
    fin_ref[0] = f_hi
    f_lo = (fin - f_hi.astype(F32)).astype(BF16)
    logits = _dot(f_hi, wrh_ref[...]) + _dot(f_lo, wrh_ref[...]) + _dot(f_hi, wrl_ref[...]) + br_ref[...]
    e = jnp.exp(logits - jnp.max(logits, axis=-1, keepdims=True))
    aff = e / jnp.sum(e, axis=-1, keepdims=True)
    aff_ref[0] = aff
    for j in range(tm // LANES):
        afft_ref[0, j] = aff[j * LANES:(j + 1) * LANES, :].T[0:N_EXPERTS, :]


def _mixout_call(x, o_f, o_b, z, ya, mod3, gng, wout16, n2g, wr_hi, wr_lo, br, tm):
    bsz, t, _ = x.shape
    full = lambda a: pl.BlockSpec(a.shape, lambda b, i: (0,) * a.ndim)
    tok = lambda w: pl.BlockSpec((1, tm, w), lambda b, i: (b, i, 0))
    return pl.pallas_call(
        functools.partial(_mixout_body, tm=tm),
        out_shape=(jax.ShapeDtypeStruct((bsz, t, D_MODEL), F32), jax.ShapeDtypeStruct((bsz, t, D_MODEL), BF16),
                   jax.ShapeDtypeStruct((bsz, t, LANES), F32),
                   jax.ShapeDtypeStruct((bsz, t // LANES, N_EXPERTS, LANES), F32)),
        grid=(bsz, t // tm),
        in_specs=[tok(D_MODEL), tok(B_W), tok(B_W), tok(B_W), tok(A_W),
                  pl.BlockSpec((1, N_MOD, D_MODEL), lambda b, i: (b, 0, 0)),
                  full(gng), full(wout16), full(n2g), full(wr_hi), full(wr_lo), full(br)],
        out_specs=(tok(D_MODEL), tok(D_MODEL), tok(LANES),
                   pl.BlockSpec((1, tm // LANES, N_EXPERTS, LANES), lambda b, i: (b, i, 0, 0))),
        compiler_params=_cparams(("parallel", "parallel")),
        name="mixout",
    )(x, o_f, o_b, z, ya, mod3, gng, wout16, n2g, wr_hi, wr_lo, br)


def _route_body(afft_ref, slott_ref, slot_ref, off_ref, *, t, cap):
    ne = N_EXPERTS
    npieces = t // LANES
    rows = npieces * ne

    def count(thr_col, strict):
        acc = jnp.zeros((ne, LANES), I32)
        for p in range(npieces):
            piece = afft_ref[0, p * ne:(p + 1) * ne, :]
            acc = acc + (piece > thr_col if strict else piece >= thr_col).astype(I32)
        return jnp.sum(acc, axis=1, keepdims=True)

    def search(i, thr):
        cand = thr | jnp.left_shift(jnp.int32(1), 30 - i)
        return jnp.where(count(pltpu.bitcast(cand, F32), False) >= cap, cand, thr)

    thr_bits = lax.fori_loop(0, 31, search, jnp.zeros((ne, 1), I32))
    thr = pltpu.bitcast(thr_bits, F32)
    need = (cap - count(thr, True)).astype(F32)

    x = afft_ref[0]
    thr_rows = jnp.concatenate([thr] * npieces, axis=0)
    need_rows = jnp.concatenate([need] * npieces, axis=0)
    gt = x > thr_rows
    eq = x == thr_rows
    ti = lax.broadcasted_iota(I32, (LANES, LANES), 0)
    tj = lax.broadcasted_iota(I32, (LANES, LANES), 1)
    triu = (ti <= tj).astype(BF16)
    ri = lax.broadcasted_iota(I32, (rows, rows), 0)
    rj = lax.broadcasted_iota(I32, (rows, rows), 1)
    earlier = (((ri & (ne - 1)) == (rj & (ne - 1))) & (rj < ri)).astype(BF16)

    def prefix(mask):
        inpiece = _dot(mask.astype(BF16), triu)
        total = jnp.broadcast_to(inpiece[:, LANES - 1:LANES], (rows, LANES)).astype(BF16)
        offset = _dot(earlier, total)
        return inpiece + offset, offset

    eq_rank, _ = prefix(eq)
    sel = gt | (eq & (eq_rank <= need_rows))
    sel_rank, sel_off = prefix(sel)
    slot = jnp.where(sel, sel_rank - 1.0, -1.0)
    slott_ref[0] = slot.astype(I32)
    off_ref[0] = sel_off.astype(I32)
    pad = jnp.zeros((LANES - ne, LANES), F32)
    for p in range(npieces):
        piece = jnp.concatenate([slot[p * ne:(p + 1) * ne, :], pad], axis=0)
        slot_ref[0, p * LANES:(p + 1) * LANES, :] = piece.T.astype(I32)


def _route_call(afft, cap):
    bsz, rows, _ = afft.shape
    t = rows // N_EXPERTS * LANES
    spec = lambda r: pl.BlockSpec((1, r, LANES), lambda b: (b, 0, 0))
    return pl.pallas_call(
        functools.partial(_route_body, t=t, cap=cap),
        out_shape=(jax.ShapeDtypeStruct((bsz, rows, LANES), I32),
                   jax.ShapeDtypeStruct((bsz, t, LANES), I32),
                   jax.ShapeDtypeStruct((bsz, rows, LANES), I32)),
        grid=(bsz,),
        in_specs=[spec(rows)],
        out_specs=(spec(rows), spec(t), spec(rows)),
        compiler_params=_cparams(("parallel",)),
        name="route",
    )(afft)


def _window_plan(base_ref, flat0, experts):
    starts, rounds = [], jnp.int32(0)
    for e in experts:
        lo = base_ref[flat0 + e]
        hi = base_ref[flat0 + N_EXPERTS + e]
        lo_al = (lo >> 4) << 4
        starts.append(lo_al)
        rounds = jnp.maximum(rounds, (hi - lo_al + SLOT_WIN - 1) // SLOT_WIN)
    return starts, rounds


def _dispatch_body(base_ref, slott_ref, fin_ref, xe_ref, *, nchunk, sub, eh_n):
    b, eh, ci = pl.program_id(0), pl.program_id(1), pl.program_id(2)
    rc = ROUTE_CHUNK

    @pl.when(ci == 0)
    def _():
        xe_ref[...] = jnp.zeros_like(xe_ref)

    srow = lax.broadcasted_iota(I32, (SLOT_WIN, rc), 0)
    for sc in range(sub):
        cc = ci * sub + sc
        flat0 = (b * (nchunk + 1) + cc) * N_EXPERTS + eh * eh_n
        f = fin_ref[0, sc * rc:(sc + 1) * rc, :]
        experts = list(range(eh_n))
        starts, rounds = _window_plan(base_ref, flat0, experts)

        def one_round(r, carry, starts=starts, f=f, sc=sc):
            rows = []
            for e in experts:
                tok_slot = jnp.concatenate(
                    [slott_ref[0, sc * (rc // LANES) + j, e:e + 1, :] for j in range(rc // LANES)], axis=1)
                rows.append((tok_slot == srow + (starts[e] + r * SLOT_WIN)).astype(BF16))
            prod = _dot(jnp.concatenate(rows, axis=0), f)
            for e in experts:
                win = pl.ds(pl.multiple_of(starts[e] + r * SLOT_WIN, SLOT_ALIGN), SLOT_WIN)
                xe_ref[0, e, win, :] = xe_ref[0, e, win, :] + prod[e * SLOT_WIN:(e + 1) * SLOT_WIN].astype(BF16)
            return carry

        one_round(jnp.int32(0), 0)
        lax.fori_loop(1, rounds, one_round, 0)


def _dispatch_call(base_flat, slott, fin, cap):
    bsz, t, _ = fin.shape
    nchunk = t // ROUTE_CHUNK
    sub = 2
    eh_n = N_EXPERTS // 2
    sp = cap + SLOT_WIN
    grid_spec = pltpu.PrefetchScalarGridSpec(
        num_scalar_prefetch=1,
        grid=(bsz, N_EXPERTS // eh_n, nchunk // sub),
        in_specs=[pl.BlockSpec((1, sub * ROUTE_CHUNK // LANES, eh_n, LANES), lambda b, eh, ci, base: (b, ci, eh, 0)),
                  pl.BlockSpec((1, sub * ROUTE_CHUNK, D_MODEL), lambda b, eh, ci, base: (b, ci, 0))],
        out_specs=pl.BlockSpec((1, eh_n, sp, D_MODEL), lambda b, eh, ci, base: (b, eh, 0, 0)))
    return pl.pallas_call(
        functools.partial(_dispatch_body, nchunk=nchunk, sub=sub, eh_n=eh_n),
        out_shape=jax.ShapeDtypeStruct((bsz, N_EXPERTS, sp, D_MODEL), BF16),
        grid_spec=grid_spec,
        compiler_params=_cparams(("parallel", "parallel", "arbitrary")),
        name="dispatch",
    )(base_flat, slott, fin)


def _experts_body(xe_ref, wg_ref, wu_ref, wd_ref, y_ref, wg16, wu16, wd16, *, cap):
    @pl.when(pl.program_id(1) == 0)
    def _():
        wg16[...] = wg_ref[0].astype(BF16)
        wu16[...] = wu_ref[0].astype(BF16)
        wd16[...] = wd_ref[0].astype(BF16)

    x = xe_ref[0, 0, 0:cap, :]
    ft = 256
    acc = None
    for f in range(EXPERT_FF // ft):
        cols = slice(f * ft, (f + 1) * ft)
        hid = (_silu(_dot(x, wg16[:, cols])) * _dot(x, wu16[:, cols])).astype(BF16)
        part = _dot(hid, wd16[cols, :])
        acc = part if acc is None else acc + part
    y_ref[0, 0, 0:cap, :] = acc.astype(BF16)
    y_ref[0, 0, cap:, :] = jnp.zeros((y_ref.shape[2] - cap, D_MODEL), BF16)


def _experts_call(xe, w_gate, w_up, w_down, cap):
    bsz, _, sp, _ = xe.shape
    wspec = lambda shape: pl.BlockSpec((1,) + shape, lambda e, b: (e, 0, 0))
    slots = pl.BlockSpec((1, 1, sp, D_MODEL), lambda e, b: (b, e, 0, 0))
    return pl.pallas_call(
        functools.partial(_experts_body, cap=cap),
        out_shape=jax.ShapeDtypeStruct(xe.shape, BF16),
        grid=(N_EXPERTS, bsz),
        in_specs=[slots, wspec((D_MODEL, EXPERT_FF)), wspec((D_MODEL, EXPERT_FF)), wspec((EXPERT_FF, D_MODEL))],
        out_specs=slots,
        scratch_shapes=[pltpu.VMEM((D_MODEL, EXPERT_FF), BF16), pltpu.VMEM((D_MODEL, EXPERT_FF), BF16),
                        pltpu.VMEM((EXPERT_FF, D_MODEL), BF16)],
        compiler_params=_cparams(("arbitrary", "arbitrary")),
        name="experts",
    )(xe, w_gate, w_up, w_down)


SLOT_SPLIT = 32


def _combine_selectors():
    k = jnp.arange(LANES)[:, None]
    e_of_lane = jnp.arange(N_EXPERTS * SLOT_WIN)[None, :] // SLOT_WIN
    sel_gate = (k == e_of_lane).astype(BF16)
    sel_slot = (SLOT_SPLIT * (k == e_of_lane) + (k - N_EXPERTS == e_of_lane)).astype(BF16)
    return sel_slot, sel_gate


def _combine_body(base_ref, slot_ref, aff_ref, h_ref, y_ref, mod_ref, fng_ref, ssel_ref, gsel_ref, o_ref, acc_ref,
                  *, nchunk, sub):
    b, ci = pl.program_id(0), pl.program_id(1)
    rc = ROUTE_CHUNK
    width = N_EXPERTS * SLOT_WIN
    lane = lax.broadcasted_iota(I32, (1, width), 1)
    lane_e = lane >> (SLOT_WIN.bit_length() - 1)
    lane_j = (lane & (SLOT_WIN - 1)).astype(F32)
    lane128 = lax.broadcasted_iota(I32, (rc, LANES), 1)
    experts = list(range(N_EXPERTS))
    for sc in range(sub):
        rows = slice(sc * rc, (sc + 1) * rc)
        flat0 = (b * (nchunk + 1) + ci * sub + sc) * N_EXPERTS
        s1 = slot_ref[0, rows, :] + 1
        halves = jnp.where(lane128 < N_EXPERTS, s1 >> (SLOT_SPLIT.bit_length() - 1),
                           pltpu.roll(s1 & (SLOT_SPLIT - 1), N_EXPERTS, 1))
        slot1 = _dot(halves.astype(F32).astype(BF16), ssel_ref[...])
        gates = _dot(aff_ref[0, rows, :].astype(BF16), gsel_ref[...])
        starts, rounds = _window_plan(base_ref, flat0, experts)
        first = jnp.zeros((1, width), I32)
        for e in experts:
            first = jnp.where(lane_e == e, starts[e] + 1, first)
        first = first.astype(F32) + lane_j

        def contribution(r, starts=starts, first=first, slot1=slot1, gates=gates):
            ywin = jnp.concatenate(
                [y_ref[0, e, pl.ds(pl.multiple_of(starts[e] + r * SLOT_WIN, SLOT_ALIGN), SLOT_WIN), :]
                 for e in experts], axis=0)
            s = jnp.where(slot1 == first + (r * SLOT_WIN).astype(F32), gates, 0.0).astype(BF16)
            return _dot(s, ywin)

        def extra_round(r, carry, contribution=contribution):
            acc_ref[...] += contribution(r)
            return carry

        acc_ref[...] = contribution(jnp.int32(0))
        lax.fori_loop(1, rounds, extra_round, 0)
        hl = h_ref[0, rows, :] + mod_ref[0, 5:6, :] * acc_ref[...]
        ms = jnp.mean(hl * hl, axis=-1, keepdims=True)
        o_ref[0, rows, :] = hl * lax.rsqrt(ms + NORM_EPS) * fng_ref[...]


def _combine_call(base_flat, slot, aff, h, y, mod3, fng):
    bsz, t, _ = h.shape
    nchunk = t // ROUTE_CHUNK
    sub = 2
    rc = ROUTE_CHUNK
    ssel, gsel = _combine_selectors()
    tok = lambda w: pl.BlockSpec((1, sub * rc, w), lambda b, i, base: (b, i, 0))
    full = lambda a: pl.BlockSpec(a.shape, lambda b, i, base: (0,) * a.ndim)
    grid_spec = pltpu.PrefetchScalarGridSpec(
        num_scalar_prefetch=1,
        grid=(bsz, nchunk // sub),
        in_specs=[tok(LANES), tok(LANES), tok(D_MODEL),
                  pl.BlockSpec((1,) + y.shape[1:], lambda b, i, base: (b, 0, 0, 0), pipeline_mode=pl.Buffered(1)),
                  pl.BlockSpec((1, N_MOD, D_MODEL), lambda b, i, base: (b, 0, 0)),
                  full(fng), full(ssel), full(gsel)],
        out_specs=tok(D_MODEL),
        scratch_shapes=[pltpu.VMEM((rc, D_MODEL), F32)])
    return pl.pallas_call(
        functools.partial(_combine_body, nchunk=nchunk, sub=sub),
        out_shape=jax.ShapeDtypeStruct(h.shape, F32),
        grid_spec=grid_spec,
        compiler_params=_cparams(("parallel", "arbitrary")),
        name="combine",
    )(base_flat, slot, aff, h, y, mod3, fng, ssel, gsel)


def _pad_lanes(a):
    return jnp.pad(a, ((0, 0), (0, LANES - a.shape[1])))


def kernel(x, c, ctx, c_ctx, w_mod, b_mod, norm1_g, norm2_g, w_in, conv_w, a_log, dt_bias, gdn_norm_g, gm_norm_g,
           gm_ws, gm_bs, w_out, w_router, b_router, w_gate, w_up, w_down, final_norm_g):
    bsz, t, _ = x.shape
    ctx_len = ctx.shape[1]
    assert w_mod.shape[0] == 1, "single-layer problem"
    assert t % 512 == 0 and ctx_len % GDN_CHUNK == 0 and bsz < 8
    cap = EC_CAPACITY * t // N_EXPERTS

    cs = jnp.zeros((8, D_MODEL), F32).at[:bsz].set(c).at[bsz].set(c_ctx)
    mod3 = _mod_call(cs, w_mod[0], b_mod[0][None, :]).reshape(8, N_MOD, D_MODEL)

    wl = w_in[0]
    n_state = QKV_W + STATE_COLS
    w_state = _pad_lanes(wl[:, QKV_W:n_state])
    w_lat = jnp.concatenate([wl[:, :QKV_W], wl[:, n_state:n_state + B_W], wl[:, n_state + B_W:], w_state],
                            axis=1).astype(BF16)
    w_ctx = jnp.concatenate([wl[:, :QKV_W], w_state], axis=1).astype(BF16)
    gp = jnp.zeros((8, LANES), F32).at[0, :2 * N_HEADS].set(a_log[0].reshape(-1)).at[1, :2 * N_HEADS].set(
        dt_bias[0].reshape(-1))
    g1 = norm1_g[0][None, :]
    cw = jnp.zeros((8, QKV_W), F32).at[:conv_w.shape[1]].set(conv_w[0])

    qkv_c, gb_c = _inproj_ctx_call(ctx, mod3, bsz, g1, w_ctx, gp, cw)
    zero_state = jnp.zeros((bsz, N_HEADS, HEAD_DIM, HEAD_DIM), F32)
    _, _, s_f, s_b = _gdn_call(qkv_c, gb_c, zero_state, zero_state, ctx_len)

    qkv, gb, z, ya = _inproj_lat_call(x, mod3, g1, w_lat, gp, cw, gm_norm_g[0][None, :], gm_ws[0].astype(BF16),
                                      _pad_lanes(gm_bs[0].T), 512)
    o_f, o_b, _, _ = _gdn_call(qkv, gb, s_f, s_b, 512)
    wr = _pad_lanes(w_router[0])
    wr_hi = wr.astype(BF16)
    wr_lo = (wr - wr_hi.astype(F32)).astype(BF16)
    br = jnp.full((1, LANES), -1e30, F32).at[0, :N_EXPERTS].set(b_router[0])
    h, fin, aff, afft = _mixout_call(x, o_f, o_b, z, ya, mod3, gdn_norm_g[0][None, :], w_out[0].astype(BF16),
                                     norm2_g[0][None, :], wr_hi, wr_lo, br, 512)

    npieces = t // LANES
    slott, slot, off = _route_call(afft.reshape(bsz, npieces * N_EXPERTS, LANES), cap)
    base = off[:, :, 0].reshape(bsz, npieces, N_EXPERTS)[:, ::ROUTE_CHUNK // LANES, :]
    base_flat = jnp.concatenate([base, jnp.full((bsz, 1, N_EXPERTS), cap, I32)], axis=1).reshape(-1)
    xe = _dispatch_call(base_flat, slott.reshape(bsz, npieces, N_EXPERTS, LANES), fin, cap)
    y = _experts_call(xe, w_gate[0], w_up[0], w_down[0], cap)
    return _combine_call(base_flat, slot, aff, h, y, mod3, final_norm_g[None, :])
```

```python
import functools

import jax
import jax.numpy as jnp
from jax import lax
from jax.experimental import pallas as pl
from jax.experimental.pallas import tpu as pltpu

F32 = jnp.float32
BF16 = jnp.bfloat16
I32 = jnp.int32

D_MODEL = 1024
N_MOD = 6
N_HEADS = 4
HEAD_DIM = 128
B_W = N_HEADS * HEAD_DIM
QKV_W = 3 * B_W
A_W = 512
A_GROUPS = 4
A_CHUNK = 128
GDN_CHUNK = 64
N_EXPERTS = 16
EC_CAPACITY = 2
EXPERT_FF = 1024
NORM_EPS = 1e-6
LANES = 128
STATE_COLS = 4 * N_HEADS

ROUTE_CHUNK = 256
SLOT_WIN = 64
SLOT_ALIGN = 16
VMEM_LIMIT = 56 * 1024 * 1024


def _cparams(sem):
    return pltpu.CompilerParams(dimension_semantics=sem, vmem_limit_bytes=VMEM_LIMIT)


def _dot(a, b):
    return jnp.dot(a, b, preferred_element_type=F32)


def _dot_nt(a, b):
    return lax.dot_general(a, b, (((1,), (1,)), ((), ())), preferred_element_type=F32)


def _dot_tn(a, b):
    return lax.dot_general(a, b, (((0,), (0,)), ((), ())), preferred_element_type=F32)


def _silu(x):
    return x * jax.nn.sigmoid(x)


def _mod_body(c_ref, w_ref, b_ref, o_ref):
    s = _silu(c_ref[...])
    o_ref[...] = _dot(s.astype(BF16), w_ref[...].astype(BF16)) + b_ref[...]


def _mod_call(cs, w_mod, b_mod):
    n = w_mod.shape[1] // D_MODEL
    return pl.pallas_call(
        _mod_body,
        out_shape=jax.ShapeDtypeStruct((8, w_mod.shape[1]), F32),
        grid=(n,),
        in_specs=[pl.BlockSpec((8, D_MODEL), lambda j: (0, 0)),
                  pl.BlockSpec((D_MODEL, D_MODEL), lambda j: (0, j)),
                  pl.BlockSpec((1, D_MODEL), lambda j: (0, j))],
        out_specs=pl.BlockSpec((8, D_MODEL), lambda j: (0, j)),
        compiler_params=_cparams(("arbitrary",)),
        name="mod",
    )(cs, w_mod, b_mod)


def _norm_mod(x, g, shift, scale):
    ms = jnp.mean(x * x, axis=-1, keepdims=True)
    return (x * lax.rsqrt(ms + NORM_EPS) * g) * (1.0 + scale) + shift


def _gate_streams(st, gp_ref):
    lane = lax.broadcasted_iota(I32, st.shape, 1)
    g = -jnp.exp(gp_ref[0:1, :]) * jax.nn.softplus(st + gp_ref[1:2, :])
    beta = jax.nn.sigmoid(st)
    return jnp.where(lane < 2 * N_HEADS, g, jnp.where(lane < STATE_COLS, beta, 0.0))


def _conv_qkv(qkv, prev_row, next_row, cw_ref, out_ref, tm):
    cs = GDN_CHUNK
    nsub = tm // cs
    w0, w1, w2 = cw_ref[0:1, :], cw_ref[1:2, :], cw_ref[2:3, :]
    row = lax.broadcasted_iota(I32, (cs, 1), 0)
    for c in range(nsub):
        rows = slice(c * cs, (c + 1) * cs)
        x = qkv[rows]
        prow = prev_row if c == 0 else qkv[c * cs - 1:c * cs]
        nrow = next_row if c == nsub - 1 else qkv[(c + 1) * cs:(c + 1) * cs + 1]
        xp = jnp.where(row == 0, prow, pltpu.roll(x, 1, 0))
        xn = jnp.where(row == cs - 1, nrow, pltpu.roll(x, cs - 1, 0))
        y = _silu(xp * w0 + x * w1 + xn * w2)
        for h in range(N_HEADS):
            cq = slice(h * HEAD_DIM, (h + 1) * HEAD_DIM)
            ck = slice(B_W + h * HEAD_DIM, B_W + (h + 1) * HEAD_DIM)
            q = y[:, cq]
            k = y[:, ck]
            out_ref[0, rows, cq] = (q * (lax.rsqrt(jnp.sum(q * q, axis=-1, keepdims=True) + NORM_EPS)
                                         * (HEAD_DIM ** -0.5))).astype(BF16)
            out_ref[0, rows, ck] = (k * lax.rsqrt(jnp.sum(k * k, axis=-1, keepdims=True) + NORM_EPS)).astype(BF16)
        out_ref[0, rows, 2 * B_W:3 * B_W] = y[:, 2 * B_W:3 * B_W].astype(BF16)


def _inproj_lat_body(x_ref, xp_ref, xn_ref, mod_ref, g1_ref, w_ref, gp_ref, cw_ref, gmg_ref, ws_ref, bst_ref,
                     qkv_ref, gb_ref, z_ref, ya_ref, *, tm):
    i = pl.program_id(1)
    shift, scale = mod_ref[0, 0:1, :], mod_ref[0, 1:2, :]
    a = _norm_mod(x_ref[0], g1_ref[...], shift, scale).astype(BF16)
    xh = jnp.concatenate([xp_ref[0], xn_ref[0]], axis=0)
    halo = _dot(_norm_mod(xh, g1_ref[...], shift, scale).astype(BF16), w_ref[:, 0:QKV_W])
    prev_row = jnp.where(i == 0, 0.0, halo[7:8, :])
    next_row = jnp.where(i == pl.num_programs(1) - 1, 0.0, halo[8:9, :])
    _conv_qkv(_dot(a, w_ref[:, 0:QKV_W]), prev_row, next_row, cw_ref, qkv_ref, tm)
    z_ref[0] = _dot(a, w_ref[:, QKV_W:QKV_W + B_W]).astype(BF16)
    c_uv = QKV_W + B_W
    gb_ref[0] = _gate_streams(_dot(a, w_ref[:, c_uv + 2 * A_W:c_uv + 2 * A_W + LANES]), gp_ref)
    uv = _dot(a, w_ref[:, c_uv:c_uv + 2 * A_W])
    uv = 0.5 * uv * (1.0 + lax.erf(uv * 0.7071067811865476))
    gd = A_W // A_GROUPS
    for grp in range(A_GROUPS):
        v = uv[:, A_W + grp * gd:A_W + (grp + 1) * gd]
        vn = v * lax.rsqrt(jnp.mean(v * v, axis=-1, keepdims=True) + NORM_EPS) * gmg_ref[:, grp * gd:(grp + 1) * gd]
        vn = vn.astype(BF16)
        bias = bst_ref[:, grp:grp + 1]
        for c in range(tm // A_CHUNK):
            rows = slice(c * A_CHUNK, (c + 1) * A_CHUNK)
            s = _dot(ws_ref[grp], vn[rows]) + bias
            ya_ref[0, rows, grp * gd:(grp + 1) * gd] = (uv[rows, grp * gd:(grp + 1) * gd] * s).astype(BF16)


def _inproj_ctx_body(x_ref, mod_ref, g1_ref, w_ref, gp_ref, cw_ref, qkv_ref, gb_ref, *, tm):
    a = _norm_mod(x_ref[0], g1_ref[...], mod_ref[0, 0:1, :], mod_ref[0, 1:2, :]).astype(BF16)
    edge = jnp.zeros((1, QKV_W), F32)
    _conv_qkv(_dot(a, w_ref[:, 0:QKV_W]), edge, edge, cw_ref, qkv_ref, tm)
    gb_ref[0] = _gate_streams(_dot(a, w_ref[:, QKV_W:QKV_W + LANES]), gp_ref)


def _inproj_lat_call(x, mod3, g1, w_lat, gp, cw, gmg, ws16, bst, tm):
    bsz, t, _ = x.shape
    hb = tm // 8
    last8 = t // 8 - 1
    full = lambda a: pl.BlockSpec(a.shape, lambda b, i: (0,) * a.ndim)
    tok = lambda w: pl.BlockSpec((1, tm, w), lambda b, i: (b, i, 0))
    return pl.pallas_call(
        functools.partial(_inproj_lat_body, tm=tm),
        out_shape=(jax.ShapeDtypeStruct((bsz, t, QKV_W), BF16),
                   jax.ShapeDtypeStruct((bsz, t, LANES), F32),
                   jax.ShapeDtypeStruct((bsz, t, B_W), BF16),
                   jax.ShapeDtypeStruct((bsz, t, A_W), BF16)),
        grid=(bsz, t // tm),
        in_specs=[tok(D_MODEL),
                  pl.BlockSpec((1, 8, D_MODEL), lambda b, i: (b, jnp.maximum(i * hb - 1, 0), 0)),
                  pl.BlockSpec((1, 8, D_MODEL), lambda b, i: (b, jnp.minimum((i + 1) * hb, last8), 0)),
                  pl.BlockSpec((1, N_MOD, D_MODEL), lambda b, i: (b, 0, 0)),
                  full(g1), full(w_lat), full(gp), full(cw), full(gmg), full(ws16), full(bst)],
        out_specs=(tok(QKV_W), tok(LANES), tok(B_W), tok(A_W)),
        compiler_params=_cparams(("parallel", "arbitrary")),
        name="inproj_lat",
    )(x, x, x, mod3, g1, w_lat, gp, cw, gmg, ws16, bst)


def _inproj_ctx_call(ctx, mod3, ctx_row, g1, w_ctx, gp, cw):
    bsz, t, _ = ctx.shape
    full = lambda a: pl.BlockSpec(a.shape, lambda b: (0,) * a.ndim)
    tok = lambda w: pl.BlockSpec((1, t, w), lambda b: (b, 0, 0))
    return pl.pallas_call(
        functools.partial(_inproj_ctx_body, tm=t),
        out_shape=(jax.ShapeDtypeStruct((bsz, t, QKV_W), BF16),
                   jax.ShapeDtypeStruct((bsz, t, LANES), F32)),
        grid=(bsz,),
        in_specs=[tok(D_MODEL),
                  pl.BlockSpec((1, N_MOD, D_MODEL), lambda b: (ctx_row, 0, 0)),
                  full(g1), full(w_ctx), full(gp), full(cw)],
        out_specs=(tok(QKV_W), tok(LANES)),
        compiler_params=_cparams(("parallel",)),
        name="inproj_ctx",
    )(ctx, mod3, g1, w_ctx, gp, cw)


GDN_CHUNKS_PER_STEP = 2


def _gdn_local(chains):
    cs = GDN_CHUNK
    for c in chains:
        c["kb"] = c["k"] * c["beta"]
        c["k16"] = c["k"].astype(BF16)
        c["decay"] = jnp.where(c["incl"], jnp.exp(jnp.where(c["incl"], c["gc"] - c["gc_row"], 0.0)), 0.0)
    for c in chains:
        c["kk"] = _dot_nt(jnp.concatenate([c["kb"], c["q"]], axis=0).astype(BF16), c["k16"])
    for c in chains:
        c["a"] = jnp.where(c["strict"], c["kk"][:cs] * c["decay"], 0.0)
        c["attn"] = (c["kk"][cs:] * c["decay"]).astype(BF16)
        c["m"] = -jnp.where(c["levels"][0], c["a"], 0.0)
    for li in range(1, len(chains[0]["levels"])):
        for c in chains:
            c["m16"] = c["m"].astype(BF16)
            c["cm"] = jnp.where(c["levels"][li], c["a"], 0.0)
        for c in chains:
            c["x"] = c["cm"] + _dot(c["m16"], c["cm"].astype(BF16))
        for c in chains:
            c["y"] = c["x"] + _dot(c["x"].astype(BF16), c["m16"])
        for c in chains:
            c["m"] = c["m"] - c["y"]
    for c in chains:
        c["rhs"] = jnp.concatenate([c["v"] * c["beta"], c["kb"] * c["egc"]], axis=1)
    for c in chains:
        uw = c["rhs"] + _dot(c["m"].astype(BF16), c["rhs"].astype(BF16))
        c["u"] = uw[:, :HEAD_DIM]
        c["wq_lhs"] = jnp.concatenate([uw[:, HEAD_DIM:], c["q"] * c["egc"]], axis=0).astype(BF16)
        c["kg"] = (c["k"] * jnp.exp(c["glast"] - c["gc"])).astype(BF16)
        c["eg"] = jnp.exp(c["glast"])


def _gdn_sequential(chains):
    cs = GDN_CHUNK
    for c in chains:
        c["s"] = c["s_ref"][c["h"]]
        c["wq"] = _dot(c["wq_lhs"], c["s"].astype(BF16))
    for c in chains:
        c["v_new"] = (c["u"] - c["wq"][:cs]).astype(BF16)
    for c in chains:
        c["o"] = c["wq"][cs:] + _dot(c["attn"], c["v_new"])
    for c in chains:
        c["s_ref"][c["h"]] = c["s"] * c["eg"] + _dot_tn(c["kg"], c["v_new"])
    for c in chains:
        c["o_ref"][0, pl.ds(c["r0"], cs), c["cols"]] = c["o"].astype(BF16)


def _gdn_body(qf_ref, qb_ref, gbf_ref, gbb_ref, s0f_ref, s0b_ref,
              of_ref, ob_ref, sff_ref, sfb_ref, sf_scr, sb_scr, *, tb, nt):
    t = pl.program_id(1)
    cs = GDN_CHUNK
    nch = tb // cs
    cps = GDN_CHUNKS_PER_STEP

    @pl.when(t == 0)
    def _():
        sf_scr[...] = s0f_ref[0]
        sb_scr[...] = s0b_ref[0]

    ii = lax.broadcasted_iota(I32, (cs, cs), 0)
    jj = lax.broadcasted_iota(I32, (cs, cs), 1)
    incl_f, strict_f = jj <= ii, jj < ii
    incl_b, strict_b = jj >= ii, jj > ii
    levels = []
    sh = 0
    while (1 << sh) < cs:
        levels.append(((ii >> (sh + 1)) == (jj >> (sh + 1))) & ((ii >> sh) != (jj >> sh)))
        sh += 1
    lv_f = [lm & strict_f for lm in levels]
    lv_b = [lm & strict_b for lm in levels]
    row = lax.broadcasted_iota(I32, (cs, LANES), 0)

    def step(n, carry):
        groups = []
        for j in range(cps):
            rf = pl.multiple_of((n * cps + j) * cs, cs)
            rb = pl.multiple_of((nch - 1 - n * cps - j) * cs, cs)
            gf = gbf_ref[0, pl.ds(rf, cs), :]
            gb = gbb_ref[0, pl.ds(rb, cs), :]
            cf, cb = gf, gb
            s = 1
            while s < cs:
                cf = cf + jnp.where(row >= s, pltpu.roll(cf, s, 0), 0.0)
                cb = cb + jnp.where(row < cs - s, pltpu.roll(cb, cs - s, 0), 0.0)
                s *= 2
            gt = jnp.concatenate([cf, cb], axis=0).T
            ecf, ecb = jnp.exp(cf), jnp.exp(cb)
            chains = []
            for h in range(N_HEADS):
                cq = slice(h * HEAD_DIM, (h + 1) * HEAD_DIM)
                ck = slice(B_W + h * HEAD_DIM, B_W + (h + 1) * HEAD_DIM)
                cv = slice(2 * B_W + h * HEAD_DIM, 2 * B_W + (h + 1) * HEAD_DIM)
                lf, lb = h, N_HEADS + h
                ld = lambda ref, r0, cols: ref[0, pl.ds(r0, cs), cols].astype(F32)
                chains.append(dict(
                    q=ld(qf_ref, rf, cq), k=ld(qf_ref, rf, ck), v=ld(qf_ref, rf, cv),
                    gc=cf[:, lf:lf + 1], gc_row=gt[lf:lf + 1, 0:cs], beta=gf[:, 2 * N_HEADS + h:2 * N_HEADS + h + 1],
                    egc=ecf[:, lf:lf + 1], glast=cf[cs - 1:cs, lf:lf + 1], incl=incl_f, strict=strict_f, levels=lv_f,
                    s_ref=sf_scr, h=h, o_ref=of_ref, r0=rf, cols=cq))
                chains.append(dict(
                    q=ld(qb_ref, rb, cq), k=ld(qb_ref, rb, ck), v=ld(qb_ref, rb, cv),
                    gc=cb[:, lb:lb + 1], gc_row=gt[lb:lb + 1, cs:2 * cs],
                    beta=gb[:, 3 * N_HEADS + h:3 * N_HEADS + h + 1],
                    egc=ecb[:, lb:lb + 1], glast=cb[0:1, lb:lb + 1], incl=incl_b, strict=strict_b, levels=lv_b,
                    s_ref=sb_scr, h=h, o_ref=ob_ref, r0=rb, cols=cq))
            groups.append(chains)
        _gdn_local([c for chains in groups for c in chains])
        for chains in groups:
            _gdn_sequential(chains)
        return carry

    lax.fori_loop(0, nch // cps, step, 0)

    @pl.when(t == nt - 1)
    def _():
        sff_ref[0] = sf_scr[...]
        sfb_ref[0] = sb_scr[...]


def _gdn_call(qkv, gb, s0f, s0b, tb):
    bsz, t, _ = qkv.shape
    nt = t // tb
    assert (tb // GDN_CHUNK) % GDN_CHUNKS_PER_STEP == 0
    st = pl.BlockSpec((1, N_HEADS, HEAD_DIM, HEAD_DIM), lambda b, i: (b, 0, 0, 0))
    fwd = lambda w: pl.BlockSpec((1, tb, w), lambda b, i: (b, i, 0))
    bwd = lambda w: pl.BlockSpec((1, tb, w), lambda b, i: (b, nt - 1 - i, 0))
    return pl.pallas_call(
        functools.partial(_gdn_body, tb=tb, nt=nt),
        out_shape=(jax.ShapeDtypeStruct((bsz, t, B_W), BF16), jax.ShapeDtypeStruct((bsz, t, B_W), BF16),
                   jax.ShapeDtypeStruct((bsz, N_HEADS, HEAD_DIM, HEAD_DIM), F32),
                   jax.ShapeDtypeStruct((bsz, N_HEADS, HEAD_DIM, HEAD_DIM), F32)),
        grid=(bsz, nt),
        in_specs=[fwd(QKV_W), bwd(QKV_W), fwd(LANES), bwd(LANES), st, st],
        out_specs=(fwd(B_W), bwd(B_W), st, st),
        scratch_shapes=[pltpu.VMEM((N_HEADS, HEAD_DIM, HEAD_DIM), F32), pltpu.VMEM((N_HEADS, HEAD_DIM, HEAD_DIM), F32)],
        compiler_params=_cparams(("parallel", "arbitrary")),
        name="gdn",
    )(qkv, qkv, gb, gb, s0f, s0b)


def _mixout_body(x_ref, of_ref, ob_ref, z_ref, ya_ref, mod_ref, gng_ref, wout_ref, n2g_ref, wrh_ref, wrl_ref, br_ref,
                 h_ref, fin_ref, aff_ref, afft_ref, *, tm):
    o = of_ref[0].astype(F32) + ob_ref[0].astype(F32)
    z = z_ref[0].astype(F32)
    parts = [ya_ref[0]]
    for h in range(N_HEADS):
        c = slice(h * HEAD_DIM, (h + 1) * HEAD_DIM)
        oh = o[:, c]
        y = oh * lax.rsqrt(jnp.mean(oh * oh, axis=-1, keepdims=True) + NORM_EPS)
        parts.append((y * gng_ref[...] * _silu(z[:, c])).astype(BF16))
    mix = _dot(jnp.concatenate(parts, axis=1), wout_ref[...])
    hl = x_ref[0] + mod_ref[0, 2:3, :] * mix
    h_ref[0] = hl
    fin = _norm_mod(hl, n2g_ref[...], mod_ref[0, 3:4, :], mod_ref[0, 4:5, :])
    f_hi = fin.astype(BF16)
    fin_ref[0] = f_hi
    f_lo = (fin - f_hi.astype(F32)).astype(BF16)
    logits = _dot(f_hi, wrh_ref[...]) + _dot(f_lo, wrh_ref[...]) + _dot(f_hi, wrl_ref[...]) + br_ref[...]
    e = jnp.exp(logits - jnp.max(logits, axis=-1, keepdims=True))
    aff = e / jnp.sum(e, axis=-1, keepdims=True)
    aff_ref[0] = aff
    for j in range(tm // LANES):
        afft_ref[0, j] = aff[j * LANES:(j + 1) * LANES, :].T[0:N_EXPERTS, :]


def _mixout_call(x, o_f, o_b, z, ya, mod3, gng, wout16, n2g, wr_hi, wr_lo, br, tm):
    bsz, t, _ = x.shape
    full = lambda a: pl.BlockSpec(a.shape, lambda b, i: (0,) * a.ndim)
    tok = lambda w: pl.BlockSpec((1, tm, w), lambda b, i: (b, i, 0))
    return pl.pallas_call(
        functools.partial(_mixout_body, tm=tm),
        out_shape=(jax.ShapeDtypeStruct((bsz, t, D_MODEL), F32), jax.ShapeDtypeStruct((bsz, t, D_MODEL), BF16),
                   jax.ShapeDtypeStruct((bsz, t, LANES), F32),
                   jax.ShapeDtypeStruct((bsz, t // LANES, N_EXPERTS, LANES), F32)),
        grid=(bsz, t // tm),
        in_specs=[tok(D_MODEL), tok(B_W), tok(B_W), tok(B_W), tok(A_W),
                  pl.BlockSpec((1, N_MOD, D_MODEL), lambda b, i: (b, 0, 0)),
                  full(gng), full(wout16), full(n2g), full(wr_hi), full(wr_lo), full(br)],
        out_specs=(tok(D_MODEL), tok(D_MODEL), tok(LANES),
                   pl.BlockSpec((1, tm // LANES, N_EXPERTS, LANES), lambda b, i: (b, i, 0, 0))),
        compiler_params=_cparams(("parallel", "parallel")),
        name="mixout",
    )(x, o_f, o_b, z, ya, mod3, gng, wout16, n2g, wr_hi, wr_lo, br)


def _route_body(afft_ref, slott_ref, slot_ref, off_ref, *, t, cap):
    ne = N_EXPERTS
    npieces = t // LANES
    rows = npieces * ne

    def count(thr_col, strict):
        acc = jnp.zeros((ne, LANES), I32)
        for p in range(npieces):
            piece = afft_ref[0, p * ne:(p + 1) * ne, :]
            acc = acc + (piece > thr_col if strict else piece >= thr_col).astype(I32)
        return jnp.sum(acc, axis=1, keepdims=True)

    def search(i, thr):
        cand = thr | jnp.left_shift(jnp.int32(1), 30 - i)
        return jnp.where(count(pltpu.bitcast(cand, F32), False) >= cap, cand, thr)

    thr_bits = lax.fori_loop(0, 31, search, jnp.zeros((ne, 1), I32))
    thr = pltpu.bitcast(thr_bits, F32)
    need = (cap - count(thr, True)).astype(F32)

    x = afft_ref[0]
    thr_rows = jnp.concatenate([thr] * npieces, axis=0)
    need_rows = jnp.concatenate([need] * npieces, axis=0)
    gt = x > thr_rows
    eq = x == thr_rows
    ti = lax.broadcasted_iota(I32, (LANES, LANES), 0)
    tj = lax.broadcasted_iota(I32, (LANES, LANES), 1)
    triu = (ti <= tj).astype(BF16)
    ri = lax.broadcasted_iota(I32, (rows, rows), 0)
    rj = lax.broadcasted_iota(I32, (rows, rows), 1)
    earlier = (((ri & (ne - 1)) == (rj & (ne - 1))) & (rj < ri)).astype(BF16)

    def prefix(mask):
        inpiece = _dot(mask.astype(BF16), triu)
        total = jnp.broadcast_to(inpiece[:, LANES - 1:LANES], (rows, LANES)).astype(BF16)
        offset = _dot(earlier, total)
        return inpiece + offset, offset

    eq_rank, _ = prefix(eq)
    sel = gt | (eq & (eq_rank <= need_rows))
    sel_rank, sel_off = prefix(sel)
    slot = jnp.where(sel, sel_rank - 1.0, -1.0)
    slott_ref[0] = slot.astype(I32)
    off_ref[0] = sel_off.astype(I32)
    pad = jnp.zeros((LANES - ne, LANES), F32)
    for p in range(npieces):
        piece = jnp.concatenate([slot[p * ne:(p + 1) * ne, :], pad], axis=0)
        slot_ref[0, p * LANES:(p + 1) * LANES, :] = piece.T.astype(I32)


def _route_call(afft, cap):
    bsz, rows, _ = afft.shape
    t = rows // N_EXPERTS * LANES
    spec = lambda r: pl.BlockSpec((1, r, LANES), lambda b: (b, 0, 0))
    return pl.pallas_call(
        functools.partial(_route_body, t=t, cap=cap),
        out_shape=(jax.ShapeDtypeStruct((bsz, rows, LANES), I32),
                   jax.ShapeDtypeStruct((bsz, t, LANES), I32),
                   jax.ShapeDtypeStruct((bsz, rows, LANES), I32)),
        grid=(bsz,),
        in_specs=[spec(rows)],
        out_specs=(spec(rows), spec(t), spec(rows)),
        compiler_params=_cparams(("parallel",)),
        name="route",
    )(afft)


def _window_plan(base_ref, flat0, experts):
    starts, rounds = [], jnp.int32(0)
    for e in experts:
        lo = base_ref[flat0 + e]
        hi = base_ref[flat0 + N_EXPERTS + e]
        lo_al = (lo >> 4) << 4
        starts.append(lo_al)
        rounds = jnp.maximum(rounds, (hi - lo_al + SLOT_WIN - 1) // SLOT_WIN)
    return starts, rounds


def _window_start(start, r, cap):
    return pl.multiple_of(jnp.minimum(start + r * SLOT_WIN, cap), SLOT_ALIGN)


def _dispatch_body(base_ref, slott_ref, fin_ref, xe_ref, *, nchunk, sub, eh_n, cap):
    b, eh, ci = pl.program_id(0), pl.program_id(1), pl.program_id(2)
    rc = ROUTE_CHUNK

    @pl.when(ci == 0)
    def _():
        xe_ref[...] = jnp.zeros_like(xe_ref)

    srow = lax.broadcasted_iota(I32, (SLOT_WIN, rc), 0)
    for sc in range(sub):
        cc = ci * sub + sc
        flat0 = (b * (nchunk + 1) + cc) * N_EXPERTS + eh * eh_n
        f = fin_ref[0, sc * rc:(sc + 1) * rc, :]
        experts = list(range(eh_n))
        starts, rounds = _window_plan(base_ref, flat0, experts)

        def one_round(r, carry, starts=starts, f=f, sc=sc):
            rows = []
            wstart = [_window_start(starts[e], r, cap) for e in experts]
            for e in experts:
                tok_slot = jnp.concatenate(
                    [slott_ref[0, sc * (rc // LANES) + j, e:e + 1, :] for j in range(rc // LANES)], axis=1)
                rows.append((tok_slot == srow + wstart[e]).astype(BF16))
            prod = _dot(jnp.concatenate(rows, axis=0), f)
            for e in experts:
                win = pl.ds(wstart[e], SLOT_WIN)
                xe_ref[0, e, win, :] = xe_ref[0, e, win, :] + prod[e * SLOT_WIN:(e + 1) * SLOT_WIN].astype(BF16)
            return carry

        one_round(jnp.int32(0), 0)
        lax.fori_loop(1, rounds, one_round, 0)


def _dispatch_call(base_flat, slott, fin, cap):
    bsz, t, _ = fin.shape
    nchunk = t // ROUTE_CHUNK
    sub = 2
    eh_n = N_EXPERTS // 2
    sp = cap + SLOT_WIN
    grid_spec = pltpu.PrefetchScalarGridSpec(
        num_scalar_prefetch=1,
        grid=(bsz, N_EXPERTS // eh_n, nchunk // sub),
        in_specs=[pl.BlockSpec((1, sub * ROUTE_CHUNK // LANES, eh_n, LANES), lambda b, eh, ci, base: (b, ci, eh, 0)),
                  pl.BlockSpec((1, sub * ROUTE_CHUNK, D_MODEL), lambda b, eh, ci, base: (b, ci, 0))],
        out_specs=pl.BlockSpec((1, eh_n, sp, D_MODEL), lambda b, eh, ci, base: (b, eh, 0, 0)))
    return pl.pallas_call(
        functools.partial(_dispatch_body, nchunk=nchunk, sub=sub, eh_n=eh_n, cap=cap),
        out_shape=jax.ShapeDtypeStruct((bsz, N_EXPERTS, sp, D_MODEL), BF16),
        grid_spec=grid_spec,
        compiler_params=_cparams(("parallel", "parallel", "arbitrary")),
        name="dispatch",
    )(base_flat, slott, fin)


def _experts_body(xe_ref, wg_ref, wu_ref, wd_ref, y_ref, wg16, wu16, wd16, *, cap):
    @pl.when(pl.program_id(1) == 0)
    def _():
        wg16[...] = wg_ref[0].astype(BF16)
        wu16[...] = wu_ref[0].astype(BF16)
        wd16[...] = wd_ref[0].astype(BF16)

    x = xe_ref[0, 0, 0:cap, :]
    ft = 256
    acc = None
    for f in range(EXPERT_FF // ft):
        cols = slice(f * ft, (f + 1) * ft)
        hid = (_silu(_dot(x, wg16[:, cols])) * _dot(x, wu16[:, cols])).astype(BF16)
        part = _dot(hid, wd16[cols, :])
        acc = part if acc is None else acc + part
    y_ref[0, 0, 0:cap, :] = acc.astype(BF16)
    y_ref[0, 0, cap:, :] = jnp.zeros((y_ref.shape[2] - cap, D_MODEL), BF16)


def _experts_call(xe, w_gate, w_up, w_down, cap):
    bsz, _, sp, _ = xe.shape
    wspec = lambda shape: pl.BlockSpec((1,) + shape, lambda e, b: (e, 0, 0))
    slots = pl.BlockSpec((1, 1, sp, D_MODEL), lambda e, b: (b, e, 0, 0))
    return pl.pallas_call(
        functools.partial(_experts_body, cap=cap),
        out_shape=jax.ShapeDtypeStruct(xe.shape, BF16),
        grid=(N_EXPERTS, bsz),
        in_specs=[slots, wspec((D_MODEL, EXPERT_FF)), wspec((D_MODEL, EXPERT_FF)), wspec((EXPERT_FF, D_MODEL))],
        out_specs=slots,
        scratch_shapes=[pltpu.VMEM((D_MODEL, EXPERT_FF), BF16), pltpu.VMEM((D_MODEL, EXPERT_FF), BF16),
                        pltpu.VMEM((EXPERT_FF, D_MODEL), BF16)],
        compiler_params=_cparams(("arbitrary", "arbitrary")),
        name="experts",
    )(xe, w_gate, w_up, w_down)


SLOT_SPLIT = 32


def _combine_selectors():
    k = jnp.arange(LANES)[:, None]
    e_of_lane = jnp.arange(N_EXPERTS * SLOT_WIN)[None, :] // SLOT_WIN
    sel_gate = (k == e_of_lane).astype(BF16)
    sel_slot = (SLOT_SPLIT * (k == e_of_lane) + (k - N_EXPERTS == e_of_lane)).astype(BF16)
    return sel_slot, sel_gate


def _combine_body(base_ref, slot_ref, aff_ref, h_ref, y_ref, mod_ref, fng_ref, ssel_ref, gsel_ref, o_ref, acc_ref,
                  *, nchunk, sub, cap):
    b, ci = pl.program_id(0), pl.program_id(1)
    rc = ROUTE_CHUNK
    width = N_EXPERTS * SLOT_WIN
    lane = lax.broadcasted_iota(I32, (1, width), 1)
    lane_e = lane >> (SLOT_WIN.bit_length() - 1)
    lane_j = (lane & (SLOT_WIN - 1)).astype(F32)
    lane128 = lax.broadcasted_iota(I32, (rc, LANES), 1)
    experts = list(range(N_EXPERTS))
    for sc in range(sub):
        rows = slice(sc * rc, (sc + 1) * rc)
        flat0 = (b * (nchunk + 1) + ci * sub + sc) * N_EXPERTS
        s1 = slot_ref[0, rows, :] + 1
        halves = jnp.where(lane128 < N_EXPERTS, s1 >> (SLOT_SPLIT.bit_length() - 1),
                           pltpu.roll(s1 & (SLOT_SPLIT - 1), N_EXPERTS, 1))
        slot1 = _dot(halves.astype(F32).astype(BF16), ssel_ref[...])
        gates = _dot(aff_ref[0, rows, :].astype(BF16), gsel_ref[...])
        starts, rounds = _window_plan(base_ref, flat0, experts)

        def contribution(r, starts=starts, slot1=slot1, gates=gates):
            wstart = [_window_start(starts[e], r, cap) for e in experts]
            ywin = jnp.concatenate([y_ref[0, e, pl.ds(wstart[e], SLOT_WIN), :] for e in experts], axis=0)
            held = jnp.zeros((1, width), I32)
            for e in experts:
                held = jnp.where(lane_e == e, wstart[e] + 1, held)
            held = held.astype(F32) + lane_j
            s = jnp.where(slot1 == held, gates, 0.0).astype(BF16)
            return _dot(s, ywin)

        def extra_round(r, carry, contribution=contribution):
            acc_ref[...] += contribution(r)
            return carry

        acc_ref[...] = contribution(jnp.int32(0))
        lax.fori_loop(1, rounds, extra_round, 0)
        hl = h_ref[0, rows, :] + mod_ref[0, 5:6, :] * acc_ref[...]
        ms = jnp.mean(hl * hl, axis=-1, keepdims=True)
        o_ref[0, rows, :] = hl * lax.rsqrt(ms + NORM_EPS) * fng_ref[...]


def _combine_call(base_flat, slot, aff, h, y, mod3, fng):
    bsz, t, _ = h.shape
    nchunk = t // ROUTE_CHUNK
    sub = 2
    rc = ROUTE_CHUNK
    ssel, gsel = _combine_selectors()
    tok = lambda w: pl.BlockSpec((1, sub * rc, w), lambda b, i, base: (b, i, 0))
    full = lambda a: pl.BlockSpec(a.shape, lambda b, i, base: (0,) * a.ndim)
    grid_spec = pltpu.PrefetchScalarGridSpec(
        num_scalar_prefetch=1,
        grid=(bsz, nchunk // sub),
        in_specs=[tok(LANES), tok(LANES), tok(D_MODEL),
                  pl.BlockSpec((1,) + y.shape[1:], lambda b, i, base: (b, 0, 0, 0), pipeline_mode=pl.Buffered(1)),
                  pl.BlockSpec((1, N_MOD, D_MODEL), lambda b, i, base: (b, 0, 0)),
                  full(fng), full(ssel), full(gsel)],
        out_specs=tok(D_MODEL),
        scratch_shapes=[pltpu.VMEM((rc, D_MODEL), F32)])
    return pl.pallas_call(
        functools.partial(_combine_body, nchunk=nchunk, sub=sub, cap=y.shape[2] - SLOT_WIN),
        out_shape=jax.ShapeDtypeStruct(h.shape, F32),
        grid_spec=grid_spec,
        compiler_params=_cparams(("parallel", "arbitrary")),
        name="combine",
    )(base_flat, slot, aff, h, y, mod3, fng, ssel, gsel)


def _pad_lanes(a):
    return jnp.pad(a, ((0, 0), (0, LANES - a.shape[1])))


def kernel(x, c, ctx, c_ctx, w_mod, b_mod, norm1_g, norm2_g, w_in, conv_w, a_log, dt_bias, gdn_norm_g, gm_norm_g,
           gm_ws, gm_bs, w_out, w_router, b_router, w_gate, w_up, w_down, final_norm_g):
    bsz, t, _ = x.shape
    ctx_len = ctx.shape[1]
    assert w_mod.shape[0] == 1, "single-layer problem"
    assert t % 512 == 0 and ctx_len % GDN_CHUNK == 0 and bsz < 8
    cap = EC_CAPACITY * t // N_EXPERTS

    cs = jnp.zeros((8, D_MODEL), F32).at[:bsz].set(c).at[bsz].set(c_ctx)
    mod3 = _mod_call(cs, w_mod[0], b_mod[0][None, :]).reshape(8, N_MOD, D_MODEL)

    wl = w_in[0]
    n_state = QKV_W + STATE_COLS
    w_state = _pad_lanes(wl[:, QKV_W:n_state])
    w_lat = jnp.concatenate([wl[:, :QKV_W], wl[:, n_state:n_state + B_W], wl[:, n_state + B_W:], w_state],
                            axis=1).astype(BF16)
    w_ctx = jnp.concatenate([wl[:, :QKV_W], w_state], axis=1).astype(BF16)
    gp = jnp.zeros((8, LANES), F32).at[0, :2 * N_HEADS].set(a_log[0].reshape(-1)).at[1, :2 * N_HEADS].set(
        dt_bias[0].reshape(-1))
    g1 = norm1_g[0][None, :]
    cw = jnp.zeros((8, QKV_W), F32).at[:conv_w.shape[1]].set(conv_w[0])

    qkv_c, gb_c = _inproj_ctx_call(ctx, mod3, bsz, g1, w_ctx, gp, cw)
    zero_state = jnp.zeros((bsz, N_HEADS, HEAD_DIM, HEAD_DIM), F32)
    _, _, s_f, s_b = _gdn_call(qkv_c, gb_c, zero_state, zero_state, ctx_len)

    qkv, gb, z, ya = _inproj_lat_call(x, mod3, g1, w_lat, gp, cw, gm_norm_g[0][None, :], gm_ws[0].astype(BF16),
                                      _pad_lanes(gm_bs[0].T), 512)
    o_f, o_b, _, _ = _gdn_call(qkv, gb, s_f, s_b, 512)
    wr = _pad_lanes(w_router[0])
    wr_hi = wr.astype(BF16)
    wr_lo = (wr - wr_hi.astype(F32)).astype(BF16)
    br = jnp.full((1, LANES), -1e30, F32).at[0, :N_EXPERTS].set(b_router[0])
    h, fin, aff, afft = _mixout_call(x, o_f, o_b, z, ya, mod3, gdn_norm_g[0][None, :], w_out[0].astype(BF16),
                                     norm2_g[0][None, :], wr_hi, wr_lo, br, 512)

    npieces = t // LANES
    slott, slot, off = _route_call(afft.reshape(bsz, npieces * N_EXPERTS, LANES), cap)
    base = off[:, :, 0].reshape(bsz, npieces, N_EXPERTS)[:, ::ROUTE_CHUNK // LANES, :]
    base_flat = jnp.concatenate([base, jnp.full((bsz, 1, N_EXPERTS), cap, I32)], axis=1).reshape(-1)
    xe = _dispatch_call(base_flat, slott.reshape(bsz, npieces, N_EXPERTS, LANES), fin, cap)
    y = _experts_call(xe, w_gate[0], w_up[0], w_down[0], cap)
    return _combine_call(base_flat, slot, aff, h, y, mod3, final_norm_g[None, :])
```

```python
import functools

import jax
import jax.numpy as jnp
from jax import lax
from jax.experimental import pallas as pl
from jax.experimental.pallas import tpu as pltpu

F32 = jnp.float32
BF16 = jnp.bfloat16
I32 = jnp.int32

D_MODEL = 1024
N_MOD = 6
N_HEADS = 4
HEAD_DIM = 128
B_W = N_HEADS * HEAD_DIM
QKV_W = 3 * B_W
A_W = 512
A_GROUPS = 4
A_CHUNK = 128
GDN_CHUNK = 64
N_EXPERTS = 16
EC_CAPACITY = 2
EXPERT_FF = 1024
NORM_EPS = 1e-6
LANES = 128
STATE_COLS = 4 * N_HEADS

ROUTE_CHUNK = 256
SLOT_WIN = 64
SLOT_ALIGN = 16
VMEM_LIMIT = 56 * 1024 * 1024


def _cparams(sem):
    return pltpu.CompilerParams(dimension_semantics=sem, vmem_limit_bytes=VMEM_LIMIT)


def _dot(a, b):
    return jnp.dot(a, b, preferred_element_type=F32)


def _dot_nt(a, b):
    return lax.dot_general(a, b, (((1,), (1,)), ((), ())), preferred_element_type=F32)


def _dot_tn(a, b):
    return lax.dot_general(a, b, (((0,), (0,)), ((), ())), preferred_element_type=F32)


def _silu(x):
    return x * jax.nn.sigmoid(x)


def _mod_body(c_ref, w_ref, b_ref, o_ref):
    s = _silu(c_ref[...])
    o_ref[...] = _dot(s.astype(BF16), w_ref[...].astype(BF16)) + b_ref[...]


def _mod_call(cs, w_mod, b_mod):
    n = w_mod.shape[1] // D_MODEL
    return pl.pallas_call(
        _mod_body,
        out_shape=jax.ShapeDtypeStruct((8, w_mod.shape[1]), F32),
        grid=(n,),
        in_specs=[pl.BlockSpec((8, D_MODEL), lambda j: (0, 0)),
                  pl.BlockSpec((D_MODEL, D_MODEL), lambda j: (0, j)),
                  pl.BlockSpec((1, D_MODEL), lambda j: (0, j))],
        out_specs=pl.BlockSpec((8, D_MODEL), lambda j: (0, j)),
        compiler_params=_cparams(("arbitrary",)),
        name="mod",
    )(cs, w_mod, b_mod)


def _norm_mod(x, g, shift, scale):
    ms = jnp.mean(x * x, axis=-1, keepdims=True)
    return (x * lax.rsqrt(ms + NORM_EPS) * g) * (1.0 + scale) + shift


def _gate_streams(st, gp_ref):
    lane = lax.broadcasted_iota(I32, st.shape, 1)
    g = -jnp.exp(gp_ref[0:1, :]) * jax.nn.softplus(st + gp_ref[1:2, :])
    beta = jax.nn.sigmoid(st)
    return jnp.where(lane < 2 * N_HEADS, g, jnp.where(lane < STATE_COLS, beta, 0.0))


def _conv_qkv(qkv, prev_row, next_row, cw_ref, out_ref, tm):
    cs = GDN_CHUNK
    nsub = tm // cs
    w0, w1, w2 = cw_ref[0:1, :], cw_ref[1:2, :], cw_ref[2:3, :]
    row = lax.broadcasted_iota(I32, (cs, 1), 0)
    for c in range(nsub):
        rows = slice(c * cs, (c + 1) * cs)
        x = qkv[rows]
        prow = prev_row if c == 0 else qkv[c * cs - 1:c * cs]
        nrow = next_row if c == nsub - 1 else qkv[(c + 1) * cs:(c + 1) * cs + 1]
        xp = jnp.where(row == 0, prow, pltpu.roll(x, 1, 0))
        xn = jnp.where(row == cs - 1, nrow, pltpu.roll(x, cs - 1, 0))
        y = _silu(xp * w0 + x * w1 + xn * w2)
        for h in range(N_HEADS):
            cq = slice(h * HEAD_DIM, (h + 1) * HEAD_DIM)
            ck = slice(B_W + h * HEAD_DIM, B_W + (h + 1) * HEAD_DIM)
            q = y[:, cq]
            k = y[:, ck]
            out_ref[0, rows, cq] = (q * (lax.rsqrt(jnp.sum(q * q, axis=-1, keepdims=True) + NORM_EPS)
                                         * (HEAD_DIM ** -0.5))).astype(BF16)
            out_ref[0, rows, ck] = (k * lax.rsqrt(jnp.sum(k * k, axis=-1, keepdims=True) + NORM_EPS)).astype(BF16)
        out_ref[0, rows, 2 * B_W:3 * B_W] = y[:, 2 * B_W:3 * B_W].astype(BF16)


def _inproj_lat_body(x_ref, xp_ref, xn_ref, mod_ref, g1_ref, w_ref, gp_ref, cw_ref, gmg_ref, ws_ref, bst_ref,
                     qkv_ref, gb_ref, z_ref, ya_ref, *, tm):
    i = pl.program_id(1)
    shift, scale = mod_ref[0, 0:1, :], mod_ref[0, 1:2, :]
    a = _norm_mod(x_ref[0], g1_ref[...], shift, scale).astype(BF16)
    xh = jnp.concatenate([xp_ref[0], xn_ref[0]], axis=0)
    halo = _dot(_norm_mod(xh, g1_ref[...], shift, scale).astype(BF16), w_ref[:, 0:QKV_W])
    prev_row = jnp.where(i == 0, 0.0, halo[7:8, :])
    next_row = jnp.where(i == pl.num_programs(1) - 1, 0.0, halo[8:9, :])
    _conv_qkv(_dot(a, w_ref[:, 0:QKV_W]), prev_row, next_row, cw_ref, qkv_ref, tm)
    z_ref[0] = _dot(a, w_ref[:, QKV_W:QKV_W + B_W]).astype(BF16)
    c_uv = QKV_W + B_W
    gb_ref[0] = _gate_streams(_dot(a, w_ref[:, c_uv + 2 * A_W:c_uv + 2 * A_W + LANES]), gp_ref)
    uv = _dot(a, w_ref[:, c_uv:c_uv + 2 * A_W])
    uv = 0.5 * uv * (1.0 + lax.erf(uv * 0.7071067811865476))
    gd = A_W // A_GROUPS
    for grp in range(A_GROUPS):
        v = uv[:, A_W + grp * gd:A_W + (grp + 1) * gd]
        vn = v * lax.rsqrt(jnp.mean(v * v, axis=-1, keepdims=True) + NORM_EPS) * gmg_ref[:, grp * gd:(grp + 1) * gd]
        vn = vn.astype(BF16)
        bias = bst_ref[:, grp:grp + 1]
        for c in range(tm // A_CHUNK):
            rows = slice(c * A_CHUNK, (c + 1) * A_CHUNK)
            s = _dot(ws_ref[grp], vn[rows]) + bias
            ya_ref[0, rows, grp * gd:(grp + 1) * gd] = (uv[rows, grp * gd:(grp + 1) * gd] * s).astype(BF16)


def _inproj_ctx_body(x_ref, mod_ref, g1_ref, w_ref, gp_ref, cw_ref, qkv_ref, gb_ref, *, tm):
    a = _norm_mod(x_ref[0], g1_ref[...], mod_ref[0, 0:1, :], mod_ref[0, 1:2, :]).astype(BF16)
    edge = jnp.zeros((1, QKV_W), F32)
    _conv_qkv(_dot(a, w_ref[:, 0:QKV_W]), edge, edge, cw_ref, qkv_ref, tm)
    gb_ref[0] = _gate_streams(_dot(a, w_ref[:, QKV_W:QKV_W + LANES]), gp_ref)


def _inproj_lat_call(x, mod3, g1, w_lat, gp, cw, gmg, ws16, bst, tm):
    bsz, t, _ = x.shape
    hb = tm // 8
    last8 = t // 8 - 1
    full = lambda a: pl.BlockSpec(a.shape, lambda b, i: (0,) * a.ndim)
    tok = lambda w: pl.BlockSpec((1, tm, w), lambda b, i: (b, i, 0))
    return pl.pallas_call(
        functools.partial(_inproj_lat_body, tm=tm),
        out_shape=(jax.ShapeDtypeStruct((bsz, t, QKV_W), BF16),
                   jax.ShapeDtypeStruct((bsz, t, LANES), F32),
                   jax.ShapeDtypeStruct((bsz, t, B_W), BF16),
                   jax.ShapeDtypeStruct((bsz, t, A_W), BF16)),
        grid=(bsz, t // tm),
        in_specs=[tok(D_MODEL),
                  pl.BlockSpec((1, 8, D_MODEL), lambda b, i: (b, jnp.maximum(i * hb - 1, 0), 0)),
                  pl.BlockSpec((1, 8, D_MODEL), lambda b, i: (b, jnp.minimum((i + 1) * hb, last8), 0)),
                  pl.BlockSpec((1, N_MOD, D_MODEL), lambda b, i: (b, 0, 0)),
                  full(g1), full(w_lat), full(gp), full(cw), full(gmg), full(ws16), full(bst)],
        out_specs=(tok(QKV_W), tok(LANES), tok(B_W), tok(A_W)),
        compiler_params=_cparams(("parallel", "arbitrary")),
        name="inproj_lat",
    )(x, x, x, mod3, g1, w_lat, gp, cw, gmg, ws16, bst)


def _inproj_ctx_call(ctx, mod3, ctx_row, g1, w_ctx, gp, cw):
    bsz, t, _ = ctx.shape
    full = lambda a: pl.BlockSpec(a.shape, lambda b: (0,) * a.ndim)
    tok = lambda w: pl.BlockSpec((1, t, w), lambda b: (b, 0, 0))
    return pl.pallas_call(
        functools.partial(_inproj_ctx_body, tm=t),
        out_shape=(jax.ShapeDtypeStruct((bsz, t, QKV_W), BF16),
                   jax.ShapeDtypeStruct((bsz, t, LANES), F32)),
        grid=(bsz,),
        in_specs=[tok(D_MODEL),
                  pl.BlockSpec((1, N_MOD, D_MODEL), lambda b: (ctx_row, 0, 0)),
                  full(g1), full(w_ctx), full(gp), full(cw)],
        out_specs=(tok(QKV_W), tok(LANES)),
        compiler_params=_cparams(("parallel",)),
        name="inproj_ctx",
    )(ctx, mod3, g1, w_ctx, gp, cw)


GDN_CHUNKS_PER_STEP = 8
GDN_STAGE_SKEW = 2
PACK_W = N_HEADS * GDN_CHUNK


def _per_head(tile, lanes, width):
    rows = tile.shape[0]
    if width == HEAD_DIM:
        return jnp.concatenate([jnp.broadcast_to(tile[:, l:l + 1], (rows, width)) for l in lanes], axis=1)
    head = lax.broadcasted_iota(I32, (rows, N_HEADS * width), 1) // width
    out = jnp.broadcast_to(tile[:, lanes[0]:lanes[0] + 1], (rows, N_HEADS * width))
    for h in range(1, N_HEADS):
        out = jnp.where(head == h, jnp.broadcast_to(tile[:, lanes[h]:lanes[h] + 1], (rows, N_HEADS * width)), out)
    return out


def _block_rows(x16, half_masks=None):
    rows, width = x16.shape
    per_head = width // N_HEADS
    zero = jnp.zeros((rows, LANES), x16.dtype)
    blocks = []
    for h in range(N_HEADS):
        tile = h * per_head // LANES
        kept = x16[:, tile * LANES:(tile + 1) * LANES]
        if per_head < LANES:
            kept = kept * half_masks[h * per_head % LANES // per_head]
        blocks.append(jnp.concatenate([kept if t == tile else zero for t in range(width // LANES)], axis=1))
    return jnp.concatenate(blocks, axis=0)


def _gdn_group_program(g, half_masks, turn):
    cs = GDN_CHUNK
    hd = HEAD_DIM
    q, k, v, beta, egc = g["q"], g["k"], g["v"], g["beta"], g["egc"]
    kb = k * beta
    decay = jnp.where(g["incl"], jnp.exp(jnp.where(g["incl"], g["gc_col"] - g["gc_row"], 0.0)), 0.0)
    kk = _dot_nt(jnp.concatenate([kb, q], axis=0).astype(BF16), _block_rows(k.astype(BF16)))
    yield
    a = jnp.where(g["strict"], kk[:cs] * decay, 0.0)
    attn = (kk[cs:] * decay).astype(BF16)
    m = -jnp.where(g["levels"][0], a, 0.0)
    for lm in g["levels"][1:]:
        m16 = m.astype(BF16)
        cm = jnp.where(lm, a, 0.0)
        x = cm + _dot(m16, _block_rows(cm.astype(BF16), half_masks))
        yield
        y = x + _dot(x.astype(BF16), _block_rows(m16, half_masks))
        yield
        m = m - y
    m16 = m.astype(BF16)
    vb = v * beta
    kbg = kb * egc
    u = vb + _dot(m16, _block_rows(vb.astype(BF16)))
    w = kbg + _dot(m16, _block_rows(kbg.astype(BF16)))
    qg = q * egc
    kg = (k * g["kdec"]).astype(BF16)
    yield
    while turn[g["backward"]] != g["order"]:
        yield
    s_ref = g["s_ref"]
    s = [s_ref[h] for h in range(N_HEADS)]
    wq = [_dot(jnp.concatenate([w[:, h * hd:(h + 1) * hd], qg[:, h * hd:(h + 1) * hd]], axis=0).astype(BF16),
               s[h].astype(BF16)) for h in range(N_HEADS)]
    yield
    v_new = (u - jnp.concatenate([r[:cs] for r in wq], axis=1)).astype(BF16)
    o = jnp.concatenate([r[cs:] for r in wq], axis=1) + _dot(attn, _block_rows(v_new))
    for h in range(N_HEADS):
        cols = slice(h * hd, (h + 1) * hd)
        s_ref[h] = s[h] * g["eg"][h] + _dot_tn(kg[:, cols], v_new[:, cols])
    g["o_ref"][0, pl.ds(g["r0"], cs), :] = o.astype(BF16)
    turn[g["backward"]] += 1


def _run_interleaved(programs, skew):
    live = {}
    tick = 0
    while live or tick <= skew * (len(programs) - 1):
        if tick % skew == 0 and tick // skew < len(programs):
            live[tick // skew] = programs[tick // skew]
        for key in sorted(live):
            for prog in live[key]:
                if next(prog, "done") == "done":
                    live[key] = [p for p in live[key] if p is not prog]
            if not live[key]:
                del live[key]
        tick += 1


def _gdn_body(qf_ref, qb_ref, gbf_ref, gbb_ref, s0f_ref, s0b_ref,
              of_ref, ob_ref, sff_ref, sfb_ref, sf_scr, sb_scr, *, tb, nt):
    t = pl.program_id(1)
    cs = GDN_CHUNK
    nch = tb // cs
    cps = min(GDN_CHUNKS_PER_STEP, nch)
    nh = N_HEADS

    @pl.when(t == 0)
    def _():
        sf_scr[...] = s0f_ref[0]
        sb_scr[...] = s0b_ref[0]

    ii = lax.broadcasted_iota(I32, (cs, PACK_W), 0)
    jj = lax.broadcasted_iota(I32, (cs, PACK_W), 1) & (cs - 1)
    incl_f, strict_f = jj <= ii, jj < ii
    incl_b, strict_b = jj >= ii, jj > ii
    levels = []
    sh = 0
    while (1 << sh) < cs:
        levels.append(((ii >> (sh + 1)) == (jj >> (sh + 1))) & ((ii >> sh) != (jj >> sh)))
        sh += 1
    lv_f = [lm & strict_f for lm in levels]
    lv_b = [lm & strict_b for lm in levels]
    row = lax.broadcasted_iota(I32, (cs, LANES), 0)
    lane = lax.broadcasted_iota(I32, (1, LANES), 1)
    half_of_tile = lax.broadcasted_iota(I32, (cs, LANES), 1) // cs
    half_masks = [(half_of_tile == i).astype(BF16) for i in range(LANES // cs)]

    def group(q_ref, r0, gate, csum, gt, backward, s_ref, o_ref, order):
        l0 = nh if backward else 0
        lanes = [l0 + h for h in range(nh)]
        last = 0 if backward else cs - 1
        ld = lambda c0: q_ref[0, pl.ds(r0, cs), c0:c0 + B_W].astype(F32)
        halves = [gt[l:l + 1, :] if (h % 2 == 0) != backward else pltpu.roll(gt[l:l + 1, :], cs, 1)
                  for h, l in enumerate(lanes)]
        gc_row = jnp.concatenate([jnp.where(lane < cs, halves[0], halves[1]),
                                  jnp.where(lane < cs, halves[2], halves[3])], axis=1)
        glast = csum[last:last + 1, :]
        return dict(
            q=ld(0), k=ld(B_W), v=ld(2 * B_W),
            beta=_per_head(gate, [2 * nh + l for l in lanes], HEAD_DIM),
            egc=_per_head(jnp.exp(csum), lanes, HEAD_DIM),
            kdec=_per_head(jnp.exp(glast - csum), lanes, HEAD_DIM),
            eg=[jnp.exp(csum[last:last + 1, l:l + 1]) for l in lanes],
            gc_col=_per_head(csum, lanes, cs), gc_row=gc_row,
            incl=incl_b if backward else incl_f, strict=strict_b if backward else strict_f,
            levels=lv_b if backward else lv_f, s_ref=s_ref, o_ref=o_ref, r0=r0, backward=backward, order=order)

    def step(n, carry):
        per_chunk = []
        for j in range(cps):
            rf = pl.multiple_of((n * cps + j) * cs, cs)
            rb = pl.multiple_of((nch - 1 - n * cps - j) * cs, cs)
            gf = gbf_ref[0, pl.ds(rf, cs), :]
            gb = gbb_ref[0, pl.ds(rb, cs), :]
            cf, cb = gf, gb
            s = 1
            while s < cs:
                cf = cf + jnp.where(row >= s, pltpu.roll(cf, s, 0), 0.0)
                cb = cb + jnp.where(row < cs - s, pltpu.roll(cb, cs - s, 0), 0.0)
                s *= 2
            gt = jnp.concatenate([cf, cb], axis=0).T
            per_chunk.append([group(qf_ref, rf, gf, cf, gt, False, sf_scr, of_ref, j),
                              group(qb_ref, rb, gb, cb, gt, True, sb_scr, ob_ref, j)])
        turn = {False: 0, True: 0}
        _run_interleaved([[_gdn_group_program(g, half_masks, turn) for g in pair] for pair in per_chunk],
                         GDN_STAGE_SKEW)
        return carry

    lax.fori_loop(0, nch // cps, step, 0)

    @pl.when(t == nt - 1)
    def _():
        sff_ref[0] = sf_scr[...]
        sfb_ref[0] = sb_scr[...]


def _gdn_call(qkv, gb, s0f, s0b, tb):
    bsz, t, _ = qkv.shape
    nt = t // tb
    assert (tb // GDN_CHUNK) % min(GDN_CHUNKS_PER_STEP, tb // GDN_CHUNK) == 0
    st = pl.BlockSpec((1, N_HEADS, HEAD_DIM, HEAD_DIM), lambda b, i: (b, 0, 0, 0))
    fwd = lambda w: pl.BlockSpec((1, tb, w), lambda b, i: (b, i, 0))
    bwd = lambda w: pl.BlockSpec((1, tb, w), lambda b, i: (b, nt - 1 - i, 0))
    return pl.pallas_call(
        functools.partial(_gdn_body, tb=tb, nt=nt),
        out_shape=(jax.ShapeDtypeStruct((bsz, t, B_W), BF16), jax.ShapeDtypeStruct((bsz, t, B_W), BF16),
                   jax.ShapeDtypeStruct((bsz, N_HEADS, HEAD_DIM, HEAD_DIM), F32),
                   jax.ShapeDtypeStruct((bsz, N_HEADS, HEAD_DIM, HEAD_DIM), F32)),
        grid=(bsz, nt),
        in_specs=[fwd(QKV_W), bwd(QKV_W), fwd(LANES), bwd(LANES), st, st],
        out_specs=(fwd(B_W), bwd(B_W), st, st),
        scratch_shapes=[pltpu.VMEM((N_HEADS, HEAD_DIM, HEAD_DIM), F32), pltpu.VMEM((N_HEADS, HEAD_DIM, HEAD_DIM), F32)],
        compiler_params=_cparams(("parallel", "arbitrary")),
        name="gdn",
    )(qkv, qkv, gb, gb, s0f, s0b)


def _mixout_body(x_ref, of_ref, ob_ref, z_ref, ya_ref, mod_ref, gng_ref, wout_ref, n2g_ref, wrh_ref, wrl_ref, br_ref,
                 h_ref, fin_ref, aff_ref, afft_ref, *, tm):
    o = of_ref[0].astype(F32) + ob_ref[0].astype(F32)
    z = z_ref[0].astype(F32)
    parts = [ya_ref[0]]
    for h in range(N_HEADS):
        c = slice(h * HEAD_DIM, (h + 1) * HEAD_DIM)
        oh = o[:, c]
        y = oh * lax.rsqrt(jnp.mean(oh * oh, axis=-1, keepdims=True) + NORM_EPS)
        parts.append((y * gng_ref[...] * _silu(z[:, c])).astype(BF16))
    mix = _dot(jnp.concatenate(parts, axis=1), wout_ref[...])
    hl = x_ref[0] + mod_ref[0, 2:3, :] * mix
    h_ref[0] = hl
    fin = _norm_mod(hl, n2g_ref[...], mod_ref[0, 3:4, :], mod_ref[0, 4:5, :])
    f_hi = fin.astype(BF16)
    fin_ref[0] = f_hi
    f_lo = (fin - f_hi.astype(F32)).astype(BF16)
    logits = _dot(f_hi, wrh_ref[...]) + _dot(f_lo, wrh_ref[...]) + _dot(f_hi, wrl_ref[...]) + br_ref[...]
    e = jnp.exp(logits - jnp.max(logits, axis=-1, keepdims=True))
    aff = e / jnp.sum(e, axis=-1, keepdims=True)
    aff_ref[0] = aff
    for j in range(tm // LANES):
        afft_ref[0, j] = aff[j * LANES:(j + 1) * LANES, :].T[0:N_EXPERTS, :]


def _mixout_call(x, o_f, o_b, z, ya, mod3, gng, wout16, n2g, wr_hi, wr_lo, br, tm):
    bsz, t, _ = x.shape
    full = lambda a: pl.BlockSpec(a.shape, lambda b, i: (0,) * a.ndim)
    tok = lambda w: pl.BlockSpec((1, tm, w), lambda b, i: (b, i, 0))
    return pl.pallas_call(
        functools.partial(_mixout_body, tm=tm),
        out_shape=(jax.ShapeDtypeStruct((bsz, t, D_MODEL), F32), jax.ShapeDtypeStruct((bsz, t, D_MODEL), BF16),
                   jax.ShapeDtypeStruct((bsz, t, LANES), F32),
                   jax.ShapeDtypeStruct((bsz, t // LANES, N_EXPERTS, LANES), F32)),
        grid=(bsz, t // tm),
        in_specs=[tok(D_MODEL), tok(B_W), tok(B_W), tok(B_W), tok(A_W),
                  pl.BlockSpec((1, N_MOD, D_MODEL), lambda b, i: (b, 0, 0)),
                  full(gng), full(wout16), full(n2g), full(wr_hi), full(wr_lo), full(br)],
        out_specs=(tok(D_MODEL), tok(D_MODEL), tok(LANES),
                   pl.BlockSpec((1, tm // LANES, N_EXPERTS, LANES), lambda b, i: (b, i, 0, 0))),
        compiler_params=_cparams(("parallel", "parallel")),
        name="mixout",
    )(x, o_f, o_b, z, ya, mod3, gng, wout16, n2g, wr_hi, wr_lo, br)


def _route_body(afft_ref, slott_ref, slot_ref, off_ref, *, t, cap):
    ne = N_EXPERTS
    npieces = t // LANES
    rows = npieces * ne

    def count(thr_col, strict):
        acc = jnp.zeros((ne, LANES), I32)
        for p in range(npieces):
            piece = afft_ref[0, p * ne:(p + 1) * ne, :]
            acc = acc + (piece > thr_col if strict else piece >= thr_col).astype(I32)
        return jnp.sum(acc, axis=1, keepdims=True)

    def search(i, thr):
        cand = thr | jnp.left_shift(jnp.int32(1), 30 - i)
        return jnp.where(count(pltpu.bitcast(cand, F32), False) >= cap, cand, thr)

    thr_bits = lax.fori_loop(0, 31, search, jnp.zeros((ne, 1), I32))
    thr = pltpu.bitcast(thr_bits, F32)
    need = (cap - count(thr, True)).astype(F32)

    x = afft_ref[0]
    thr_rows = jnp.concatenate([thr] * npieces, axis=0)
    need_rows = jnp.concatenate([need] * npieces, axis=0)
    gt = x > thr_rows
    eq = x == thr_rows
    ti = lax.broadcasted_iota(I32, (LANES, LANES), 0)
    tj = lax.broadcasted_iota(I32, (LANES, LANES), 1)
    triu = (ti <= tj).astype(BF16)
    ri = lax.broadcasted_iota(I32, (rows, rows), 0)
    rj = lax.broadcasted_iota(I32, (rows, rows), 1)
    earlier = (((ri & (ne - 1)) == (rj & (ne - 1))) & (rj < ri)).astype(BF16)

    def prefix(mask):
        inpiece = _dot(mask.astype(BF16), triu)
        total = jnp.broadcast_to(inpiece[:, LANES - 1:LANES], (rows, LANES)).astype(BF16)
        offset = _dot(earlier, total)
        return inpiece + offset, offset

    eq_rank, _ = prefix(eq)
    sel = gt | (eq & (eq_rank <= need_rows))
    sel_rank, sel_off = prefix(sel)
    slot = jnp.where(sel, sel_rank - 1.0, -1.0)
    slott_ref[0] = slot.astype(I32)
    off_ref[0] = sel_off.astype(I32)
    pad = jnp.zeros((LANES - ne, LANES), F32)
    for p in range(npieces):
        piece = jnp.concatenate([slot[p * ne:(p + 1) * ne, :], pad], axis=0)
        slot_ref[0, p * LANES:(p + 1) * LANES, :] = piece.T.astype(I32)


def _route_call(afft, cap):
    bsz, rows, _ = afft.shape
    t = rows // N_EXPERTS * LANES
    spec = lambda r: pl.BlockSpec((1, r, LANES), lambda b: (b, 0, 0))
    return pl.pallas_call(
        functools.partial(_route_body, t=t, cap=cap),
        out_shape=(jax.ShapeDtypeStruct((bsz, rows, LANES), I32),
                   jax.ShapeDtypeStruct((bsz, t, LANES), I32),
                   jax.ShapeDtypeStruct((bsz, rows, LANES), I32)),
        grid=(bsz,),
        in_specs=[spec(rows)],
        out_specs=(spec(rows), spec(t), spec(rows)),
        compiler_params=_cparams(("parallel",)),
        name="route",
    )(afft)


def _window_plan(base_ref, flat0, experts):
    starts, rounds = [], jnp.int32(0)
    for e in experts:
        lo = base_ref[flat0 + e]
        hi = base_ref[flat0 + N_EXPERTS + e]
        lo_al = (lo >> 4) << 4
        starts.append(lo_al)
        rounds = jnp.maximum(rounds, (hi - lo_al + SLOT_WIN - 1) // SLOT_WIN)
    return starts, rounds


def _window_start(start, r, cap):
    return pl.multiple_of(jnp.minimum(start + r * SLOT_WIN, cap), SLOT_ALIGN)


def _dispatch_body(base_ref, slott_ref, fin_ref, xe_ref, *, nchunk, sub, eh_n, cap):
    b, eh, ci = pl.program_id(0), pl.program_id(1), pl.program_id(2)
    rc = ROUTE_CHUNK

    @pl.when(ci == 0)
    def _():
        xe_ref[...] = jnp.zeros_like(xe_ref)

    srow = lax.broadcasted_iota(I32, (SLOT_WIN, rc), 0)
    for sc in range(sub):
        cc = ci * sub + sc
        flat0 = (b * (nchunk + 1) + cc) * N_EXPERTS + eh * eh_n
        f = fin_ref[0, sc * rc:(sc + 1) * rc, :]
        experts = list(range(eh_n))
        starts, rounds = _window_plan(base_ref, flat0, experts)

        def one_round(r, carry, starts=starts, f=f, sc=sc):
            rows = []
            wstart = [_window_start(starts[e], r, cap) for e in experts]
            for e in experts:
                tok_slot = jnp.concatenate(
                    [slott_ref[0, sc * (rc // LANES) + j, e:e + 1, :] for j in range(rc // LANES)], axis=1)
                rows.append((tok_slot == srow + wstart[e]).astype(BF16))
            prod = _dot(jnp.concatenate(rows, axis=0), f)
            for e in experts:
                win = pl.ds(wstart[e], SLOT_WIN)
                xe_ref[0, e, win, :] = xe_ref[0, e, win, :] + prod[e * SLOT_WIN:(e + 1) * SLOT_WIN].astype(BF16)
            return carry

        one_round(jnp.int32(0), 0)
        lax.fori_loop(1, rounds, one_round, 0)


def _dispatch_call(base_flat, slott, fin, cap):
    bsz, t, _ = fin.shape
    nchunk = t // ROUTE_CHUNK
    sub = 2
    eh_n = N_EXPERTS // 2
    sp = cap + SLOT_WIN
    grid_spec = pltpu.PrefetchScalarGridSpec(
        num_scalar_prefetch=1,
        grid=(bsz, N_EXPERTS // eh_n, nchunk // sub),
        in_specs=[pl.BlockSpec((1, sub * ROUTE_CHUNK // LANES, eh_n, LANES), lambda b, eh, ci, base: (b, ci, eh, 0)),
                  pl.BlockSpec((1, sub * ROUTE_CHUNK, D_MODEL), lambda b, eh, ci, base: (b, ci, 0))],
        out_specs=pl.BlockSpec((1, eh_n, sp, D_MODEL), lambda b, eh, ci, base: (b, eh, 0, 0)))
    return pl.pallas_call(
        functools.partial(_dispatch_body, nchunk=nchunk, sub=sub, eh_n=eh_n, cap=cap),
        out_shape=jax.ShapeDtypeStruct((bsz, N_EXPERTS, sp, D_MODEL), BF16),
        grid_spec=grid_spec,
        compiler_params=_cparams(("parallel", "parallel", "arbitrary")),
        name="dispatch",
    )(base_flat, slott, fin)


def _experts_body(xe_ref, wg_ref, wu_ref, wd_ref, y_ref, wg16, wu16, wd16, *, cap):
    @pl.when(pl.program_id(1) == 0)
    def _():
        wg16[...] = wg_ref[0].astype(BF16)
        wu16[...] = wu_ref[0].astype(BF16)
        wd16[...] = wd_ref[0].astype(BF16)

    x = xe_ref[0, 0, 0:cap, :]
    ft = 256
    acc = None
    for f in range(EXPERT_FF // ft):
        cols = slice(f * ft, (f + 1) * ft)
        hid = (_silu(_dot(x, wg16[:, cols])) * _dot(x, wu16[:, cols])).astype(BF16)
        part = _dot(hid, wd16[cols, :])
        acc = part if acc is None else acc + part
    y_ref[0, 0, 0:cap, :] = acc.astype(BF16)
    y_ref[0, 0, cap:, :] = jnp.zeros((y_ref.shape[2] - cap, D_MODEL), BF16)


def _experts_call(xe, w_gate, w_up, w_down, cap):
    bsz, _, sp, _ = xe.shape
    wspec = lambda shape: pl.BlockSpec((1,) + shape, lambda e, b: (e, 0, 0))
    slots = pl.BlockSpec((1, 1, sp, D_MODEL), lambda e, b: (b, e, 0, 0))
    return pl.pallas_call(
        functools.partial(_experts_body, cap=cap),
        out_shape=jax.ShapeDtypeStruct(xe.shape, BF16),
        grid=(N_EXPERTS, bsz),
        in_specs=[slots, wspec((D_MODEL, EXPERT_FF)), wspec((D_MODEL, EXPERT_FF)), wspec((EXPERT_FF, D_MODEL))],
        out_specs=slots,
        scratch_shapes=[pltpu.VMEM((D_MODEL, EXPERT_FF), BF16), pltpu.VMEM((D_MODEL, EXPERT_FF), BF16),
                        pltpu.VMEM((EXPERT_FF, D_MODEL), BF16)],
        compiler_params=_cparams(("arbitrary", "arbitrary")),
        name="experts",
    )(xe, w_gate, w_up, w_down)


SLOT_SPLIT = 32


def _combine_selectors():
    k = jnp.arange(LANES)[:, None]
    e_of_lane = jnp.arange(N_EXPERTS * SLOT_WIN)[None, :] // SLOT_WIN
    sel_gate = (k == e_of_lane).astype(BF16)
    sel_slot = (SLOT_SPLIT * (k == e_of_lane) + (k - N_EXPERTS == e_of_lane)).astype(BF16)
    return sel_slot, sel_gate


def _combine_body(base_ref, slot_ref, aff_ref, h_ref, y_ref, mod_ref, fng_ref, ssel_ref, gsel_ref, o_ref, acc_ref,
                  *, nchunk, sub, cap):
    b, ci = pl.program_id(0), pl.program_id(1)
    rc = ROUTE_CHUNK
    width = N_EXPERTS * SLOT_WIN
    lane = lax.broadcasted_iota(I32, (1, width), 1)
    lane_e = lane >> (SLOT_WIN.bit_length() - 1)
    lane_j = (lane & (SLOT_WIN - 1)).astype(F32)
    lane128 = lax.broadcasted_iota(I32, (rc, LANES), 1)
    experts = list(range(N_EXPERTS))
    for sc in range(sub):
        rows = slice(sc * rc, (sc + 1) * rc)
        flat0 = (b * (nchunk + 1) + ci * sub + sc) * N_EXPERTS
        s1 = slot_ref[0, rows, :] + 1
        halves = jnp.where(lane128 < N_EXPERTS, s1 >> (SLOT_SPLIT.bit_length() - 1),
                           pltpu.roll(s1 & (SLOT_SPLIT - 1), N_EXPERTS, 1))
        slot1 = _dot(halves.astype(F32).astype(BF16), ssel_ref[...])
        gates = _dot(aff_ref[0, rows, :].astype(BF16), gsel_ref[...])
        starts, rounds = _window_plan(base_ref, flat0, experts)

        def contribution(r, starts=starts, slot1=slot1, gates=gates):
            wstart = [_window_start(starts[e], r, cap) for e in experts]
            ywin = jnp.concatenate([y_ref[0, e, pl.ds(wstart[e], SLOT_WIN), :] for e in experts], axis=0)
            held = jnp.zeros((1, width), I32)
            for e in experts:
                held = jnp.where(lane_e == e, wstart[e] + 1, held)
            held = held.astype(F32) + lane_j
            s = jnp.where(slot1 == held, gates, 0.0).astype(BF16)
            return _dot(s, ywin)

        def extra_round(r, carry, contribution=contribution):
            acc_ref[...] += contribution(r)
            return carry

        acc_ref[...] = contribution(jnp.int32(0))
        lax.fori_loop(1, rounds, extra_round, 0)
        hl = h_ref[0, rows, :] + mod_ref[0, 5:6, :] * acc_ref[...]
        ms = jnp.mean(hl * hl, axis=-1, keepdims=True)
        o_ref[0, rows, :] = hl * lax.rsqrt(ms + NORM_EPS) * fng_ref[...]


def _combine_call(base_flat, slot, aff, h, y, mod3, fng):
    bsz, t, _ = h.shape
    nchunk = t // ROUTE_CHUNK
    sub = 2
    rc = ROUTE_CHUNK
    ssel, gsel = _combine_selectors()
    tok = lambda w: pl.BlockSpec((1, sub * rc, w), lambda b, i, base: (b, i, 0))
    full = lambda a: pl.BlockSpec(a.shape, lambda b, i, base: (0,) * a.ndim)
    grid_spec = pltpu.PrefetchScalarGridSpec(
        num_scalar_prefetch=1,
        grid=(bsz, nchunk // sub),
        in_specs=[tok(LANES), tok(LANES), tok(D_MODEL),
                  pl.BlockSpec((1,) + y.shape[1:], lambda b, i, base: (b, 0, 0, 0), pipeline_mode=pl.Buffered(1)),
                  pl.BlockSpec((1, N_MOD, D_MODEL), lambda b, i, base: (b, 0, 0)),
                  full(fng), full(ssel), full(gsel)],
        out_specs=tok(D_MODEL),
        scratch_shapes=[pltpu.VMEM((rc, D_MODEL), F32)])
    return pl.pallas_call(
        functools.partial(_combine_body, nchunk=nchunk, sub=sub, cap=y.shape[2] - SLOT_WIN),
        out_shape=jax.ShapeDtypeStruct(h.shape, F32),
        grid_spec=grid_spec,
        compiler_params=_cparams(("parallel", "arbitrary")),
        name="combine",
    )(base_flat, slot, aff, h, y, mod3, fng, ssel, gsel)


def _pad_lanes(a):
    return jnp.pad(a, ((0, 0), (0, LANES - a.shape[1])))


def kernel(x, c, ctx, c_ctx, w_mod, b_mod, norm1_g, norm2_g, w_in, conv_w, a_log, dt_bias, gdn_norm_g, gm_norm_g,
           gm_ws, gm_bs, w_out, w_router, b_router, w_gate, w_up, w_down, final_norm_g):
    bsz, t, _ = x.shape
    ctx_len = ctx.shape[1]
    assert w_mod.shape[0] == 1, "single-layer problem"
    assert t % 512 == 0 and ctx_len % GDN_CHUNK == 0 and bsz < 8
    cap = EC_CAPACITY * t // N_EXPERTS

    cs = jnp.zeros((8, D_MODEL), F32).at[:bsz].set(c).at[bsz].set(c_ctx)
    mod3 = _mod_call(cs, w_mod[0], b_mod[0][None, :]).reshape(8, N_MOD, D_MODEL)

    wl = w_in[0]
    n_state = QKV_W + STATE_COLS
    w_state = _pad_lanes(wl[:, QKV_W:n_state])
    w_lat = jnp.concatenate([wl[:, :QKV_W], wl[:, n_state:n_state + B_W], wl[:, n_state + B_W:], w_state],
                            axis=1).astype(BF16)
    w_ctx = jnp.concatenate([wl[:, :QKV_W], w_state], axis=1).astype(BF16)
    gp = jnp.zeros((8, LANES), F32).at[0, :2 * N_HEADS].set(a_log[0].reshape(-1)).at[1, :2 * N_HEADS].set(
        dt_bias[0].reshape(-1))
    g1 = norm1_g[0][None, :]
    cw = jnp.zeros((8, QKV_W), F32).at[:conv_w.shape[1]].set(conv_w[0])

    qkv_c, gb_c = _inproj_ctx_call(ctx, mod3, bsz, g1, w_ctx, gp, cw)
    zero_state = jnp.zeros((bsz, N_HEADS, HEAD_DIM, HEAD_DIM), F32)
    _, _, s_f, s_b = _gdn_call(qkv_c, gb_c, zero_state, zero_state, ctx_len)

    qkv, gb, z, ya = _inproj_lat_call(x, mod3, g1, w_lat, gp, cw, gm_norm_g[0][None, :], gm_ws[0].astype(BF16),
                                      _pad_lanes(gm_bs[0].T), 512)
    o_f, o_b, _, _ = _gdn_call(qkv, gb, s_f, s_b, 512)
    wr = _pad_lanes(w_router[0])
    wr_hi = wr.astype(BF16)
    wr_lo = (wr - wr_hi.astype(F32)).astype(BF16)
    br = jnp.full((1, LANES), -1e30, F32).at[0, :N_EXPERTS].set(b_router[0])
    h, fin, aff, afft = _mixout_call(x, o_f, o_b, z, ya, mod3, gdn_norm_g[0][None, :], w_out[0].astype(BF16),
                                     norm2_g[0][None, :], wr_hi, wr_lo, br, 512)

    npieces = t // LANES
    slott, slot, off = _route_call(afft.reshape(bsz, npieces * N_EXPERTS, LANES), cap)
    base = off[:, :, 0].reshape(bsz, npieces, N_EXPERTS)[:, ::ROUTE_CHUNK // LANES, :]
    base_flat = jnp.concatenate([base, jnp.full((bsz, 1, N_EXPERTS), cap, I32)], axis=1).reshape(-1)
    xe = _dispatch_call(base_flat, slott.reshape(bsz, npieces, N_EXPERTS, LANES), fin, cap)
    y = _experts_call(xe, w_gate[0], w_up[0], w_down[0], cap)
    return _combine_call(base_flat, slot, aff, h, y, mod3, final_norm_g[None, :])
```

```python
import functools

import jax
import jax.numpy as jnp
from jax import lax
from jax.experimental import pallas as pl
from jax.experimental.pallas import tpu as pltpu

F32 = jnp.float32
BF16 = jnp.bfloat16
I32 = jnp.int32

D_MODEL = 1024
N_MOD = 6
N_HEADS = 4
HEAD_DIM = 128
B_W = N_HEADS * HEAD_DIM
QKV_W = 3 * B_W
A_W = 512
A_GROUPS = 4
A_CHUNK = 128
GDN_CHUNK = 64
N_EXPERTS = 16
EC_CAPACITY = 2
EXPERT_FF = 1024
NORM_EPS = 1e-6
LANES = 128
STATE_COLS = 4 * N_HEADS

ROUTE_CHUNK = 256
SLOT_WIN = 64
SLOT_ALIGN = 16
VMEM_LIMIT = 56 * 1024 * 1024


def _cparams(sem):
    return pltpu.CompilerParams(dimension_semantics=sem, vmem_limit_bytes=VMEM_LIMIT)


def _dot(a, b):
    return jnp.dot(a, b, preferred_element_type=F32)


def _dot_nt(a, b):
    return lax.dot_general(a, b, (((1,), (1,)), ((), ())), preferred_element_type=F32)


def _dot_tn(a, b):
    return lax.dot_general(a, b, (((0,), (0,)), ((), ())), preferred_element_type=F32)


def _silu(x):
    return x * jax.nn.sigmoid(x)


def _mod_body(c_ref, w_ref, b_ref, o_ref):
    s = _silu(c_ref[...])
    o_ref[...] = _dot(s.astype(BF16), w_ref[...].astype(BF16)) + b_ref[...]


def _mod_call(cs, w_mod, b_mod):
    n = w_mod.shape[1] // D_MODEL
    return pl.pallas_call(
        _mod_body,
        out_shape=jax.ShapeDtypeStruct((8, w_mod.shape[1]), F32),
        grid=(n,),
        in_specs=[pl.BlockSpec((8, D_MODEL), lambda j: (0, 0)),
                  pl.BlockSpec((D_MODEL, D_MODEL), lambda j: (0, j)),
                  pl.BlockSpec((1, D_MODEL), lambda j: (0, j))],
        out_specs=pl.BlockSpec((8, D_MODEL), lambda j: (0, j)),
        compiler_params=_cparams(("arbitrary",)),
        name="mod",
    )(cs, w_mod, b_mod)


def _norm_mod(x, g, shift, scale):
    ms = jnp.mean(x * x, axis=-1, keepdims=True)
    return (x * lax.rsqrt(ms + NORM_EPS) * g) * (1.0 + scale) + shift


def _gate_streams(st, gp_ref):
    lane = lax.broadcasted_iota(I32, st.shape, 1)
    g = -jnp.exp(gp_ref[0:1, :]) * jax.nn.softplus(st + gp_ref[1:2, :])
    beta = jax.nn.sigmoid(st)
    return jnp.where(lane < 2 * N_HEADS, g, jnp.where(lane < STATE_COLS, beta, 0.0))


def _conv_qkv(qkv, prev_row, next_row, cw_ref, out_ref, tm):
    cs = GDN_CHUNK
    nsub = tm // cs
    w0, w1, w2 = cw_ref[0:1, :], cw_ref[1:2, :], cw_ref[2:3, :]
    row = lax.broadcasted_iota(I32, (cs, 1), 0)
    for c in range(nsub):
        rows = slice(c * cs, (c + 1) * cs)
        x = qkv[rows]
        prow = prev_row if c == 0 else qkv[c * cs - 1:c * cs]
        nrow = next_row if c == nsub - 1 else qkv[(c + 1) * cs:(c + 1) * cs + 1]
        xp = jnp.where(row == 0, prow, pltpu.roll(x, 1, 0))
        xn = jnp.where(row == cs - 1, nrow, pltpu.roll(x, cs - 1, 0))
        y = _silu(xp * w0 + x * w1 + xn * w2)
        for h in range(N_HEADS):
            cq = slice(h * HEAD_DIM, (h + 1) * HEAD_DIM)
            ck = slice(B_W + h * HEAD_DIM, B_W + (h + 1) * HEAD_DIM)
            q = y[:, cq]
            k = y[:, ck]
            out_ref[0, rows, cq] = (q * (lax.rsqrt(jnp.sum(q * q, axis=-1, keepdims=True) + NORM_EPS)
                                         * (HEAD_DIM ** -0.5))).astype(BF16)
            out_ref[0, rows, ck] = (k * lax.rsqrt(jnp.sum(k * k, axis=-1, keepdims=True) + NORM_EPS)).astype(BF16)
        out_ref[0, rows, 2 * B_W:3 * B_W] = y[:, 2 * B_W:3 * B_W].astype(BF16)


def _inproj_lat_body(x_ref, xp_ref, xn_ref, mod_ref, g1_ref, w_ref, gp_ref, cw_ref, gmg_ref, ws_ref, bst_ref,
                     qkv_ref, gb_ref, z_ref, ya_ref, *, tm):
    i = pl.program_id(1)
    shift, scale = mod_ref[0, 0:1, :], mod_ref[0, 1:2, :]
    a = _norm_mod(x_ref[0], g1_ref[...], shift, scale).astype(BF16)
    xh = jnp.concatenate([xp_ref[0], xn_ref[0]], axis=0)
    halo = _dot(_norm_mod(xh, g1_ref[...], shift, scale).astype(BF16), w_ref[:, 0:QKV_W])
    prev_row = jnp.where(i == 0, 0.0, halo[7:8, :])
    next_row = jnp.where(i == pl.num_programs(1) - 1, 0.0, halo[8:9, :])
    _conv_qkv(_dot(a, w_ref[:, 0:QKV_W]), prev_row, next_row, cw_ref, qkv_ref, tm)
    z_ref[0] = _dot(a, w_ref[:, QKV_W:QKV_W + B_W]).astype(BF16)
    c_uv = QKV_W + B_W
    gb_ref[0] = _gate_streams(_dot(a, w_ref[:, c_uv + 2 * A_W:c_uv + 2 * A_W + LANES]), gp_ref)
    uv = _dot(a, w_ref[:, c_uv:c_uv + 2 * A_W])
    uv = 0.5 * uv * (1.0 + lax.erf(uv * 0.7071067811865476))
    gd = A_W // A_GROUPS
    for grp in range(A_GROUPS):
        v = uv[:, A_W + grp * gd:A_W + (grp + 1) * gd]
        vn = v * lax.rsqrt(jnp.mean(v * v, axis=-1, keepdims=True) + NORM_EPS) * gmg_ref[:, grp * gd:(grp + 1) * gd]
        vn = vn.astype(BF16)
        bias = bst_ref[:, grp:grp + 1]
        for c in range(tm // A_CHUNK):
            rows = slice(c * A_CHUNK, (c + 1) * A_CHUNK)
            s = _dot(ws_ref[grp], vn[rows]) + bias
            ya_ref[0, rows, grp * gd:(grp + 1) * gd] = (uv[rows, grp * gd:(grp + 1) * gd] * s).astype(BF16)


def _inproj_ctx_body(x_ref, mod_ref, g1_ref, w_ref, gp_ref, cw_ref, qkv_ref, gb_ref, *, tm):
    a = _norm_mod(x_ref[0], g1_ref[...], mod_ref[0, 0:1, :], mod_ref[0, 1:2, :]).astype(BF16)
    edge = jnp.zeros((1, QKV_W), F32)
    _conv_qkv(_dot(a, w_ref[:, 0:QKV_W]), edge, edge, cw_ref, qkv_ref, tm)
    gb_ref[0] = _gate_streams(_dot(a, w_ref[:, QKV_W:QKV_W + LANES]), gp_ref)


def _inproj_lat_call(x, mod3, g1, w_lat, gp, cw, gmg, ws16, bst, tm):
    bsz, t, _ = x.shape
    hb = tm // 8
    last8 = t // 8 - 1
    full = lambda a: pl.BlockSpec(a.shape, lambda b, i: (0,) * a.ndim)
    tok = lambda w: pl.BlockSpec((1, tm, w), lambda b, i: (b, i, 0))
    return pl.pallas_call(
        functools.partial(_inproj_lat_body, tm=tm),
        out_shape=(jax.ShapeDtypeStruct((bsz, t, QKV_W), BF16),
                   jax.ShapeDtypeStruct((bsz, t, LANES), F32),
                   jax.ShapeDtypeStruct((bsz, t, B_W), BF16),
                   jax.ShapeDtypeStruct((bsz, t, A_W), BF16)),
        grid=(bsz, t // tm),
        in_specs=[tok(D_MODEL),
                  pl.BlockSpec((1, 8, D_MODEL), lambda b, i: (b, jnp.maximum(i * hb - 1, 0), 0)),
                  pl.BlockSpec((1, 8, D_MODEL), lambda b, i: (b, jnp.minimum((i + 1) * hb, last8), 0)),
                  pl.BlockSpec((1, N_MOD, D_MODEL), lambda b, i: (b, 0, 0)),
                  full(g1), full(w_lat), full(gp), full(cw), full(gmg), full(ws16), full(bst)],
        out_specs=(tok(QKV_W), tok(LANES), tok(B_W), tok(A_W)),
        compiler_params=_cparams(("parallel", "arbitrary")),
        name="inproj_lat",
    )(x, x, x, mod3, g1, w_lat, gp, cw, gmg, ws16, bst)


def _inproj_ctx_call(ctx, mod3, ctx_row, g1, w_ctx, gp, cw):
    bsz, t, _ = ctx.shape
    full = lambda a: pl.BlockSpec(a.shape, lambda b: (0,) * a.ndim)
    tok = lambda w: pl.BlockSpec((1, t, w), lambda b: (b, 0, 0))
    return pl.pallas_call(
        functools.partial(_inproj_ctx_body, tm=t),
        out_shape=(jax.ShapeDtypeStruct((bsz, t, QKV_W), BF16),
                   jax.ShapeDtypeStruct((bsz, t, LANES), F32)),
        grid=(bsz,),
        in_specs=[tok(D_MODEL),
                  pl.BlockSpec((1, N_MOD, D_MODEL), lambda b: (ctx_row, 0, 0)),
                  full(g1), full(w_ctx), full(gp), full(cw)],
        out_specs=(tok(QKV_W), tok(LANES)),
        compiler_params=_cparams(("parallel",)),
        name="inproj_ctx",
    )(ctx, mod3, g1, w_ctx, gp, cw)


GDN_CHUNKS_PER_STEP = 8
GDN_STAGE_SKEW = 2
PACK_W = N_HEADS * GDN_CHUNK


def _per_head(tile, lanes, width):
    rows = tile.shape[0]
    if width == HEAD_DIM:
        return jnp.concatenate([jnp.broadcast_to(tile[:, l:l + 1], (rows, width)) for l in lanes], axis=1)
    head = lax.broadcasted_iota(I32, (rows, N_HEADS * width), 1) // width
    out = jnp.broadcast_to(tile[:, lanes[0]:lanes[0] + 1], (rows, N_HEADS * width))
    for h in range(1, N_HEADS):
        out = jnp.where(head == h, jnp.broadcast_to(tile[:, lanes[h]:lanes[h] + 1], (rows, N_HEADS * width)), out)
    return out


def _block_rows(x16, half_masks=None):
    rows, width = x16.shape
    per_head = width // N_HEADS
    zero = jnp.zeros((rows, LANES), x16.dtype)
    blocks = []
    for h in range(N_HEADS):
        tile = h * per_head // LANES
        kept = x16[:, tile * LANES:(tile + 1) * LANES]
        if per_head < LANES:
            kept = kept * half_masks[h * per_head % LANES // per_head]
        blocks.append(jnp.concatenate([kept if t == tile else zero for t in range(width // LANES)], axis=1))
    return jnp.concatenate(blocks, axis=0)


def _gdn_group_program(g, half_masks, turn):
    cs = GDN_CHUNK
    hd = HEAD_DIM
    q, k, v, beta, egc = g["q"], g["k"], g["v"], g["beta"], g["egc"]
    kb = k * beta
    decay = jnp.where(g["incl"], jnp.exp(jnp.where(g["incl"], g["gc_col"] - g["gc_row"], 0.0)), 0.0)
    kk = _dot_nt(jnp.concatenate([kb, q], axis=0).astype(BF16), _block_rows(k.astype(BF16)))
    yield
    a = jnp.where(g["strict"], kk[:cs] * decay, 0.0)
    attn = (kk[cs:] * decay).astype(BF16)
    m = -jnp.where(g["levels"][0], a, 0.0)
    for lm in g["levels"][1:]:
        m16 = m.astype(BF16)
        cm = jnp.where(lm, a, 0.0)
        x = cm + _dot(m16, _block_rows(cm.astype(BF16), half_masks))
        yield
        y = x + _dot(x.astype(BF16), _block_rows(m16, half_masks))
        yield
        m = m - y
    m16 = m.astype(BF16)
    vb = v * beta
    kbg = kb * egc
    u = vb + _dot(m16, _block_rows(vb.astype(BF16)))
    w = kbg + _dot(m16, _block_rows(kbg.astype(BF16)))
    qg = q * egc
    kg = (k * g["kdec"]).astype(BF16)
    yield
    while turn[g["backward"]] != g["order"]:
        yield
    s_ref = g["s_ref"]
    s = [s_ref[h] for h in range(N_HEADS)]
    wq = [_dot(jnp.concatenate([w[:, h * hd:(h + 1) * hd], qg[:, h * hd:(h + 1) * hd]], axis=0).astype(BF16),
               s[h].astype(BF16)) for h in range(N_HEADS)]
    yield
    v_new = (u - jnp.concatenate([r[:cs] for r in wq], axis=1)).astype(BF16)
    o = jnp.concatenate([r[cs:] for r in wq], axis=1) + _dot(attn, _block_rows(v_new))
    for h in range(N_HEADS):
        cols = slice(h * hd, (h + 1) * hd)
        s_ref[h] = s[h] * g["eg"][h] + _dot_tn(kg[:, cols], v_new[:, cols])
    g["o_ref"][0, pl.ds(g["r0"], cs), :] = o.astype(BF16)
    turn[g["backward"]] += 1


def _run_interleaved(programs, skew):
    live = {}
    tick = 0
    while live or tick <= skew * (len(programs) - 1):
        if tick % skew == 0 and tick // skew < len(programs):
            live[tick // skew] = programs[tick // skew]
        for key in sorted(live):
            for prog in live[key]:
                if next(prog, "done") == "done":
                    live[key] = [p for p in live[key] if p is not prog]
            if not live[key]:
                del live[key]
        tick += 1


def _gdn_body(qf_ref, qb_ref, gbf_ref, gbb_ref, s0f_ref, s0b_ref,
              of_ref, ob_ref, sff_ref, sfb_ref, sf_scr, sb_scr, *, tb, nt):
    t = pl.program_id(1)
    cs = GDN_CHUNK
    nch = tb // cs
    cps = min(GDN_CHUNKS_PER_STEP, nch)
    nh = N_HEADS

    @pl.when(t == 0)
    def _():
        sf_scr[...] = s0f_ref[0]
        sb_scr[...] = s0b_ref[0]

    ii = lax.broadcasted_iota(I32, (cs, PACK_W), 0)
    jj = lax.broadcasted_iota(I32, (cs, PACK_W), 1) & (cs - 1)
    incl_f, strict_f = jj <= ii, jj < ii
    incl_b, strict_b = jj >= ii, jj > ii
    levels = []
    sh = 0
    while (1 << sh) < cs:
        levels.append(((ii >> (sh + 1)) == (jj >> (sh + 1))) & ((ii >> sh) != (jj >> sh)))
        sh += 1
    lv_f = [lm & strict_f for lm in levels]
    lv_b = [lm & strict_b for lm in levels]
    row = lax.broadcasted_iota(I32, (cs, LANES), 0)
    lane = lax.broadcasted_iota(I32, (1, LANES), 1)
    half_of_tile = lax.broadcasted_iota(I32, (cs, LANES), 1) // cs
    half_masks = [(half_of_tile == i).astype(BF16) for i in range(LANES // cs)]

    def group(q_ref, r0, gate, csum, gt, backward, s_ref, o_ref, order):
        l0 = nh if backward else 0
        lanes = [l0 + h for h in range(nh)]
        last = 0 if backward else cs - 1
        ld = lambda c0: q_ref[0, pl.ds(r0, cs), c0:c0 + B_W].astype(F32)
        halves = [gt[l:l + 1, :] if (h % 2 == 0) != backward else pltpu.roll(gt[l:l + 1, :], cs, 1)
                  for h, l in enumerate(lanes)]
        gc_row = jnp.concatenate([jnp.where(lane < cs, halves[0], halves[1]),
                                  jnp.where(lane < cs, halves[2], halves[3])], axis=1)
        glast = csum[last:last + 1, :]
        return dict(
            q=ld(0), k=ld(B_W), v=ld(2 * B_W),
            beta=_per_head(gate, [2 * nh + l for l in lanes], HEAD_DIM),
            egc=_per_head(jnp.exp(csum), lanes, HEAD_DIM),
            kdec=_per_head(jnp.exp(glast - csum), lanes, HEAD_DIM),
            eg=[jnp.exp(csum[last:last + 1, l:l + 1]) for l in lanes],
            gc_col=_per_head(csum, lanes, cs), gc_row=gc_row,
            incl=incl_b if backward else incl_f, strict=strict_b if backward else strict_f,
            levels=lv_b if backward else lv_f, s_ref=s_ref, o_ref=o_ref, r0=r0, backward=backward, order=order)

    def step(n, carry):
        per_chunk = []
        for j in range(cps):
            rf = pl.multiple_of((n * cps + j) * cs, cs)
            rb = pl.multiple_of((nch - 1 - n * cps - j) * cs, cs)
            gf = gbf_ref[0, pl.ds(rf, cs), :]
            gb = gbb_ref[0, pl.ds(rb, cs), :]
            cf, cb = gf, gb
            s = 1
            while s < cs:
                cf = cf + jnp.where(row >= s, pltpu.roll(cf, s, 0), 0.0)
                cb = cb + jnp.where(row < cs - s, pltpu.roll(cb, cs - s, 0), 0.0)
                s *= 2
            gt = jnp.concatenate([cf, cb], axis=0).T
            per_chunk.append([group(qf_ref, rf, gf, cf, gt, False, sf_scr, of_ref, j),
                              group(qb_ref, rb, gb, cb, gt, True, sb_scr, ob_ref, j)])
        turn = {False: 0, True: 0}
        _run_interleaved([[_gdn_group_program(g, half_masks, turn) for g in pair] for pair in per_chunk],
                         GDN_STAGE_SKEW)
        return carry

    lax.fori_loop(0, nch // cps, step, 0)

    @pl.when(t == nt - 1)
    def _():
        sff_ref[0] = sf_scr[...]
        sfb_ref[0] = sb_scr[...]


def _gdn_call(qkv, gb, s0f, s0b, tb):
    bsz, t, _ = qkv.shape
    nt = t // tb
    assert (tb // GDN_CHUNK) % min(GDN_CHUNKS_PER_STEP, tb // GDN_CHUNK) == 0
    st = pl.BlockSpec((1, N_HEADS, HEAD_DIM, HEAD_DIM), lambda b, i: (b, 0, 0, 0))
    fwd = lambda w: pl.BlockSpec((1, tb, w), lambda b, i: (b, i, 0))
    bwd = lambda w: pl.BlockSpec((1, tb, w), lambda b, i: (b, nt - 1 - i, 0))
    return pl.pallas_call(
        functools.partial(_gdn_body, tb=tb, nt=nt),
        out_shape=(jax.ShapeDtypeStruct((bsz, t, B_W), BF16), jax.ShapeDtypeStruct((bsz, t, B_W), BF16),
                   jax.ShapeDtypeStruct((bsz, N_HEADS, HEAD_DIM, HEAD_DIM), F32),
                   jax.ShapeDtypeStruct((bsz, N_HEADS, HEAD_DIM, HEAD_DIM), F32)),
        grid=(bsz, nt),
        in_specs=[fwd(QKV_W), bwd(QKV_W), fwd(LANES), bwd(LANES), st, st],
        out_specs=(fwd(B_W), bwd(B_W), st, st),
        scratch_shapes=[pltpu.VMEM((N_HEADS, HEAD_DIM, HEAD_DIM), F32), pltpu.VMEM((N_HEADS, HEAD_DIM, HEAD_DIM), F32)],
        compiler_params=_cparams(("parallel", "arbitrary")),
        name="gdn",
    )(qkv, qkv, gb, gb, s0f, s0b)


def _mixout_body(x_ref, of_ref, ob_ref, z_ref, ya_ref, mod_ref, gng_ref, wout_ref, n2g_ref, wrh_ref, wrl_ref, br_ref,
                 h_ref, fin_ref, aff_ref, afft_ref, *, tm):
    o = of_ref[0].astype(F32) + ob_ref[0].astype(F32)
    z = z_ref[0].astype(F32)
    parts = [ya_ref[0]]
    for h in range(N_HEADS):
        c = slice(h * HEAD_DIM, (h + 1) * HEAD_DIM)
        oh = o[:, c]
        y = oh * lax.rsqrt(jnp.mean(oh * oh, axis=-1, keepdims=True) + NORM_EPS)
        parts.append((y * gng_ref[...] * _silu(z[:, c])).astype(BF16))
    mix = _dot(jnp.concatenate(parts, axis=1), wout_ref[...])
    hl = x_ref[0] + mod_ref[0, 2:3, :] * mix
    h_ref[0] = hl
    fin = _norm_mod(hl, n2g_ref[...], mod_ref[0, 3:4, :], mod_ref[0, 4:5, :])
    f_hi = fin.astype(BF16)
    fin_ref[0] = f_hi
    f_lo = (fin - f_hi.astype(F32)).astype(BF16)
    logits = _dot(f_hi, wrh_ref[...]) + _dot(f_lo, wrh_ref[...]) + _dot(f_hi, wrl_ref[...]) + br_ref[...]
    e = jnp.exp(logits - jnp.max(logits, axis=-1, keepdims=True))
    aff = e / jnp.sum(e, axis=-1, keepdims=True)
    aff_ref[0] = aff
    for j in range(tm // LANES):
        afft_ref[0, j] = aff[j * LANES:(j + 1) * LANES, :].T[0:N_EXPERTS, :]


def _mixout_call(x, o_f, o_b, z, ya, mod3, gng, wout16, n2g, wr_hi, wr_lo, br, tm):
    bsz, t, _ = x.shape
    full = lambda a: pl.BlockSpec(a.shape, lambda b, i: (0,) * a.ndim)
    tok = lambda w: pl.BlockSpec((1, tm, w), lambda b, i: (b, i, 0))
    return pl.pallas_call(
        functools.partial(_mixout_body, tm=tm),
        out_shape=(jax.ShapeDtypeStruct((bsz, t, D_MODEL), F32), jax.ShapeDtypeStruct((bsz, t, D_MODEL), BF16),
                   jax.ShapeDtypeStruct((bsz, t, LANES), F32),
                   jax.ShapeDtypeStruct((bsz, t // LANES, N_EXPERTS, LANES), F32)),
        grid=(bsz, t // tm),
        in_specs=[tok(D_MODEL), tok(B_W), tok(B_W), tok(B_W), tok(A_W),
                  pl.BlockSpec((1, N_MOD, D_MODEL), lambda b, i: (b, 0, 0)),
                  full(gng), full(wout16), full(n2g), full(wr_hi), full(wr_lo), full(br)],
        out_specs=(tok(D_MODEL), tok(D_MODEL), tok(LANES),
                   pl.BlockSpec((1, tm // LANES, N_EXPERTS, LANES), lambda b, i: (b, i, 0, 0))),
        compiler_params=_cparams(("parallel", "parallel")),
        name="mixout",
    )(x, o_f, o_b, z, ya, mod3, gng, wout16, n2g, wr_hi, wr_lo, br)


def _route_body(afft_ref, slott_ref, slot_ref, off_ref, *, t, cap):
    ne = N_EXPERTS
    npieces = t // LANES
    rows = npieces * ne

    def count(thr_col, strict):
        acc = jnp.zeros((ne, LANES), I32)
        for p in range(npieces):
            piece = afft_ref[0, p * ne:(p + 1) * ne, :]
            acc = acc + (piece > thr_col if strict else piece >= thr_col).astype(I32)
        return jnp.sum(acc, axis=1, keepdims=True)

    def search(i, thr):
        cand = thr | jnp.left_shift(jnp.int32(1), 30 - i)
        return jnp.where(count(pltpu.bitcast(cand, F32), False) >= cap, cand, thr)

    thr_bits = lax.fori_loop(0, 31, search, jnp.zeros((ne, 1), I32))
    thr = pltpu.bitcast(thr_bits, F32)
    need = (cap - count(thr, True)).astype(F32)

    x = afft_ref[0]
    thr_rows = jnp.concatenate([thr] * npieces, axis=0)
    need_rows = jnp.concatenate([need] * npieces, axis=0)
    gt = x > thr_rows
    eq = x == thr_rows
    ti = lax.broadcasted_iota(I32, (LANES, LANES), 0)
    tj = lax.broadcasted_iota(I32, (LANES, LANES), 1)
    triu = (ti <= tj).astype(BF16)
    ri = lax.broadcasted_iota(I32, (rows, rows), 0)
    rj = lax.broadcasted_iota(I32, (rows, rows), 1)
    earlier = (((ri & (ne - 1)) == (rj & (ne - 1))) & (rj < ri)).astype(BF16)

    def prefix(mask):
        inpiece = _dot(mask.astype(BF16), triu)
        total = jnp.broadcast_to(inpiece[:, LANES - 1:LANES], (rows, LANES)).astype(BF16)
        offset = _dot(earlier, total)
        return inpiece + offset, offset

    eq_rank, _ = prefix(eq)
    sel = gt | (eq & (eq_rank <= need_rows))
    sel_rank, sel_off = prefix(sel)
    slot = jnp.where(sel, sel_rank - 1.0, -1.0)
    slott_ref[0] = slot.astype(I32)
    off_ref[0] = sel_off.astype(I32)
    pad = jnp.zeros((LANES - ne, LANES), F32)
    for p in range(npieces):
        piece = jnp.concatenate([slot[p * ne:(p + 1) * ne, :], pad], axis=0)
        slot_ref[0, p * LANES:(p + 1) * LANES, :] = piece.T.astype(I32)


def _route_call(afft, cap):
    bsz, rows, _ = afft.shape
    t = rows // N_EXPERTS * LANES
    spec = lambda r: pl.BlockSpec((1, r, LANES), lambda b: (b, 0, 0))
    return pl.pallas_call(
        functools.partial(_route_body, t=t, cap=cap),
        out_shape=(jax.ShapeDtypeStruct((bsz, rows, LANES), I32),
                   jax.ShapeDtypeStruct((bsz, t, LANES), I32),
                   jax.ShapeDtypeStruct((bsz, rows, LANES), I32)),
        grid=(bsz,),
        in_specs=[spec(rows)],
        out_specs=(spec(rows), spec(t), spec(rows)),
        compiler_params=_cparams(("parallel",)),
        name="route",
    )(afft)


def _window_plan(base_ref, flat0, experts):
    starts, rounds = [], jnp.int32(0)
    for e in experts:
        lo = base_ref[flat0 + e]
        hi = base_ref[flat0 + N_EXPERTS + e]
        lo_al = (lo >> 4) << 4
        starts.append(lo_al)
        rounds = jnp.maximum(rounds, (hi - lo_al + SLOT_WIN - 1) // SLOT_WIN)
    return starts, rounds


def _window_start(start, r, cap):
    return pl.multiple_of(jnp.minimum(start + r * SLOT_WIN, cap), SLOT_ALIGN)


def _dispatch_body(base_ref, slott_ref, fin_ref, xe_ref, *, nchunk, sub, eh_n, cap):
    b, eh, ci = pl.program_id(0), pl.program_id(1), pl.program_id(2)
    rc = ROUTE_CHUNK

    @pl.when(ci == 0)
    def _():
        xe_ref[...] = jnp.zeros_like(xe_ref)

    srow = lax.broadcasted_iota(I32, (SLOT_WIN, rc), 0)
    for sc in range(sub):
        cc = ci * sub + sc
        flat0 = (b * (nchunk + 1) + cc) * N_EXPERTS + eh * eh_n
        f = fin_ref[0, sc * rc:(sc + 1) * rc, :]
        experts = list(range(eh_n))
        starts, rounds = _window_plan(base_ref, flat0, experts)

        def one_round(r, carry, starts=starts, f=f, sc=sc):
            rows = []
            wstart = [_window_start(starts[e], r, cap) for e in experts]
            for e in experts:
                tok_slot = jnp.concatenate(
                    [slott_ref[0, sc * (rc // LANES) + j, e:e + 1, :] for j in range(rc // LANES)], axis=1)
                rows.append((tok_slot == srow + wstart[e]).astype(BF16))
            prod = _dot(jnp.concatenate(rows, axis=0), f)
            for e in experts:
                win = pl.ds(wstart[e], SLOT_WIN)
                xe_ref[0, e, win, :] = xe_ref[0, e, win, :] + prod[e * SLOT_WIN:(e + 1) * SLOT_WIN].astype(BF16)
            return carry

        one_round(jnp.int32(0), 0)
        lax.fori_loop(1, rounds, one_round, 0)


def _dispatch_call(base_flat, slott, fin, cap):
    bsz, t, _ = fin.shape
    nchunk = t // ROUTE_CHUNK
    sub = 2
    eh_n = N_EXPERTS // 2
    sp = cap + SLOT_WIN
    grid_spec = pltpu.PrefetchScalarGridSpec(
        num_scalar_prefetch=1,
        grid=(bsz, N_EXPERTS // eh_n, nchunk // sub),
        in_specs=[pl.BlockSpec((1, sub * ROUTE_CHUNK // LANES, eh_n, LANES), lambda b, eh, ci, base: (b, ci, eh, 0)),
                  pl.BlockSpec((1, sub * ROUTE_CHUNK, D_MODEL), lambda b, eh, ci, base: (b, ci, 0))],
        out_specs=pl.BlockSpec((1, eh_n, sp, D_MODEL), lambda b, eh, ci, base: (b, eh, 0, 0)))
    return pl.pallas_call(
        functools.partial(_dispatch_body, nchunk=nchunk, sub=sub, eh_n=eh_n, cap=cap),
        out_shape=jax.ShapeDtypeStruct((bsz, N_EXPERTS, sp, D_MODEL), BF16),
        grid_spec=grid_spec,
        compiler_params=_cparams(("parallel", "parallel", "arbitrary")),
        name="dispatch",
    )(base_flat, slott, fin)


EXPERT_FF_TILE = 512


def _experts_body(xe_ref, wg_ref, wu_ref, wd_ref, y_ref, acc_ref, *, cap, bsz):
    f = pl.program_id(1)
    wg16 = wg_ref[0].astype(BF16)
    wu16 = wu_ref[0].astype(BF16)
    wd16 = wd_ref[0].astype(BF16)
    for b in range(bsz):
        x = xe_ref[b, 0, 0:cap, :]
        hid = (_silu(_dot(x, wg16)) * _dot(x, wu16)).astype(BF16)
        part = _dot(hid, wd16)

        @pl.when(f == 0)
        def _(b=b, part=part):
            acc_ref[b] = part

        @pl.when(f != 0)
        def _(b=b, part=part):
            acc_ref[b] += part

    @pl.when(f == pl.num_programs(1) - 1)
    def _():
        for b in range(bsz):
            y_ref[b, 0, 0:cap, :] = acc_ref[b].astype(BF16)
            y_ref[b, 0, cap:, :] = jnp.zeros((y_ref.shape[2] - cap, D_MODEL), BF16)


def _experts_call(xe, w_gate, w_up, w_down, cap):
    bsz, _, sp, _ = xe.shape
    ft = EXPERT_FF_TILE
    slots = pl.BlockSpec((bsz, 1, sp, D_MODEL), lambda e, f: (0, e, 0, 0))
    return pl.pallas_call(
        functools.partial(_experts_body, cap=cap, bsz=bsz),
        out_shape=jax.ShapeDtypeStruct(xe.shape, BF16),
        grid=(N_EXPERTS, EXPERT_FF // ft),
        in_specs=[slots,
                  pl.BlockSpec((1, D_MODEL, ft), lambda e, f: (e, 0, f)),
                  pl.BlockSpec((1, D_MODEL, ft), lambda e, f: (e, 0, f)),
                  pl.BlockSpec((1, ft, D_MODEL), lambda e, f: (e, f, 0))],
        out_specs=slots,
        scratch_shapes=[pltpu.VMEM((bsz, cap, D_MODEL), F32)],
        compiler_params=_cparams(("arbitrary", "arbitrary")),
        name="experts",
    )(xe, w_gate, w_up, w_down)


SLOT_SPLIT = 32


def _combine_selectors():
    k = jnp.arange(LANES)[:, None]
    e_of_lane = jnp.arange(N_EXPERTS * SLOT_WIN)[None, :] // SLOT_WIN
    sel_gate = (k == e_of_lane).astype(BF16)
    sel_slot = (SLOT_SPLIT * (k == e_of_lane) + (k - N_EXPERTS == e_of_lane)).astype(BF16)
    return sel_slot, sel_gate


def _combine_body(base_ref, slot_ref, aff_ref, h_ref, y_ref, mod_ref, fng_ref, ssel_ref, gsel_ref, o_ref, acc_ref,
                  *, nchunk, sub, cap):
    b, ci = pl.program_id(0), pl.program_id(1)
    rc = ROUTE_CHUNK
    width = N_EXPERTS * SLOT_WIN
    lane = lax.broadcasted_iota(I32, (1, width), 1)
    lane_e = lane >> (SLOT_WIN.bit_length() - 1)
    lane_j = (lane & (SLOT_WIN - 1)).astype(F32)
    lane128 = lax.broadcasted_iota(I32, (rc, LANES), 1)
    experts = list(range(N_EXPERTS))
    for sc in range(sub):
        rows = slice(sc * rc, (sc + 1) * rc)
        flat0 = (b * (nchunk + 1) + ci * sub + sc) * N_EXPERTS
        s1 = slot_ref[0, rows, :] + 1
        halves = jnp.where(lane128 < N_EXPERTS, s1 >> (SLOT_SPLIT.bit_length() - 1),
                           pltpu.roll(s1 & (SLOT_SPLIT - 1), N_EXPERTS, 1))
        slot1 = _dot(halves.astype(F32).astype(BF16), ssel_ref[...])
        gates = _dot(aff_ref[0, rows, :].astype(BF16), gsel_ref[...])
        starts, rounds = _window_plan(base_ref, flat0, experts)

        def contribution(r, starts=starts, slot1=slot1, gates=gates):
            wstart = [_window_start(starts[e], r, cap) for e in experts]
            ywin = jnp.concatenate([y_ref[0, e, pl.ds(wstart[e], SLOT_WIN), :] for e in experts], axis=0)
            held = jnp.zeros((1, width), I32)
            for e in experts:
                held = jnp.where(lane_e == e, wstart[e] + 1, held)
            held = held.astype(F32) + lane_j
            s = jnp.where(slot1 == held, gates, 0.0).astype(BF16)
            return _dot(s, ywin)

        def extra_round(r, carry, contribution=contribution):
            acc_ref[...] += contribution(r)
            return carry

        acc_ref[...] = contribution(jnp.int32(0))
        lax.fori_loop(1, rounds, extra_round, 0)
        hl = h_ref[0, rows, :] + mod_ref[0, 5:6, :] * acc_ref[...]
        ms = jnp.mean(hl * hl, axis=-1, keepdims=True)
        o_ref[0, rows, :] = hl * lax.rsqrt(ms + NORM_EPS) * fng_ref[...]


def _combine_call(base_flat, slot, aff, h, y, mod3, fng):
    bsz, t, _ = h.shape
    nchunk = t // ROUTE_CHUNK
    sub = 2
    rc = ROUTE_CHUNK
    ssel, gsel = _combine_selectors()
    tok = lambda w: pl.BlockSpec((1, sub * rc, w), lambda b, i, base: (b, i, 0))
    full = lambda a: pl.BlockSpec(a.shape, lambda b, i, base: (0,) * a.ndim)
    grid_spec = pltpu.PrefetchScalarGridSpec(
        num_scalar_prefetch=1,
        grid=(bsz, nchunk // sub),
        in_specs=[tok(LANES), tok(LANES), tok(D_MODEL),
                  pl.BlockSpec((1,) + y.shape[1:], lambda b, i, base: (b, 0, 0, 0), pipeline_mode=pl.Buffered(1)),
                  pl.BlockSpec((1, N_MOD, D_MODEL), lambda b, i, base: (b, 0, 0)),
                  full(fng), full(ssel), full(gsel)],
        out_specs=tok(D_MODEL),
        scratch_shapes=[pltpu.VMEM((rc, D_MODEL), F32)])
    return pl.pallas_call(
        functools.partial(_combine_body, nchunk=nchunk, sub=sub, cap=y.shape[2] - SLOT_WIN),
        out_shape=jax.ShapeDtypeStruct(h.shape, F32),
        grid_spec=grid_spec,
        compiler_params=_cparams(("parallel", "arbitrary")),
        name="combine",
    )(base_flat, slot, aff, h, y, mod3, fng, ssel, gsel)


def _pad_lanes(a):
    return jnp.pad(a, ((0, 0), (0, LANES - a.shape[1])))


def kernel(x, c, ctx, c_ctx, w_mod, b_mod, norm1_g, norm2_g, w_in, conv_w, a_log, dt_bias, gdn_norm_g, gm_norm_g,
           gm_ws, gm_bs, w_out, w_router, b_router, w_gate, w_up, w_down, final_norm_g):
    bsz, t, _ = x.shape
    ctx_len = ctx.shape[1]
    assert w_mod.shape[0] == 1, "single-layer problem"
    assert t % 1024 == 0 and ctx_len % GDN_CHUNK == 0 and bsz < 8
    cap = EC_CAPACITY * t // N_EXPERTS

    cs = jnp.zeros((8, D_MODEL), F32).at[:bsz].set(c).at[bsz].set(c_ctx)
    mod3 = _mod_call(cs, w_mod[0], b_mod[0][None, :]).reshape(8, N_MOD, D_MODEL)

    wl = w_in[0]
    n_state = QKV_W + STATE_COLS
    w_state = _pad_lanes(wl[:, QKV_W:n_state])
    w_lat = jnp.concatenate([wl[:, :QKV_W], wl[:, n_state:n_state + B_W], wl[:, n_state + B_W:], w_state],
                            axis=1).astype(BF16)
    w_ctx = jnp.concatenate([wl[:, :QKV_W], w_state], axis=1).astype(BF16)
    gp = jnp.zeros((8, LANES), F32).at[0, :2 * N_HEADS].set(a_log[0].reshape(-1)).at[1, :2 * N_HEADS].set(
        dt_bias[0].reshape(-1))
    g1 = norm1_g[0][None, :]
    cw = jnp.zeros((8, QKV_W), F32).at[:conv_w.shape[1]].set(conv_w[0])

    qkv_c, gb_c = _inproj_ctx_call(ctx, mod3, bsz, g1, w_ctx, gp, cw)
    zero_state = jnp.zeros((bsz, N_HEADS, HEAD_DIM, HEAD_DIM), F32)
    _, _, s_f, s_b = _gdn_call(qkv_c, gb_c, zero_state, zero_state, ctx_len)

    qkv, gb, z, ya = _inproj_lat_call(x, mod3, g1, w_lat, gp, cw, gm_norm_g[0][None, :], gm_ws[0].astype(BF16),
                                      _pad_lanes(gm_bs[0].T), 1024)
    o_f, o_b, _, _ = _gdn_call(qkv, gb, s_f, s_b, 512)
    wr = _pad_lanes(w_router[0])
    wr_hi = wr.astype(BF16)
    wr_lo = (wr - wr_hi.astype(F32)).astype(BF16)
    br = jnp.full((1, LANES), -1e30, F32).at[0, :N_EXPERTS].set(b_router[0])
    h, fin, aff, afft = _mixout_call(x, o_f, o_b, z, ya, mod3, gdn_norm_g[0][None, :], w_out[0].astype(BF16),
                                     norm2_g[0][None, :], wr_hi, wr_lo, br, 1024)

    npieces = t // LANES
    slott, slot, off = _route_call(afft.reshape(bsz, npieces * N_EXPERTS, LANES), cap)
    base = off[:, :, 0].reshape(bsz, npieces, N_EXPERTS)[:, ::ROUTE_CHUNK // LANES, :]
    base_flat = jnp.concatenate([base, jnp.full((bsz, 1, N_EXPERTS), cap, I32)], axis=1).reshape(-1)
    xe = _dispatch_call(base_flat, slott.reshape(bsz, npieces, N_EXPERTS, LANES), fin, cap)
    y = _experts_call(xe, w_gate[0], w_up[0], w_down[0], cap)
    return _combine_call(base_flat, slot, aff, h, y, mod3, final_norm_g[None, :])
```

```python
import functools

import jax
import jax.numpy as jnp
from jax import lax
from jax.experimental import pallas as pl
from jax.experimental.pallas import tpu as pltpu

F32 = jnp.float32
BF16 = jnp.bfloat16
I32 = jnp.int32

D_MODEL = 1024
N_MOD = 6
N_HEADS = 4
HEAD_DIM = 128
B_W = N_HEADS * HEAD_DIM
QKV_W = 3 * B_W
A_W = 512
A_GROUPS = 4
A_CHUNK = 128
GDN_CHUNK = 64
N_EXPERTS = 16
EC_CAPACITY = 2
EXPERT_FF = 1024
NORM_EPS = 1e-6
LANES = 128
STATE_COLS = 4 * N_HEADS

ROUTE_CHUNK = 256
SLOT_WIN = 64
SLOT_ALIGN = 16
VMEM_LIMIT = 56 * 1024 * 1024


def _cparams(sem):
    return pltpu.CompilerParams(dimension_semantics=sem, vmem_limit_bytes=VMEM_LIMIT)


def _dot(a, b):
    return jnp.dot(a, b, preferred_element_type=F32)


def _dot_nt(a, b):
    return lax.dot_general(a, b, (((1,), (1,)), ((), ())), preferred_element_type=F32)


def _dot_tn(a, b):
    return lax.dot_general(a, b, (((0,), (0,)), ((), ())), preferred_element_type=F32)


def _silu(x):
    return x * jax.nn.sigmoid(x)


def _mod_body(c_ref, w_ref, b_ref, o_ref):
    s = _silu(c_ref[...])
    o_ref[...] = _dot(s.astype(BF16), w_ref[...].astype(BF16)) + b_ref[...]


def _mod_call(cs, w_mod, b_mod):
    n = w_mod.shape[1] // D_MODEL
    return pl.pallas_call(
        _mod_body,
        out_shape=jax.ShapeDtypeStruct((8, w_mod.shape[1]), F32),
        grid=(n,),
        in_specs=[pl.BlockSpec((8, D_MODEL), lambda j: (0, 0)),
                  pl.BlockSpec((D_MODEL, D_MODEL), lambda j: (0, j)),
                  pl.BlockSpec((1, D_MODEL), lambda j: (0, j))],
        out_specs=pl.BlockSpec((8, D_MODEL), lambda j: (0, j)),
        compiler_params=_cparams(("arbitrary",)),
        name="mod",
    )(cs, w_mod, b_mod)


def _norm_mod(x, g, shift, scale):
    ms = jnp.mean(x * x, axis=-1, keepdims=True)
    return (x * lax.rsqrt(ms + NORM_EPS) * g) * (1.0 + scale) + shift


def _gate_streams(st, gp_ref):
    lane = lax.broadcasted_iota(I32, st.shape, 1)
    g = -jnp.exp(gp_ref[0:1, :]) * jax.nn.softplus(st + gp_ref[1:2, :])
    beta = jax.nn.sigmoid(st)
    return jnp.where(lane < 2 * N_HEADS, g, jnp.where(lane < STATE_COLS, beta, 0.0))


def _conv_qkv(qkv, prev_row, next_row, cw_ref, out_ref, tm):
    cs = GDN_CHUNK
    nsub = tm // cs
    w0, w1, w2 = cw_ref[0:1, :], cw_ref[1:2, :], cw_ref[2:3, :]
    row = lax.broadcasted_iota(I32, (cs, 1), 0)
    for c in range(nsub):
        rows = slice(c * cs, (c + 1) * cs)
        x = qkv[rows]
        prow = prev_row if c == 0 else qkv[c * cs - 1:c * cs]
        nrow = next_row if c == nsub - 1 else qkv[(c + 1) * cs:(c + 1) * cs + 1]
        xp = jnp.where(row == 0, prow, pltpu.roll(x, 1, 0))
        xn = jnp.where(row == cs - 1, nrow, pltpu.roll(x, cs - 1, 0))
        y = _silu(xp * w0 + x * w1 + xn * w2)
        for h in range(N_HEADS):
            cq = slice(h * HEAD_DIM, (h + 1) * HEAD_DIM)
            ck = slice(B_W + h * HEAD_DIM, B_W + (h + 1) * HEAD_DIM)
            q = y[:, cq]
            k = y[:, ck]
            out_ref[0, rows, cq] = (q * (lax.rsqrt(jnp.sum(q * q, axis=-1, keepdims=True) + NORM_EPS)
                                         * (HEAD_DIM ** -0.5))).astype(BF16)
            out_ref[0, rows, ck] = (k * lax.rsqrt(jnp.sum(k * k, axis=-1, keepdims=True) + NORM_EPS)).astype(BF16)
        out_ref[0, rows, 2 * B_W:3 * B_W] = y[:, 2 * B_W:3 * B_W].astype(BF16)


def _inproj_lat_body(x_ref, xp_ref, xn_ref, mod_ref, g1_ref, w_ref, gp_ref, cw_ref, gmg_ref, ws_ref, bst_ref,
                     qkv_ref, gb_ref, z_ref, ya_ref, *, tm):
    i = pl.program_id(1)
    shift, scale = mod_ref[0, 0:1, :], mod_ref[0, 1:2, :]
    a = _norm_mod(x_ref[0], g1_ref[...], shift, scale).astype(BF16)
    xh = jnp.concatenate([xp_ref[0], xn_ref[0]], axis=0)
    halo = _dot(_norm_mod(xh, g1_ref[...], shift, scale).astype(BF16), w_ref[:, 0:QKV_W])
    prev_row = jnp.where(i == 0, 0.0, halo[7:8, :])
    next_row = jnp.where(i == pl.num_programs(1) - 1, 0.0, halo[8:9, :])
    _conv_qkv(_dot(a, w_ref[:, 0:QKV_W]), prev_row, next_row, cw_ref, qkv_ref, tm)
    z_ref[0] = _dot(a, w_ref[:, QKV_W:QKV_W + B_W]).astype(BF16)
    c_uv = QKV_W + B_W
    gb_ref[0] = _gate_streams(_dot(a, w_ref[:, c_uv + 2 * A_W:c_uv + 2 * A_W + LANES]), gp_ref)
    uv = _dot(a, w_ref[:, c_uv:c_uv + 2 * A_W])
    uv = 0.5 * uv * (1.0 + lax.erf(uv * 0.7071067811865476))
    gd = A_W // A_GROUPS
    for grp in range(A_GROUPS):
        v = uv[:, A_W + grp * gd:A_W + (grp + 1) * gd]
        vn = v * lax.rsqrt(jnp.mean(v * v, axis=-1, keepdims=True) + NORM_EPS) * gmg_ref[:, grp * gd:(grp + 1) * gd]
        vn = vn.astype(BF16)
        bias = bst_ref[:, grp:grp + 1]
        for c in range(tm // A_CHUNK):
            rows = slice(c * A_CHUNK, (c + 1) * A_CHUNK)
            s = _dot(ws_ref[grp], vn[rows]) + bias
            ya_ref[0, rows, grp * gd:(grp + 1) * gd] = (uv[rows, grp * gd:(grp + 1) * gd] * s).astype(BF16)


def _inproj_ctx_body(x_ref, mod_ref, g1_ref, w_ref, gp_ref, cw_ref, qkv_ref, gb_ref, *, tm):
    a = _norm_mod(x_ref[0], g1_ref[...], mod_ref[0, 0:1, :], mod_ref[0, 1:2, :]).astype(BF16)
    edge = jnp.zeros((1, QKV_W), F32)
    _conv_qkv(_dot(a, w_ref[:, 0:QKV_W]), edge, edge, cw_ref, qkv_ref, tm)
    c_state = QKV_W + B_W + 2 * A_W
    gb_ref[0] = _gate_streams(_dot(a, w_ref[:, c_state:c_state + LANES]), gp_ref)


def _inproj_lat_call(x, mod3, g1, w_lat, gp, cw, gmg, ws16, bst, tm):
    bsz, t, _ = x.shape
    hb = tm // 8
    last8 = t // 8 - 1
    full = lambda a: pl.BlockSpec(a.shape, lambda b, i: (0,) * a.ndim)
    tok = lambda w: pl.BlockSpec((1, tm, w), lambda b, i: (b, i, 0))
    return pl.pallas_call(
        functools.partial(_inproj_lat_body, tm=tm),
        out_shape=(jax.ShapeDtypeStruct((bsz, t, QKV_W), BF16),
                   jax.ShapeDtypeStruct((bsz, t, LANES), F32),
                   jax.ShapeDtypeStruct((bsz, t, B_W), BF16),
                   jax.ShapeDtypeStruct((bsz, t, A_W), BF16)),
        grid=(bsz, t // tm),
        in_specs=[tok(D_MODEL),
                  pl.BlockSpec((1, 8, D_MODEL), lambda b, i: (b, jnp.maximum(i * hb - 1, 0), 0)),
                  pl.BlockSpec((1, 8, D_MODEL), lambda b, i: (b, jnp.minimum((i + 1) * hb, last8), 0)),
                  pl.BlockSpec((1, N_MOD, D_MODEL), lambda b, i: (b, 0, 0)),
                  full(g1), full(w_lat), full(gp), full(cw), full(gmg), full(ws16), full(bst)],
        out_specs=(tok(QKV_W), tok(LANES), tok(B_W), tok(A_W)),
        compiler_params=_cparams(("parallel", "arbitrary")),
        name="inproj_lat",
    )(x, x, x, mod3, g1, w_lat, gp, cw, gmg, ws16, bst)


def _inproj_ctx_call(ctx, mod3, ctx_row, g1, w_ctx, gp, cw):
    bsz, t, _ = ctx.shape
    full = lambda a: pl.BlockSpec(a.shape, lambda b: (0,) * a.ndim)
    tok = lambda w: pl.BlockSpec((1, t, w), lambda b: (b, 0, 0))
    return pl.pallas_call(
        functools.partial(_inproj_ctx_body, tm=t),
        out_shape=(jax.ShapeDtypeStruct((bsz, t, QKV_W), BF16),
                   jax.ShapeDtypeStruct((bsz, t, LANES), F32)),
        grid=(bsz,),
        in_specs=[tok(D_MODEL),
                  pl.BlockSpec((1, N_MOD, D_MODEL), lambda b: (ctx_row, 0, 0)),
                  full(g1), full(w_ctx), full(gp), full(cw)],
        out_specs=(tok(QKV_W), tok(LANES)),
        compiler_params=_cparams(("parallel",)),
        name="inproj_ctx",
    )(ctx, mod3, g1, w_ctx, gp, cw)


GDN_CHUNKS_PER_STEP = 8
GDN_STAGE_SKEW = 2
PACK_W = N_HEADS * GDN_CHUNK


def _per_head(tile, lanes, width):
    rows = tile.shape[0]
    if width == HEAD_DIM:
        return jnp.concatenate([jnp.broadcast_to(tile[:, l:l + 1], (rows, width)) for l in lanes], axis=1)
    head = lax.broadcasted_iota(I32, (rows, N_HEADS * width), 1) // width
    out = jnp.broadcast_to(tile[:, lanes[0]:lanes[0] + 1], (rows, N_HEADS * width))
    for h in range(1, N_HEADS):
        out = jnp.where(head == h, jnp.broadcast_to(tile[:, lanes[h]:lanes[h] + 1], (rows, N_HEADS * width)), out)
    return out


def _block_rows(x16, half_masks=None):
    rows, width = x16.shape
    per_head = width // N_HEADS
    zero = jnp.zeros((rows, LANES), x16.dtype)
    blocks = []
    for h in range(N_HEADS):
        tile = h * per_head // LANES
        kept = x16[:, tile * LANES:(tile + 1) * LANES]
        if per_head < LANES:
            kept = kept * half_masks[h * per_head % LANES // per_head]
        blocks.append(jnp.concatenate([kept if t == tile else zero for t in range(width // LANES)], axis=1))
    return jnp.concatenate(blocks, axis=0)


def _gdn_group_program(g, half_masks, turn):
    cs = GDN_CHUNK
    hd = HEAD_DIM
    q, k, v, beta, egc = g["q"], g["k"], g["v"], g["beta"], g["egc"]
    kb = k * beta
    decay = jnp.where(g["incl"], jnp.exp(jnp.where(g["incl"], g["gc_col"] - g["gc_row"], 0.0)), 0.0)
    kk = _dot_nt(jnp.concatenate([kb, q], axis=0).astype(BF16), _block_rows(k.astype(BF16)))
    yield
    a = jnp.where(g["strict"], kk[:cs] * decay, 0.0)
    attn = (kk[cs:] * decay).astype(BF16)
    m = -jnp.where(g["levels"][0], a, 0.0)
    for lm in g["levels"][1:]:
        m16 = m.astype(BF16)
        cm = jnp.where(lm, a, 0.0)
        x = cm + _dot(m16, _block_rows(cm.astype(BF16), half_masks))
        yield
        y = x + _dot(x.astype(BF16), _block_rows(m16, half_masks))
        yield
        m = m - y
    m16 = m.astype(BF16)
    vb = v * beta
    kbg = kb * egc
    u = vb + _dot(m16, _block_rows(vb.astype(BF16)))
    w = kbg + _dot(m16, _block_rows(kbg.astype(BF16)))
    qg = q * egc
    kg = (k * g["kdec"]).astype(BF16)
    yield
    while turn[g["backward"]] != g["order"]:
        yield
    s_ref = g["s_ref"]
    s = [s_ref[h] for h in range(N_HEADS)]
    wq = [_dot(jnp.concatenate([w[:, h * hd:(h + 1) * hd], qg[:, h * hd:(h + 1) * hd]], axis=0).astype(BF16),
               s[h].astype(BF16)) for h in range(N_HEADS)]
    yield
    v_new = (u - jnp.concatenate([r[:cs] for r in wq], axis=1)).astype(BF16)
    o = jnp.concatenate([r[cs:] for r in wq], axis=1) + _dot(attn, _block_rows(v_new))
    for h in range(N_HEADS):
        cols = slice(h * hd, (h + 1) * hd)
        s_ref[h] = s[h] * g["eg"][h] + _dot_tn(kg[:, cols], v_new[:, cols])
    g["o_ref"][0, pl.ds(g["r0"], cs), :] = o.astype(BF16)
    turn[g["backward"]] += 1


def _run_interleaved(programs, skew):
    live = {}
    tick = 0
    while live or tick <= skew * (len(programs) - 1):
        if tick % skew == 0 and tick // skew < len(programs):
            live[tick // skew] = programs[tick // skew]
        for key in sorted(live):
            for prog in live[key]:
                if next(prog, "done") == "done":
                    live[key] = [p for p in live[key] if p is not prog]
            if not live[key]:
                del live[key]
        tick += 1


def _gdn_body(qf_ref, qb_ref, gbf_ref, gbb_ref, s0f_ref, s0b_ref,
              of_ref, ob_ref, sff_ref, sfb_ref, sf_scr, sb_scr, *, tb, nt):
    t = pl.program_id(1)
    cs = GDN_CHUNK
    nch = tb // cs
    cps = min(GDN_CHUNKS_PER_STEP, nch)
    nh = N_HEADS

    @pl.when(t == 0)
    def _():
        sf_scr[...] = s0f_ref[0]
        sb_scr[...] = s0b_ref[0]

    ii = lax.broadcasted_iota(I32, (cs, PACK_W), 0)
    jj = lax.broadcasted_iota(I32, (cs, PACK_W), 1) & (cs - 1)
    incl_f, strict_f = jj <= ii, jj < ii
    incl_b, strict_b = jj >= ii, jj > ii
    levels = []
    sh = 0
    while (1 << sh) < cs:
        levels.append(((ii >> (sh + 1)) == (jj >> (sh + 1))) & ((ii >> sh) != (jj >> sh)))
        sh += 1
    lv_f = [lm & strict_f for lm in levels]
    lv_b = [lm & strict_b for lm in levels]
    row = lax.broadcasted_iota(I32, (cs, LANES), 0)
    lane = lax.broadcasted_iota(I32, (1, LANES), 1)
    half_of_tile = lax.broadcasted_iota(I32, (cs, LANES), 1) // cs
    half_masks = [(half_of_tile == i).astype(BF16) for i in range(LANES // cs)]

    def group(q_ref, r0, gate, csum, gt, backward, s_ref, o_ref, order):
        l0 = nh if backward else 0
        lanes = [l0 + h for h in range(nh)]
        last = 0 if backward else cs - 1
        ld = lambda c0: q_ref[0, pl.ds(r0, cs), c0:c0 + B_W].astype(F32)
        halves = [gt[l:l + 1, :] if (h % 2 == 0) != backward else pltpu.roll(gt[l:l + 1, :], cs, 1)
                  for h, l in enumerate(lanes)]
        gc_row = jnp.concatenate([jnp.where(lane < cs, halves[0], halves[1]),
                                  jnp.where(lane < cs, halves[2], halves[3])], axis=1)
        glast = csum[last:last + 1, :]
        return dict(
            q=ld(0), k=ld(B_W), v=ld(2 * B_W),
            beta=_per_head(gate, [2 * nh + l for l in lanes], HEAD_DIM),
            egc=_per_head(jnp.exp(csum), lanes, HEAD_DIM),
            kdec=_per_head(jnp.exp(glast - csum), lanes, HEAD_DIM),
            eg=[jnp.exp(csum[last:last + 1, l:l + 1]) for l in lanes],
            gc_col=_per_head(csum, lanes, cs), gc_row=gc_row,
            incl=incl_b if backward else incl_f, strict=strict_b if backward else strict_f,
            levels=lv_b if backward else lv_f, s_ref=s_ref, o_ref=o_ref, r0=r0, backward=backward, order=order)

    def step(n, carry):
        per_chunk = []
        for j in range(cps):
            rf = pl.multiple_of((n * cps + j) * cs, cs)
            rb = pl.multiple_of((nch - 1 - n * cps - j) * cs, cs)
            gf = gbf_ref[0, pl.ds(rf, cs), :]
            gb = gbb_ref[0, pl.ds(rb, cs), :]
            cf, cb = gf, gb
            s = 1
            while s < cs:
                cf = cf + jnp.where(row >= s, pltpu.roll(cf, s, 0), 0.0)
                cb = cb + jnp.where(row < cs - s, pltpu.roll(cb, cs - s, 0), 0.0)
                s *= 2
            gt = jnp.concatenate([cf, cb], axis=0).T
            per_chunk.append([group(qf_ref, rf, gf, cf, gt, False, sf_scr, of_ref, j),
                              group(qb_ref, rb, gb, cb, gt, True, sb_scr, ob_ref, j)])
        turn = {False: 0, True: 0}
        _run_interleaved([[_gdn_group_program(g, half_masks, turn) for g in pair] for pair in per_chunk],
                         GDN_STAGE_SKEW)
        return carry

    lax.fori_loop(0, nch // cps, step, 0)

    @pl.when(t == nt - 1)
    def _():
        sff_ref[0] = sf_scr[...]
        sfb_ref[0] = sb_scr[...]


def _gdn_call(qkv, gb, s0f, s0b, tb):
    bsz, t, _ = qkv.shape
    nt = t // tb
    assert (tb // GDN_CHUNK) % min(GDN_CHUNKS_PER_STEP, tb // GDN_CHUNK) == 0
    st = pl.BlockSpec((1, N_HEADS, HEAD_DIM, HEAD_DIM), lambda b, i: (b, 0, 0, 0))
    fwd = lambda w: pl.BlockSpec((1, tb, w), lambda b, i: (b, i, 0))
    bwd = lambda w: pl.BlockSpec((1, tb, w), lambda b, i: (b, nt - 1 - i, 0))
    return pl.pallas_call(
        functools.partial(_gdn_body, tb=tb, nt=nt),
        out_shape=(jax.ShapeDtypeStruct((bsz, t, B_W), BF16), jax.ShapeDtypeStruct((bsz, t, B_W), BF16),
                   jax.ShapeDtypeStruct((bsz, N_HEADS, HEAD_DIM, HEAD_DIM), F32),
                   jax.ShapeDtypeStruct((bsz, N_HEADS, HEAD_DIM, HEAD_DIM), F32)),
        grid=(bsz, nt),
        in_specs=[fwd(QKV_W), bwd(QKV_W), fwd(LANES), bwd(LANES), st, st],
        out_specs=(fwd(B_W), bwd(B_W), st, st),
        scratch_shapes=[pltpu.VMEM((N_HEADS, HEAD_DIM, HEAD_DIM), F32), pltpu.VMEM((N_HEADS, HEAD_DIM, HEAD_DIM), F32)],
        compiler_params=_cparams(("parallel", "arbitrary")),
        name="gdn",
    )(qkv, qkv, gb, gb, s0f, s0b)


def _mixout_body(x_ref, of_ref, ob_ref, z_ref, ya_ref, mod_ref, gng_ref, wout_ref, n2g_ref, wrh_ref, wrl_ref, br_ref,
                 h_ref, fin_ref, aff_ref, afft_ref, *, tm):
    o = of_ref[0].astype(F32) + ob_ref[0].astype(F32)
    z = z_ref[0].astype(F32)
    parts = [ya_ref[0]]
    for h in range(N_HEADS):
        c = slice(h * HEAD_DIM, (h + 1) * HEAD_DIM)
        oh = o[:, c]
        y = oh * lax.rsqrt(jnp.mean(oh * oh, axis=-1, keepdims=True) + NORM_EPS)
        parts.append((y * gng_ref[...] * _silu(z[:, c])).astype(BF16))
    mix = _dot(jnp.concatenate(parts, axis=1), wout_ref[...])
    hl = x_ref[0] + mod_ref[0, 2:3, :] * mix
    h_ref[0] = hl
    fin = _norm_mod(hl, n2g_ref[...], mod_ref[0, 3:4, :], mod_ref[0, 4:5, :])
    f_hi = fin.astype(BF16)
    fin_ref[0] = f_hi
    f_lo = (fin - f_hi.astype(F32)).astype(BF16)
    logits = _dot(f_hi, wrh_ref[...]) + _dot(f_lo, wrh_ref[...]) + _dot(f_hi, wrl_ref[...]) + br_ref[...]
    e = jnp.exp(logits - jnp.max(logits, axis=-1, keepdims=True))
    aff = e / jnp.sum(e, axis=-1, keepdims=True)
    aff_ref[0] = aff
    for j in range(tm // LANES):
        afft_ref[0, j] = aff[j * LANES:(j + 1) * LANES, :].T[0:N_EXPERTS, :]


def _mixout_call(x, o_f, o_b, z, ya, mod3, gng, wout16, n2g, wr_hi, wr_lo, br, tm):
    bsz, t, _ = x.shape
    full = lambda a: pl.BlockSpec(a.shape, lambda b, i: (0,) * a.ndim)
    tok = lambda w: pl.BlockSpec((1, tm, w), lambda b, i: (b, i, 0))
    return pl.pallas_call(
        functools.partial(_mixout_body, tm=tm),
        out_shape=(jax.ShapeDtypeStruct((bsz, t, D_MODEL), F32), jax.ShapeDtypeStruct((bsz, t, D_MODEL), BF16),
                   jax.ShapeDtypeStruct((bsz, t, LANES), F32),
                   jax.ShapeDtypeStruct((bsz, t // LANES, N_EXPERTS, LANES), F32)),
        grid=(bsz, t // tm),
        in_specs=[tok(D_MODEL), tok(B_W), tok(B_W), tok(B_W), tok(A_W),
                  pl.BlockSpec((1, N_MOD, D_MODEL), lambda b, i: (b, 0, 0)),
                  full(gng), full(wout16), full(n2g), full(wr_hi), full(wr_lo), full(br)],
        out_specs=(tok(D_MODEL), tok(D_MODEL), tok(LANES),
                   pl.BlockSpec((1, tm // LANES, N_EXPERTS, LANES), lambda b, i: (b, i, 0, 0))),
        compiler_params=_cparams(("parallel", "parallel")),
        name="mixout",
    )(x, o_f, o_b, z, ya, mod3, gng, wout16, n2g, wr_hi, wr_lo, br)


def _route_body(afft_ref, slott_ref, slot_ref, off_ref, *, t, cap):
    ne = N_EXPERTS
    npieces = t // LANES
    rows = npieces * ne

    def count(thr_col, strict):
        acc = jnp.zeros((ne, LANES), I32)
        for p in range(npieces):
            piece = afft_ref[0, p * ne:(p + 1) * ne, :]
            acc = acc + (piece > thr_col if strict else piece >= thr_col).astype(I32)
        return jnp.sum(acc, axis=1, keepdims=True)

    def search(i, thr):
        cand = thr | jnp.left_shift(jnp.int32(1), 30 - i)
        return jnp.where(count(pltpu.bitcast(cand, F32), False) >= cap, cand, thr)

    thr_bits = lax.fori_loop(0, 31, search, jnp.zeros((ne, 1), I32))
    thr = pltpu.bitcast(thr_bits, F32)
    need = (cap - count(thr, True)).astype(F32)

    x = afft_ref[0]
    thr_rows = jnp.concatenate([thr] * npieces, axis=0)
    need_rows = jnp.concatenate([need] * npieces, axis=0)
    gt = x > thr_rows
    eq = x == thr_rows
    ti = lax.broadcasted_iota(I32, (LANES, LANES), 0)
    tj = lax.broadcasted_iota(I32, (LANES, LANES), 1)
    triu = (ti <= tj).astype(BF16)
    ri = lax.broadcasted_iota(I32, (rows, rows), 0)
    rj = lax.broadcasted_iota(I32, (rows, rows), 1)
    earlier = (((ri & (ne - 1)) == (rj & (ne - 1))) & (rj < ri)).astype(BF16)

    def prefix(mask):
        inpiece = _dot(mask.astype(BF16), triu)
        total = jnp.broadcast_to(inpiece[:, LANES - 1:LANES], (rows, LANES)).astype(BF16)
        offset = _dot(earlier, total)
        return inpiece + offset, offset

    eq_rank, _ = prefix(eq)
    sel = gt | (eq & (eq_rank <= need_rows))
    sel_rank, sel_off = prefix(sel)
    slot = jnp.where(sel, sel_rank - 1.0, -1.0)
    slott_ref[0] = slot.astype(I32)
    off_ref[0] = sel_off.astype(I32)
    pad = jnp.zeros((LANES - ne, LANES), F32)
    for p in range(npieces):
        piece = jnp.concatenate([slot[p * ne:(p + 1) * ne, :], pad], axis=0)
        slot_ref[0, p * LANES:(p + 1) * LANES, :] = piece.T.astype(I32)


def _route_call(afft, cap):
    bsz, rows, _ = afft.shape
    t = rows // N_EXPERTS * LANES
    spec = lambda r: pl.BlockSpec((1, r, LANES), lambda b: (b, 0, 0))
    return pl.pallas_call(
        functools.partial(_route_body, t=t, cap=cap),
        out_shape=(jax.ShapeDtypeStruct((bsz, rows, LANES), I32),
                   jax.ShapeDtypeStruct((bsz, t, LANES), I32),
                   jax.ShapeDtypeStruct((bsz, rows, LANES), I32)),
        grid=(bsz,),
        in_specs=[spec(rows)],
        out_specs=(spec(rows), spec(t), spec(rows)),
        compiler_params=_cparams(("parallel",)),
        name="route",
    )(afft)


def _window_plan(base_ref, flat0, experts):
    starts, rounds = [], jnp.int32(0)
    for e in experts:
        lo = base_ref[flat0 + e]
        hi = base_ref[flat0 + N_EXPERTS + e]
        lo_al = (lo >> 4) << 4
        starts.append(lo_al)
        rounds = jnp.maximum(rounds, (hi - lo_al + SLOT_WIN - 1) // SLOT_WIN)
    return starts, rounds


def _window_start(start, r, cap):
    return pl.multiple_of(jnp.minimum(start + r * SLOT_WIN, cap), SLOT_ALIGN)


def _dispatch_body(base_ref, slott_ref, fin_ref, xe_ref, *, nchunk, sub, eh_n, cap):
    b, eh, ci = pl.program_id(0), pl.program_id(1), pl.program_id(2)
    rc = ROUTE_CHUNK

    @pl.when(ci == 0)
    def _():
        xe_ref[...] = jnp.zeros_like(xe_ref)

    srow = lax.broadcasted_iota(I32, (SLOT_WIN, rc), 0)
    for sc in range(sub):
        cc = ci * sub + sc
        flat0 = (b * (nchunk + 1) + cc) * N_EXPERTS + eh * eh_n
        f = fin_ref[0, sc * rc:(sc + 1) * rc, :]
        experts = list(range(eh_n))
        starts, rounds = _window_plan(base_ref, flat0, experts)

        def one_round(r, carry, starts=starts, f=f, sc=sc):
            rows = []
            wstart = [_window_start(starts[e], r, cap) for e in experts]
            for e in experts:
                tok_slot = jnp.concatenate(
                    [slott_ref[0, sc * (rc // LANES) + j, e:e + 1, :] for j in range(rc // LANES)], axis=1)
                rows.append((tok_slot == srow + wstart[e]).astype(BF16))
            prod = _dot(jnp.concatenate(rows, axis=0), f)
            for e in experts:
                win = pl.ds(wstart[e], SLOT_WIN)
                xe_ref[0, e, win, :] = xe_ref[0, e, win, :] + prod[e * SLOT_WIN:(e + 1) * SLOT_WIN].astype(BF16)
            return carry

        one_round(jnp.int32(0), 0)
        lax.fori_loop(1, rounds, one_round, 0)


def _dispatch_call(base_flat, slott, fin, cap):
    bsz, t, _ = fin.shape
    nchunk = t // ROUTE_CHUNK
    sub = 2
    eh_n = N_EXPERTS // 2
    sp = cap + SLOT_WIN
    grid_spec = pltpu.PrefetchScalarGridSpec(
        num_scalar_prefetch=1,
        grid=(bsz, N_EXPERTS // eh_n, nchunk // sub),
        in_specs=[pl.BlockSpec((1, sub * ROUTE_CHUNK // LANES, eh_n, LANES), lambda b, eh, ci, base: (b, ci, eh, 0)),
                  pl.BlockSpec((1, sub * ROUTE_CHUNK, D_MODEL), lambda b, eh, ci, base: (b, ci, 0))],
        out_specs=pl.BlockSpec((1, eh_n, sp, D_MODEL), lambda b, eh, ci, base: (b, eh, 0, 0)))
    return pl.pallas_call(
        functools.partial(_dispatch_body, nchunk=nchunk, sub=sub, eh_n=eh_n, cap=cap),
        out_shape=jax.ShapeDtypeStruct((bsz, N_EXPERTS, sp, D_MODEL), BF16),
        grid_spec=grid_spec,
        compiler_params=_cparams(("parallel", "parallel", "arbitrary")),
        name="dispatch",
    )(base_flat, slott, fin)


def _experts_body(xe_ref, wg_ref, wu_ref, wd_ref, y_ref, *, cap):
    x = xe_ref[0, 0, 0:cap, :]
    ft = 256
    acc = None
    for f in range(EXPERT_FF // ft):
        cols = slice(f * ft, (f + 1) * ft)
        wg16 = wg_ref[0, :, cols].astype(BF16)
        wu16 = wu_ref[0, :, cols].astype(BF16)
        wd16 = wd_ref[0, cols, :].astype(BF16)
        hid = (_silu(_dot(x, wg16)) * _dot(x, wu16)).astype(BF16)
        part = _dot(hid, wd16)
        acc = part if acc is None else acc + part
    y_ref[0, 0, 0:cap, :] = acc.astype(BF16)
    y_ref[0, 0, cap:, :] = jnp.zeros((y_ref.shape[2] - cap, D_MODEL), BF16)


def _experts_call(xe, w_gate, w_up, w_down, cap):
    bsz, _, sp, _ = xe.shape
    wspec = lambda shape: pl.BlockSpec((1,) + shape, lambda e, b: (e, 0, 0))
    slots = pl.BlockSpec((1, 1, sp, D_MODEL), lambda e, b: (b, e, 0, 0))
    return pl.pallas_call(
        functools.partial(_experts_body, cap=cap),
        out_shape=jax.ShapeDtypeStruct(xe.shape, BF16),
        grid=(N_EXPERTS, bsz),
        in_specs=[slots, wspec((D_MODEL, EXPERT_FF)), wspec((D_MODEL, EXPERT_FF)), wspec((EXPERT_FF, D_MODEL))],
        out_specs=slots,
        compiler_params=_cparams(("parallel", "parallel")),
        name="experts",
    )(xe, w_gate, w_up, w_down)


SLOT_SPLIT = 32


def _combine_selectors():
    k = jnp.arange(LANES)[:, None]
    e_of_lane = jnp.arange(N_EXPERTS * SLOT_WIN)[None, :] // SLOT_WIN
    sel_gate = (k == e_of_lane).astype(BF16)
    sel_slot = (SLOT_SPLIT * (k == e_of_lane) + (k - N_EXPERTS == e_of_lane)).astype(BF16)
    return sel_slot, sel_gate


def _combine_body(base_ref, slot_ref, aff_ref, h_ref, y_ref, mod_ref, fng_ref, ssel_ref, gsel_ref, o_ref, acc_ref,
                  *, nchunk, sub, cap):
    b, ci = pl.program_id(0), pl.program_id(1)
    rc = ROUTE_CHUNK
    width = N_EXPERTS * SLOT_WIN
    lane = lax.broadcasted_iota(I32, (1, width), 1)
    lane_e = lane >> (SLOT_WIN.bit_length() - 1)
    lane_j = (lane & (SLOT_WIN - 1)).astype(F32)
    lane128 = lax.broadcasted_iota(I32, (rc, LANES), 1)
    experts = list(range(N_EXPERTS))
    for sc in range(sub):
        rows = slice(sc * rc, (sc + 1) * rc)
        flat0 = (b * (nchunk + 1) + ci * sub + sc) * N_EXPERTS
        s1 = slot_ref[0, rows, :] + 1
        halves = jnp.where(lane128 < N_EXPERTS, s1 >> (SLOT_SPLIT.bit_length() - 1),
                           pltpu.roll(s1 & (SLOT_SPLIT - 1), N_EXPERTS, 1))
        slot1 = _dot(halves.astype(F32).astype(BF16), ssel_ref[...])
        gates = _dot(aff_ref[0, rows, :].astype(BF16), gsel_ref[...])
        starts, rounds = _window_plan(base_ref, flat0, experts)

        def contribution(r, starts=starts, slot1=slot1, gates=gates):
            wstart = [_window_start(starts[e], r, cap) for e in experts]
            ywin = jnp.concatenate([y_ref[0, e, pl.ds(wstart[e], SLOT_WIN), :] for e in experts], axis=0)
            held = jnp.zeros((1, width), I32)
            for e in experts:
                held = jnp.where(lane_e == e, wstart[e] + 1, held)
            held = held.astype(F32) + lane_j
            s = jnp.where(slot1 == held, gates, 0.0).astype(BF16)
            return _dot(s, ywin)

        def extra_round(r, carry, contribution=contribution):
            acc_ref[...] += contribution(r)
            return carry

        acc_ref[...] = contribution(jnp.int32(0))
        lax.fori_loop(1, rounds, extra_round, 0)
        hl = h_ref[0, rows, :] + mod_ref[0, 5:6, :] * acc_ref[...]
        ms = jnp.mean(hl * hl, axis=-1, keepdims=True)
        o_ref[0, rows, :] = hl * lax.rsqrt(ms + NORM_EPS) * fng_ref[...]


def _combine_call(base_flat, slot, aff, h, y, mod3, fng):
    bsz, t, _ = h.shape
    nchunk = t // ROUTE_CHUNK
    sub = 2
    rc = ROUTE_CHUNK
    ssel, gsel = _combine_selectors()
    tok = lambda w: pl.BlockSpec((1, sub * rc, w), lambda b, i, base: (b, i, 0))
    full = lambda a: pl.BlockSpec(a.shape, lambda b, i, base: (0,) * a.ndim)
    grid_spec = pltpu.PrefetchScalarGridSpec(
        num_scalar_prefetch=1,
        grid=(bsz, nchunk // sub),
        in_specs=[tok(LANES), tok(LANES), tok(D_MODEL),
                  pl.BlockSpec((1,) + y.shape[1:], lambda b, i, base: (b, 0, 0, 0), pipeline_mode=pl.Buffered(1)),
                  pl.BlockSpec((1, N_MOD, D_MODEL), lambda b, i, base: (b, 0, 0)),
                  full(fng), full(ssel), full(gsel)],
        out_specs=tok(D_MODEL),
        scratch_shapes=[pltpu.VMEM((rc, D_MODEL), F32)])
    return pl.pallas_call(
        functools.partial(_combine_body, nchunk=nchunk, sub=sub, cap=y.shape[2] - SLOT_WIN),
        out_shape=jax.ShapeDtypeStruct(h.shape, F32),
        grid_spec=grid_spec,
        compiler_params=_cparams(("parallel", "arbitrary")),
        name="combine",
    )(base_flat, slot, aff, h, y, mod3, fng, ssel, gsel)


def _pad_lanes(a):
    return jnp.pad(a, ((0, 0), (0, LANES - a.shape[1])))


def kernel(x, c, ctx, c_ctx, w_mod, b_mod, norm1_g, norm2_g, w_in, conv_w, a_log, dt_bias, gdn_norm_g, gm_norm_g,
           gm_ws, gm_bs, w_out, w_router, b_router, w_gate, w_up, w_down, final_norm_g):
    bsz, t, _ = x.shape
    ctx_len = ctx.shape[1]
    assert w_mod.shape[0] == 1, "single-layer problem"
    assert t % 1024 == 0 and ctx_len % GDN_CHUNK == 0 and bsz < 8
    cap = EC_CAPACITY * t // N_EXPERTS

    cs = jnp.zeros((8, D_MODEL), F32).at[:bsz].set(c).at[bsz].set(c_ctx)
    mod3 = _mod_call(cs, w_mod[0], b_mod[0][None, :]).reshape(8, N_MOD, D_MODEL)

    wl = w_in[0]
    n_state = QKV_W + STATE_COLS
    w_state = _pad_lanes(wl[:, QKV_W:n_state])
    w_lat = jnp.concatenate([wl[:, :QKV_W], wl[:, n_state:n_state + B_W], wl[:, n_state + B_W:], w_state],
                            axis=1).astype(BF16)
    gp = jnp.zeros((8, LANES), F32).at[0, :2 * N_HEADS].set(a_log[0].reshape(-1)).at[1, :2 * N_HEADS].set(
        dt_bias[0].reshape(-1))
    g1 = norm1_g[0][None, :]
    cw = jnp.zeros((8, QKV_W), F32).at[:conv_w.shape[1]].set(conv_w[0])

    qkv_c, gb_c = _inproj_ctx_call(ctx, mod3, bsz, g1, w_lat, gp, cw)
    zero_state = jnp.zeros((bsz, N_HEADS, HEAD_DIM, HEAD_DIM), F32)
    _, _, s_f, s_b = _gdn_call(qkv_c, gb_c, zero_state, zero_state, ctx_len)

    qkv, gb, z, ya = _inproj_lat_call(x, mod3, g1, w_lat, gp, cw, gm_norm_g[0][None, :], gm_ws[0].astype(BF16),
                                      _pad_lanes(gm_bs[0].T), 1024)
    o_f, o_b, _, _ = _gdn_call(qkv, gb, s_f, s_b, 512)
    wr = _pad_lanes(w_router[0])
    wr_hi = wr.astype(BF16)
    wr_lo = (wr - wr_hi.astype(F32)).astype(BF16)
    br = jnp.full((1, LANES), -1e30, F32).at[0, :N_EXPERTS].set(b_router[0])
    h, fin, aff, afft = _mixout_call(x, o_f, o_b, z, ya, mod3, gdn_norm_g[0][None, :], w_out[0].astype(BF16),
                                     norm2_g[0][None, :], wr_hi, wr_lo, br, 1024)

    npieces = t // LANES
    slott, slot, off = _route_call(afft.reshape(bsz, npieces * N_EXPERTS, LANES), cap)
    base = off[:, :, 0].reshape(bsz, npieces, N_EXPERTS)[:, ::ROUTE_CHUNK // LANES, :]
    base_flat = jnp.concatenate([base, jnp.full((bsz, 1, N_EXPERTS), cap, I32)], axis=1).reshape(-1)
    xe = _dispatch_call(base_flat, slott.reshape(bsz, npieces, N_EXPERTS, LANES), fin, cap)
    y = _experts_call(xe, w_gate[0], w_up[0], w_down[0], cap)
    return _combine_call(base_flat, slot, aff, h, y, mod3, final_norm_g[None, :])
```

```python
import functools

import jax
import jax.numpy as jnp
from jax import lax
from jax.experimental import pallas as pl
from jax.experimental.pallas import tpu as pltpu

F32 = jnp.float32
BF16 = jnp.bfloat16
I32 = jnp.int32

D_MODEL = 1024
N_MOD = 6
N_HEADS = 4
HEAD_DIM = 128
B_W = N_HEADS * HEAD_DIM
QKV_W = 3 * B_W
A_W = 512
A_GROUPS = 4
A_CHUNK = 128
GDN_CHUNK = 64
N_EXPERTS = 16
EC_CAPACITY = 2
EXPERT_FF = 1024
NORM_EPS = 1e-6
LANES = 128
STATE_COLS = 4 * N_HEADS

ROUTE_CHUNK = 256
SLOT_WIN = 64
SLOT_ALIGN = 16
VMEM_LIMIT = 56 * 1024 * 1024


def _cparams(sem):
    return pltpu.CompilerParams(dimension_semantics=sem, vmem_limit_bytes=VMEM_LIMIT)


def _dot(a, b):
    return jnp.dot(a, b, preferred_element_type=F32)


def _dot_nt(a, b):
    return lax.dot_general(a, b, (((1,), (1,)), ((), ())), preferred_element_type=F32)


def _dot_tn(a, b):
    return lax.dot_general(a, b, (((0,), (0,)), ((), ())), preferred_element_type=F32)


def _silu(x):
    return x * jax.nn.sigmoid(x)


def _mod_body(c_ref, w_ref, b_ref, o_ref):
    s = _silu(c_ref[...])
    o_ref[...] = _dot(s.astype(BF16), w_ref[...].astype(BF16)) + b_ref[...]


def _mod_call(cs, w_mod, b_mod):
    n = w_mod.shape[1] // D_MODEL
    return pl.pallas_call(
        _mod_body,
        out_shape=jax.ShapeDtypeStruct((8, w_mod.shape[1]), F32),
        grid=(n,),
        in_specs=[pl.BlockSpec((8, D_MODEL), lambda j: (0, 0)),
                  pl.BlockSpec((D_MODEL, D_MODEL), lambda j: (0, j)),
                  pl.BlockSpec((1, D_MODEL), lambda j: (0, j))],
        out_specs=pl.BlockSpec((8, D_MODEL), lambda j: (0, j)),
        compiler_params=_cparams(("arbitrary",)),
        name="mod",
    )(cs, w_mod, b_mod)


def _norm_mod(x, g, shift, scale):
    ms = jnp.mean(x * x, axis=-1, keepdims=True)
    return (x * lax.rsqrt(ms + NORM_EPS) * g) * (1.0 + scale) + shift


def _gate_streams(st, gp_ref):
    lane = lax.broadcasted_iota(I32, st.shape, 1)
    g = -jnp.exp(gp_ref[0:1, :]) * jax.nn.softplus(st + gp_ref[1:2, :])
    beta = jax.nn.sigmoid(st)
    return jnp.where(lane < 2 * N_HEADS, g, jnp.where(lane < STATE_COLS, beta, 0.0))


def _conv_qkv(qkv, prev_row, next_row, cw_ref, out_ref, tm):
    cs = GDN_CHUNK
    nsub = tm // cs
    w0, w1, w2 = cw_ref[0:1, :], cw_ref[1:2, :], cw_ref[2:3, :]
    row = lax.broadcasted_iota(I32, (cs, 1), 0)
    for c in range(nsub):
        rows = slice(c * cs, (c + 1) * cs)
        x = qkv[rows]
        prow = prev_row if c == 0 else qkv[c * cs - 1:c * cs]
        nrow = next_row if c == nsub - 1 else qkv[(c + 1) * cs:(c + 1) * cs + 1]
        xp = jnp.where(row == 0, prow, pltpu.roll(x, 1, 0))
        xn = jnp.where(row == cs - 1, nrow, pltpu.roll(x, cs - 1, 0))
        y = _silu(xp * w0 + x * w1 + xn * w2)
        for h in range(N_HEADS):
            cq = slice(h * HEAD_DIM, (h + 1) * HEAD_DIM)
            ck = slice(B_W + h * HEAD_DIM, B_W + (h + 1) * HEAD_DIM)
            q = y[:, cq]
            k = y[:, ck]
            out_ref[0, rows, cq] = (q * (lax.rsqrt(jnp.sum(q * q, axis=-1, keepdims=True) + NORM_EPS)
                                         * (HEAD_DIM ** -0.5))).astype(BF16)
            out_ref[0, rows, ck] = (k * lax.rsqrt(jnp.sum(k * k, axis=-1, keepdims=True) + NORM_EPS)).astype(BF16)
        out_ref[0, rows, 2 * B_W:3 * B_W] = y[:, 2 * B_W:3 * B_W].astype(BF16)


def _inproj_lat_body(x_ref, xp_ref, xn_ref, mod_ref, g1_ref, w_ref, gp_ref, cw_ref, gmg_ref, ws_ref, bst_ref,
                     qkv_ref, gb_ref, z_ref, ya_ref, *, tm):
    i = pl.program_id(1)
    shift, scale = mod_ref[0, 0:1, :], mod_ref[0, 1:2, :]
    a = _norm_mod(x_ref[0], g1_ref[...], shift, scale).astype(BF16)
    xh = jnp.concatenate([xp_ref[0], xn_ref[0]], axis=0)
    halo = _dot(_norm_mod(xh, g1_ref[...], shift, scale).astype(BF16), w_ref[:, 0:QKV_W])
    prev_row = jnp.where(i == 0, 0.0, halo[7:8, :])
    next_row = jnp.where(i == pl.num_programs(1) - 1, 0.0, halo[8:9, :])
    _conv_qkv(_dot(a, w_ref[:, 0:QKV_W]), prev_row, next_row, cw_ref, qkv_ref, tm)
    z_ref[0] = _dot(a, w_ref[:, QKV_W:QKV_W + B_W]).astype(BF16)
    c_uv = QKV_W + B_W
    gb_ref[0] = _gate_streams(_dot(a, w_ref[:, c_uv + 2 * A_W:c_uv + 2 * A_W + LANES]), gp_ref)
    uv = _dot(a, w_ref[:, c_uv:c_uv + 2 * A_W])
    uv = 0.5 * uv * (1.0 + lax.erf(uv * 0.7071067811865476))
    gd = A_W // A_GROUPS
    for grp in range(A_GROUPS):
        v = uv[:, A_W + grp * gd:A_W + (grp + 1) * gd]
        vn = v * lax.rsqrt(jnp.mean(v * v, axis=-1, keepdims=True) + NORM_EPS) * gmg_ref[:, grp * gd:(grp + 1) * gd]
        vn = vn.astype(BF16)
        bias = bst_ref[:, grp:grp + 1]
        for c in range(tm // A_CHUNK):
            rows = slice(c * A_CHUNK, (c + 1) * A_CHUNK)
            s = _dot(ws_ref[grp], vn[rows]) + bias
            ya_ref[0, rows, grp * gd:(grp + 1) * gd] = (uv[rows, grp * gd:(grp + 1) * gd] * s).astype(BF16)


def _inproj_ctx_body(x_ref, mod_ref, g1_ref, w_ref, gp_ref, cw_ref, qkv_ref, gb_ref, *, tm):
    a = _norm_mod(x_ref[0], g1_ref[...], mod_ref[0, 0:1, :], mod_ref[0, 1:2, :]).astype(BF16)
    edge = jnp.zeros((1, QKV_W), F32)
    _conv_qkv(_dot(a, w_ref[:, 0:QKV_W]), edge, edge, cw_ref, qkv_ref, tm)
    c_state = QKV_W + B_W + 2 * A_W
    gb_ref[0] = _gate_streams(_dot(a, w_ref[:, c_state:c_state + LANES]), gp_ref)


def _inproj_lat_call(x, mod3, g1, w_lat, gp, cw, gmg, ws16, bst, tm):
    bsz, t, _ = x.shape
    hb = tm // 8
    last8 = t // 8 - 1
    full = lambda a: pl.BlockSpec(a.shape, lambda b, i: (0,) * a.ndim)
    tok = lambda w: pl.BlockSpec((1, tm, w), lambda b, i: (b, i, 0))
    return pl.pallas_call(
        functools.partial(_inproj_lat_body, tm=tm),
        out_shape=(jax.ShapeDtypeStruct((bsz, t, QKV_W), BF16),
                   jax.ShapeDtypeStruct((bsz, t, LANES), F32),
                   jax.ShapeDtypeStruct((bsz, t, B_W), BF16),
                   jax.ShapeDtypeStruct((bsz, t, A_W), BF16)),
        grid=(bsz, t // tm),
        in_specs=[tok(D_MODEL),
                  pl.BlockSpec((1, 8, D_MODEL), lambda b, i: (b, jnp.maximum(i * hb - 1, 0), 0)),
                  pl.BlockSpec((1, 8, D_MODEL), lambda b, i: (b, jnp.minimum((i + 1) * hb, last8), 0)),
                  pl.BlockSpec((1, N_MOD, D_MODEL), lambda b, i: (b, 0, 0)),
                  full(g1), full(w_lat), full(gp), full(cw), full(gmg), full(ws16), full(bst)],
        out_specs=(tok(QKV_W), tok(LANES), tok(B_W), tok(A_W)),
        compiler_params=_cparams(("parallel", "arbitrary")),
        name="inproj_lat",
    )(x, x, x, mod3, g1, w_lat, gp, cw, gmg, ws16, bst)


def _inproj_ctx_call(ctx, mod3, ctx_row, g1, w_ctx, gp, cw):
    bsz, t, _ = ctx.shape
    full = lambda a: pl.BlockSpec(a.shape, lambda b: (0,) * a.ndim)
    tok = lambda w: pl.BlockSpec((1, t, w), lambda b: (b, 0, 0))
    return pl.pallas_call(
        functools.partial(_inproj_ctx_body, tm=t),
        out_shape=(jax.ShapeDtypeStruct((bsz, t, QKV_W), BF16),
                   jax.ShapeDtypeStruct((bsz, t, LANES), F32)),
        grid=(bsz,),
        in_specs=[tok(D_MODEL),
                  pl.BlockSpec((1, N_MOD, D_MODEL), lambda b: (ctx_row, 0, 0)),
                  full(g1), full(w_ctx), full(gp), full(cw)],
        out_specs=(tok(QKV_W), tok(LANES)),
        compiler_params=_cparams(("parallel",)),
        name="inproj_ctx",
    )(ctx, mod3, g1, w_ctx, gp, cw)


GDN_CHUNKS_PER_STEP = 8
GDN_STAGE_SKEW = 2
PACK_W = N_HEADS * GDN_CHUNK


def _per_head(tile, lanes, width):
    rows = tile.shape[0]
    if width == HEAD_DIM:
        return jnp.concatenate([jnp.broadcast_to(tile[:, l:l + 1], (rows, width)) for l in lanes], axis=1)
    head = lax.broadcasted_iota(I32, (rows, N_HEADS * width), 1) // width
    out = jnp.broadcast_to(tile[:, lanes[0]:lanes[0] + 1], (rows, N_HEADS * width))
    for h in range(1, N_HEADS):
        out = jnp.where(head == h, jnp.broadcast_to(tile[:, lanes[h]:lanes[h] + 1], (rows, N_HEADS * width)), out)
    return out


def _block_rows(x16, half_masks=None):
    rows, width = x16.shape
    per_head = width // N_HEADS
    zero = jnp.zeros((rows, LANES), x16.dtype)
    blocks = []
    for h in range(N_HEADS):
        tile = h * per_head // LANES
        kept = x16[:, tile * LANES:(tile + 1) * LANES]
        if per_head < LANES:
            kept = kept * half_masks[h * per_head % LANES // per_head]
        blocks.append(jnp.concatenate([kept if t == tile else zero for t in range(width // LANES)], axis=1))
    return jnp.concatenate(blocks, axis=0)


def _gdn_group_program(g, half_masks, turn):
    cs = GDN_CHUNK
    hd = HEAD_DIM
    q, k, v, beta, egc = g["q"], g["k"], g["v"], g["beta"], g["egc"]
    kb = k * beta
    decay = jnp.where(g["incl"], jnp.exp(jnp.where(g["incl"], g["gc_col"] - g["gc_row"], 0.0)), 0.0)
    kk = _dot_nt(jnp.concatenate([kb, q], axis=0).astype(BF16), _block_rows(k.astype(BF16)))
    yield
    a = jnp.where(g["strict"], kk[:cs] * decay, 0.0)
    attn = (kk[cs:] * decay).astype(BF16)
    m = -jnp.where(g["levels"][0], a, 0.0)
    for lm in g["levels"][1:]:
        m16 = m.astype(BF16)
        cm = jnp.where(lm, a, 0.0)
        x = cm + _dot(m16, _block_rows(cm.astype(BF16), half_masks))
        yield
        y = x + _dot(x.astype(BF16), _block_rows(m16, half_masks))
        yield
        m = m - y
    m16 = m.astype(BF16)
    vb = v * beta
    kbg = kb * egc
    u = vb + _dot(m16, _block_rows(vb.astype(BF16)))
    w = kbg + _dot(m16, _block_rows(kbg.astype(BF16)))
    qg = q * egc
    kg = (k * g["kdec"]).astype(BF16)
    yield
    while turn[g["backward"]] != g["order"]:
        yield
    s_ref = g["s_ref"]
    s = [s_ref[h] for h in range(N_HEADS)]
    wq = [_dot(jnp.concatenate([w[:, h * hd:(h + 1) * hd], qg[:, h * hd:(h + 1) * hd]], axis=0).astype(BF16),
               s[h].astype(BF16)) for h in range(N_HEADS)]
    yield
    v_new = (u - jnp.concatenate([r[:cs] for r in wq], axis=1)).astype(BF16)
    o = jnp.concatenate([r[cs:] for r in wq], axis=1) + _dot(attn, _block_rows(v_new))
    for h in range(N_HEADS):
        cols = slice(h * hd, (h + 1) * hd)
        s_ref[h] = s[h] * g["eg"][h] + _dot_tn(kg[:, cols], v_new[:, cols])
    g["o_ref"][0, pl.ds(g["r0"], cs), :] = o.astype(BF16)
    turn[g["backward"]] += 1


def _run_interleaved(programs, skew):
    live = {}
    tick = 0
    while live or tick <= skew * (len(programs) - 1):
        if tick % skew == 0 and tick // skew < len(programs):
            live[tick // skew] = programs[tick // skew]
        for key in sorted(live):
            for prog in live[key]:
                if next(prog, "done") == "done":
                    live[key] = [p for p in live[key] if p is not prog]
            if not live[key]:
                del live[key]
        tick += 1


def _gdn_body(qf_ref, qb_ref, gbf_ref, gbb_ref, s0f_ref, s0b_ref,
              of_ref, ob_ref, sff_ref, sfb_ref, sf_scr, sb_scr, *, tb, nt):
    t = pl.program_id(1)
    cs = GDN_CHUNK
    nch = tb // cs
    cps = min(GDN_CHUNKS_PER_STEP, nch)
    nh = N_HEADS

    @pl.when(t == 0)
    def _():
        sf_scr[...] = s0f_ref[0]
        sb_scr[...] = s0b_ref[0]

    ii = lax.broadcasted_iota(I32, (cs, PACK_W), 0)
    jj = lax.broadcasted_iota(I32, (cs, PACK_W), 1) & (cs - 1)
    incl_f, strict_f = jj <= ii, jj < ii
    incl_b, strict_b = jj >= ii, jj > ii
    levels = []
    sh = 0
    while (1 << sh) < cs:
        levels.append(((ii >> (sh + 1)) == (jj >> (sh + 1))) & ((ii >> sh) != (jj >> sh)))
        sh += 1
    lv_f = [lm & strict_f for lm in levels]
    lv_b = [lm & strict_b for lm in levels]
    row = lax.broadcasted_iota(I32, (cs, LANES), 0)
    lane = lax.broadcasted_iota(I32, (1, LANES), 1)
    half_of_tile = lax.broadcasted_iota(I32, (cs, LANES), 1) // cs
    half_masks = [(half_of_tile == i).astype(BF16) for i in range(LANES // cs)]

    def group(q_ref, r0, gate, csum, gt, backward, s_ref, o_ref, order):
        l0 = nh if backward else 0
        lanes = [l0 + h for h in range(nh)]
        last = 0 if backward else cs - 1
        ld = lambda c0: q_ref[0, pl.ds(r0, cs), c0:c0 + B_W].astype(F32)
        halves = [gt[l:l + 1, :] if (h % 2 == 0) != backward else pltpu.roll(gt[l:l + 1, :], cs, 1)
                  for h, l in enumerate(lanes)]
        gc_row = jnp.concatenate([jnp.where(lane < cs, halves[0], halves[1]),
                                  jnp.where(lane < cs, halves[2], halves[3])], axis=1)
        glast = csum[last:last + 1, :]
        return dict(
            q=ld(0), k=ld(B_W), v=ld(2 * B_W),
            beta=_per_head(gate, [2 * nh + l for l in lanes], HEAD_DIM),
            egc=_per_head(jnp.exp(csum), lanes, HEAD_DIM),
            kdec=_per_head(jnp.exp(glast - csum), lanes, HEAD_DIM),
            eg=[jnp.exp(csum[last:last + 1, l:l + 1]) for l in lanes],
            gc_col=_per_head(csum, lanes, cs), gc_row=gc_row,
            incl=incl_b if backward else incl_f, strict=strict_b if backward else strict_f,
            levels=lv_b if backward else lv_f, s_ref=s_ref, o_ref=o_ref, r0=r0, backward=backward, order=order)

    def step(n, carry):
        per_chunk = []
        for j in range(cps):
            rf = pl.multiple_of((n * cps + j) * cs, cs)
            rb = pl.multiple_of((nch - 1 - n * cps - j) * cs, cs)
            gf = gbf_ref[0, pl.ds(rf, cs), :]
            gb = gbb_ref[0, pl.ds(rb, cs), :]
            cf, cb = gf, gb
            s = 1
            while s < cs:
                cf = cf + jnp.where(row >= s, pltpu.roll(cf, s, 0), 0.0)
                cb = cb + jnp.where(row < cs - s, pltpu.roll(cb, cs - s, 0), 0.0)
                s *= 2
            gt = jnp.concatenate([cf, cb], axis=0).T
            per_chunk.append([group(qf_ref, rf, gf, cf, gt, False, sf_scr, of_ref, j),
                              group(qb_ref, rb, gb, cb, gt, True, sb_scr, ob_ref, j)])
        turn = {False: 0, True: 0}
        _run_interleaved([[_gdn_group_program(g, half_masks, turn) for g in pair] for pair in per_chunk],
                         GDN_STAGE_SKEW)
        return carry

    lax.fori_loop(0, nch // cps, step, 0)

    @pl.when(t == nt - 1)
    def _():
        sff_ref[0] = sf_scr[...]
        sfb_ref[0] = sb_scr[...]


def _gdn_call(qkv, gb, s0f, s0b, tb):
    bsz, t, _ = qkv.shape
    nt = t // tb
    assert (tb // GDN_CHUNK) % min(GDN_CHUNKS_PER_STEP, tb // GDN_CHUNK) == 0
    st = pl.BlockSpec((1, N_HEADS, HEAD_DIM, HEAD_DIM), lambda b, i: (b, 0, 0, 0))
    fwd = lambda w: pl.BlockSpec((1, tb, w), lambda b, i: (b, i, 0))
    bwd = lambda w: pl.BlockSpec((1, tb, w), lambda b, i: (b, nt - 1 - i, 0))
    return pl.pallas_call(
        functools.partial(_gdn_body, tb=tb, nt=nt),
        out_shape=(jax.ShapeDtypeStruct((bsz, t, B_W), BF16), jax.ShapeDtypeStruct((bsz, t, B_W), BF16),
                   jax.ShapeDtypeStruct((bsz, N_HEADS, HEAD_DIM, HEAD_DIM), F32),
                   jax.ShapeDtypeStruct((bsz, N_HEADS, HEAD_DIM, HEAD_DIM), F32)),
        grid=(bsz, nt),
        in_specs=[fwd(QKV_W), bwd(QKV_W), fwd(LANES), bwd(LANES), st, st],
        out_specs=(fwd(B_W), bwd(B_W), st, st),
        scratch_shapes=[pltpu.VMEM((N_HEADS, HEAD_DIM, HEAD_DIM), F32), pltpu.VMEM((N_HEADS, HEAD_DIM, HEAD_DIM), F32)],
        compiler_params=_cparams(("parallel", "arbitrary")),
        name="gdn",
    )(qkv, qkv, gb, gb, s0f, s0b)


def _mixout_body(x_ref, of_ref, ob_ref, z_ref, ya_ref, mod_ref, gng_ref, wout_ref, n2g_ref, wrh_ref, wrl_ref, br_ref,
                 h_ref, fin_ref, aff_ref, afft_ref, *, tm):
    o = of_ref[0].astype(F32) + ob_ref[0].astype(F32)
    z = z_ref[0].astype(F32)
    parts = [ya_ref[0]]
    for h in range(N_HEADS):
        c = slice(h * HEAD_DIM, (h + 1) * HEAD_DIM)
        oh = o[:, c]
        y = oh * lax.rsqrt(jnp.mean(oh * oh, axis=-1, keepdims=True) + NORM_EPS)
        parts.append((y * gng_ref[...] * _silu(z[:, c])).astype(BF16))
    mix = _dot(jnp.concatenate(parts, axis=1), wout_ref[...])
    hl = x_ref[0] + mod_ref[0, 2:3, :] * mix
    h_ref[0] = hl
    fin = _norm_mod(hl, n2g_ref[...], mod_ref[0, 3:4, :], mod_ref[0, 4:5, :])
    f_hi = fin.astype(BF16)
    fin_ref[0] = f_hi
    f_lo = (fin - f_hi.astype(F32)).astype(BF16)
    both = _dot(f_hi, wrl_ref[...])
    logits = both[:, :LANES] + _dot(f_lo, wrh_ref[...]) + both[:, LANES:] + br_ref[...]
    e = jnp.exp(logits - jnp.max(logits, axis=-1, keepdims=True))
    aff = e / jnp.sum(e, axis=-1, keepdims=True)
    aff_ref[0] = aff
    for j in range(tm // LANES):
        afft_ref[0, j] = aff[j * LANES:(j + 1) * LANES, :].T[0:N_EXPERTS, :]


def _mixout_call(x, o_f, o_b, z, ya, mod3, gng, wout16, n2g, wr_hi, wr_lo, br, tm):
    bsz, t, _ = x.shape
    full = lambda a: pl.BlockSpec(a.shape, lambda b, i: (0,) * a.ndim)
    tok = lambda w: pl.BlockSpec((1, tm, w), lambda b, i: (b, i, 0))
    return pl.pallas_call(
        functools.partial(_mixout_body, tm=tm),
        out_shape=(jax.ShapeDtypeStruct((bsz, t, D_MODEL), F32), jax.ShapeDtypeStruct((bsz, t, D_MODEL), BF16),
                   jax.ShapeDtypeStruct((bsz, t, LANES), F32),
                   jax.ShapeDtypeStruct((bsz, t // LANES, N_EXPERTS, LANES), F32)),
        grid=(bsz, t // tm),
        in_specs=[tok(D_MODEL), tok(B_W), tok(B_W), tok(B_W), tok(A_W),
                  pl.BlockSpec((1, N_MOD, D_MODEL), lambda b, i: (b, 0, 0)),
                  full(gng), full(wout16), full(n2g), full(wr_hi), full(wr_lo), full(br)],
        out_specs=(tok(D_MODEL), tok(D_MODEL), tok(LANES),
                   pl.BlockSpec((1, tm // LANES, N_EXPERTS, LANES), lambda b, i: (b, i, 0, 0))),
        compiler_params=_cparams(("parallel", "parallel")),
        name="mixout",
    )(x, o_f, o_b, z, ya, mod3, gng, wout16, n2g, wr_hi, wr_lo, br)


def _route_body(afft_ref, slott_ref, slot_ref, off_ref, *, t, cap):
    ne = N_EXPERTS
    npieces = t // LANES
    rows = npieces * ne

    def count(thr_col, strict):
        acc = jnp.zeros((ne, LANES), I32)
        for p in range(npieces):
            piece = afft_ref[0, p * ne:(p + 1) * ne, :]
            acc = acc + (piece > thr_col if strict else piece >= thr_col).astype(I32)
        return jnp.sum(acc, axis=1, keepdims=True)

    def search(i, thr):
        cand = thr | jnp.left_shift(jnp.int32(1), 30 - i)
        return jnp.where(count(pltpu.bitcast(cand, F32), False) >= cap, cand, thr)

    thr_bits = lax.fori_loop(0, 31, search, jnp.zeros((ne, 1), I32))
    thr = pltpu.bitcast(thr_bits, F32)
    need = (cap - count(thr, True)).astype(F32)

    x = afft_ref[0]
    thr_rows = jnp.concatenate([thr] * npieces, axis=0)
    need_rows = jnp.concatenate([need] * npieces, axis=0)
    gt = x > thr_rows
    eq = x == thr_rows
    ti = lax.broadcasted_iota(I32, (LANES, LANES), 0)
    tj = lax.broadcasted_iota(I32, (LANES, LANES), 1)
    triu = (ti <= tj).astype(BF16)
    ri = lax.broadcasted_iota(I32, (rows, rows), 0)
    rj = lax.broadcasted_iota(I32, (rows, rows), 1)
    earlier = (((ri & (ne - 1)) == (rj & (ne - 1))) & (rj < ri)).astype(BF16)

    def prefix(mask):
        inpiece = _dot(mask.astype(BF16), triu)
        total = jnp.broadcast_to(inpiece[:, LANES - 1:LANES], (rows, LANES)).astype(BF16)
        offset = _dot(earlier, total)
        return inpiece + offset, offset

    eq_rank, _ = prefix(eq)
    sel = gt | (eq & (eq_rank <= need_rows))
    sel_rank, sel_off = prefix(sel)
    slot = jnp.where(sel, sel_rank - 1.0, -1.0)
    slott_ref[0] = slot.astype(I32)
    off_ref[0] = sel_off.astype(I32)
    pad = jnp.zeros((LANES - ne, LANES), F32)
    for p in range(npieces):
        piece = jnp.concatenate([slot[p * ne:(p + 1) * ne, :], pad], axis=0)
        slot_ref[0, p * LANES:(p + 1) * LANES, :] = piece.T.astype(I32)


def _route_call(afft, cap):
    bsz, rows, _ = afft.shape
    t = rows // N_EXPERTS * LANES
    spec = lambda r: pl.BlockSpec((1, r, LANES), lambda b: (b, 0, 0))
    return pl.pallas_call(
        functools.partial(_route_body, t=t, cap=cap),
        out_shape=(jax.ShapeDtypeStruct((bsz, rows, LANES), I32),
                   jax.ShapeDtypeStruct((bsz, t, LANES), I32),
                   jax.ShapeDtypeStruct((bsz, rows, LANES), I32)),
        grid=(bsz,),
        in_specs=[spec(rows)],
        out_specs=(spec(rows), spec(t), spec(rows)),
        compiler_params=_cparams(("parallel",)),
        name="route",
    )(afft)


def _window_plan(base_ref, flat0, experts):
    starts, rounds = [], jnp.int32(0)
    for e in experts:
        lo = base_ref[flat0 + e]
        hi = base_ref[flat0 + N_EXPERTS + e]
        lo_al = (lo >> 4) << 4
        starts.append(lo_al)
        rounds = jnp.maximum(rounds, (hi - lo_al + SLOT_WIN - 1) // SLOT_WIN)
    return starts, rounds


def _window_start(start, r, cap):
    return pl.multiple_of(jnp.minimum(start + r * SLOT_WIN, cap), SLOT_ALIGN)


def _dispatch_body(base_ref, slott_ref, fin_ref, xe_ref, *, nchunk, sub, eh_n, cap):
    b, eh, ci = pl.program_id(0), pl.program_id(1), pl.program_id(2)
    rc = ROUTE_CHUNK

    @pl.when(ci == 0)
    def _():
        xe_ref[...] = jnp.zeros_like(xe_ref)

    srow = lax.broadcasted_iota(I32, (SLOT_WIN, rc), 0)
    for sc in range(sub):
        cc = ci * sub + sc
        flat0 = (b * (nchunk + 1) + cc) * N_EXPERTS + eh * eh_n
        f = fin_ref[0, sc * rc:(sc + 1) * rc, :]
        experts = list(range(eh_n))
        starts, rounds = _window_plan(base_ref, flat0, experts)

        def one_round(r, carry, starts=starts, f=f, sc=sc):
            rows = []
            wstart = [_window_start(starts[e], r, cap) for e in experts]
            for e in experts:
                tok_slot = jnp.concatenate(
                    [slott_ref[0, sc * (rc // LANES) + j, e:e + 1, :] for j in range(rc // LANES)], axis=1)
                rows.append((tok_slot == srow + wstart[e]).astype(BF16))
            prod = _dot(jnp.concatenate(rows, axis=0), f)
            for e in experts:
                win = pl.ds(wstart[e], SLOT_WIN)
                xe_ref[0, e, win, :] = xe_ref[0, e, win, :] + prod[e * SLOT_WIN:(e + 1) * SLOT_WIN].astype(BF16)
            return carry

        one_round(jnp.int32(0), 0)
        lax.fori_loop(1, rounds, one_round, 0)


def _dispatch_call(base_flat, slott, fin, cap):
    bsz, t, _ = fin.shape
    nchunk = t // ROUTE_CHUNK
    sub = 2
    eh_n = N_EXPERTS // 2
    sp = cap + SLOT_WIN
    grid_spec = pltpu.PrefetchScalarGridSpec(
        num_scalar_prefetch=1,
        grid=(bsz, N_EXPERTS // eh_n, nchunk // sub),
        in_specs=[pl.BlockSpec((1, sub * ROUTE_CHUNK // LANES, eh_n, LANES), lambda b, eh, ci, base: (b, ci, eh, 0)),
                  pl.BlockSpec((1, sub * ROUTE_CHUNK, D_MODEL), lambda b, eh, ci, base: (b, ci, 0))],
        out_specs=pl.BlockSpec((1, eh_n, sp, D_MODEL), lambda b, eh, ci, base: (b, eh, 0, 0)))
    return pl.pallas_call(
        functools.partial(_dispatch_body, nchunk=nchunk, sub=sub, eh_n=eh_n, cap=cap),
        out_shape=jax.ShapeDtypeStruct((bsz, N_EXPERTS, sp, D_MODEL), BF16),
        grid_spec=grid_spec,
        compiler_params=_cparams(("parallel", "parallel", "arbitrary")),
        name="dispatch",
    )(base_flat, slott, fin)


def _experts_body(xe_ref, wg_ref, wu_ref, wd_ref, y_ref, *, cap):
    x = xe_ref[0, 0, 0:cap, :]
    ft = 256
    acc = None
    for f in range(EXPERT_FF // ft):
        cols = slice(f * ft, (f + 1) * ft)
        wg16 = wg_ref[0, :, cols].astype(BF16)
        wu16 = wu_ref[0, :, cols].astype(BF16)
        wd16 = wd_ref[0, cols, :].astype(BF16)
        hid = (_silu(_dot(x, wg16)) * _dot(x, wu16)).astype(BF16)
        part = _dot(hid, wd16)
        acc = part if acc is None else acc + part
    y_ref[0, 0, 0:cap, :] = acc.astype(BF16)
    y_ref[0, 0, cap:, :] = jnp.zeros((y_ref.shape[2] - cap, D_MODEL), BF16)


def _experts_call(xe, w_gate, w_up, w_down, cap):
    bsz, _, sp, _ = xe.shape
    wspec = lambda shape: pl.BlockSpec((1,) + shape, lambda e, b: (e, 0, 0))
    slots = pl.BlockSpec((1, 1, sp, D_MODEL), lambda e, b: (b, e, 0, 0))
    return pl.pallas_call(
        functools.partial(_experts_body, cap=cap),
        out_shape=jax.ShapeDtypeStruct(xe.shape, BF16),
        grid=(N_EXPERTS, bsz),
        in_specs=[slots, wspec((D_MODEL, EXPERT_FF)), wspec((D_MODEL, EXPERT_FF)), wspec((EXPERT_FF, D_MODEL))],
        out_specs=slots,
        compiler_params=_cparams(("parallel", "parallel")),
        name="experts",
    )(xe, w_gate, w_up, w_down)


SLOT_SPLIT = 32


def _combine_selectors():
    k = jnp.arange(LANES)[:, None]
    e_of_lane = jnp.arange(N_EXPERTS * SLOT_WIN)[None, :] // SLOT_WIN
    sel_gate = (k == e_of_lane).astype(BF16)
    sel_slot = (SLOT_SPLIT * (k == e_of_lane) + (k - N_EXPERTS == e_of_lane)).astype(BF16)
    return sel_slot, sel_gate


def _combine_body(base_ref, slot_ref, aff_ref, h_ref, y_ref, mod_ref, fng_ref, ssel_ref, gsel_ref, o_ref, acc_ref,
                  *, nchunk, sub, cap):
    b, ci = pl.program_id(0), pl.program_id(1)
    rc = ROUTE_CHUNK
    width = N_EXPERTS * SLOT_WIN
    lane = lax.broadcasted_iota(I32, (1, width), 1)
    lane_e = lane >> (SLOT_WIN.bit_length() - 1)
    lane_j = (lane & (SLOT_WIN - 1)).astype(F32)
    lane128 = lax.broadcasted_iota(I32, (rc, LANES), 1)
    experts = list(range(N_EXPERTS))
    for sc in range(sub):
        rows = slice(sc * rc, (sc + 1) * rc)
        flat0 = (b * (nchunk + 1) + ci * sub + sc) * N_EXPERTS
        s1 = slot_ref[0, rows, :] + 1
        halves = jnp.where(lane128 < N_EXPERTS, s1 >> (SLOT_SPLIT.bit_length() - 1),
                           pltpu.roll(s1 & (SLOT_SPLIT - 1), N_EXPERTS, 1))
        slot1 = _dot(halves.astype(F32).astype(BF16), ssel_ref[...])
        gates = _dot(aff_ref[0, rows, :].astype(BF16), gsel_ref[...])
        starts, rounds = _window_plan(base_ref, flat0, experts)

        def contribution(r, starts=starts, slot1=slot1, gates=gates):
            wstart = [_window_start(starts[e], r, cap) for e in experts]
            ywin = jnp.concatenate([y_ref[0, e, pl.ds(wstart[e], SLOT_WIN), :] for e in experts], axis=0)
            held = jnp.zeros((1, width), I32)
            for e in experts:
                held = jnp.where(lane_e == e, wstart[e] + 1, held)
            held = held.astype(F32) + lane_j
            s = jnp.where(slot1 == held, gates, 0.0).astype(BF16)
            return _dot(s, ywin)

        def extra_round(r, carry, contribution=contribution):
            acc_ref[...] += contribution(r)
            return carry

        acc_ref[...] = contribution(jnp.int32(0))
        lax.fori_loop(1, rounds, extra_round, 0)
        hl = h_ref[0, rows, :] + mod_ref[0, 5:6, :] * acc_ref[...]
        ms = jnp.mean(hl * hl, axis=-1, keepdims=True)
        o_ref[0, rows, :] = hl * lax.rsqrt(ms + NORM_EPS) * fng_ref[...]


def _combine_call(base_flat, slot, aff, h, y, mod3, fng):
    bsz, t, _ = h.shape
    nchunk = t // ROUTE_CHUNK
    sub = 2
    rc = ROUTE_CHUNK
    ssel, gsel = _combine_selectors()
    tok = lambda w: pl.BlockSpec((1, sub * rc, w), lambda b, i, base: (b, i, 0))
    full = lambda a: pl.BlockSpec(a.shape, lambda b, i, base: (0,) * a.ndim)
    grid_spec = pltpu.PrefetchScalarGridSpec(
        num_scalar_prefetch=1,
        grid=(bsz, nchunk // sub),
        in_specs=[tok(LANES), tok(LANES), tok(D_MODEL),
                  pl.BlockSpec((1,) + y.shape[1:], lambda b, i, base: (b, 0, 0, 0), pipeline_mode=pl.Buffered(1)),
                  pl.BlockSpec((1, N_MOD, D_MODEL), lambda b, i, base: (b, 0, 0)),
                  full(fng), full(ssel), full(gsel)],
        out_specs=tok(D_MODEL),
        scratch_shapes=[pltpu.VMEM((rc, D_MODEL), F32)])
    return pl.pallas_call(
        functools.partial(_combine_body, nchunk=nchunk, sub=sub, cap=y.shape[2] - SLOT_WIN),
        out_shape=jax.ShapeDtypeStruct(h.shape, F32),
        grid_spec=grid_spec,
        compiler_params=_cparams(("parallel", "arbitrary")),
        name="combine",
    )(base_flat, slot, aff, h, y, mod3, fng, ssel, gsel)


def _pad_lanes(a):
    return jnp.pad(a, ((0, 0), (0, LANES - a.shape[1])))


def kernel(x, c, ctx, c_ctx, w_mod, b_mod, norm1_g, norm2_g, w_in, conv_w, a_log, dt_bias, gdn_norm_g, gm_norm_g,
           gm_ws, gm_bs, w_out, w_router, b_router, w_gate, w_up, w_down, final_norm_g):
    bsz, t, _ = x.shape
    ctx_len = ctx.shape[1]
    assert w_mod.shape[0] == 1, "single-layer problem"
    assert t % 1024 == 0 and ctx_len % GDN_CHUNK == 0 and bsz < 8
    cap = EC_CAPACITY * t // N_EXPERTS

    cs = jnp.zeros((8, D_MODEL), F32).at[:bsz].set(c).at[bsz].set(c_ctx)
    mod3 = _mod_call(cs, w_mod[0], b_mod[0][None, :]).reshape(8, N_MOD, D_MODEL)

    wl = w_in[0]
    n_state = QKV_W + STATE_COLS
    w_state = _pad_lanes(wl[:, QKV_W:n_state])
    w_lat = jnp.concatenate([wl[:, :QKV_W], wl[:, n_state:n_state + B_W], wl[:, n_state + B_W:], w_state],
                            axis=1).astype(BF16)
    gp = jnp.zeros((8, LANES), F32).at[0, :2 * N_HEADS].set(a_log[0].reshape(-1)).at[1, :2 * N_HEADS].set(
        dt_bias[0].reshape(-1))
    g1 = norm1_g[0][None, :]
    cw = jnp.zeros((8, QKV_W), F32).at[:conv_w.shape[1]].set(conv_w[0])

    qkv_c, gb_c = _inproj_ctx_call(ctx, mod3, bsz, g1, w_lat, gp, cw)
    zero_state = jnp.zeros((bsz, N_HEADS, HEAD_DIM, HEAD_DIM), F32)
    _, _, s_f, s_b = _gdn_call(qkv_c, gb_c, zero_state, zero_state, ctx_len)

    qkv, gb, z, ya = _inproj_lat_call(x, mod3, g1, w_lat, gp, cw, gm_norm_g[0][None, :], gm_ws[0].astype(BF16),
                                      _pad_lanes(gm_bs[0].T), 1024)
    o_f, o_b, _, _ = _gdn_call(qkv, gb, s_f, s_b, 1024)
    wr = _pad_lanes(w_router[0])
    wr_hi = wr.astype(BF16)
    wr_lo = jnp.concatenate([wr_hi, (wr - wr_hi.astype(F32)).astype(BF16)], axis=1)
    br = jnp.full((1, LANES), -1e30, F32).at[0, :N_EXPERTS].set(b_router[0])
    h, fin, aff, afft = _mixout_call(x, o_f, o_b, z, ya, mod3, gdn_norm_g[0][None, :], w_out[0].astype(BF16),
                                     norm2_g[0][None, :], wr_hi, wr_lo, br, 1024)

    npieces = t // LANES
    slott, slot, off = _route_call(afft.reshape(bsz, npieces * N_EXPERTS, LANES), cap)
    base = off[:, :, 0].reshape(bsz, npieces, N_EXPERTS)[:, ::ROUTE_CHUNK // LANES, :]
    base_flat = jnp.concatenate([base, jnp.full((bsz, 1, N_EXPERTS), cap, I32)], axis=1).reshape(-1)
    xe = _dispatch_call(base_flat, slott.reshape(bsz, npieces, N_EXPERTS, LANES), fin, cap)
    y = _experts_call(xe, w_gate[0], w_up[0], w_down[0], cap)
    return _combine_call(base_flat, slot, aff, h, y, mod3, final_norm_g[None, :])
```

```python
import functools

import jax
import jax.numpy as jnp
from jax import lax
from jax.experimental import pallas as pl
from jax.experimental.pallas import tpu as pltpu

F32 = jnp.float32
BF16 = jnp.bfloat16
I32 = jnp.int32

D_MODEL = 1024
N_MOD = 6
N_HEADS = 4
HEAD_DIM = 128
B_W = N_HEADS * HEAD_DIM
QKV_W = 3 * B_W
A_W = 512
A_GROUPS = 4
A_CHUNK = 128
GDN_CHUNK = 64
N_EXPERTS = 16
EC_CAPACITY = 2
EXPERT_FF = 1024
NORM_EPS = 1e-6
LANES = 128
STATE_COLS = 4 * N_HEADS

ROUTE_CHUNK = 256
SLOT_WIN = 64
SLOT_ALIGN = 16
VMEM_LIMIT = 56 * 1024 * 1024


def _cparams(sem):
    return pltpu.CompilerParams(dimension_semantics=sem, vmem_limit_bytes=VMEM_LIMIT)


def _dot(a, b):
    return jnp.dot(a, b, preferred_element_type=F32)


def _dot_nt(a, b):
    return lax.dot_general(a, b, (((1,), (1,)), ((), ())), preferred_element_type=F32)


def _dot_tn(a, b):
    return lax.dot_general(a, b, (((0,), (0,)), ((), ())), preferred_element_type=F32)


def _silu(x):
    return x * jax.nn.sigmoid(x)


def _mod_body(c_ref, w_ref, b_ref, o_ref):
    s = _silu(c_ref[...])
    o_ref[...] = _dot(s.astype(BF16), w_ref[...].astype(BF16)) + b_ref[...]


def _mod_call(cs, w_mod, b_mod):
    n = w_mod.shape[1] // D_MODEL
    return pl.pallas_call(
        _mod_body,
        out_shape=jax.ShapeDtypeStruct((8, w_mod.shape[1]), F32),
        grid=(n,),
        in_specs=[pl.BlockSpec((8, D_MODEL), lambda j: (0, 0)),
                  pl.BlockSpec((D_MODEL, D_MODEL), lambda j: (0, j)),
                  pl.BlockSpec((1, D_MODEL), lambda j: (0, j))],
        out_specs=pl.BlockSpec((8, D_MODEL), lambda j: (0, j)),
        compiler_params=_cparams(("arbitrary",)),
        name="mod",
    )(cs, w_mod, b_mod)


def _norm_mod(x, g, shift, scale):
    ms = jnp.mean(x * x, axis=-1, keepdims=True)
    return (x * lax.rsqrt(ms + NORM_EPS) * g) * (1.0 + scale) + shift


def _gate_streams(st, gp_ref):
    lane = lax.broadcasted_iota(I32, st.shape, 1)
    g = -jnp.exp(gp_ref[0:1, :]) * jax.nn.softplus(st + gp_ref[1:2, :])
    beta = jax.nn.sigmoid(st)
    return jnp.where(lane < 2 * N_HEADS, g, jnp.where(lane < STATE_COLS, beta, 0.0))


def _conv_qkv(qkv, prev_row, next_row, cw_ref, out_ref, tm):
    cs = GDN_CHUNK
    nsub = tm // cs
    w0, w1, w2 = cw_ref[0:1, :], cw_ref[1:2, :], cw_ref[2:3, :]
    row = lax.broadcasted_iota(I32, (cs, 1), 0)
    for c in range(nsub):
        rows = slice(c * cs, (c + 1) * cs)
        x = qkv[rows]
        prow = prev_row if c == 0 else qkv[c * cs - 1:c * cs]
        nrow = next_row if c == nsub - 1 else qkv[(c + 1) * cs:(c + 1) * cs + 1]
        xp = jnp.where(row == 0, prow, pltpu.roll(x, 1, 0))
        xn = jnp.where(row == cs - 1, nrow, pltpu.roll(x, cs - 1, 0))
        y = _silu(xp * w0 + x * w1 + xn * w2)
        for h in range(N_HEADS):
            cq = slice(h * HEAD_DIM, (h + 1) * HEAD_DIM)
            ck = slice(B_W + h * HEAD_DIM, B_W + (h + 1) * HEAD_DIM)
            q = y[:, cq]
            k = y[:, ck]
            out_ref[0, rows, cq] = (q * (lax.rsqrt(jnp.sum(q * q, axis=-1, keepdims=True) + NORM_EPS)
                                         * (HEAD_DIM ** -0.5))).astype(BF16)
            out_ref[0, rows, ck] = (k * lax.rsqrt(jnp.sum(k * k, axis=-1, keepdims=True) + NORM_EPS)).astype(BF16)
        out_ref[0, rows, 2 * B_W:3 * B_W] = y[:, 2 * B_W:3 * B_W].astype(BF16)


def _inproj_lat_body(x_ref, xp_ref, xn_ref, mod_ref, g1_ref, w_ref, gp_ref, cw_ref, gmg_ref, ws_ref, bst_ref,
                     qkv_ref, gb_ref, z_ref, ya_ref, *, tm):
    i = pl.program_id(1)
    shift, scale = mod_ref[0, 0:1, :], mod_ref[0, 1:2, :]
    a = _norm_mod(x_ref[0], g1_ref[...], shift, scale).astype(BF16)
    xh = jnp.concatenate([xp_ref[0], xn_ref[0]], axis=0)
    halo = _dot(_norm_mod(xh, g1_ref[...], shift, scale).astype(BF16), w_ref[:, 0:QKV_W])
    prev_row = jnp.where(i == 0, 0.0, halo[7:8, :])
    next_row = jnp.where(i == pl.num_programs(1) - 1, 0.0, halo[8:9, :])
    _conv_qkv(_dot(a, w_ref[:, 0:QKV_W]), prev_row, next_row, cw_ref, qkv_ref, tm)
    z_ref[0] = _dot(a, w_ref[:, QKV_W:QKV_W + B_W]).astype(BF16)
    c_uv = QKV_W + B_W
    gb_ref[0] = _gate_streams(_dot(a, w_ref[:, c_uv + 2 * A_W:c_uv + 2 * A_W + LANES]), gp_ref)
    uv = _dot(a, w_ref[:, c_uv:c_uv + 2 * A_W])
    uv = 0.5 * uv * (1.0 + lax.erf(uv * 0.7071067811865476))
    gd = A_W // A_GROUPS
    for grp in range(A_GROUPS):
        v = uv[:, A_W + grp * gd:A_W + (grp + 1) * gd]
        vn = v * lax.rsqrt(jnp.mean(v * v, axis=-1, keepdims=True) + NORM_EPS) * gmg_ref[:, grp * gd:(grp + 1) * gd]
        vn = vn.astype(BF16)
        bias = bst_ref[:, grp:grp + 1]
        for c in range(tm // A_CHUNK):
            rows = slice(c * A_CHUNK, (c + 1) * A_CHUNK)
            s = _dot(ws_ref[grp], vn[rows]) + bias
            ya_ref[0, rows, grp * gd:(grp + 1) * gd] = (uv[rows, grp * gd:(grp + 1) * gd] * s).astype(BF16)


def _inproj_ctx_body(x_ref, mod_ref, g1_ref, w_ref, gp_ref, cw_ref, qkv_ref, gb_ref, *, tm):
    a = _norm_mod(x_ref[0], g1_ref[...], mod_ref[0, 0:1, :], mod_ref[0, 1:2, :]).astype(BF16)
    edge = jnp.zeros((1, QKV_W), F32)
    _conv_qkv(_dot(a, w_ref[:, 0:QKV_W]), edge, edge, cw_ref, qkv_ref, tm)
    c_state = QKV_W + B_W + 2 * A_W
    gb_ref[0] = _gate_streams(_dot(a, w_ref[:, c_state:c_state + LANES]), gp_ref)


def _inproj_lat_call(x, mod3, g1, w_lat, gp, cw, gmg, ws16, bst, tm):
    bsz, t, _ = x.shape
    hb = tm // 8
    last8 = t // 8 - 1
    full = lambda a: pl.BlockSpec(a.shape, lambda b, i: (0,) * a.ndim)
    tok = lambda w: pl.BlockSpec((1, tm, w), lambda b, i: (b, i, 0))
    return pl.pallas_call(
        functools.partial(_inproj_lat_body, tm=tm),
        out_shape=(jax.ShapeDtypeStruct((bsz, t, QKV_W), BF16),
                   jax.ShapeDtypeStruct((bsz, t, LANES), F32),
                   jax.ShapeDtypeStruct((bsz, t, B_W), BF16),
                   jax.ShapeDtypeStruct((bsz, t, A_W), BF16)),
        grid=(bsz, t // tm),
        in_specs=[tok(D_MODEL),
                  pl.BlockSpec((1, 8, D_MODEL), lambda b, i: (b, jnp.maximum(i * hb - 1, 0), 0)),
                  pl.BlockSpec((1, 8, D_MODEL), lambda b, i: (b, jnp.minimum((i + 1) * hb, last8), 0)),
                  pl.BlockSpec((1, N_MOD, D_MODEL), lambda b, i: (b, 0, 0)),
                  full(g1), full(w_lat), full(gp), full(cw), full(gmg), full(ws16), full(bst)],
        out_specs=(tok(QKV_W), tok(LANES), tok(B_W), tok(A_W)),
        compiler_params=_cparams(("parallel", "arbitrary")),
        name="inproj_lat",
    )(x, x, x, mod3, g1, w_lat, gp, cw, gmg, ws16, bst)


def _inproj_ctx_call(ctx, mod3, ctx_row, g1, w_ctx, gp, cw):
    bsz, t, _ = ctx.shape
    full = lambda a: pl.BlockSpec(a.shape, lambda b: (0,) * a.ndim)
    tok = lambda w: pl.BlockSpec((1, t, w), lambda b: (b, 0, 0))
    return pl.pallas_call(
        functools.partial(_inproj_ctx_body, tm=t),
        out_shape=(jax.ShapeDtypeStruct((bsz, t, QKV_W), BF16),
                   jax.ShapeDtypeStruct((bsz, t, LANES), F32)),
        grid=(bsz,),
        in_specs=[tok(D_MODEL),
                  pl.BlockSpec((1, N_MOD, D_MODEL), lambda b: (ctx_row, 0, 0)),
                  full(g1), full(w_ctx), full(gp), full(cw)],
        out_specs=(tok(QKV_W), tok(LANES)),
        compiler_params=_cparams(("parallel",)),
        name="inproj_ctx",
    )(ctx, mod3, g1, w_ctx, gp, cw)


GDN_CHUNKS_PER_STEP = 8
GDN_STAGE_SKEW = 2
PACK_W = N_HEADS * GDN_CHUNK


def _per_head(tile, lanes, width):
    rows = tile.shape[0]
    if width == HEAD_DIM:
        return jnp.concatenate([jnp.broadcast_to(tile[:, l:l + 1], (rows, width)) for l in lanes], axis=1)
    head = lax.broadcasted_iota(I32, (rows, N_HEADS * width), 1) // width
    out = jnp.broadcast_to(tile[:, lanes[0]:lanes[0] + 1], (rows, N_HEADS * width))
    for h in range(1, N_HEADS):
        out = jnp.where(head == h, jnp.broadcast_to(tile[:, lanes[h]:lanes[h] + 1], (rows, N_HEADS * width)), out)
    return out


def _block_rows(x16, half_masks=None):
    rows, width = x16.shape
    per_head = width // N_HEADS
    zero = jnp.zeros((rows, LANES), x16.dtype)
    blocks = []
    for h in range(N_HEADS):
        tile = h * per_head // LANES
        kept = x16[:, tile * LANES:(tile + 1) * LANES]
        if per_head < LANES:
            kept = kept * half_masks[h * per_head % LANES // per_head]
        blocks.append(jnp.concatenate([kept if t == tile else zero for t in range(width // LANES)], axis=1))
    return jnp.concatenate(blocks, axis=0)


def _gdn_group_program(g, half_masks, turn):
    cs = GDN_CHUNK
    hd = HEAD_DIM
    q, k, v, beta, egc = g["q"], g["k"], g["v"], g["beta"], g["egc"]
    kb = k * beta
    decay = jnp.where(g["incl"], jnp.exp(jnp.where(g["incl"], g["gc_col"] - g["gc_row"], 0.0)), 0.0)
    kk = _dot_nt(jnp.concatenate([kb, q], axis=0).astype(BF16), _block_rows(k.astype(BF16)))
    yield
    a = jnp.where(g["strict"], kk[:cs] * decay, 0.0)
    attn = (kk[cs:] * decay).astype(BF16)
    m = -jnp.where(g["levels"][0], a, 0.0)
    for lm in g["levels"][1:]:
        m16 = m.astype(BF16)
        cm = jnp.where(lm, a, 0.0)
        x = cm + _dot(m16, _block_rows(cm.astype(BF16), half_masks))
        yield
        y = x + _dot(x.astype(BF16), _block_rows(m16, half_masks))
        yield
        m = m - y
    m16 = m.astype(BF16)
    vb = v * beta
    kbg = kb * egc
    u = vb + _dot(m16, _block_rows(vb.astype(BF16)))
    w = kbg + _dot(m16, _block_rows(kbg.astype(BF16)))
    qg = q * egc
    kg = (k * g["kdec"]).astype(BF16)
    yield
    while turn[g["backward"]] != g["order"]:
        yield
    s_ref = g["s_ref"]
    s = [s_ref[h] for h in range(N_HEADS)]
    wq = [_dot(jnp.concatenate([w[:, h * hd:(h + 1) * hd], qg[:, h * hd:(h + 1) * hd]], axis=0).astype(BF16),
               s[h].astype(BF16)) for h in range(N_HEADS)]
    yield
    v_new = (u - jnp.concatenate([r[:cs] for r in wq], axis=1)).astype(BF16)
    o = jnp.concatenate([r[cs:] for r in wq], axis=1) + _dot(attn, _block_rows(v_new))
    for h in range(N_HEADS):
        cols = slice(h * hd, (h + 1) * hd)
        s_ref[h] = s[h] * g["eg"][h] + _dot_tn(kg[:, cols], v_new[:, cols])
    g["o_ref"][0, pl.ds(g["r0"], cs), :] = o.astype(BF16)
    turn[g["backward"]] += 1


def _run_interleaved(programs, skew):
    live = {}
    tick = 0
    while live or tick <= skew * (len(programs) - 1):
        if tick % skew == 0 and tick // skew < len(programs):
            live[tick // skew] = programs[tick // skew]
        for key in sorted(live):
            for prog in live[key]:
                if next(prog, "done") == "done":
                    live[key] = [p for p in live[key] if p is not prog]
            if not live[key]:
                del live[key]
        tick += 1


def _gdn_body(qf_ref, qb_ref, gbf_ref, gbb_ref, s0f_ref, s0b_ref,
              of_ref, ob_ref, sff_ref, sfb_ref, sf_scr, sb_scr, *, tb, nt):
    t = pl.program_id(1)
    cs = GDN_CHUNK
    nch = tb // cs
    cps = min(GDN_CHUNKS_PER_STEP, nch)
    nh = N_HEADS

    @pl.when(t == 0)
    def _():
        sf_scr[...] = s0f_ref[0]
        sb_scr[...] = s0b_ref[0]

    ii = lax.broadcasted_iota(I32, (cs, PACK_W), 0)
    jj = lax.broadcasted_iota(I32, (cs, PACK_W), 1) & (cs - 1)
    incl_f, strict_f = jj <= ii, jj < ii
    incl_b, strict_b = jj >= ii, jj > ii
    levels = []
    sh = 0
    while (1 << sh) < cs:
        levels.append(((ii >> (sh + 1)) == (jj >> (sh + 1))) & ((ii >> sh) != (jj >> sh)))
        sh += 1
    lv_f = [lm & strict_f for lm in levels]
    lv_b = [lm & strict_b for lm in levels]
    row = lax.broadcasted_iota(I32, (cs, LANES), 0)
    lane = lax.broadcasted_iota(I32, (1, LANES), 1)
    half_of_tile = lax.broadcasted_iota(I32, (cs, LANES), 1) // cs
    half_masks = [(half_of_tile == i).astype(BF16) for i in range(LANES // cs)]

    def group(q_ref, r0, gate, csum, gt, backward, s_ref, o_ref, order):
        l0 = nh if backward else 0
        lanes = [l0 + h for h in range(nh)]
        last = 0 if backward else cs - 1
        ld = lambda c0: q_ref[0, pl.ds(r0, cs), c0:c0 + B_W].astype(F32)
        halves = [gt[l:l + 1, :] if (h % 2 == 0) != backward else pltpu.roll(gt[l:l + 1, :], cs, 1)
                  for h, l in enumerate(lanes)]
        gc_row = jnp.concatenate([jnp.where(lane < cs, halves[0], halves[1]),
                                  jnp.where(lane < cs, halves[2], halves[3])], axis=1)
        glast = csum[last:last + 1, :]
        return dict(
            q=ld(0), k=ld(B_W), v=ld(2 * B_W),
            beta=_per_head(gate, [2 * nh + l for l in lanes], HEAD_DIM),
            egc=_per_head(jnp.exp(csum), lanes, HEAD_DIM),
            kdec=_per_head(jnp.exp(glast - csum), lanes, HEAD_DIM),
            eg=[jnp.exp(csum[last:last + 1, l:l + 1]) for l in lanes],
            gc_col=_per_head(csum, lanes, cs), gc_row=gc_row,
            incl=incl_b if backward else incl_f, strict=strict_b if backward else strict_f,
            levels=lv_b if backward else lv_f, s_ref=s_ref, o_ref=o_ref, r0=r0, backward=backward, order=order)

    def step(n, carry):
        per_chunk = []
        for j in range(cps):
            rf = pl.multiple_of((n * cps + j) * cs, cs)
            rb = pl.multiple_of((nch - 1 - n * cps - j) * cs, cs)
            gf = gbf_ref[0, pl.ds(rf, cs), :]
            gb = gbb_ref[0, pl.ds(rb, cs), :]
            cf, cb = gf, gb
            s = 1
            while s < cs:
                cf = cf + jnp.where(row >= s, pltpu.roll(cf, s, 0), 0.0)
                cb = cb + jnp.where(row < cs - s, pltpu.roll(cb, cs - s, 0), 0.0)
                s *= 2
            gt = jnp.concatenate([cf, cb], axis=0).T
            per_chunk.append([group(qf_ref, rf, gf, cf, gt, False, sf_scr, of_ref, j),
                              group(qb_ref, rb, gb, cb, gt, True, sb_scr, ob_ref, j)])
        turn = {False: 0, True: 0}
        _run_interleaved([[_gdn_group_program(g, half_masks, turn) for g in pair] for pair in per_chunk],
                         GDN_STAGE_SKEW)
        return carry

    lax.fori_loop(0, nch // cps, step, 0)

    @pl.when(t == nt - 1)
    def _():
        sff_ref[0] = sf_scr[...]
        sfb_ref[0] = sb_scr[...]


def _gdn_call(qkv, gb, s0f, s0b, tb):
    bsz, t, _ = qkv.shape
    nt = t // tb
    assert (tb // GDN_CHUNK) % min(GDN_CHUNKS_PER_STEP, tb // GDN_CHUNK) == 0
    st = pl.BlockSpec((1, N_HEADS, HEAD_DIM, HEAD_DIM), lambda b, i: (b, 0, 0, 0))
    fwd = lambda w: pl.BlockSpec((1, tb, w), lambda b, i: (b, i, 0))
    bwd = lambda w: pl.BlockSpec((1, tb, w), lambda b, i: (b, nt - 1 - i, 0))
    return pl.pallas_call(
        functools.partial(_gdn_body, tb=tb, nt=nt),
        out_shape=(jax.ShapeDtypeStruct((bsz, t, B_W), BF16), jax.ShapeDtypeStruct((bsz, t, B_W), BF16),
                   jax.ShapeDtypeStruct((bsz, N_HEADS, HEAD_DIM, HEAD_DIM), F32),
                   jax.ShapeDtypeStruct((bsz, N_HEADS, HEAD_DIM, HEAD_DIM), F32)),
        grid=(bsz, nt),
        in_specs=[fwd(QKV_W), bwd(QKV_W), fwd(LANES), bwd(LANES), st, st],
        out_specs=(fwd(B_W), bwd(B_W), st, st),
        scratch_shapes=[pltpu.VMEM((N_HEADS, HEAD_DIM, HEAD_DIM), F32), pltpu.VMEM((N_HEADS, HEAD_DIM, HEAD_DIM), F32)],
        compiler_params=_cparams(("parallel", "arbitrary")),
        name="gdn",
    )(qkv, qkv, gb, gb, s0f, s0b)


def _mixout_body(x_ref, of_ref, ob_ref, z_ref, ya_ref, mod_ref, gng_ref, wout_ref, n2g_ref, wrh_ref, wrl_ref, br_ref,
                 h_ref, fin_ref, aff_ref, afft_ref, *, tm):
    o = of_ref[0].astype(F32) + ob_ref[0].astype(F32)
    z = z_ref[0].astype(F32)
    parts = [ya_ref[0]]
    for h in range(N_HEADS):
        c = slice(h * HEAD_DIM, (h + 1) * HEAD_DIM)
        oh = o[:, c]
        y = oh * lax.rsqrt(jnp.mean(oh * oh, axis=-1, keepdims=True) + NORM_EPS)
        parts.append((y * gng_ref[...] * _silu(z[:, c])).astype(BF16))
    mix = _dot(jnp.concatenate(parts, axis=1), wout_ref[...])
    hl = x_ref[0] + mod_ref[0, 2:3, :] * mix
    h_ref[0] = hl
    fin = _norm_mod(hl, n2g_ref[...], mod_ref[0, 3:4, :], mod_ref[0, 4:5, :])
    f_hi = fin.astype(BF16)
    fin_ref[0] = f_hi
    f_lo = (fin - f_hi.astype(F32)).astype(BF16)
    both = _dot(f_hi, wrl_ref[...])
    logits = both[:, :LANES] + _dot(f_lo, wrh_ref[...]) + both[:, LANES:] + br_ref[...]
    e = jnp.exp(logits - jnp.max(logits, axis=-1, keepdims=True))
    aff = e / jnp.sum(e, axis=-1, keepdims=True)
    aff_ref[0] = aff
    for j in range(tm // LANES):
        afft_ref[0, j] = aff[j * LANES:(j + 1) * LANES, :].T[0:N_EXPERTS, :]


def _mixout_call(x, o_f, o_b, z, ya, mod3, gng, wout16, n2g, wr_hi, wr_lo, br, tm):
    bsz, t, _ = x.shape
    full = lambda a: pl.BlockSpec(a.shape, lambda b, i: (0,) * a.ndim)
    tok = lambda w: pl.BlockSpec((1, tm, w), lambda b, i: (b, i, 0))
    return pl.pallas_call(
        functools.partial(_mixout_body, tm=tm),
        out_shape=(jax.ShapeDtypeStruct((bsz, t, D_MODEL), F32), jax.ShapeDtypeStruct((bsz, t, D_MODEL), BF16),
                   jax.ShapeDtypeStruct((bsz, t, LANES), F32),
                   jax.ShapeDtypeStruct((bsz, t // LANES, N_EXPERTS, LANES), F32)),
        grid=(bsz, t // tm),
        in_specs=[tok(D_MODEL), tok(B_W), tok(B_W), tok(B_W), tok(A_W),
                  pl.BlockSpec((1, N_MOD, D_MODEL), lambda b, i: (b, 0, 0)),
                  full(gng), full(wout16), full(n2g), full(wr_hi), full(wr_lo), full(br)],
        out_specs=(tok(D_MODEL), tok(D_MODEL), tok(LANES),
                   pl.BlockSpec((1, tm // LANES, N_EXPERTS, LANES), lambda b, i: (b, i, 0, 0))),
        compiler_params=_cparams(("parallel", "parallel")),
        name="mixout",
    )(x, o_f, o_b, z, ya, mod3, gng, wout16, n2g, wr_hi, wr_lo, br)


def _route_body(afft_ref, slott_ref, slot_ref, off_ref, *, t, cap):
    ne = N_EXPERTS
    npieces = t // LANES
    rows = npieces * ne

    def count(thr_col, strict):
        acc = jnp.zeros((ne, LANES), I32)
        for p in range(npieces):
            piece = afft_ref[0, p * ne:(p + 1) * ne, :]
            acc = acc + (piece > thr_col if strict else piece >= thr_col).astype(I32)
        return jnp.sum(acc, axis=1, keepdims=True)

    def search(i, thr):
        cand = thr | jnp.left_shift(jnp.int32(1), 30 - i)
        return jnp.where(count(pltpu.bitcast(cand, F32), False) >= cap, cand, thr)

    thr_bits = lax.fori_loop(0, 31, search, jnp.zeros((ne, 1), I32))
    thr = pltpu.bitcast(thr_bits, F32)
    need = (cap - count(thr, True)).astype(F32)

    x = afft_ref[0]
    thr_rows = jnp.concatenate([thr] * npieces, axis=0)
    need_rows = jnp.concatenate([need] * npieces, axis=0)
    gt = x > thr_rows
    eq = x == thr_rows
    ti = lax.broadcasted_iota(I32, (LANES, LANES), 0)
    tj = lax.broadcasted_iota(I32, (LANES, LANES), 1)
    triu = (ti <= tj).astype(BF16)
    ri = lax.broadcasted_iota(I32, (rows, rows), 0)
    rj = lax.broadcasted_iota(I32, (rows, rows), 1)
    earlier = (((ri & (ne - 1)) == (rj & (ne - 1))) & (rj < ri)).astype(BF16)

    def prefix(mask):
        inpiece = _dot(mask.astype(BF16), triu)
        total = jnp.broadcast_to(inpiece[:, LANES - 1:LANES], (rows, LANES)).astype(BF16)
        offset = _dot(earlier, total)
        return inpiece + offset, offset

    eq_rank, _ = prefix(eq)
    sel = gt | (eq & (eq_rank <= need_rows))
    sel_rank, sel_off = prefix(sel)
    slot = jnp.where(sel, sel_rank - 1.0, -1.0)
    slott_ref[0] = slot.astype(I32)
    off_ref[0] = sel_off.astype(I32)
    pad = jnp.zeros((LANES - ne, LANES), F32)
    for p in range(npieces):
        piece = jnp.concatenate([slot[p * ne:(p + 1) * ne, :], pad], axis=0)
        slot_ref[0, p * LANES:(p + 1) * LANES, :] = piece.T.astype(I32)


def _route_call(afft, cap):
    bsz, rows, _ = afft.shape
    t = rows // N_EXPERTS * LANES
    spec = lambda r: pl.BlockSpec((1, r, LANES), lambda b: (b, 0, 0))
    return pl.pallas_call(
        functools.partial(_route_body, t=t, cap=cap),
        out_shape=(jax.ShapeDtypeStruct((bsz, rows, LANES), I32),
                   jax.ShapeDtypeStruct((bsz, t, LANES), I32),
                   jax.ShapeDtypeStruct((bsz, rows, LANES), I32)),
        grid=(bsz,),
        in_specs=[spec(rows)],
        out_specs=(spec(rows), spec(t), spec(rows)),
        compiler_params=_cparams(("parallel",)),
        name="route",
    )(afft)


def _window_plan(base_ref, flat0, experts):
    starts, rounds = [], jnp.int32(0)
    for e in experts:
        lo = base_ref[flat0 + e]
        hi = base_ref[flat0 + N_EXPERTS + e]
        lo_al = (lo >> 4) << 4
        starts.append(lo_al)
        rounds = jnp.maximum(rounds, (hi - lo_al + SLOT_WIN - 1) // SLOT_WIN)
    return starts, rounds


def _window_start(start, r, cap):
    return pl.multiple_of(jnp.minimum(start + r * SLOT_WIN, cap), SLOT_ALIGN)


def _dispatch_body(base_ref, slott_ref, fin_ref, xe_ref, *, nchunk, sub, eh_n, cap):
    b, eh, ci = pl.program_id(0), pl.program_id(1), pl.program_id(2)
    rc = ROUTE_CHUNK

    @pl.when(ci == 0)
    def _():
        xe_ref[...] = jnp.zeros_like(xe_ref)

    srow = lax.broadcasted_iota(I32, (SLOT_WIN, rc), 0)
    for sc in range(sub):
        cc = ci * sub + sc
        flat0 = (b * (nchunk + 1) + cc) * N_EXPERTS + eh * eh_n
        f = fin_ref[0, sc * rc:(sc + 1) * rc, :]
        experts = list(range(eh_n))
        starts, rounds = _window_plan(base_ref, flat0, experts)

        def one_round(r, carry, starts=starts, f=f, sc=sc):
            rows = []
            wstart = [_window_start(starts[e], r, cap) for e in experts]
            for e in experts:
                tok_slot = jnp.concatenate(
                    [slott_ref[0, sc * (rc // LANES) + j, e:e + 1, :] for j in range(rc // LANES)], axis=1)
                rows.append((tok_slot == srow + wstart[e]).astype(BF16))
            prod = _dot(jnp.concatenate(rows, axis=0), f)
            for e in experts:
                win = pl.ds(wstart[e], SLOT_WIN)
                xe_ref[0, e, win, :] = xe_ref[0, e, win, :] + prod[e * SLOT_WIN:(e + 1) * SLOT_WIN].astype(BF16)
            return carry

        one_round(jnp.int32(0), 0)
        lax.fori_loop(1, rounds, one_round, 0)


def _dispatch_call(base_flat, slott, fin, cap):
    bsz, t, _ = fin.shape
    nchunk = t // ROUTE_CHUNK
    sub = 4
    eh_n = N_EXPERTS // 2
    sp = cap + SLOT_WIN
    grid_spec = pltpu.PrefetchScalarGridSpec(
        num_scalar_prefetch=1,
        grid=(bsz, N_EXPERTS // eh_n, nchunk // sub),
        in_specs=[pl.BlockSpec((1, sub * ROUTE_CHUNK // LANES, eh_n, LANES), lambda b, eh, ci, base: (b, ci, eh, 0)),
                  pl.BlockSpec((1, sub * ROUTE_CHUNK, D_MODEL), lambda b, eh, ci, base: (b, ci, 0))],
        out_specs=pl.BlockSpec((1, eh_n, sp, D_MODEL), lambda b, eh, ci, base: (b, eh, 0, 0)))
    return pl.pallas_call(
        functools.partial(_dispatch_body, nchunk=nchunk, sub=sub, eh_n=eh_n, cap=cap),
        out_shape=jax.ShapeDtypeStruct((bsz, N_EXPERTS, sp, D_MODEL), BF16),
        grid_spec=grid_spec,
        compiler_params=_cparams(("parallel", "parallel", "arbitrary")),
        name="dispatch",
    )(base_flat, slott, fin)


def _experts_body(xe_ref, wg_ref, wu_ref, wd_ref, y_ref, *, cap):
    x = xe_ref[0, 0, 0:cap, :]
    ft = 256
    acc = None
    for f in range(EXPERT_FF // ft):
        cols = slice(f * ft, (f + 1) * ft)
        wg16 = wg_ref[0, :, cols].astype(BF16)
        wu16 = wu_ref[0, :, cols].astype(BF16)
        wd16 = wd_ref[0, cols, :].astype(BF16)
        hid = (_silu(_dot(x, wg16)) * _dot(x, wu16)).astype(BF16)
        part = _dot(hid, wd16)
        acc = part if acc is None else acc + part
    y_ref[0, 0, 0:cap, :] = acc.astype(BF16)
    y_ref[0, 0, cap:, :] = jnp.zeros((y_ref.shape[2] - cap, D_MODEL), BF16)


def _experts_call(xe, w_gate, w_up, w_down, cap):
    bsz, _, sp, _ = xe.shape
    wspec = lambda shape: pl.BlockSpec((1,) + shape, lambda e, b: (e, 0, 0))
    slots = pl.BlockSpec((1, 1, sp, D_MODEL), lambda e, b: (b, e, 0, 0))
    return pl.pallas_call(
        functools.partial(_experts_body, cap=cap),
        out_shape=jax.ShapeDtypeStruct(xe.shape, BF16),
        grid=(N_EXPERTS, bsz),
        in_specs=[slots, wspec((D_MODEL, EXPERT_FF)), wspec((D_MODEL, EXPERT_FF)), wspec((EXPERT_FF, D_MODEL))],
        out_specs=slots,
        compiler_params=_cparams(("parallel", "parallel")),
        name="experts",
    )(xe, w_gate, w_up, w_down)


SLOT_SPLIT = 32


def _combine_selectors():
    k = jnp.arange(LANES)[:, None]
    e_of_lane = jnp.arange(N_EXPERTS * SLOT_WIN)[None, :] // SLOT_WIN
    sel_gate = (k == e_of_lane).astype(BF16)
    sel_slot = (SLOT_SPLIT * (k == e_of_lane) + (k - N_EXPERTS == e_of_lane)).astype(BF16)
    return sel_slot, sel_gate


def _combine_body(base_ref, slot_ref, aff_ref, h_ref, y_ref, mod_ref, fng_ref, ssel_ref, gsel_ref, o_ref, acc_ref,
                  *, nchunk, sub, cap):
    b, ci = pl.program_id(0), pl.program_id(1)
    rc = ROUTE_CHUNK
    width = N_EXPERTS * SLOT_WIN
    lane = lax.broadcasted_iota(I32, (1, width), 1)
    lane_e = lane >> (SLOT_WIN.bit_length() - 1)
    lane_j = (lane & (SLOT_WIN - 1)).astype(F32)
    lane128 = lax.broadcasted_iota(I32, (rc, LANES), 1)
    experts = list(range(N_EXPERTS))
    for sc in range(sub):
        rows = slice(sc * rc, (sc + 1) * rc)
        flat0 = (b * (nchunk + 1) + ci * sub + sc) * N_EXPERTS
        s1 = slot_ref[0, rows, :] + 1
        halves = jnp.where(lane128 < N_EXPERTS, s1 >> (SLOT_SPLIT.bit_length() - 1),
                           pltpu.roll(s1 & (SLOT_SPLIT - 1), N_EXPERTS, 1))
        slot1 = _dot(halves.astype(F32).astype(BF16), ssel_ref[...])
        gates = _dot(aff_ref[0, rows, :].astype(BF16), gsel_ref[...])
        starts, rounds = _window_plan(base_ref, flat0, experts)

        def contribution(r, starts=starts, slot1=slot1, gates=gates):
            wstart = [_window_start(starts[e], r, cap) for e in experts]
            ywin = jnp.concatenate([y_ref[0, e, pl.ds(wstart[e], SLOT_WIN), :] for e in experts], axis=0)
            held = jnp.zeros((1, width), I32)
            for e in experts:
                held = jnp.where(lane_e == e, wstart[e] + 1, held)
            held = held.astype(F32) + lane_j
            s = jnp.where(slot1 == held, gates, 0.0).astype(BF16)
            return _dot(s, ywin)

        def extra_round(r, carry, contribution=contribution):
            acc_ref[...] += contribution(r)
            return carry

        acc_ref[...] = contribution(jnp.int32(0))
        lax.fori_loop(1, rounds, extra_round, 0)
        hl = h_ref[0, rows, :] + mod_ref[0, 5:6, :] * acc_ref[...]
        ms = jnp.mean(hl * hl, axis=-1, keepdims=True)
        o_ref[0, rows, :] = hl * lax.rsqrt(ms + NORM_EPS) * fng_ref[...]


def _combine_call(base_flat, slot, aff, h, y, mod3, fng):
    bsz, t, _ = h.shape
    nchunk = t // ROUTE_CHUNK
    sub = 2
    rc = ROUTE_CHUNK
    ssel, gsel = _combine_selectors()
    tok = lambda w: pl.BlockSpec((1, sub * rc, w), lambda b, i, base: (b, i, 0))
    full = lambda a: pl.BlockSpec(a.shape, lambda b, i, base: (0,) * a.ndim)
    grid_spec = pltpu.PrefetchScalarGridSpec(
        num_scalar_prefetch=1,
        grid=(bsz, nchunk // sub),
        in_specs=[tok(LANES), tok(LANES), tok(D_MODEL),
                  pl.BlockSpec((1,) + y.shape[1:], lambda b, i, base: (b, 0, 0, 0), pipeline_mode=pl.Buffered(1)),
                  pl.BlockSpec((1, N_MOD, D_MODEL), lambda b, i, base: (b, 0, 0)),
                  full(fng), full(ssel), full(gsel)],
        out_specs=tok(D_MODEL),
        scratch_shapes=[pltpu.VMEM((rc, D_MODEL), F32)])
    return pl.pallas_call(
        functools.partial(_combine_body, nchunk=nchunk, sub=sub, cap=y.shape[2] - SLOT_WIN),
        out_shape=jax.ShapeDtypeStruct(h.shape, F32),
        grid_spec=grid_spec,
        compiler_params=_cparams(("parallel", "arbitrary")),
        name="combine",
    )(base_flat, slot, aff, h, y, mod3, fng, ssel, gsel)


def _pad_lanes(a):
    return jnp.pad(a, ((0, 0), (0, LANES - a.shape[1])))


def kernel(x, c, ctx, c_ctx, w_mod, b_mod, norm1_g, norm2_g, w_in, conv_w, a_log, dt_bias, gdn_norm_g, gm_norm_g,
           gm_ws, gm_bs, w_out, w_router, b_router, w_gate, w_up, w_down, final_norm_g):
    bsz, t, _ = x.shape
    ctx_len = ctx.shape[1]
    assert w_mod.shape[0] == 1, "single-layer problem"
    assert t % 1024 == 0 and ctx_len % GDN_CHUNK == 0 and bsz < 8
    cap = EC_CAPACITY * t // N_EXPERTS

    cs = jnp.zeros((8, D_MODEL), F32).at[:bsz].set(c).at[bsz].set(c_ctx)
    mod3 = _mod_call(cs, w_mod[0], b_mod[0][None, :]).reshape(8, N_MOD, D_MODEL)

    wl = w_in[0]
    n_state = QKV_W + STATE_COLS
    w_state = _pad_lanes(wl[:, QKV_W:n_state])
    w_lat = jnp.concatenate([wl[:, :QKV_W], wl[:, n_state:n_state + B_W], wl[:, n_state + B_W:], w_state],
                            axis=1).astype(BF16)
    gp = jnp.zeros((8, LANES), F32).at[0, :2 * N_HEADS].set(a_log[0].reshape(-1)).at[1, :2 * N_HEADS].set(
        dt_bias[0].reshape(-1))
    g1 = norm1_g[0][None, :]
    cw = jnp.zeros((8, QKV_W), F32).at[:conv_w.shape[1]].set(conv_w[0])

    qkv_c, gb_c = _inproj_ctx_call(ctx, mod3, bsz, g1, w_lat, gp, cw)
    zero_state = jnp.zeros((bsz, N_HEADS, HEAD_DIM, HEAD_DIM), F32)
    _, _, s_f, s_b = _gdn_call(qkv_c, gb_c, zero_state, zero_state, ctx_len)

    qkv, gb, z, ya = _inproj_lat_call(x, mod3, g1, w_lat, gp, cw, gm_norm_g[0][None, :], gm_ws[0].astype(BF16),
                                      _pad_lanes(gm_bs[0].T), 1024)
    o_f, o_b, _, _ = _gdn_call(qkv, gb, s_f, s_b, 1024)
    wr = _pad_lanes(w_router[0])
    wr_hi = wr.astype(BF16)
    wr_lo = jnp.concatenate([wr_hi, (wr - wr_hi.astype(F32)).astype(BF16)], axis=1)
    br = jnp.full((1, LANES), -1e30, F32).at[0, :N_EXPERTS].set(b_router[0])
    h, fin, aff, afft = _mixout_call(x, o_f, o_b, z, ya, mod3, gdn_norm_g[0][None, :], w_out[0].astype(BF16),
                                     norm2_g[0][None, :], wr_hi, wr_lo, br, 1024)

    npieces = t // LANES
    slott, slot, off = _route_call(afft.reshape(bsz, npieces * N_EXPERTS, LANES), cap)
    base = off[:, :, 0].reshape(bsz, npieces, N_EXPERTS)[:, ::ROUTE_CHUNK // LANES, :]
    base_flat = jnp.concatenate([base, jnp.full((bsz, 1, N_EXPERTS), cap, I32)], axis=1).reshape(-1)
    xe = _dispatch_call(base_flat, slott.reshape(bsz, npieces, N_EXPERTS, LANES), fin, cap)
    y = _experts_call(xe, w_gate[0], w_up[0], w_down[0], cap)
    return _combine_call(base_flat, slot, aff, h, y, mod3, final_norm_g[None, :])
```

```python
import functools

import jax
import jax.numpy as jnp
from jax import lax
from jax.experimental import pallas as pl
from jax.experimental.pallas import tpu as pltpu

F32 = jnp.float32
BF16 = jnp.bfloat16
I32 = jnp.int32

D_MODEL = 1024
N_MOD = 6
N_HEADS = 4
HEAD_DIM = 128
B_W = N_HEADS * HEAD_DIM
QKV_W = 3 * B_W
A_W = 512
A_GROUPS = 4
A_CHUNK = 128
GDN_CHUNK = 64
N_EXPERTS = 16
EC_CAPACITY = 2
EXPERT_FF = 1024
NORM_EPS = 1e-6
LANES = 128
STATE_COLS = 4 * N_HEADS

ROUTE_CHUNK = 256
SLOT_WIN = 64
SLOT_ALIGN = 16
VMEM_LIMIT = 60 * 1024 * 1024


def _cparams(sem):
    return pltpu.CompilerParams(dimension_semantics=sem, vmem_limit_bytes=VMEM_LIMIT)


def _dot(a, b):
    return jnp.dot(a, b, preferred_element_type=F32)


def _dot_nt(a, b):
    return lax.dot_general(a, b, (((1,), (1,)), ((), ())), preferred_element_type=F32)


def _dot_tn(a, b):
    return lax.dot_general(a, b, (((0,), (0,)), ((), ())), preferred_element_type=F32)


def _silu(x):
    return x * jax.nn.sigmoid(x)


def _mod_body(c_ref, w_ref, b_ref, o_ref):
    s = _silu(c_ref[...])
    o_ref[...] = _dot(s.astype(BF16), w_ref[...].astype(BF16)) + b_ref[...]


def _mod_call(cs, w_mod, b_mod):
    n = w_mod.shape[1] // D_MODEL
    return pl.pallas_call(
        _mod_body,
        out_shape=jax.ShapeDtypeStruct((8, w_mod.shape[1]), F32),
        grid=(n,),
        in_specs=[pl.BlockSpec((8, D_MODEL), lambda j: (0, 0)),
                  pl.BlockSpec((D_MODEL, D_MODEL), lambda j: (0, j)),
                  pl.BlockSpec((1, D_MODEL), lambda j: (0, j))],
        out_specs=pl.BlockSpec((8, D_MODEL), lambda j: (0, j)),
        compiler_params=_cparams(("arbitrary",)),
        name="mod",
    )(cs, w_mod, b_mod)


def _norm_mod(x, g, shift, scale):
    ms = jnp.mean(x * x, axis=-1, keepdims=True)
    return (x * lax.rsqrt(ms + NORM_EPS) * g) * (1.0 + scale) + shift


def _gate_streams(st, gp_ref):
    lane = lax.broadcasted_iota(I32, st.shape, 1)
    g = -jnp.exp(gp_ref[0:1, :]) * jax.nn.softplus(st + gp_ref[1:2, :])
    beta = jax.nn.sigmoid(st)
    return jnp.where(lane < 2 * N_HEADS, g, jnp.where(lane < STATE_COLS, beta, 0.0))


def _conv_qkv(qkv, prev_row, next_row, cw_ref, out_ref, tm):
    cs = GDN_CHUNK
    nsub = tm // cs
    w0, w1, w2 = cw_ref[0:1, :], cw_ref[1:2, :], cw_ref[2:3, :]
    row = lax.broadcasted_iota(I32, (cs, 1), 0)
    for c in range(nsub):
        rows = slice(c * cs, (c + 1) * cs)
        x = qkv[rows]
        prow = prev_row if c == 0 else qkv[c * cs - 1:c * cs]
        nrow = next_row if c == nsub - 1 else qkv[(c + 1) * cs:(c + 1) * cs + 1]
        xp = jnp.where(row == 0, prow, pltpu.roll(x, 1, 0))
        xn = jnp.where(row == cs - 1, nrow, pltpu.roll(x, cs - 1, 0))
        y = _silu(xp * w0 + x * w1 + xn * w2)
        for h in range(N_HEADS):
            cq = slice(h * HEAD_DIM, (h + 1) * HEAD_DIM)
            ck = slice(B_W + h * HEAD_DIM, B_W + (h + 1) * HEAD_DIM)
            q = y[:, cq]
            k = y[:, ck]
            out_ref[0, rows, cq] = (q * (lax.rsqrt(jnp.sum(q * q, axis=-1, keepdims=True) + NORM_EPS)
                                         * (HEAD_DIM ** -0.5))).astype(BF16)
            out_ref[0, rows, ck] = (k * lax.rsqrt(jnp.sum(k * k, axis=-1, keepdims=True) + NORM_EPS)).astype(BF16)
        out_ref[0, rows, 2 * B_W:3 * B_W] = y[:, 2 * B_W:3 * B_W].astype(BF16)


def _inproj_lat_body(x_ref, xp_ref, xn_ref, mod_ref, g1_ref, w_ref, gp_ref, cw_ref, gmg_ref, ws_ref, bst_ref,
                     qkv_ref, gb_ref, z_ref, ya_ref, *, tm):
    i = pl.program_id(1)
    shift, scale = mod_ref[0, 0:1, :], mod_ref[0, 1:2, :]
    a = _norm_mod(x_ref[0], g1_ref[...], shift, scale).astype(BF16)
    xh = jnp.concatenate([xp_ref[0], xn_ref[0]], axis=0)
    halo = _dot(_norm_mod(xh, g1_ref[...], shift, scale).astype(BF16), w_ref[:, 0:QKV_W])
    prev_row = jnp.where(i == 0, 0.0, halo[7:8, :])
    next_row = jnp.where(i == pl.num_programs(1) - 1, 0.0, halo[8:9, :])
    _conv_qkv(_dot(a, w_ref[:, 0:QKV_W]), prev_row, next_row, cw_ref, qkv_ref, tm)
    z_ref[0] = _dot(a, w_ref[:, QKV_W:QKV_W + B_W]).astype(BF16)
    c_uv = QKV_W + B_W
    gb_ref[0] = _gate_streams(_dot(a, w_ref[:, c_uv + 2 * A_W:c_uv + 2 * A_W + LANES]), gp_ref)
    uv = _dot(a, w_ref[:, c_uv:c_uv + 2 * A_W])
    uv = 0.5 * uv * (1.0 + lax.erf(uv * 0.7071067811865476))
    gd = A_W // A_GROUPS
    for grp in range(A_GROUPS):
        v = uv[:, A_W + grp * gd:A_W + (grp + 1) * gd]
        vn = v * lax.rsqrt(jnp.mean(v * v, axis=-1, keepdims=True) + NORM_EPS) * gmg_ref[:, grp * gd:(grp + 1) * gd]
        vn = vn.astype(BF16)
        bias = bst_ref[:, grp:grp + 1]
        for c in range(tm // A_CHUNK):
            rows = slice(c * A_CHUNK, (c + 1) * A_CHUNK)
            s = _dot(ws_ref[grp], vn[rows]) + bias
            ya_ref[0, rows, grp * gd:(grp + 1) * gd] = (uv[rows, grp * gd:(grp + 1) * gd] * s).astype(BF16)


def _inproj_ctx_body(x_ref, mod_ref, g1_ref, w_ref, gp_ref, cw_ref, qkv_ref, gb_ref, *, tm):
    a = _norm_mod(x_ref[0], g1_ref[...], mod_ref[0, 0:1, :], mod_ref[0, 1:2, :]).astype(BF16)
    edge = jnp.zeros((1, QKV_W), F32)
    _conv_qkv(_dot(a, w_ref[:, 0:QKV_W]), edge, edge, cw_ref, qkv_ref, tm)
    c_state = QKV_W + B_W + 2 * A_W
    gb_ref[0] = _gate_streams(_dot(a, w_ref[:, c_state:c_state + LANES]), gp_ref)


def _inproj_lat_call(x, mod3, g1, w_lat, gp, cw, gmg, ws16, bst, tm):
    bsz, t, _ = x.shape
    hb = tm // 8
    last8 = t // 8 - 1
    full = lambda a: pl.BlockSpec(a.shape, lambda b, i: (0,) * a.ndim)
    tok = lambda w: pl.BlockSpec((1, tm, w), lambda b, i: (b, i, 0))
    return pl.pallas_call(
        functools.partial(_inproj_lat_body, tm=tm),
        out_shape=(jax.ShapeDtypeStruct((bsz, t, QKV_W), BF16),
                   jax.ShapeDtypeStruct((bsz, t, LANES), F32),
                   jax.ShapeDtypeStruct((bsz, t, B_W), BF16),
                   jax.ShapeDtypeStruct((bsz, t, A_W), BF16)),
        grid=(bsz, t // tm),
        in_specs=[tok(D_MODEL),
                  pl.BlockSpec((1, 8, D_MODEL), lambda b, i: (b, jnp.maximum(i * hb - 1, 0), 0)),
                  pl.BlockSpec((1, 8, D_MODEL), lambda b, i: (b, jnp.minimum((i + 1) * hb, last8), 0)),
                  pl.BlockSpec((1, N_MOD, D_MODEL), lambda b, i: (b, 0, 0)),
                  full(g1), full(w_lat), full(gp), full(cw), full(gmg), full(ws16), full(bst)],
        out_specs=(tok(QKV_W), tok(LANES), tok(B_W), tok(A_W)),
        compiler_params=_cparams(("parallel", "arbitrary")),
        name="inproj_lat",
    )(x, x, x, mod3, g1, w_lat, gp, cw, gmg, ws16, bst)


def _inproj_ctx_call(ctx, mod3, ctx_row, g1, w_ctx, gp, cw):
    bsz, t, _ = ctx.shape
    full = lambda a: pl.BlockSpec(a.shape, lambda b: (0,) * a.ndim)
    tok = lambda w: pl.BlockSpec((1, t, w), lambda b: (b, 0, 0))
    return pl.pallas_call(
        functools.partial(_inproj_ctx_body, tm=t),
        out_shape=(jax.ShapeDtypeStruct((bsz, t, QKV_W), BF16),
                   jax.ShapeDtypeStruct((bsz, t, LANES), F32)),
        grid=(bsz,),
        in_specs=[tok(D_MODEL),
                  pl.BlockSpec((1, N_MOD, D_MODEL), lambda b: (ctx_row, 0, 0)),
                  full(g1), full(w_ctx), full(gp), full(cw)],
        out_specs=(tok(QKV_W), tok(LANES)),
        compiler_params=_cparams(("parallel",)),
        name="inproj_ctx",
    )(ctx, mod3, g1, w_ctx, gp, cw)


GDN_CHUNKS_PER_STEP = 8
GDN_STAGE_SKEW = 2
PACK_W = N_HEADS * GDN_CHUNK


def _per_head(tile, lanes, width):
    rows = tile.shape[0]
    if width == HEAD_DIM:
        return jnp.concatenate([jnp.broadcast_to(tile[:, l:l + 1], (rows, width)) for l in lanes], axis=1)
    head = lax.broadcasted_iota(I32, (rows, N_HEADS * width), 1) // width
    out = jnp.broadcast_to(tile[:, lanes[0]:lanes[0] + 1], (rows, N_HEADS * width))
    for h in range(1, N_HEADS):
        out = jnp.where(head == h, jnp.broadcast_to(tile[:, lanes[h]:lanes[h] + 1], (rows, N_HEADS * width)), out)
    return out


def _block_rows(x16, half_masks=None):
    rows, width = x16.shape
    per_head = width // N_HEADS
    zero = jnp.zeros((rows, LANES), x16.dtype)
    blocks = []
    for h in range(N_HEADS):
        tile = h * per_head // LANES
        kept = x16[:, tile * LANES:(tile + 1) * LANES]
        if per_head < LANES:
            kept = kept * half_masks[h * per_head % LANES // per_head]
        blocks.append(jnp.concatenate([kept if t == tile else zero for t in range(width // LANES)], axis=1))
    return jnp.concatenate(blocks, axis=0)


def _gdn_group_program(g, half_masks, turn):
    cs = GDN_CHUNK
    hd = HEAD_DIM
    q, k, v, beta, egc = g["q"], g["k"], g["v"], g["beta"], g["egc"]
    kb = k * beta
    decay = jnp.where(g["incl"], jnp.exp(jnp.where(g["incl"], g["gc_col"] - g["gc_row"], 0.0)), 0.0)
    kk = _dot_nt(jnp.concatenate([kb, q], axis=0).astype(BF16), _block_rows(k.astype(BF16)))
    yield
    a = jnp.where(g["strict"], kk[:cs] * decay, 0.0)
    attn = (kk[cs:] * decay).astype(BF16)
    m = -jnp.where(g["levels"][0], a, 0.0)
    for lm in g["levels"][1:]:
        m16 = m.astype(BF16)
        cm = jnp.where(lm, a, 0.0)
        x = cm + _dot(m16, _block_rows(cm.astype(BF16), half_masks))
        yield
        y = x + _dot(x.astype(BF16), _block_rows(m16, half_masks))
        yield
        m = m - y
    m16 = m.astype(BF16)
    vb = v * beta
    kbg = kb * egc
    u = vb + _dot(m16, _block_rows(vb.astype(BF16)))
    w = kbg + _dot(m16, _block_rows(kbg.astype(BF16)))
    qg = q * egc
    kg = (k * g["kdec"]).astype(BF16)
    yield
    while turn[g["backward"]] != g["order"]:
        yield
    s_ref = g["s_ref"]
    s = [s_ref[h] for h in range(N_HEADS)]
    wq = [_dot(jnp.concatenate([w[:, h * hd:(h + 1) * hd], qg[:, h * hd:(h + 1) * hd]], axis=0).astype(BF16),
               s[h].astype(BF16)) for h in range(N_HEADS)]
    yield
    v_new = (u - jnp.concatenate([r[:cs] for r in wq], axis=1)).astype(BF16)
    o = jnp.concatenate([r[cs:] for r in wq], axis=1) + _dot(attn, _block_rows(v_new))
    for h in range(N_HEADS):
        cols = slice(h * hd, (h + 1) * hd)
        s_ref[h] = s[h] * g["eg"][h] + _dot_tn(kg[:, cols], v_new[:, cols])
    g["o_ref"][0, pl.ds(g["r0"], cs), :] = o.astype(BF16)
    turn[g["backward"]] += 1


def _run_interleaved(programs, skew):
    live = {}
    tick = 0
    while live or tick <= skew * (len(programs) - 1):
        if tick % skew == 0 and tick // skew < len(programs):
            live[tick // skew] = programs[tick // skew]
        for key in sorted(live):
            for prog in live[key]:
                if next(prog, "done") == "done":
                    live[key] = [p for p in live[key] if p is not prog]
            if not live[key]:
                del live[key]
        tick += 1


def _gdn_body(qf_ref, qb_ref, gbf_ref, gbb_ref, s0f_ref, s0b_ref,
              of_ref, ob_ref, sff_ref, sfb_ref, sf_scr, sb_scr, *, tb, nt):
    t = pl.program_id(1)
    cs = GDN_CHUNK
    nch = tb // cs
    cps = min(GDN_CHUNKS_PER_STEP, nch)
    nh = N_HEADS

    @pl.when(t == 0)
    def _():
        sf_scr[...] = s0f_ref[0]
        sb_scr[...] = s0b_ref[0]

    ii = lax.broadcasted_iota(I32, (cs, PACK_W), 0)
    jj = lax.broadcasted_iota(I32, (cs, PACK_W), 1) & (cs - 1)
    incl_f, strict_f = jj <= ii, jj < ii
    incl_b, strict_b = jj >= ii, jj > ii
    levels = []
    sh = 0
    while (1 << sh) < cs:
        levels.append(((ii >> (sh + 1)) == (jj >> (sh + 1))) & ((ii >> sh) != (jj >> sh)))
        sh += 1
    lv_f = [lm & strict_f for lm in levels]
    lv_b = [lm & strict_b for lm in levels]
    row = lax.broadcasted_iota(I32, (cs, LANES), 0)
    lane = lax.broadcasted_iota(I32, (1, LANES), 1)
    half_of_tile = lax.broadcasted_iota(I32, (cs, LANES), 1) // cs
    half_masks = [(half_of_tile == i).astype(BF16) for i in range(LANES // cs)]

    def group(q_ref, r0, gate, csum, gt, backward, s_ref, o_ref, order):
        l0 = nh if backward else 0
        lanes = [l0 + h for h in range(nh)]
        last = 0 if backward else cs - 1
        ld = lambda c0: q_ref[0, pl.ds(r0, cs), c0:c0 + B_W].astype(F32)
        halves = [gt[l:l + 1, :] if (h % 2 == 0) != backward else pltpu.roll(gt[l:l + 1, :], cs, 1)
                  for h, l in enumerate(lanes)]
        gc_row = jnp.concatenate([jnp.where(lane < cs, halves[0], halves[1]),
                                  jnp.where(lane < cs, halves[2], halves[3])], axis=1)
        glast = csum[last:last + 1, :]
        return dict(
            q=ld(0), k=ld(B_W), v=ld(2 * B_W),
            beta=_per_head(gate, [2 * nh + l for l in lanes], HEAD_DIM),
            egc=_per_head(jnp.exp(csum), lanes, HEAD_DIM),
            kdec=_per_head(jnp.exp(glast - csum), lanes, HEAD_DIM),
            eg=[jnp.exp(csum[last:last + 1, l:l + 1]) for l in lanes],
            gc_col=_per_head(csum, lanes, cs), gc_row=gc_row,
            incl=incl_b if backward else incl_f, strict=strict_b if backward else strict_f,
            levels=lv_b if backward else lv_f, s_ref=s_ref, o_ref=o_ref, r0=r0, backward=backward, order=order)

    def step(n, carry):
        per_chunk = []
        for j in range(cps):
            rf = pl.multiple_of((n * cps + j) * cs, cs)
            rb = pl.multiple_of((nch - 1 - n * cps - j) * cs, cs)
            gf = gbf_ref[0, pl.ds(rf, cs), :]
            gb = gbb_ref[0, pl.ds(rb, cs), :]
            cf, cb = gf, gb
            s = 1
            while s < cs:
                cf = cf + jnp.where(row >= s, pltpu.roll(cf, s, 0), 0.0)
                cb = cb + jnp.where(row < cs - s, pltpu.roll(cb, cs - s, 0), 0.0)
                s *= 2
            gt = jnp.concatenate([cf, cb], axis=0).T
            per_chunk.append([group(qf_ref, rf, gf, cf, gt, False, sf_scr, of_ref, j),
                              group(qb_ref, rb, gb, cb, gt, True, sb_scr, ob_ref, j)])
        turn = {False: 0, True: 0}
        _run_interleaved([[_gdn_group_program(g, half_masks, turn) for g in pair] for pair in per_chunk],
                         GDN_STAGE_SKEW)
        return carry

    lax.fori_loop(0, nch // cps, step, 0)

    @pl.when(t == nt - 1)
    def _():
        sff_ref[0] = sf_scr[...]
        sfb_ref[0] = sb_scr[...]


def _gdn_call(qkv, gb, s0f, s0b, tb):
    bsz, t, _ = qkv.shape
    nt = t // tb
    assert (tb // GDN_CHUNK) % min(GDN_CHUNKS_PER_STEP, tb // GDN_CHUNK) == 0
    st = pl.BlockSpec((1, N_HEADS, HEAD_DIM, HEAD_DIM), lambda b, i: (b, 0, 0, 0))
    fwd = lambda w: pl.BlockSpec((1, tb, w), lambda b, i: (b, i, 0))
    bwd = lambda w: pl.BlockSpec((1, tb, w), lambda b, i: (b, nt - 1 - i, 0))
    return pl.pallas_call(
        functools.partial(_gdn_body, tb=tb, nt=nt),
        out_shape=(jax.ShapeDtypeStruct((bsz, t, B_W), BF16), jax.ShapeDtypeStruct((bsz, t, B_W), BF16),
                   jax.ShapeDtypeStruct((bsz, N_HEADS, HEAD_DIM, HEAD_DIM), F32),
                   jax.ShapeDtypeStruct((bsz, N_HEADS, HEAD_DIM, HEAD_DIM), F32)),
        grid=(bsz, nt),
        in_specs=[fwd(QKV_W), bwd(QKV_W), fwd(LANES), bwd(LANES), st, st],
        out_specs=(fwd(B_W), bwd(B_W), st, st),
        scratch_shapes=[pltpu.VMEM((N_HEADS, HEAD_DIM, HEAD_DIM), F32), pltpu.VMEM((N_HEADS, HEAD_DIM, HEAD_DIM), F32)],
        compiler_params=_cparams(("parallel", "arbitrary")),
        name="gdn",
    )(qkv, qkv, gb, gb, s0f, s0b)


def _mixout_body(x_ref, of_ref, ob_ref, z_ref, ya_ref, mod_ref, gng_ref, wout_ref, n2g_ref, wrh_ref, wrl_ref, br_ref,
                 h_ref, fin_ref, aff_ref, afft_ref, *, tm):
    o = of_ref[0].astype(F32) + ob_ref[0].astype(F32)
    z = z_ref[0].astype(F32)
    parts = [ya_ref[0]]
    for h in range(N_HEADS):
        c = slice(h * HEAD_DIM, (h + 1) * HEAD_DIM)
        oh = o[:, c]
        y = oh * lax.rsqrt(jnp.mean(oh * oh, axis=-1, keepdims=True) + NORM_EPS)
        parts.append((y * gng_ref[...] * _silu(z[:, c])).astype(BF16))
    mix = _dot(jnp.concatenate(parts, axis=1), wout_ref[...])
    hl = x_ref[0] + mod_ref[0, 2:3, :] * mix
    h_ref[0] = hl
    fin = _norm_mod(hl, n2g_ref[...], mod_ref[0, 3:4, :], mod_ref[0, 4:5, :])
    f_hi = fin.astype(BF16)
    fin_ref[0] = f_hi
    f_lo = (fin - f_hi.astype(F32)).astype(BF16)
    both = _dot(f_hi, wrl_ref[...])
    logits = both[:, :LANES] + _dot(f_lo, wrh_ref[...]) + both[:, LANES:] + br_ref[...]
    e = jnp.exp(logits - jnp.max(logits, axis=-1, keepdims=True))
    aff = e / jnp.sum(e, axis=-1, keepdims=True)
    aff_ref[0] = aff
    for j in range(tm // LANES):
        afft_ref[0, j] = aff[j * LANES:(j + 1) * LANES, :].T[0:N_EXPERTS, :]


def _mixout_call(x, o_f, o_b, z, ya, mod3, gng, wout16, n2g, wr_hi, wr_lo, br, tm):
    bsz, t, _ = x.shape
    full = lambda a: pl.BlockSpec(a.shape, lambda b, i: (0,) * a.ndim)
    tok = lambda w: pl.BlockSpec((1, tm, w), lambda b, i: (b, i, 0))
    return pl.pallas_call(
        functools.partial(_mixout_body, tm=tm),
        out_shape=(jax.ShapeDtypeStruct((bsz, t, D_MODEL), F32), jax.ShapeDtypeStruct((bsz, t, D_MODEL), BF16),
                   jax.ShapeDtypeStruct((bsz, t, LANES), F32),
                   jax.ShapeDtypeStruct((bsz, t // LANES, N_EXPERTS, LANES), F32)),
        grid=(bsz, t // tm),
        in_specs=[tok(D_MODEL), tok(B_W), tok(B_W), tok(B_W), tok(A_W),
                  pl.BlockSpec((1, N_MOD, D_MODEL), lambda b, i: (b, 0, 0)),
                  full(gng), full(wout16), full(n2g), full(wr_hi), full(wr_lo), full(br)],
        out_specs=(tok(D_MODEL), tok(D_MODEL), tok(LANES),
                   pl.BlockSpec((1, tm // LANES, N_EXPERTS, LANES), lambda b, i: (b, i, 0, 0))),
        compiler_params=_cparams(("parallel", "parallel")),
        name="mixout",
    )(x, o_f, o_b, z, ya, mod3, gng, wout16, n2g, wr_hi, wr_lo, br)


def _route_body(afft_ref, slott_ref, slot_ref, off_ref, *, t, cap):
    ne = N_EXPERTS
    npieces = t // LANES
    rows = npieces * ne

    def count(thr_col, strict):
        acc = jnp.zeros((ne, LANES), I32)
        for p in range(npieces):
            piece = afft_ref[0, p * ne:(p + 1) * ne, :]
            acc = acc + (piece > thr_col if strict else piece >= thr_col).astype(I32)
        return jnp.sum(acc, axis=1, keepdims=True)

    def search(i, thr):
        cand = thr | jnp.left_shift(jnp.int32(1), 30 - i)
        return jnp.where(count(pltpu.bitcast(cand, F32), False) >= cap, cand, thr)

    thr_bits = lax.fori_loop(0, 31, search, jnp.zeros((ne, 1), I32))
    thr = pltpu.bitcast(thr_bits, F32)
    need = (cap - count(thr, True)).astype(F32)

    x = afft_ref[0]
    thr_rows = jnp.concatenate([thr] * npieces, axis=0)
    need_rows = jnp.concatenate([need] * npieces, axis=0)
    gt = x > thr_rows
    eq = x == thr_rows
    ti = lax.broadcasted_iota(I32, (LANES, LANES), 0)
    tj = lax.broadcasted_iota(I32, (LANES, LANES), 1)
    triu = (ti <= tj).astype(BF16)
    ri = lax.broadcasted_iota(I32, (rows, rows), 0)
    rj = lax.broadcasted_iota(I32, (rows, rows), 1)
    earlier = (((ri & (ne - 1)) == (rj & (ne - 1))) & (rj < ri)).astype(BF16)

    def prefix(mask):
        inpiece = _dot(mask.astype(BF16), triu)
        total = jnp.broadcast_to(inpiece[:, LANES - 1:LANES], (rows, LANES)).astype(BF16)
        offset = _dot(earlier, total)
        return inpiece + offset, offset

    eq_rank, _ = prefix(eq)
    sel = gt | (eq & (eq_rank <= need_rows))
    sel_rank, sel_off = prefix(sel)
    slot = jnp.where(sel, sel_rank - 1.0, -1.0)
    slott_ref[0] = slot.astype(I32)
    off_ref[0] = sel_off.astype(I32)
    pad = jnp.zeros((LANES - ne, LANES), F32)
    for p in range(npieces):
        piece = jnp.concatenate([slot[p * ne:(p + 1) * ne, :], pad], axis=0)
        slot_ref[0, p * LANES:(p + 1) * LANES, :] = piece.T.astype(I32)


def _route_call(afft, cap):
    bsz, rows, _ = afft.shape
    t = rows // N_EXPERTS * LANES
    spec = lambda r: pl.BlockSpec((1, r, LANES), lambda b: (b, 0, 0))
    return pl.pallas_call(
        functools.partial(_route_body, t=t, cap=cap),
        out_shape=(jax.ShapeDtypeStruct((bsz, rows, LANES), I32),
                   jax.ShapeDtypeStruct((bsz, t, LANES), I32),
                   jax.ShapeDtypeStruct((bsz, rows, LANES), I32)),
        grid=(bsz,),
        in_specs=[spec(rows)],
        out_specs=(spec(rows), spec(t), spec(rows)),
        compiler_params=_cparams(("parallel",)),
        name="route",
    )(afft)


def _window_plan(base_ref, flat0, experts):
    starts, rounds = [], jnp.int32(0)
    for e in experts:
        lo = base_ref[flat0 + e]
        hi = base_ref[flat0 + N_EXPERTS + e]
        lo_al = (lo >> 4) << 4
        starts.append(lo_al)
        rounds = jnp.maximum(rounds, (hi - lo_al + SLOT_WIN - 1) // SLOT_WIN)
    return starts, rounds


def _window_start(start, r, cap):
    return pl.multiple_of(jnp.minimum(start + r * SLOT_WIN, cap), SLOT_ALIGN)


def _dispatch_body(base_ref, slott_ref, fin_ref, xe_ref, *, nchunk, sub, eh_n, cap):
    b, eh, ci = pl.program_id(0), pl.program_id(1), pl.program_id(2)
    rc = ROUTE_CHUNK

    @pl.when(ci == 0)
    def _():
        xe_ref[...] = jnp.zeros_like(xe_ref)

    srow = lax.broadcasted_iota(I32, (SLOT_WIN, rc), 0)
    for sc in range(sub):
        cc = ci * sub + sc
        flat0 = (b * (nchunk + 1) + cc) * N_EXPERTS + eh * eh_n
        f = fin_ref[0, sc * rc:(sc + 1) * rc, :]
        experts = list(range(eh_n))
        starts, rounds = _window_plan(base_ref, flat0, experts)

        def one_round(r, carry, starts=starts, f=f, sc=sc):
            rows = []
            wstart = [_window_start(starts[e], r, cap) for e in experts]
            for e in experts:
                tok_slot = jnp.concatenate(
                    [slott_ref[0, sc * (rc // LANES) + j, e:e + 1, :] for j in range(rc // LANES)], axis=1)
                rows.append((tok_slot == srow + wstart[e]).astype(BF16))
            prod = _dot(jnp.concatenate(rows, axis=0), f)
            for e in experts:
                win = pl.ds(wstart[e], SLOT_WIN)
                xe_ref[0, e, win, :] = xe_ref[0, e, win, :] + prod[e * SLOT_WIN:(e + 1) * SLOT_WIN].astype(BF16)
            return carry

        one_round(jnp.int32(0), 0)
        lax.fori_loop(1, rounds, one_round, 0)


def _dispatch_call(base_flat, slott, fin, cap):
    bsz, t, _ = fin.shape
    nchunk = t // ROUTE_CHUNK
    sub = 8
    eh_n = N_EXPERTS // 2
    sp = cap + SLOT_WIN
    grid_spec = pltpu.PrefetchScalarGridSpec(
        num_scalar_prefetch=1,
        grid=(bsz, N_EXPERTS // eh_n, nchunk // sub),
        in_specs=[pl.BlockSpec((1, sub * ROUTE_CHUNK // LANES, eh_n, LANES), lambda b, eh, ci, base: (b, ci, eh, 0)),
                  pl.BlockSpec((1, sub * ROUTE_CHUNK, D_MODEL), lambda b, eh, ci, base: (b, ci, 0))],
        out_specs=pl.BlockSpec((1, eh_n, sp, D_MODEL), lambda b, eh, ci, base: (b, eh, 0, 0)))
    return pl.pallas_call(
        functools.partial(_dispatch_body, nchunk=nchunk, sub=sub, eh_n=eh_n, cap=cap),
        out_shape=jax.ShapeDtypeStruct((bsz, N_EXPERTS, sp, D_MODEL), BF16),
        grid_spec=grid_spec,
        compiler_params=_cparams(("parallel", "parallel", "arbitrary")),
        name="dispatch",
    )(base_flat, slott, fin)


def _experts_body(xe_ref, wg_ref, wu_ref, wd_ref, y_ref, *, cap):
    x = xe_ref[0, 0, 0:cap, :]
    ft = 256
    acc = None
    for f in range(EXPERT_FF // ft):
        cols = slice(f * ft, (f + 1) * ft)
        wg16 = wg_ref[0, :, cols].astype(BF16)
        wu16 = wu_ref[0, :, cols].astype(BF16)
        wd16 = wd_ref[0, cols, :].astype(BF16)
        hid = (_silu(_dot(x, wg16)) * _dot(x, wu16)).astype(BF16)
        part = _dot(hid, wd16)
        acc = part if acc is None else acc + part
    y_ref[0, 0, 0:cap, :] = acc.astype(BF16)
    y_ref[0, 0, cap:, :] = jnp.zeros((y_ref.shape[2] - cap, D_MODEL), BF16)


def _experts_call(xe, w_gate, w_up, w_down, cap):
    bsz, _, sp, _ = xe.shape
    wspec = lambda shape: pl.BlockSpec((1,) + shape, lambda e, b: (e, 0, 0))
    slots = pl.BlockSpec((1, 1, sp, D_MODEL), lambda e, b: (b, e, 0, 0))
    return pl.pallas_call(
        functools.partial(_experts_body, cap=cap),
        out_shape=jax.ShapeDtypeStruct(xe.shape, BF16),
        grid=(N_EXPERTS, bsz),
        in_specs=[slots, wspec((D_MODEL, EXPERT_FF)), wspec((D_MODEL, EXPERT_FF)), wspec((EXPERT_FF, D_MODEL))],
        out_specs=slots,
        compiler_params=_cparams(("parallel", "parallel")),
        name="experts",
    )(xe, w_gate, w_up, w_down)


SLOT_SPLIT = 32


def _combine_selectors():
    k = jnp.arange(LANES)[:, None]
    e_of_lane = jnp.arange(N_EXPERTS * SLOT_WIN)[None, :] // SLOT_WIN
    sel_gate = (k == e_of_lane).astype(BF16)
    sel_slot = (SLOT_SPLIT * (k == e_of_lane) + (k - N_EXPERTS == e_of_lane)).astype(BF16)
    return sel_slot, sel_gate


def _combine_body(base_ref, slot_ref, aff_ref, h_ref, y_ref, mod_ref, fng_ref, ssel_ref, gsel_ref, o_ref, acc_ref,
                  *, nchunk, sub, cap):
    b, ci = pl.program_id(0), pl.program_id(1)
    rc = ROUTE_CHUNK
    width = N_EXPERTS * SLOT_WIN
    lane = lax.broadcasted_iota(I32, (1, width), 1)
    lane_e = lane >> (SLOT_WIN.bit_length() - 1)
    lane_j = (lane & (SLOT_WIN - 1)).astype(F32)
    lane128 = lax.broadcasted_iota(I32, (rc, LANES), 1)
    experts = list(range(N_EXPERTS))
    for sc in range(sub):
        rows = slice(sc * rc, (sc + 1) * rc)
        flat0 = (b * (nchunk + 1) + ci * sub + sc) * N_EXPERTS
        s1 = slot_ref[0, rows, :] + 1
        halves = jnp.where(lane128 < N_EXPERTS, s1 >> (SLOT_SPLIT.bit_length() - 1),
                           pltpu.roll(s1 & (SLOT_SPLIT - 1), N_EXPERTS, 1))
        slot1 = _dot(halves.astype(F32).astype(BF16), ssel_ref[...])
        gates = _dot(aff_ref[0, rows, :].astype(BF16), gsel_ref[...])
        starts, rounds = _window_plan(base_ref, flat0, experts)

        def contribution(r, starts=starts, slot1=slot1, gates=gates):
            wstart = [_window_start(starts[e], r, cap) for e in experts]
            ywin = jnp.concatenate([y_ref[0, e, pl.ds(wstart[e], SLOT_WIN), :] for e in experts], axis=0)
            held = jnp.zeros((1, width), I32)
            for e in experts:
                held = jnp.where(lane_e == e, wstart[e] + 1, held)
            held = held.astype(F32) + lane_j
            s = jnp.where(slot1 == held, gates, 0.0).astype(BF16)
            return _dot(s, ywin)

        def extra_round(r, carry, contribution=contribution):
            acc_ref[...] += contribution(r)
            return carry

        acc_ref[...] = contribution(jnp.int32(0))
        lax.fori_loop(1, rounds, extra_round, 0)
        hl = h_ref[0, rows, :] + mod_ref[0, 5:6, :] * acc_ref[...]
        ms = jnp.mean(hl * hl, axis=-1, keepdims=True)
        o_ref[0, rows, :] = hl * lax.rsqrt(ms + NORM_EPS) * fng_ref[...]


def _combine_call(base_flat, slot, aff, h, y, mod3, fng):
    bsz, t, _ = h.shape
    nchunk = t // ROUTE_CHUNK
    sub = 4
    rc = ROUTE_CHUNK
    ssel, gsel = _combine_selectors()
    tok = lambda w: pl.BlockSpec((1, sub * rc, w), lambda b, i, base: (b, i, 0))
    full = lambda a: pl.BlockSpec(a.shape, lambda b, i, base: (0,) * a.ndim)
    grid_spec = pltpu.PrefetchScalarGridSpec(
        num_scalar_prefetch=1,
        grid=(bsz, nchunk // sub),
        in_specs=[tok(LANES), tok(LANES), tok(D_MODEL),
                  pl.BlockSpec((1,) + y.shape[1:], lambda b, i, base: (b, 0, 0, 0), pipeline_mode=pl.Buffered(1)),
                  pl.BlockSpec((1, N_MOD, D_MODEL), lambda b, i, base: (b, 0, 0)),
                  full(fng), full(ssel), full(gsel)],
        out_specs=tok(D_MODEL),
        scratch_shapes=[pltpu.VMEM((rc, D_MODEL), F32)])
    return pl.pallas_call(
        functools.partial(_combine_body, nchunk=nchunk, sub=sub, cap=y.shape[2] - SLOT_WIN),
        out_shape=jax.ShapeDtypeStruct(h.shape, F32),
        grid_spec=grid_spec,
        compiler_params=_cparams(("parallel", "arbitrary")),
        name="combine",
    )(base_flat, slot, aff, h, y, mod3, fng, ssel, gsel)


def _pad_lanes(a):
    return jnp.pad(a, ((0, 0), (0, LANES - a.shape[1])))


def kernel(x, c, ctx, c_ctx, w_mod, b_mod, norm1_g, norm2_g, w_in, conv_w, a_log, dt_bias, gdn_norm_g, gm_norm_g,
           gm_ws, gm_bs, w_out, w_router, b_router, w_gate, w_up, w_down, final_norm_g):
    bsz, t, _ = x.shape
    ctx_len = ctx.shape[1]
    assert w_mod.shape[0] == 1, "single-layer problem"
    assert t % 2048 == 0 and ctx_len % GDN_CHUNK == 0 and bsz < 8
    cap = EC_CAPACITY * t // N_EXPERTS

    cs = jnp.zeros((8, D_MODEL), F32).at[:bsz].set(c).at[bsz].set(c_ctx)
    mod3 = _mod_call(cs, w_mod[0], b_mod[0][None, :]).reshape(8, N_MOD, D_MODEL)

    wl = w_in[0]
    n_state = QKV_W + STATE_COLS
    w_state = _pad_lanes(wl[:, QKV_W:n_state])
    w_lat = jnp.concatenate([wl[:, :QKV_W], wl[:, n_state:n_state + B_W], wl[:, n_state + B_W:], w_state],
                            axis=1).astype(BF16)
    gp = jnp.zeros((8, LANES), F32).at[0, :2 * N_HEADS].set(a_log[0].reshape(-1)).at[1, :2 * N_HEADS].set(
        dt_bias[0].reshape(-1))
    g1 = norm1_g[0][None, :]
    cw = jnp.zeros((8, QKV_W), F32).at[:conv_w.shape[1]].set(conv_w[0])

    qkv_c, gb_c = _inproj_ctx_call(ctx, mod3, bsz, g1, w_lat, gp, cw)
    zero_state = jnp.zeros((bsz, N_HEADS, HEAD_DIM, HEAD_DIM), F32)
    _, _, s_f, s_b = _gdn_call(qkv_c, gb_c, zero_state, zero_state, ctx_len)

    qkv, gb, z, ya = _inproj_lat_call(x, mod3, g1, w_lat, gp, cw, gm_norm_g[0][None, :], gm_ws[0].astype(BF16),
                                      _pad_lanes(gm_bs[0].T), 1024)
    o_f, o_b, _, _ = _gdn_call(qkv, gb, s_f, s_b, 1024)
    wr = _pad_lanes(w_router[0])
    wr_hi = wr.astype(BF16)
    wr_lo = jnp.concatenate([wr_hi, (wr - wr_hi.astype(F32)).astype(BF16)], axis=1)
    br = jnp.full((1, LANES), -1e30, F32).at[0, :N_EXPERTS].set(b_router[0])
    h, fin, aff, afft = _mixout_call(x, o_f, o_b, z, ya, mod3, gdn_norm_g[0][None, :], w_out[0].astype(BF16),
                                     norm2_g[0][None, :], wr_hi, wr_lo, br, 1024)

    npieces = t // LANES
    slott, slot, off = _route_call(afft.reshape(bsz, npieces * N_EXPERTS, LANES), cap)
    base = off[:, :, 0].reshape(bsz, npieces, N_EXPERTS)[:, ::ROUTE_CHUNK // LANES, :]
    base_flat = jnp.concatenate([base, jnp.full((bsz, 1, N_EXPERTS), cap, I32)], axis=1).reshape(-1)
    xe = _dispatch_call(base_flat, slott.reshape(bsz, npieces, N_EXPERTS, LANES), fin, cap)
    y = _experts_call(xe, w_gate[0], w_up[0], w_down[0], cap)
    return _combine_call(base_flat, slot, aff, h, y, mod3, final_norm_g[None, :])
```

```python
import functools

import jax
import jax.numpy as jnp
from jax import lax
from jax.experimental import pallas as pl
from jax.experimental.pallas import tpu as pltpu

F32 = jnp.float32
BF16 = jnp.bfloat16
I32 = jnp.int32

D_MODEL = 1024
N_MOD = 6
N_HEADS = 4
HEAD_DIM = 128
B_W = N_HEADS * HEAD_DIM
QKV_W = 3 * B_W
A_W = 512
A_GROUPS = 4
A_CHUNK = 128
GDN_CHUNK = 64
N_EXPERTS = 16
EC_CAPACITY = 2
EXPERT_FF = 1024
NORM_EPS = 1e-6
LANES = 128
STATE_COLS = 4 * N_HEADS

ROUTE_CHUNK = 256
SLOT_WIN = 64
SLOT_ALIGN = 16
VMEM_LIMIT = 60 * 1024 * 1024


def _cparams(sem):
    return pltpu.CompilerParams(dimension_semantics=sem, vmem_limit_bytes=VMEM_LIMIT)


def _dot(a, b):
    return jnp.dot(a, b, preferred_element_type=F32)


def _dot_nt(a, b):
    return lax.dot_general(a, b, (((1,), (1,)), ((), ())), preferred_element_type=F32)


def _dot_tn(a, b):
    return lax.dot_general(a, b, (((0,), (0,)), ((), ())), preferred_element_type=F32)


def _silu(x):
    return x * jax.nn.sigmoid(x)


def _mod_body(c_ref, w_ref, b_ref, o_ref):
    s = _silu(c_ref[...])
    o_ref[...] = _dot(s.astype(BF16), w_ref[...].astype(BF16)) + b_ref[...]


def _mod_call(cs, w_mod, b_mod):
    n = w_mod.shape[1] // D_MODEL
    return pl.pallas_call(
        _mod_body,
        out_shape=jax.ShapeDtypeStruct((8, w_mod.shape[1]), F32),
        grid=(n,),
        in_specs=[pl.BlockSpec((8, D_MODEL), lambda j: (0, 0)),
                  pl.BlockSpec((D_MODEL, D_MODEL), lambda j: (0, j)),
                  pl.BlockSpec((1, D_MODEL), lambda j: (0, j))],
        out_specs=pl.BlockSpec((8, D_MODEL), lambda j: (0, j)),
        compiler_params=_cparams(("arbitrary",)),
        name="mod",
    )(cs, w_mod, b_mod)


def _norm_mod(x, g, shift, scale):
    ms = jnp.mean(x * x, axis=-1, keepdims=True)
    return (x * lax.rsqrt(ms + NORM_EPS) * g) * (1.0 + scale) + shift


def _gate_streams(st, gp_ref):
    lane = lax.broadcasted_iota(I32, st.shape, 1)
    g = -jnp.exp(gp_ref[0:1, :]) * jax.nn.softplus(st + gp_ref[1:2, :])
    beta = jax.nn.sigmoid(st)
    return jnp.where(lane < 2 * N_HEADS, g, jnp.where(lane < STATE_COLS, beta, 0.0))


def _conv_qkv(qkv, prev_row, next_row, cw_ref, out_ref, tm):
    cs = GDN_CHUNK
    nsub = tm // cs
    w0, w1, w2 = cw_ref[0:1, :], cw_ref[1:2, :], cw_ref[2:3, :]
    row = lax.broadcasted_iota(I32, (cs, 1), 0)
    for c in range(nsub):
        rows = slice(c * cs, (c + 1) * cs)
        x = qkv[rows]
        prow = prev_row if c == 0 else qkv[c * cs - 1:c * cs]
        nrow = next_row if c == nsub - 1 else qkv[(c + 1) * cs:(c + 1) * cs + 1]
        xp = jnp.where(row == 0, prow, pltpu.roll(x, 1, 0))
        xn = jnp.where(row == cs - 1, nrow, pltpu.roll(x, cs - 1, 0))
        y = _silu(xp * w0 + x * w1 + xn * w2)
        for h in range(N_HEADS):
            cq = slice(h * HEAD_DIM, (h + 1) * HEAD_DIM)
            ck = slice(B_W + h * HEAD_DIM, B_W + (h + 1) * HEAD_DIM)
            q = y[:, cq]
            k = y[:, ck]
            out_ref[0, rows, cq] = (q * (lax.rsqrt(jnp.sum(q * q, axis=-1, keepdims=True) + NORM_EPS)
                                         * (HEAD_DIM ** -0.5))).astype(BF16)
            out_ref[0, rows, ck] = (k * lax.rsqrt(jnp.sum(k * k, axis=-1, keepdims=True) + NORM_EPS)).astype(BF16)
        out_ref[0, rows, 2 * B_W:3 * B_W] = y[:, 2 * B_W:3 * B_W].astype(BF16)


def _inproj_lat_body(x_ref, xp_ref, xn_ref, mod_ref, g1_ref, w_ref, gp_ref, cw_ref, gmg_ref, ws_ref, bst_ref,
                     qkv_ref, gb_ref, z_ref, ya_ref, *, tm):
    i = pl.program_id(1)
    shift, scale = mod_ref[0, 0:1, :], mod_ref[0, 1:2, :]
    a = _norm_mod(x_ref[0], g1_ref[...], shift, scale).astype(BF16)
    xh = jnp.concatenate([xp_ref[0], xn_ref[0]], axis=0)
    halo = _dot(_norm_mod(xh, g1_ref[...], shift, scale).astype(BF16), w_ref[:, 0:QKV_W])
    prev_row = jnp.where(i == 0, 0.0, halo[7:8, :])
    next_row = jnp.where(i == pl.num_programs(1) - 1, 0.0, halo[8:9, :])
    _conv_qkv(_dot(a, w_ref[:, 0:QKV_W]), prev_row, next_row, cw_ref, qkv_ref, tm)
    z_ref[0] = _dot(a, w_ref[:, QKV_W:QKV_W + B_W]).astype(BF16)
    c_uv = QKV_W + B_W
    gb_ref[0] = _gate_streams(_dot(a, w_ref[:, c_uv + 2 * A_W:c_uv + 2 * A_W + LANES]), gp_ref)
    uv = _dot(a, w_ref[:, c_uv:c_uv + 2 * A_W])
    uv = 0.5 * uv * (1.0 + lax.erf(uv * 0.7071067811865476))
    gd = A_W // A_GROUPS
    for grp in range(A_GROUPS):
        v = uv[:, A_W + grp * gd:A_W + (grp + 1) * gd]
        vn = v * lax.rsqrt(jnp.mean(v * v, axis=-1, keepdims=True) + NORM_EPS) * gmg_ref[:, grp * gd:(grp + 1) * gd]
        vn = vn.astype(BF16)
        bias = bst_ref[:, grp:grp + 1]
        for c in range(tm // A_CHUNK):
            rows = slice(c * A_CHUNK, (c + 1) * A_CHUNK)
            s = _dot(ws_ref[grp], vn[rows]) + bias
            ya_ref[0, rows, grp * gd:(grp + 1) * gd] = (uv[rows, grp * gd:(grp + 1) * gd] * s).astype(BF16)


def _inproj_ctx_body(x_ref, mod_ref, g1_ref, w_ref, gp_ref, cw_ref, qkv_ref, gb_ref, *, tm):
    a = _norm_mod(x_ref[0], g1_ref[...], mod_ref[0, 0:1, :], mod_ref[0, 1:2, :]).astype(BF16)
    edge = jnp.zeros((1, QKV_W), F32)
    _conv_qkv(_dot(a, w_ref[:, 0:QKV_W]), edge, edge, cw_ref, qkv_ref, tm)
    c_state = QKV_W + B_W + 2 * A_W
    gb_ref[0] = _gate_streams(_dot(a, w_ref[:, c_state:c_state + LANES]), gp_ref)


def _inproj_lat_call(x, mod3, g1, w_lat, gp, cw, gmg, ws16, bst, tm):
    bsz, t, _ = x.shape
    hb = tm // 8
    last8 = t // 8 - 1
    full = lambda a: pl.BlockSpec(a.shape, lambda b, i: (0,) * a.ndim)
    tok = lambda w: pl.BlockSpec((1, tm, w), lambda b, i: (b, i, 0))
    return pl.pallas_call(
        functools.partial(_inproj_lat_body, tm=tm),
        out_shape=(jax.ShapeDtypeStruct((bsz, t, QKV_W), BF16),
                   jax.ShapeDtypeStruct((bsz, t, LANES), F32),
                   jax.ShapeDtypeStruct((bsz, t, B_W), BF16),
                   jax.ShapeDtypeStruct((bsz, t, A_W), BF16)),
        grid=(bsz, t // tm),
        in_specs=[tok(D_MODEL),
                  pl.BlockSpec((1, 8, D_MODEL), lambda b, i: (b, jnp.maximum(i * hb - 1, 0), 0)),
                  pl.BlockSpec((1, 8, D_MODEL), lambda b, i: (b, jnp.minimum((i + 1) * hb, last8), 0)),
                  pl.BlockSpec((1, N_MOD, D_MODEL), lambda b, i: (b, 0, 0)),
                  full(g1), full(w_lat), full(gp), full(cw), full(gmg), full(ws16), full(bst)],
        out_specs=(tok(QKV_W), tok(LANES), tok(B_W), tok(A_W)),
        compiler_params=_cparams(("parallel", "arbitrary")),
        name="inproj_lat",
    )(x, x, x, mod3, g1, w_lat, gp, cw, gmg, ws16, bst)


def _inproj_ctx_call(ctx, mod3, ctx_row, g1, w_ctx, gp, cw):
    bsz, t, _ = ctx.shape
    full = lambda a: pl.BlockSpec(a.shape, lambda b: (0,) * a.ndim)
    tok = lambda w: pl.BlockSpec((1, t, w), lambda b: (b, 0, 0))
    return pl.pallas_call(
        functools.partial(_inproj_ctx_body, tm=t),
        out_shape=(jax.ShapeDtypeStruct((bsz, t, QKV_W), BF16),
                   jax.ShapeDtypeStruct((bsz, t, LANES), F32)),
        grid=(bsz,),
        in_specs=[tok(D_MODEL),
                  pl.BlockSpec((1, N_MOD, D_MODEL), lambda b: (ctx_row, 0, 0)),
                  full(g1), full(w_ctx), full(gp), full(cw)],
        out_specs=(tok(QKV_W), tok(LANES)),
        compiler_params=_cparams(("parallel",)),
        name="inproj_ctx",
    )(ctx, mod3, g1, w_ctx, gp, cw)


GDN_CHUNKS_PER_STEP = 8
GDN_STAGE_SKEW = 2
PACK_W = N_HEADS * GDN_CHUNK


def _per_head(tile, lanes, width):
    rows = tile.shape[0]
    if width == HEAD_DIM:
        return jnp.concatenate([jnp.broadcast_to(tile[:, l:l + 1], (rows, width)) for l in lanes], axis=1)
    head = lax.broadcasted_iota(I32, (rows, N_HEADS * width), 1) // width
    out = jnp.broadcast_to(tile[:, lanes[0]:lanes[0] + 1], (rows, N_HEADS * width))
    for h in range(1, N_HEADS):
        out = jnp.where(head == h, jnp.broadcast_to(tile[:, lanes[h]:lanes[h] + 1], (rows, N_HEADS * width)), out)
    return out


def _block_rows(x16, half_masks=None):
    rows, width = x16.shape
    per_head = width // N_HEADS
    zero = jnp.zeros((rows, LANES), x16.dtype)
    blocks = []
    for h in range(N_HEADS):
        tile = h * per_head // LANES
        kept = x16[:, tile * LANES:(tile + 1) * LANES]
        if per_head < LANES:
            kept = kept * half_masks[h * per_head % LANES // per_head]
        blocks.append(jnp.concatenate([kept if t == tile else zero for t in range(width // LANES)], axis=1))
    return jnp.concatenate(blocks, axis=0)


def _gdn_group_program(g, half_masks, turn):
    cs = GDN_CHUNK
    hd = HEAD_DIM
    q16, k16, v16, beta16, egc16 = g["q"], g["k"], g["v"], g["beta"], g["egc"]
    kb16 = k16 * beta16
    decay = jnp.where(g["incl"], jnp.exp(jnp.where(g["incl"], g["gc_col"] - g["gc_row"], 0.0)), 0.0)
    kk = _dot_nt(jnp.concatenate([kb16, q16], axis=0), _block_rows(k16))
    yield
    a = jnp.where(g["strict"], kk[:cs] * decay, 0.0)
    attn = (kk[cs:] * decay).astype(BF16)
    m = -jnp.where(g["levels"][0], a, 0.0)
    for lm in g["levels"][1:]:
        m16 = m.astype(BF16)
        cm = jnp.where(lm, a, 0.0)
        x = cm + _dot(m16, _block_rows(cm.astype(BF16), half_masks))
        yield
        y = x + _dot(x.astype(BF16), _block_rows(m16, half_masks))
        yield
        m = m - y
    t16 = (m + g["eye"]).astype(BF16)
    u = _dot(t16, _block_rows(v16 * beta16))
    w = _dot(t16, _block_rows(kb16 * egc16))
    qg16 = q16 * egc16
    kg = k16 * g["kdec"]
    yield
    while turn[g["backward"]] != g["order"]:
        yield
    s_ref = g["s_ref"]
    s = [s_ref[h] for h in range(N_HEADS)]
    wq = [_dot(jnp.concatenate([w[:, h * hd:(h + 1) * hd].astype(BF16), qg16[:, h * hd:(h + 1) * hd]], axis=0),
               s[h].astype(BF16)) for h in range(N_HEADS)]
    yield
    v_new = (u - jnp.concatenate([r[:cs] for r in wq], axis=1)).astype(BF16)
    o = jnp.concatenate([r[cs:] for r in wq], axis=1) + _dot(attn, _block_rows(v_new))
    for h in range(N_HEADS):
        cols = slice(h * hd, (h + 1) * hd)
        s_ref[h] = s[h] * g["eg"][h] + _dot_tn(kg[:, cols], v_new[:, cols])
    g["o_ref"][0, pl.ds(g["r0"], cs), :] = o.astype(BF16)
    turn[g["backward"]] += 1


def _run_interleaved(programs, skew):
    live = {}
    tick = 0
    while live or tick <= skew * (len(programs) - 1):
        if tick % skew == 0 and tick // skew < len(programs):
            live[tick // skew] = programs[tick // skew]
        for key in sorted(live):
            for prog in live[key]:
                if next(prog, "done") == "done":
                    live[key] = [p for p in live[key] if p is not prog]
            if not live[key]:
                del live[key]
        tick += 1


def _gdn_body(qf_ref, qb_ref, gbf_ref, gbb_ref, s0f_ref, s0b_ref,
              of_ref, ob_ref, sff_ref, sfb_ref, sf_scr, sb_scr, *, tb, nt):
    t = pl.program_id(1)
    cs = GDN_CHUNK
    nch = tb // cs
    cps = min(GDN_CHUNKS_PER_STEP, nch)
    nh = N_HEADS

    @pl.when(t == 0)
    def _():
        sf_scr[...] = s0f_ref[0]
        sb_scr[...] = s0b_ref[0]

    ii = lax.broadcasted_iota(I32, (cs, PACK_W), 0)
    jj = lax.broadcasted_iota(I32, (cs, PACK_W), 1) & (cs - 1)
    incl_f, strict_f = jj <= ii, jj < ii
    incl_b, strict_b = jj >= ii, jj > ii
    levels = []
    sh = 0
    while (1 << sh) < cs:
        levels.append(((ii >> (sh + 1)) == (jj >> (sh + 1))) & ((ii >> sh) != (jj >> sh)))
        sh += 1
    lv_f = [lm & strict_f for lm in levels]
    lv_b = [lm & strict_b for lm in levels]
    eye = (ii == jj).astype(F32)
    row = lax.broadcasted_iota(I32, (cs, LANES), 0)
    lane = lax.broadcasted_iota(I32, (1, LANES), 1)
    half_of_tile = lax.broadcasted_iota(I32, (cs, LANES), 1) // cs
    half_masks = [(half_of_tile == i).astype(BF16) for i in range(LANES // cs)]

    def group(q_ref, r0, gate, csum, gt, backward, s_ref, o_ref, order):
        l0 = nh if backward else 0
        lanes = [l0 + h for h in range(nh)]
        last = 0 if backward else cs - 1
        ld = lambda c0: q_ref[0, pl.ds(r0, cs), c0:c0 + B_W]
        halves = [gt[l:l + 1, :] if (h % 2 == 0) != backward else pltpu.roll(gt[l:l + 1, :], cs, 1)
                  for h, l in enumerate(lanes)]
        gc_row = jnp.concatenate([jnp.where(lane < cs, halves[0], halves[1]),
                                  jnp.where(lane < cs, halves[2], halves[3])], axis=1)
        glast = csum[last:last + 1, :]
        return dict(
            q=ld(0), k=ld(B_W), v=ld(2 * B_W),
            beta=_per_head(gate, [2 * nh + l for l in lanes], HEAD_DIM).astype(BF16),
            egc=_per_head(jnp.exp(csum), lanes, HEAD_DIM).astype(BF16),
            kdec=_per_head(jnp.exp(glast - csum), lanes, HEAD_DIM).astype(BF16), eye=eye,
            eg=[jnp.exp(csum[last:last + 1, l:l + 1]) for l in lanes],
            gc_col=_per_head(csum, lanes, cs), gc_row=gc_row,
            incl=incl_b if backward else incl_f, strict=strict_b if backward else strict_f,
            levels=lv_b if backward else lv_f, s_ref=s_ref, o_ref=o_ref, r0=r0, backward=backward, order=order)

    def step(n, carry):
        per_chunk = []
        for j in range(cps):
            rf = pl.multiple_of((n * cps + j) * cs, cs)
            rb = pl.multiple_of((nch - 1 - n * cps - j) * cs, cs)
            gf = gbf_ref[0, pl.ds(rf, cs), :]
            gb = gbb_ref[0, pl.ds(rb, cs), :]
            cf, cb = gf, gb
            s = 1
            while s < cs:
                cf = cf + jnp.where(row >= s, pltpu.roll(cf, s, 0), 0.0)
                cb = cb + jnp.where(row < cs - s, pltpu.roll(cb, cs - s, 0), 0.0)
                s *= 2
            gt = jnp.concatenate([cf, cb], axis=0).T
            per_chunk.append([group(qf_ref, rf, gf, cf, gt, False, sf_scr, of_ref, j),
                              group(qb_ref, rb, gb, cb, gt, True, sb_scr, ob_ref, j)])
        turn = {False: 0, True: 0}
        _run_interleaved([[_gdn_group_program(g, half_masks, turn) for g in pair] for pair in per_chunk],
                         GDN_STAGE_SKEW)
        return carry

    lax.fori_loop(0, nch // cps, step, 0)

    @pl.when(t == nt - 1)
    def _():
        sff_ref[0] = sf_scr[...]
        sfb_ref[0] = sb_scr[...]


def _gdn_call(qkv, gb, s0f, s0b, tb):
    bsz, t, _ = qkv.shape
    nt = t // tb
    assert (tb // GDN_CHUNK) % min(GDN_CHUNKS_PER_STEP, tb // GDN_CHUNK) == 0
    st = pl.BlockSpec((1, N_HEADS, HEAD_DIM, HEAD_DIM), lambda b, i: (b, 0, 0, 0))
    fwd = lambda w: pl.BlockSpec((1, tb, w), lambda b, i: (b, i, 0))
    bwd = lambda w: pl.BlockSpec((1, tb, w), lambda b, i: (b, nt - 1 - i, 0))
    return pl.pallas_call(
        functools.partial(_gdn_body, tb=tb, nt=nt),
        out_shape=(jax.ShapeDtypeStruct((bsz, t, B_W), BF16), jax.ShapeDtypeStruct((bsz, t, B_W), BF16),
                   jax.ShapeDtypeStruct((bsz, N_HEADS, HEAD_DIM, HEAD_DIM), F32),
                   jax.ShapeDtypeStruct((bsz, N_HEADS, HEAD_DIM, HEAD_DIM), F32)),
        grid=(bsz, nt),
        in_specs=[fwd(QKV_W), bwd(QKV_W), fwd(LANES), bwd(LANES), st, st],
        out_specs=(fwd(B_W), bwd(B_W), st, st),
        scratch_shapes=[pltpu.VMEM((N_HEADS, HEAD_DIM, HEAD_DIM), F32), pltpu.VMEM((N_HEADS, HEAD_DIM, HEAD_DIM), F32)],
        compiler_params=_cparams(("parallel", "arbitrary")),
        name="gdn",
    )(qkv, qkv, gb, gb, s0f, s0b)


def _mixout_body(x_ref, of_ref, ob_ref, z_ref, ya_ref, mod_ref, gng_ref, wout_ref, n2g_ref, wrh_ref, wrl_ref, br_ref,
                 h_ref, fin_ref, aff_ref, afft_ref, *, tm):
    o = of_ref[0].astype(F32) + ob_ref[0].astype(F32)
    z = z_ref[0].astype(F32)
    parts = [ya_ref[0]]
    for h in range(N_HEADS):
        c = slice(h * HEAD_DIM, (h + 1) * HEAD_DIM)
        oh = o[:, c]
        y = oh * lax.rsqrt(jnp.mean(oh * oh, axis=-1, keepdims=True) + NORM_EPS)
        parts.append((y * gng_ref[...] * _silu(z[:, c])).astype(BF16))
    mix = _dot(jnp.concatenate(parts, axis=1), wout_ref[...])
    hl = x_ref[0] + mod_ref[0, 2:3, :] * mix
    h_ref[0] = hl
    fin = _norm_mod(hl, n2g_ref[...], mod_ref[0, 3:4, :], mod_ref[0, 4:5, :])
    f_hi = fin.astype(BF16)
    fin_ref[0] = f_hi
    f_lo = (fin - f_hi.astype(F32)).astype(BF16)
    both = _dot(f_hi, wrl_ref[...])
    logits = both[:, :LANES] + _dot(f_lo, wrh_ref[...]) + both[:, LANES:] + br_ref[...]
    e = jnp.exp(logits - jnp.max(logits, axis=-1, keepdims=True))
    aff = e / jnp.sum(e, axis=-1, keepdims=True)
    aff_ref[0] = aff
    for j in range(tm // LANES):
        afft_ref[0, j] = aff[j * LANES:(j + 1) * LANES, :].T[0:N_EXPERTS, :]


def _mixout_call(x, o_f, o_b, z, ya, mod3, gng, wout16, n2g, wr_hi, wr_lo, br, tm):
    bsz, t, _ = x.shape
    full = lambda a: pl.BlockSpec(a.shape, lambda b, i: (0,) * a.ndim)
    tok = lambda w: pl.BlockSpec((1, tm, w), lambda b, i: (b, i, 0))
    return pl.pallas_call(
        functools.partial(_mixout_body, tm=tm),
        out_shape=(jax.ShapeDtypeStruct((bsz, t, D_MODEL), F32), jax.ShapeDtypeStruct((bsz, t, D_MODEL), BF16),
                   jax.ShapeDtypeStruct((bsz, t, LANES), F32),
                   jax.ShapeDtypeStruct((bsz, t // LANES, N_EXPERTS, LANES), F32)),
        grid=(bsz, t // tm),
        in_specs=[tok(D_MODEL), tok(B_W), tok(B_W), tok(B_W), tok(A_W),
                  pl.BlockSpec((1, N_MOD, D_MODEL), lambda b, i: (b, 0, 0)),
                  full(gng), full(wout16), full(n2g), full(wr_hi), full(wr_lo), full(br)],
        out_specs=(tok(D_MODEL), tok(D_MODEL), tok(LANES),
                   pl.BlockSpec((1, tm // LANES, N_EXPERTS, LANES), lambda b, i: (b, i, 0, 0))),
        compiler_params=_cparams(("parallel", "parallel")),
        name="mixout",
    )(x, o_f, o_b, z, ya, mod3, gng, wout16, n2g, wr_hi, wr_lo, br)


def _route_body(afft_ref, slott_ref, slot_ref, off_ref, *, t, cap):
    ne = N_EXPERTS
    npieces = t // LANES
    rows = npieces * ne

    def count(thr_col, strict):
        acc = jnp.zeros((ne, LANES), I32)
        for p in range(npieces):
            piece = afft_ref[0, p * ne:(p + 1) * ne, :]
            acc = acc + (piece > thr_col if strict else piece >= thr_col).astype(I32)
        return jnp.sum(acc, axis=1, keepdims=True)

    def search(i, thr):
        cand = thr | jnp.left_shift(jnp.int32(1), 30 - i)
        return jnp.where(count(pltpu.bitcast(cand, F32), False) >= cap, cand, thr)

    thr_bits = lax.fori_loop(0, 31, search, jnp.zeros((ne, 1), I32))
    thr = pltpu.bitcast(thr_bits, F32)
    need = (cap - count(thr, True)).astype(F32)

    x = afft_ref[0]
    thr_rows = jnp.concatenate([thr] * npieces, axis=0)
    need_rows = jnp.concatenate([need] * npieces, axis=0)
    gt = x > thr_rows
    eq = x == thr_rows
    ti = lax.broadcasted_iota(I32, (LANES, LANES), 0)
    tj = lax.broadcasted_iota(I32, (LANES, LANES), 1)
    triu = (ti <= tj).astype(BF16)
    ri = lax.broadcasted_iota(I32, (rows, rows), 0)
    rj = lax.broadcasted_iota(I32, (rows, rows), 1)
    earlier = (((ri & (ne - 1)) == (rj & (ne - 1))) & (rj < ri)).astype(BF16)

    def prefix(mask):
        inpiece = _dot(mask.astype(BF16), triu)
        total = jnp.broadcast_to(inpiece[:, LANES - 1:LANES], (rows, LANES)).astype(BF16)
        offset = _dot(earlier, total)
        return inpiece + offset, offset

    eq_rank, _ = prefix(eq)
    sel = gt | (eq & (eq_rank <= need_rows))
    sel_rank, sel_off = prefix(sel)
    slot = jnp.where(sel, sel_rank - 1.0, -1.0)
    slott_ref[0] = slot.astype(I32)
    off_ref[0] = sel_off.astype(I32)
    pad = jnp.zeros((LANES - ne, LANES), F32)
    for p in range(npieces):
        piece = jnp.concatenate([slot[p * ne:(p + 1) * ne, :], pad], axis=0)
        slot_ref[0, p * LANES:(p + 1) * LANES, :] = piece.T.astype(I32)


def _route_call(afft, cap):
    bsz, rows, _ = afft.shape
    t = rows // N_EXPERTS * LANES
    spec = lambda r: pl.BlockSpec((1, r, LANES), lambda b: (b, 0, 0))
    return pl.pallas_call(
        functools.partial(_route_body, t=t, cap=cap),
        out_shape=(jax.ShapeDtypeStruct((bsz, rows, LANES), I32),
                   jax.ShapeDtypeStruct((bsz, t, LANES), I32),
                   jax.ShapeDtypeStruct((bsz, rows, LANES), I32)),
        grid=(bsz,),
        in_specs=[spec(rows)],
        out_specs=(spec(rows), spec(t), spec(rows)),
        compiler_params=_cparams(("parallel",)),
        name="route",
    )(afft)


def _window_plan(base_ref, flat0, experts):
    starts, rounds = [], jnp.int32(0)
    for e in experts:
        lo = base_ref[flat0 + e]
        hi = base_ref[flat0 + N_EXPERTS + e]
        lo_al = (lo >> 4) << 4
        starts.append(lo_al)
        rounds = jnp.maximum(rounds, (hi - lo_al + SLOT_WIN - 1) // SLOT_WIN)
    return starts, rounds


def _window_start(start, r, cap):
    return pl.multiple_of(jnp.minimum(start + r * SLOT_WIN, cap), SLOT_ALIGN)


def _dispatch_body(base_ref, slott_ref, fin_ref, xe_ref, *, nchunk, sub, eh_n, cap):
    b, eh, ci = pl.program_id(0), pl.program_id(1), pl.program_id(2)
    rc = ROUTE_CHUNK

    @pl.when(ci == 0)
    def _():
        xe_ref[...] = jnp.zeros_like(xe_ref)

    srow = lax.broadcasted_iota(I32, (SLOT_WIN, rc), 0)
    for sc in range(sub):
        cc = ci * sub + sc
        flat0 = (b * (nchunk + 1) + cc) * N_EXPERTS + eh * eh_n
        f = fin_ref[0, sc * rc:(sc + 1) * rc, :]
        experts = list(range(eh_n))
        starts, rounds = _window_plan(base_ref, flat0, experts)

        def one_round(r, carry, starts=starts, f=f, sc=sc):
            rows = []
            wstart = [_window_start(starts[e], r, cap) for e in experts]
            for e in experts:
                tok_slot = jnp.concatenate(
                    [slott_ref[0, sc * (rc // LANES) + j, e:e + 1, :] for j in range(rc // LANES)], axis=1)
                rows.append((tok_slot == srow + wstart[e]).astype(BF16))
            prod = _dot(jnp.concatenate(rows, axis=0), f)
            for e in experts:
                win = pl.ds(wstart[e], SLOT_WIN)
                xe_ref[0, e, win, :] = xe_ref[0, e, win, :] + prod[e * SLOT_WIN:(e + 1) * SLOT_WIN].astype(BF16)
            return carry

        one_round(jnp.int32(0), 0)
        lax.fori_loop(1, rounds, one_round, 0)


def _dispatch_call(base_flat, slott, fin, cap):
    bsz, t, _ = fin.shape
    nchunk = t // ROUTE_CHUNK
    sub = 8
    eh_n = N_EXPERTS // 2
    sp = cap + SLOT_WIN
    grid_spec = pltpu.PrefetchScalarGridSpec(
        num_scalar_prefetch=1,
        grid=(bsz, N_EXPERTS // eh_n, nchunk // sub),
        in_specs=[pl.BlockSpec((1, sub * ROUTE_CHUNK // LANES, eh_n, LANES), lambda b, eh, ci, base: (b, ci, eh, 0)),
                  pl.BlockSpec((1, sub * ROUTE_CHUNK, D_MODEL), lambda b, eh, ci, base: (b, ci, 0))],
        out_specs=pl.BlockSpec((1, eh_n, sp, D_MODEL), lambda b, eh, ci, base: (b, eh, 0, 0)))
    return pl.pallas_call(
        functools.partial(_dispatch_body, nchunk=nchunk, sub=sub, eh_n=eh_n, cap=cap),
        out_shape=jax.ShapeDtypeStruct((bsz, N_EXPERTS, sp, D_MODEL), BF16),
        grid_spec=grid_spec,
        compiler_params=_cparams(("parallel", "parallel", "arbitrary")),
        name="dispatch",
    )(base_flat, slott, fin)


def _experts_body(xe_ref, wg_ref, wu_ref, wd_ref, y_ref, *, cap):
    x = xe_ref[0, 0, 0:cap, :]
    ft = 256
    acc = None
    for f in range(EXPERT_FF // ft):
        cols = slice(f * ft, (f + 1) * ft)
        wg16 = wg_ref[0, :, cols].astype(BF16)
        wu16 = wu_ref[0, :, cols].astype(BF16)
        wd16 = wd_ref[0, cols, :].astype(BF16)
        hid = (_silu(_dot(x, wg16)) * _dot(x, wu16)).astype(BF16)
        part = _dot(hid, wd16)
        acc = part if acc is None else acc + part
    y_ref[0, 0, 0:cap, :] = acc.astype(BF16)
    y_ref[0, 0, cap:, :] = jnp.zeros((y_ref.shape[2] - cap, D_MODEL), BF16)


def _experts_call(xe, w_gate, w_up, w_down, cap):
    bsz, _, sp, _ = xe.shape
    wspec = lambda shape: pl.BlockSpec((1,) + shape, lambda e, b: (e, 0, 0))
    slots = pl.BlockSpec((1, 1, sp, D_MODEL), lambda e, b: (b, e, 0, 0))
    return pl.pallas_call(
        functools.partial(_experts_body, cap=cap),
        out_shape=jax.ShapeDtypeStruct(xe.shape, BF16),
        grid=(N_EXPERTS, bsz),
        in_specs=[slots, wspec((D_MODEL, EXPERT_FF)), wspec((D_MODEL, EXPERT_FF)), wspec((EXPERT_FF, D_MODEL))],
        out_specs=slots,
        compiler_params=_cparams(("parallel", "parallel")),
        name="experts",
    )(xe, w_gate, w_up, w_down)


SLOT_SPLIT = 32


def _combine_selectors():
    k = jnp.arange(LANES)[:, None]
    e_of_lane = jnp.arange(N_EXPERTS * SLOT_WIN)[None, :] // SLOT_WIN
    sel_gate = (k == e_of_lane).astype(BF16)
    sel_slot = (SLOT_SPLIT * (k == e_of_lane) + (k - N_EXPERTS == e_of_lane)).astype(BF16)
    return sel_slot, sel_gate


def _combine_body(base_ref, slot_ref, aff_ref, h_ref, y_ref, mod_ref, fng_ref, ssel_ref, gsel_ref, o_ref, acc_ref,
                  *, nchunk, sub, cap):
    b, ci = pl.program_id(0), pl.program_id(1)
    rc = ROUTE_CHUNK
    width = N_EXPERTS * SLOT_WIN
    lane = lax.broadcasted_iota(I32, (1, width), 1)
    lane_e = lane >> (SLOT_WIN.bit_length() - 1)
    lane_j = (lane & (SLOT_WIN - 1)).astype(F32)
    lane128 = lax.broadcasted_iota(I32, (rc, LANES), 1)
    experts = list(range(N_EXPERTS))
    for sc in range(sub):
        rows = slice(sc * rc, (sc + 1) * rc)
        flat0 = (b * (nchunk + 1) + ci * sub + sc) * N_EXPERTS
        s1 = slot_ref[0, rows, :] + 1
        halves = jnp.where(lane128 < N_EXPERTS, s1 >> (SLOT_SPLIT.bit_length() - 1),
                           pltpu.roll(s1 & (SLOT_SPLIT - 1), N_EXPERTS, 1))
        slot1 = _dot(halves.astype(F32).astype(BF16), ssel_ref[...])
        gates = _dot(aff_ref[0, rows, :].astype(BF16), gsel_ref[...])
        starts, rounds = _window_plan(base_ref, flat0, experts)

        def contribution(r, starts=starts, slot1=slot1, gates=gates):
            wstart = [_window_start(starts[e], r, cap) for e in experts]
            ywin = jnp.concatenate([y_ref[0, e, pl.ds(wstart[e], SLOT_WIN), :] for e in experts], axis=0)
            held = jnp.zeros((1, width), I32)
            for e in experts:
                held = jnp.where(lane_e == e, wstart[e] + 1, held)
            held = held.astype(F32) + lane_j
            s = jnp.where(slot1 == held, gates, 0.0).astype(BF16)
            return _dot(s, ywin)

        def extra_round(r, carry, contribution=contribution):
            acc_ref[...] += contribution(r)
            return carry

        acc_ref[...] = contribution(jnp.int32(0))
        lax.fori_loop(1, rounds, extra_round, 0)
        hl = h_ref[0, rows, :] + mod_ref[0, 5:6, :] * acc_ref[...]
        ms = jnp.mean(hl * hl, axis=-1, keepdims=True)
        o_ref[0, rows, :] = hl * lax.rsqrt(ms + NORM_EPS) * fng_ref[...]


def _combine_call(base_flat, slot, aff, h, y, mod3, fng):
    bsz, t, _ = h.shape
    nchunk = t // ROUTE_CHUNK
    sub = 4
    rc = ROUTE_CHUNK
    ssel, gsel = _combine_selectors()
    tok = lambda w: pl.BlockSpec((1, sub * rc, w), lambda b, i, base: (b, i, 0))
    full = lambda a: pl.BlockSpec(a.shape, lambda b, i, base: (0,) * a.ndim)
    grid_spec = pltpu.PrefetchScalarGridSpec(
        num_scalar_prefetch=1,
        grid=(bsz, nchunk // sub),
        in_specs=[tok(LANES), tok(LANES), tok(D_MODEL),
                  pl.BlockSpec((1,) + y.shape[1:], lambda b, i, base: (b, 0, 0, 0), pipeline_mode=pl.Buffered(1)),
                  pl.BlockSpec((1, N_MOD, D_MODEL), lambda b, i, base: (b, 0, 0)),
                  full(fng), full(ssel), full(gsel)],
        out_specs=tok(D_MODEL),
        scratch_shapes=[pltpu.VMEM((rc, D_MODEL), F32)])
    return pl.pallas_call(
        functools.partial(_combine_body, nchunk=nchunk, sub=sub, cap=y.shape[2] - SLOT_WIN),
        out_shape=jax.ShapeDtypeStruct(h.shape, F32),
        grid_spec=grid_spec,
        compiler_params=_cparams(("parallel", "arbitrary")),
        name="combine",
    )(base_flat, slot, aff, h, y, mod3, fng, ssel, gsel)


def _pad_lanes(a):
    return jnp.pad(a, ((0, 0), (0, LANES - a.shape[1])))


def kernel(x, c, ctx, c_ctx, w_mod, b_mod, norm1_g, norm2_g, w_in, conv_w, a_log, dt_bias, gdn_norm_g, gm_norm_g,
           gm_ws, gm_bs, w_out, w_router, b_router, w_gate, w_up, w_down, final_norm_g):
    bsz, t, _ = x.shape
    ctx_len = ctx.shape[1]
    assert w_mod.shape[0] == 1, "single-layer problem"
    assert t % 2048 == 0 and ctx_len % GDN_CHUNK == 0 and bsz < 8
    cap = EC_CAPACITY * t // N_EXPERTS

    cs = jnp.zeros((8, D_MODEL), F32).at[:bsz].set(c).at[bsz].set(c_ctx)
    mod3 = _mod_call(cs, w_mod[0], b_mod[0][None, :]).reshape(8, N_MOD, D_MODEL)

    wl = w_in[0]
    n_state = QKV_W + STATE_COLS
    w_state = _pad_lanes(wl[:, QKV_W:n_state])
    w_lat = jnp.concatenate([wl[:, :QKV_W], wl[:, n_state:n_state + B_W], wl[:, n_state + B_W:], w_state],
                            axis=1).astype(BF16)
    gp = jnp.zeros((8, LANES), F32).at[0, :2 * N_HEADS].set(a_log[0].reshape(-1)).at[1, :2 * N_HEADS].set(
        dt_bias[0].reshape(-1))
    g1 = norm1_g[0][None, :]
    cw = jnp.zeros((8, QKV_W), F32).at[:conv_w.shape[1]].set(conv_w[0])

    qkv_c, gb_c = _inproj_ctx_call(ctx, mod3, bsz, g1, w_lat, gp, cw)
    zero_state = jnp.zeros((bsz, N_HEADS, HEAD_DIM, HEAD_DIM), F32)
    _, _, s_f, s_b = _gdn_call(qkv_c, gb_c, zero_state, zero_state, ctx_len)

    qkv, gb, z, ya = _inproj_lat_call(x, mod3, g1, w_lat, gp, cw, gm_norm_g[0][None, :], gm_ws[0].astype(BF16),
                                      _pad_lanes(gm_bs[0].T), 1024)
    o_f, o_b, _, _ = _gdn_call(qkv, gb, s_f, s_b, 1024)
    wr = _pad_lanes(w_router[0])
    wr_hi = wr.astype(BF16)
    wr_lo = jnp.concatenate([wr_hi, (wr - wr_hi.astype(F32)).astype(BF16)], axis=1)
    br = jnp.full((1, LANES), -1e30, F32).at[0, :N_EXPERTS].set(b_router[0])
    h, fin, aff, afft = _mixout_call(x, o_f, o_b, z, ya, mod3, gdn_norm_g[0][None, :], w_out[0].astype(BF16),
                                     norm2_g[0][None, :], wr_hi, wr_lo, br, 1024)

    npieces = t // LANES
    slott, slot, off = _route_call(afft.reshape(bsz, npieces * N_EXPERTS, LANES), cap)
    base = off[:, :, 0].reshape(bsz, npieces, N_EXPERTS)[:, ::ROUTE_CHUNK // LANES, :]
    base_flat = jnp.concatenate([base, jnp.full((bsz, 1, N_EXPERTS), cap, I32)], axis=1).reshape(-1)
    xe = _dispatch_call(base_flat, slott.reshape(bsz, npieces, N_EXPERTS, LANES), fin, cap)
    y = _experts_call(xe, w_gate[0], w_up[0], w_down[0], cap)
    return _combine_call(base_flat, slot, aff, h, y, mod3, final_norm_g[None, :])
```

```python
import functools

import jax
import jax.numpy as jnp
from jax import lax
from jax.experimental import pallas as pl
from jax.experimental.pallas import tpu as pltpu

F32 = jnp.float32
BF16 = jnp.bfloat16
I32 = jnp.int32

D_MODEL = 1024
N_MOD = 6
N_HEADS = 4
HEAD_DIM = 128
B_W = N_HEADS * HEAD_DIM
QKV_W = 3 * B_W
A_W = 512
A_GROUPS = 4
A_CHUNK = 128
GDN_CHUNK = 64
N_EXPERTS = 16
EC_CAPACITY = 2
EXPERT_FF = 1024
NORM_EPS = 1e-6
LANES = 128
STATE_COLS = 4 * N_HEADS

ROUTE_CHUNK = 256
SLOT_WIN = 64
SLOT_ALIGN = 16
VMEM_LIMIT = 60 * 1024 * 1024


def _cparams(sem):
    return pltpu.CompilerParams(dimension_semantics=sem, vmem_limit_bytes=VMEM_LIMIT)


def _dot(a, b):
    return jnp.dot(a, b, preferred_element_type=F32)


def _dot_nt(a, b):
    return lax.dot_general(a, b, (((1,), (1,)), ((), ())), preferred_element_type=F32)


def _dot_tn(a, b):
    return lax.dot_general(a, b, (((0,), (0,)), ((), ())), preferred_element_type=F32)


def _silu(x):
    return x * jax.nn.sigmoid(x)


def _mod_body(c_ref, w_ref, b_ref, o_ref):
    s = _silu(c_ref[...])
    o_ref[...] = _dot(s.astype(BF16), w_ref[...].astype(BF16)) + b_ref[...]


def _mod_call(cs, w_mod, b_mod):
    n = w_mod.shape[1] // D_MODEL
    return pl.pallas_call(
        _mod_body,
        out_shape=jax.ShapeDtypeStruct((8, w_mod.shape[1]), F32),
        grid=(n,),
        in_specs=[pl.BlockSpec((8, D_MODEL), lambda j: (0, 0)),
                  pl.BlockSpec((D_MODEL, D_MODEL), lambda j: (0, j)),
                  pl.BlockSpec((1, D_MODEL), lambda j: (0, j))],
        out_specs=pl.BlockSpec((8, D_MODEL), lambda j: (0, j)),
        compiler_params=_cparams(("arbitrary",)),
        name="mod",
    )(cs, w_mod, b_mod)


def _norm_mod(x, g, shift, scale):
    ms = jnp.mean(x * x, axis=-1, keepdims=True)
    return (x * lax.rsqrt(ms + NORM_EPS) * g) * (1.0 + scale) + shift


def _gate_streams(st, gp_ref):
    lane = lax.broadcasted_iota(I32, st.shape, 1)
    g = -jnp.exp(gp_ref[0:1, :]) * jax.nn.softplus(st + gp_ref[1:2, :])
    beta = jax.nn.sigmoid(st)
    return jnp.where(lane < 2 * N_HEADS, g, jnp.where(lane < STATE_COLS, beta, 0.0))


def _conv_qkv(qkv, prev_row, next_row, cw_ref, out_ref, tm):
    cs = GDN_CHUNK
    nsub = tm // cs
    w0, w1, w2 = cw_ref[0:1, :], cw_ref[1:2, :], cw_ref[2:3, :]
    row = lax.broadcasted_iota(I32, (cs, 1), 0)
    for c in range(nsub):
        rows = slice(c * cs, (c + 1) * cs)
        x = qkv[rows]
        prow = prev_row if c == 0 else qkv[c * cs - 1:c * cs]
        nrow = next_row if c == nsub - 1 else qkv[(c + 1) * cs:(c + 1) * cs + 1]
        xp = jnp.where(row == 0, prow, pltpu.roll(x, 1, 0))
        xn = jnp.where(row == cs - 1, nrow, pltpu.roll(x, cs - 1, 0))
        y = _silu(xp * w0 + x * w1 + xn * w2)
        for h in range(N_HEADS):
            cq = slice(h * HEAD_DIM, (h + 1) * HEAD_DIM)
            ck = slice(B_W + h * HEAD_DIM, B_W + (h + 1) * HEAD_DIM)
            q = y[:, cq]
            k = y[:, ck]
            out_ref[0, rows, cq] = (q * (lax.rsqrt(jnp.sum(q * q, axis=-1, keepdims=True) + NORM_EPS)
                                         * (HEAD_DIM ** -0.5))).astype(BF16)
            out_ref[0, rows, ck] = (k * lax.rsqrt(jnp.sum(k * k, axis=-1, keepdims=True) + NORM_EPS)).astype(BF16)
        out_ref[0, rows, 2 * B_W:3 * B_W] = y[:, 2 * B_W:3 * B_W].astype(BF16)


def _inproj_lat_body(x_ref, xp_ref, xn_ref, mod_ref, g1_ref, w_ref, gp_ref, cw_ref, gmg_ref, ws_ref, bst_ref,
                     qkv_ref, gb_ref, z_ref, ya_ref, *, tm):
    i = pl.program_id(1)
    shift, scale = mod_ref[0, 0:1, :], mod_ref[0, 1:2, :]
    a = _norm_mod(x_ref[0], g1_ref[...], shift, scale).astype(BF16)
    xh = jnp.concatenate([xp_ref[0], xn_ref[0]], axis=0)
    halo = _dot(_norm_mod(xh, g1_ref[...], shift, scale).astype(BF16), w_ref[:, 0:QKV_W])
    prev_row = jnp.where(i == 0, 0.0, halo[7:8, :])
    next_row = jnp.where(i == pl.num_programs(1) - 1, 0.0, halo[8:9, :])
    _conv_qkv(_dot(a, w_ref[:, 0:QKV_W]), prev_row, next_row, cw_ref, qkv_ref, tm)
    z_ref[0] = _dot(a, w_ref[:, QKV_W:QKV_W + B_W]).astype(BF16)
    c_uv = QKV_W + B_W
    gb_ref[0] = _gate_streams(_dot(a, w_ref[:, c_uv + 2 * A_W:c_uv + 2 * A_W + LANES]), gp_ref)
    uv = _dot(a, w_ref[:, c_uv:c_uv + 2 * A_W])
    uv = 0.5 * uv * (1.0 + lax.erf(uv * 0.7071067811865476))
    gd = A_W // A_GROUPS
    for grp in range(A_GROUPS):
        v = uv[:, A_W + grp * gd:A_W + (grp + 1) * gd]
        vn = v * lax.rsqrt(jnp.mean(v * v, axis=-1, keepdims=True) + NORM_EPS) * gmg_ref[:, grp * gd:(grp + 1) * gd]
        vn = vn.astype(BF16)
        bias = bst_ref[:, grp:grp + 1]
        for c in range(tm // A_CHUNK):
            rows = slice(c * A_CHUNK, (c + 1) * A_CHUNK)
            s = _dot(ws_ref[grp], vn[rows]) + bias
            ya_ref[0, rows, grp * gd:(grp + 1) * gd] = (uv[rows, grp * gd:(grp + 1) * gd] * s).astype(BF16)


def _inproj_ctx_body(x_ref, mod_ref, g1_ref, w_ref, gp_ref, cw_ref, qkv_ref, gb_ref, *, tm):
    a = _norm_mod(x_ref[0], g1_ref[...], mod_ref[0, 0:1, :], mod_ref[0, 1:2, :]).astype(BF16)
    edge = jnp.zeros((1, QKV_W), F32)
    _conv_qkv(_dot(a, w_ref[:, 0:QKV_W]), edge, edge, cw_ref, qkv_ref, tm)
    c_state = QKV_W + B_W + 2 * A_W
    gb_ref[0] = _gate_streams(_dot(a, w_ref[:, c_state:c_state + LANES]), gp_ref)


def _inproj_lat_call(x, mod3, g1, w_lat, gp, cw, gmg, ws16, bst, tm):
    bsz, t, _ = x.shape
    hb = tm // 8
    last8 = t // 8 - 1
    full = lambda a: pl.BlockSpec(a.shape, lambda b, i: (0,) * a.ndim)
    tok = lambda w: pl.BlockSpec((1, tm, w), lambda b, i: (b, i, 0))
    return pl.pallas_call(
        functools.partial(_inproj_lat_body, tm=tm),
        out_shape=(jax.ShapeDtypeStruct((bsz, t, QKV_W), BF16),
                   jax.ShapeDtypeStruct((bsz, t, LANES), F32),
                   jax.ShapeDtypeStruct((bsz, t, B_W), BF16),
                   jax.ShapeDtypeStruct((bsz, t, A_W), BF16)),
        grid=(bsz, t // tm),
        in_specs=[tok(D_MODEL),
                  pl.BlockSpec((1, 8, D_MODEL), lambda b, i: (b, jnp.maximum(i * hb - 1, 0), 0)),
                  pl.BlockSpec((1, 8, D_MODEL), lambda b, i: (b, jnp.minimum((i + 1) * hb, last8), 0)),
                  pl.BlockSpec((1, N_MOD, D_MODEL), lambda b, i: (b, 0, 0)),
                  full(g1), full(w_lat), full(gp), full(cw), full(gmg), full(ws16), full(bst)],
        out_specs=(tok(QKV_W), tok(LANES), tok(B_W), tok(A_W)),
        compiler_params=_cparams(("parallel", "arbitrary")),
        name="inproj_lat",
    )(x, x, x, mod3, g1, w_lat, gp, cw, gmg, ws16, bst)


def _inproj_ctx_call(ctx, mod3, ctx_row, g1, w_ctx, gp, cw):
    bsz, t, _ = ctx.shape
    full = lambda a: pl.BlockSpec(a.shape, lambda b: (0,) * a.ndim)
    tok = lambda w: pl.BlockSpec((1, t, w), lambda b: (b, 0, 0))
    return pl.pallas_call(
        functools.partial(_inproj_ctx_body, tm=t),
        out_shape=(jax.ShapeDtypeStruct((bsz, t, QKV_W), BF16),
                   jax.ShapeDtypeStruct((bsz, t, LANES), F32)),
        grid=(bsz,),
        in_specs=[tok(D_MODEL),
                  pl.BlockSpec((1, N_MOD, D_MODEL), lambda b: (ctx_row, 0, 0)),
                  full(g1), full(w_ctx), full(gp), full(cw)],
        out_specs=(tok(QKV_W), tok(LANES)),
        compiler_params=_cparams(("parallel",)),
        name="inproj_ctx",
    )(ctx, mod3, g1, w_ctx, gp, cw)


GDN_CHUNKS_PER_STEP = 8
GDN_STAGE_SKEW = 2
PACK_W = N_HEADS * GDN_CHUNK


def _per_head(tile, lanes, width):
    rows = tile.shape[0]
    if width == HEAD_DIM:
        return jnp.concatenate([jnp.broadcast_to(tile[:, l:l + 1], (rows, width)) for l in lanes], axis=1)
    head = lax.broadcasted_iota(I32, (rows, N_HEADS * width), 1) // width
    out = jnp.broadcast_to(tile[:, lanes[0]:lanes[0] + 1], (rows, N_HEADS * width))
    for h in range(1, N_HEADS):
        out = jnp.where(head == h, jnp.broadcast_to(tile[:, lanes[h]:lanes[h] + 1], (rows, N_HEADS * width)), out)
    return out


def _block_rows(x16, half_masks=None):
    rows, width = x16.shape
    per_head = width // N_HEADS
    zero = jnp.zeros((rows, LANES), x16.dtype)
    blocks = []
    for h in range(N_HEADS):
        tile = h * per_head // LANES
        kept = x16[:, tile * LANES:(tile + 1) * LANES]
        if per_head < LANES:
            kept = kept * half_masks[h * per_head % LANES // per_head]
        blocks.append(jnp.concatenate([kept if t == tile else zero for t in range(width // LANES)], axis=1))
    return jnp.concatenate(blocks, axis=0)


def _gdn_group_program(g, half_masks, turn):
    cs = GDN_CHUNK
    hd = HEAD_DIM
    q16, k16, v16, beta16, egc16 = g["q"], g["k"], g["v"], g["beta"], g["egc"]
    kb16 = k16 * beta16
    decay = jnp.where(g["incl"], jnp.exp(jnp.where(g["incl"], g["gc_col"] - g["gc_row"], 0.0)), 0.0)
    kk = _dot_nt(jnp.concatenate([kb16, q16], axis=0), _block_rows(k16))
    yield
    a = jnp.where(g["strict"], kk[:cs] * decay, 0.0)
    attn = (kk[cs:] * decay).astype(BF16)
    m = -jnp.where(g["levels"][0], a, 0.0)
    for lm in g["levels"][1:]:
        m16 = m.astype(BF16)
        cm = jnp.where(lm, a, 0.0)
        x = cm + _dot(m16, _block_rows(cm.astype(BF16), half_masks))
        yield
        y = x + _dot(x.astype(BF16), _block_rows(m16, half_masks))
        yield
        m = m - y
    t16 = (m + g["eye"]).astype(BF16)
    u = _dot(t16, _block_rows(v16 * beta16))
    w = _dot(t16, _block_rows(kb16 * egc16))
    qg16 = q16 * egc16
    kg = k16 * g["kdec"]
    yield
    while turn[g["backward"]] != g["order"]:
        yield
    s_ref = g["s_ref"]
    s = [s_ref[h] for h in range(N_HEADS)]
    wq = [_dot(jnp.concatenate([w[:, h * hd:(h + 1) * hd].astype(BF16), qg16[:, h * hd:(h + 1) * hd]], axis=0),
               s[h].astype(BF16)) for h in range(N_HEADS)]
    yield
    v_new = (u - jnp.concatenate([r[:cs] for r in wq], axis=1)).astype(BF16)
    o = jnp.concatenate([r[cs:] for r in wq], axis=1) + _dot(attn, _block_rows(v_new))
    for h in range(N_HEADS):
        cols = slice(h * hd, (h + 1) * hd)
        s_ref[h] = s[h] * g["eg"][h] + _dot_tn(kg[:, cols], v_new[:, cols])
    g["o_ref"][0, pl.ds(g["r0"], cs), :] = o.astype(BF16)
    turn[g["backward"]] += 1


def _run_interleaved(programs, skew):
    live = {}
    tick = 0
    while live or tick <= skew * (len(programs) - 1):
        if tick % skew == 0 and tick // skew < len(programs):
            live[tick // skew] = programs[tick // skew]
        for key in sorted(live):
            for prog in live[key]:
                if next(prog, "done") == "done":
                    live[key] = [p for p in live[key] if p is not prog]
            if not live[key]:
                del live[key]
        tick += 1


def _gdn_body(qf_ref, qb_ref, gbf_ref, gbb_ref, s0f_ref, s0b_ref,
              of_ref, ob_ref, sff_ref, sfb_ref, sf_scr, sb_scr, *, tb, nt):
    t = pl.program_id(1)
    cs = GDN_CHUNK
    nch = tb // cs
    cps = min(GDN_CHUNKS_PER_STEP, nch)
    nh = N_HEADS

    @pl.when(t == 0)
    def _():
        sf_scr[...] = s0f_ref[0]
        sb_scr[...] = s0b_ref[0]

    ii = lax.broadcasted_iota(I32, (cs, PACK_W), 0)
    jj = lax.broadcasted_iota(I32, (cs, PACK_W), 1) & (cs - 1)
    incl_f, strict_f = jj <= ii, jj < ii
    incl_b, strict_b = jj >= ii, jj > ii
    levels = []
    sh = 0
    while (1 << sh) < cs:
        levels.append(((ii >> (sh + 1)) == (jj >> (sh + 1))) & ((ii >> sh) != (jj >> sh)))
        sh += 1
    lv_f = [lm & strict_f for lm in levels]
    lv_b = [lm & strict_b for lm in levels]
    eye = (ii == jj).astype(F32)
    row = lax.broadcasted_iota(I32, (cs, LANES), 0)
    lane = lax.broadcasted_iota(I32, (1, LANES), 1)
    half_of_tile = lax.broadcasted_iota(I32, (cs, LANES), 1) // cs
    half_masks = [(half_of_tile == i).astype(BF16) for i in range(LANES // cs)]

    def group(q_ref, r0, gate, csum, gt, backward, s_ref, o_ref, order):
        l0 = nh if backward else 0
        lanes = [l0 + h for h in range(nh)]
        last = 0 if backward else cs - 1
        ld = lambda c0: q_ref[0, pl.ds(r0, cs), c0:c0 + B_W]
        halves = [gt[l:l + 1, :] if (h % 2 == 0) != backward else pltpu.roll(gt[l:l + 1, :], cs, 1)
                  for h, l in enumerate(lanes)]
        gc_row = jnp.concatenate([jnp.where(lane < cs, halves[0], halves[1]),
                                  jnp.where(lane < cs, halves[2], halves[3])], axis=1)
        glast = csum[last:last + 1, :]
        return dict(
            q=ld(0), k=ld(B_W), v=ld(2 * B_W),
            beta=_per_head(gate, [2 * nh + l for l in lanes], HEAD_DIM).astype(BF16),
            egc=_per_head(jnp.exp(csum), lanes, HEAD_DIM).astype(BF16),
            kdec=_per_head(jnp.exp(glast - csum), lanes, HEAD_DIM).astype(BF16), eye=eye,
            eg=[jnp.exp(csum[last:last + 1, l:l + 1]) for l in lanes],
            gc_col=_per_head(csum, lanes, cs), gc_row=gc_row,
            incl=incl_b if backward else incl_f, strict=strict_b if backward else strict_f,
            levels=lv_b if backward else lv_f, s_ref=s_ref, o_ref=o_ref, r0=r0, backward=backward, order=order)

    def step(n, carry):
        per_chunk = []
        for j in range(cps):
            rf = pl.multiple_of((n * cps + j) * cs, cs)
            rb = pl.multiple_of((nch - 1 - n * cps - j) * cs, cs)
            gf = gbf_ref[0, pl.ds(rf, cs), :]
            gb = gbb_ref[0, pl.ds(rb, cs), :]
            cf, cb = gf, gb
            s = 1
            while s < cs:
                cf = cf + jnp.where(row >= s, pltpu.roll(cf, s, 0), 0.0)
                cb = cb + jnp.where(row < cs - s, pltpu.roll(cb, cs - s, 0), 0.0)
                s *= 2
            gt = jnp.concatenate([cf, cb], axis=0).T
            per_chunk.append([group(qf_ref, rf, gf, cf, gt, False, sf_scr, of_ref, j),
                              group(qb_ref, rb, gb, cb, gt, True, sb_scr, ob_ref, j)])
        turn = {False: 0, True: 0}
        _run_interleaved([[_gdn_group_program(g, half_masks, turn) for g in pair] for pair in per_chunk],
                         GDN_STAGE_SKEW)
        return carry

    lax.fori_loop(0, nch // cps, step, 0)

    @pl.when(t == nt - 1)
    def _():
        sff_ref[0] = sf_scr[...]
        sfb_ref[0] = sb_scr[...]


def _gdn_call(qkv, gb, s0f, s0b, tb):
    bsz, t, _ = qkv.shape
    nt = t // tb
    assert (tb // GDN_CHUNK) % min(GDN_CHUNKS_PER_STEP, tb // GDN_CHUNK) == 0
    st = pl.BlockSpec((1, N_HEADS, HEAD_DIM, HEAD_DIM), lambda b, i: (b, 0, 0, 0))
    fwd = lambda w: pl.BlockSpec((1, tb, w), lambda b, i: (b, i, 0))
    bwd = lambda w: pl.BlockSpec((1, tb, w), lambda b, i: (b, nt - 1 - i, 0))
    return pl.pallas_call(
        functools.partial(_gdn_body, tb=tb, nt=nt),
        out_shape=(jax.ShapeDtypeStruct((bsz, t, B_W), BF16), jax.ShapeDtypeStruct((bsz, t, B_W), BF16),
                   jax.ShapeDtypeStruct((bsz, N_HEADS, HEAD_DIM, HEAD_DIM), F32),
                   jax.ShapeDtypeStruct((bsz, N_HEADS, HEAD_DIM, HEAD_DIM), F32)),
        grid=(bsz, nt),
        in_specs=[fwd(QKV_W), bwd(QKV_W), fwd(LANES), bwd(LANES), st, st],
        out_specs=(fwd(B_W), bwd(B_W), st, st),
        scratch_shapes=[pltpu.VMEM((N_HEADS, HEAD_DIM, HEAD_DIM), F32), pltpu.VMEM((N_HEADS, HEAD_DIM, HEAD_DIM), F32)],
        compiler_params=_cparams(("parallel", "arbitrary")),
        name="gdn",
    )(qkv, qkv, gb, gb, s0f, s0b)


def _mixout_body(x_ref, of_ref, ob_ref, z_ref, ya_ref, mod_ref, gng_ref, wout_ref, n2g_ref, wrh_ref, wrl_ref, br_ref,
                 h_ref, fin_ref, afft_ref, *, tm):
    o = of_ref[0].astype(F32) + ob_ref[0].astype(F32)
    z = z_ref[0].astype(F32)
    parts = [ya_ref[0]]
    for h in range(N_HEADS):
        c = slice(h * HEAD_DIM, (h + 1) * HEAD_DIM)
        oh = o[:, c]
        y = oh * lax.rsqrt(jnp.mean(oh * oh, axis=-1, keepdims=True) + NORM_EPS)
        parts.append((y * gng_ref[...] * _silu(z[:, c])).astype(BF16))
    mix = _dot(jnp.concatenate(parts, axis=1), wout_ref[...])
    hl = x_ref[0] + mod_ref[0, 2:3, :] * mix
    h_ref[0] = hl
    fin = _norm_mod(hl, n2g_ref[...], mod_ref[0, 3:4, :], mod_ref[0, 4:5, :])
    f_hi = fin.astype(BF16)
    fin_ref[0] = f_hi
    f_lo = (fin - f_hi.astype(F32)).astype(BF16)
    both = _dot(f_hi, wrl_ref[...])
    logits = both[:, :LANES] + _dot(f_lo, wrh_ref[...]) + both[:, LANES:] + br_ref[...]
    e = jnp.exp(logits - jnp.max(logits, axis=-1, keepdims=True))
    aff = e / jnp.sum(e, axis=-1, keepdims=True)
    for j in range(tm // LANES):
        afft_ref[0, j] = aff[j * LANES:(j + 1) * LANES, :].T[0:N_EXPERTS, :]


def _mixout_call(x, o_f, o_b, z, ya, mod3, gng, wout16, n2g, wr_hi, wr_lo, br, tm):
    bsz, t, _ = x.shape
    full = lambda a: pl.BlockSpec(a.shape, lambda b, i: (0,) * a.ndim)
    tok = lambda w: pl.BlockSpec((1, tm, w), lambda b, i: (b, i, 0))
    return pl.pallas_call(
        functools.partial(_mixout_body, tm=tm),
        out_shape=(jax.ShapeDtypeStruct((bsz, t, D_MODEL), F32), jax.ShapeDtypeStruct((bsz, t, D_MODEL), BF16),
                   jax.ShapeDtypeStruct((bsz, t // LANES, N_EXPERTS, LANES), F32)),
        grid=(bsz, t // tm),
        in_specs=[tok(D_MODEL), tok(B_W), tok(B_W), tok(B_W), tok(A_W),
                  pl.BlockSpec((1, N_MOD, D_MODEL), lambda b, i: (b, 0, 0)),
                  full(gng), full(wout16), full(n2g), full(wr_hi), full(wr_lo), full(br)],
        out_specs=(tok(D_MODEL), tok(D_MODEL),
                   pl.BlockSpec((1, tm // LANES, N_EXPERTS, LANES), lambda b, i: (b, i, 0, 0))),
        compiler_params=_cparams(("parallel", "parallel")),
        name="mixout",
    )(x, o_f, o_b, z, ya, mod3, gng, wout16, n2g, wr_hi, wr_lo, br)


def _route_body(afft_ref, slott_ref, off_ref, *, t, cap):
    ne = N_EXPERTS
    npieces = t // LANES
    rows = npieces * ne

    def count(thr_col, strict):
        acc = jnp.zeros((ne, LANES), I32)
        for p in range(npieces):
            piece = afft_ref[0, p * ne:(p + 1) * ne, :]
            acc = acc + (piece > thr_col if strict else piece >= thr_col).astype(I32)
        return jnp.sum(acc, axis=1, keepdims=True)

    def search(i, thr):
        cand = thr | jnp.left_shift(jnp.int32(1), 30 - i)
        return jnp.where(count(pltpu.bitcast(cand, F32), False) >= cap, cand, thr)

    thr_bits = lax.fori_loop(0, 31, search, jnp.zeros((ne, 1), I32))
    thr = pltpu.bitcast(thr_bits, F32)
    need = (cap - count(thr, True)).astype(F32)

    x = afft_ref[0]
    thr_rows = jnp.concatenate([thr] * npieces, axis=0)
    need_rows = jnp.concatenate([need] * npieces, axis=0)
    gt = x > thr_rows
    eq = x == thr_rows
    ti = lax.broadcasted_iota(I32, (LANES, LANES), 0)
    tj = lax.broadcasted_iota(I32, (LANES, LANES), 1)
    triu = (ti <= tj).astype(BF16)
    ri = lax.broadcasted_iota(I32, (rows, rows), 0)
    rj = lax.broadcasted_iota(I32, (rows, rows), 1)
    earlier = (((ri & (ne - 1)) == (rj & (ne - 1))) & (rj < ri)).astype(BF16)

    def prefix(mask):
        inpiece = _dot(mask.astype(BF16), triu)
        total = jnp.broadcast_to(inpiece[:, LANES - 1:LANES], (rows, LANES)).astype(BF16)
        offset = _dot(earlier, total)
        return inpiece + offset, offset

    eq_rank, _ = prefix(eq)
    sel = gt | (eq & (eq_rank <= need_rows))
    sel_rank, sel_off = prefix(sel)
    slott_ref[0] = jnp.where(sel, sel_rank - 1.0, -1.0).astype(I32)
    off_ref[0] = sel_off.astype(I32)


def _route_call(afft, cap):
    bsz, rows, _ = afft.shape
    t = rows // N_EXPERTS * LANES
    spec = lambda r: pl.BlockSpec((1, r, LANES), lambda b: (b, 0, 0))
    return pl.pallas_call(
        functools.partial(_route_body, t=t, cap=cap),
        out_shape=(jax.ShapeDtypeStruct((bsz, rows, LANES), I32),
                   jax.ShapeDtypeStruct((bsz, rows, LANES), I32)),
        grid=(bsz,),
        in_specs=[spec(rows)],
        out_specs=(spec(rows), spec(rows)),
        compiler_params=_cparams(("parallel",)),
        name="route",
    )(afft)


def _window_plan(base_ref, flat0, experts):
    starts, rounds = [], jnp.int32(0)
    for e in experts:
        lo = base_ref[flat0 + e]
        hi = base_ref[flat0 + N_EXPERTS + e]
        lo_al = (lo >> 4) << 4
        starts.append(lo_al)
        rounds = jnp.maximum(rounds, (hi - lo_al + SLOT_WIN - 1) // SLOT_WIN)
    return starts, rounds


def _window_start(start, r, cap):
    return pl.multiple_of(jnp.minimum(start + r * SLOT_WIN, cap), SLOT_ALIGN)


def _dispatch_body(base_ref, slott_ref, fin_ref, xe_ref, *, nchunk, sub, eh_n, cap):
    b, eh, ci = pl.program_id(0), pl.program_id(1), pl.program_id(2)
    rc = ROUTE_CHUNK

    @pl.when(ci == 0)
    def _():
        xe_ref[...] = jnp.zeros_like(xe_ref)

    srow = lax.broadcasted_iota(I32, (SLOT_WIN, rc), 0)
    for sc in range(sub):
        cc = ci * sub + sc
        flat0 = (b * (nchunk + 1) + cc) * N_EXPERTS + eh * eh_n
        f = fin_ref[0, sc * rc:(sc + 1) * rc, :]
        experts = list(range(eh_n))
        starts, rounds = _window_plan(base_ref, flat0, experts)

        def one_round(r, carry, starts=starts, f=f, sc=sc):
            rows = []
            wstart = [_window_start(starts[e], r, cap) for e in experts]
            for e in experts:
                tok_slot = jnp.concatenate(
                    [slott_ref[0, sc * (rc // LANES) + j, e:e + 1, :] for j in range(rc // LANES)], axis=1)
                rows.append((tok_slot == srow + wstart[e]).astype(BF16))
            prod = _dot(jnp.concatenate(rows, axis=0), f)
            for e in experts:
                win = pl.ds(wstart[e], SLOT_WIN)
                xe_ref[0, e, win, :] = xe_ref[0, e, win, :] + prod[e * SLOT_WIN:(e + 1) * SLOT_WIN].astype(BF16)
            return carry

        one_round(jnp.int32(0), 0)
        lax.fori_loop(1, rounds, one_round, 0)


def _dispatch_call(base_flat, slott, fin, cap):
    bsz, t, _ = fin.shape
    nchunk = t // ROUTE_CHUNK
    sub = 8
    eh_n = N_EXPERTS // 2
    sp = cap + SLOT_WIN
    grid_spec = pltpu.PrefetchScalarGridSpec(
        num_scalar_prefetch=1,
        grid=(bsz, N_EXPERTS // eh_n, nchunk // sub),
        in_specs=[pl.BlockSpec((1, sub * ROUTE_CHUNK // LANES, eh_n, LANES), lambda b, eh, ci, base: (b, ci, eh, 0)),
                  pl.BlockSpec((1, sub * ROUTE_CHUNK, D_MODEL), lambda b, eh, ci, base: (b, ci, 0))],
        out_specs=pl.BlockSpec((1, eh_n, sp, D_MODEL), lambda b, eh, ci, base: (b, eh, 0, 0)))
    return pl.pallas_call(
        functools.partial(_dispatch_body, nchunk=nchunk, sub=sub, eh_n=eh_n, cap=cap),
        out_shape=jax.ShapeDtypeStruct((bsz, N_EXPERTS, sp, D_MODEL), BF16),
        grid_spec=grid_spec,
        compiler_params=_cparams(("parallel", "parallel", "arbitrary")),
        name="dispatch",
    )(base_flat, slott, fin)


def _experts_body(xe_ref, wg_ref, wu_ref, wd_ref, y_ref, *, cap):
    x = xe_ref[0, 0, 0:cap, :]
    ft = 256
    acc = None
    for f in range(EXPERT_FF // ft):
        cols = slice(f * ft, (f + 1) * ft)
        wg16 = wg_ref[0, :, cols].astype(BF16)
        wu16 = wu_ref[0, :, cols].astype(BF16)
        wd16 = wd_ref[0, cols, :].astype(BF16)
        hid = (_silu(_dot(x, wg16)) * _dot(x, wu16)).astype(BF16)
        part = _dot(hid, wd16)
        acc = part if acc is None else acc + part
    y_ref[0, 0, 0:cap, :] = acc.astype(BF16)
    y_ref[0, 0, cap:, :] = jnp.zeros((y_ref.shape[2] - cap, D_MODEL), BF16)


def _experts_call(xe, w_gate, w_up, w_down, cap):
    bsz, _, sp, _ = xe.shape
    wspec = lambda shape: pl.BlockSpec((1,) + shape, lambda e, b: (e, 0, 0))
    slots = pl.BlockSpec((1, 1, sp, D_MODEL), lambda e, b: (b, e, 0, 0))
    return pl.pallas_call(
        functools.partial(_experts_body, cap=cap),
        out_shape=jax.ShapeDtypeStruct(xe.shape, BF16),
        grid=(N_EXPERTS, bsz),
        in_specs=[slots, wspec((D_MODEL, EXPERT_FF)), wspec((D_MODEL, EXPERT_FF)), wspec((EXPERT_FF, D_MODEL))],
        out_specs=slots,
        compiler_params=_cparams(("parallel", "parallel")),
        name="experts",
    )(xe, w_gate, w_up, w_down)


def _combine_body(base_ref, slott_ref, afft_ref, h_ref, y_ref, mod_ref, fng_ref, o_ref, acc_ref, *, nchunk, sub, cap):
    b, ci = pl.program_id(0), pl.program_id(1)
    rc = ROUTE_CHUNK
    pieces = rc // LANES
    srow = lax.broadcasted_iota(I32, (SLOT_WIN, rc), 0)
    experts = list(range(N_EXPERTS))
    for sc in range(sub):
        rows = slice(sc * rc, (sc + 1) * rc)
        flat0 = (b * (nchunk + 1) + ci * sub + sc) * N_EXPERTS
        starts, rounds = _window_plan(base_ref, flat0, experts)
        tok_slot = [jnp.concatenate([slott_ref[0, sc * pieces + j, e:e + 1, :] for j in range(pieces)], axis=1)
                    for e in experts]
        tok_gate = [jnp.concatenate([afft_ref[0, sc * pieces + j, e:e + 1, :] for j in range(pieces)], axis=1)
                    for e in experts]

        def contribution(r, starts=starts, tok_slot=tok_slot, tok_gate=tok_gate):
            wstart = [_window_start(starts[e], r, cap) for e in experts]
            ywin = jnp.concatenate([y_ref[0, e, pl.ds(wstart[e], SLOT_WIN), :] for e in experts], axis=0)
            st = jnp.concatenate([jnp.where(tok_slot[e] == srow + wstart[e], tok_gate[e], 0.0).astype(BF16)
                                  for e in experts], axis=0)
            return _dot_tn(st, ywin)

        def extra_round(r, carry, contribution=contribution):
            acc_ref[...] += contribution(r)
            return carry

        acc_ref[...] = contribution(jnp.int32(0))
        lax.fori_loop(1, rounds, extra_round, 0)
        hl = h_ref[0, rows, :] + mod_ref[0, 5:6, :] * acc_ref[...]
        ms = jnp.mean(hl * hl, axis=-1, keepdims=True)
        o_ref[0, rows, :] = hl * lax.rsqrt(ms + NORM_EPS) * fng_ref[...]


def _combine_call(base_flat, slott, afft, h, y, mod3, fng):
    bsz, t, _ = h.shape
    nchunk = t // ROUTE_CHUNK
    sub = 4
    rc = ROUTE_CHUNK
    tok = lambda w: pl.BlockSpec((1, sub * rc, w), lambda b, i, base: (b, i, 0))
    piece = pl.BlockSpec((1, sub * rc // LANES, N_EXPERTS, LANES), lambda b, i, base: (b, i, 0, 0))
    grid_spec = pltpu.PrefetchScalarGridSpec(
        num_scalar_prefetch=1,
        grid=(bsz, nchunk // sub),
        in_specs=[piece, piece, tok(D_MODEL),
                  pl.BlockSpec((1,) + y.shape[1:], lambda b, i, base: (b, 0, 0, 0), pipeline_mode=pl.Buffered(1)),
                  pl.BlockSpec((1, N_MOD, D_MODEL), lambda b, i, base: (b, 0, 0)),
                  pl.BlockSpec(fng.shape, lambda b, i, base: (0, 0))],
        out_specs=tok(D_MODEL),
        scratch_shapes=[pltpu.VMEM((rc, D_MODEL), F32)])
    return pl.pallas_call(
        functools.partial(_combine_body, nchunk=nchunk, sub=sub, cap=y.shape[2] - SLOT_WIN),
        out_shape=jax.ShapeDtypeStruct(h.shape, F32),
        grid_spec=grid_spec,
        compiler_params=_cparams(("parallel", "arbitrary")),
        name="combine",
    )(base_flat, slott, afft, h, y, mod3, fng)


def _pad_lanes(a):
    return jnp.pad(a, ((0, 0), (0, LANES - a.shape[1])))


def kernel(x, c, ctx, c_ctx, w_mod, b_mod, norm1_g, norm2_g, w_in, conv_w, a_log, dt_bias, gdn_norm_g, gm_norm_g,
           gm_ws, gm_bs, w_out, w_router, b_router, w_gate, w_up, w_down, final_norm_g):
    bsz, t, _ = x.shape
    ctx_len = ctx.shape[1]
    assert w_mod.shape[0] == 1, "single-layer problem"
    assert t % 2048 == 0 and ctx_len % GDN_CHUNK == 0 and bsz < 8
    cap = EC_CAPACITY * t // N_EXPERTS

    cs = jnp.zeros((8, D_MODEL), F32).at[:bsz].set(c).at[bsz].set(c_ctx)
    mod3 = _mod_call(cs, w_mod[0], b_mod[0][None, :]).reshape(8, N_MOD, D_MODEL)

    wl = w_in[0]
    n_state = QKV_W + STATE_COLS
    w_state = _pad_lanes(wl[:, QKV_W:n_state])
    w_lat = jnp.concatenate([wl[:, :QKV_W], wl[:, n_state:n_state + B_W], wl[:, n_state + B_W:], w_state],
                            axis=1).astype(BF16)
    gp = jnp.zeros((8, LANES), F32).at[0, :2 * N_HEADS].set(a_log[0].reshape(-1)).at[1, :2 * N_HEADS].set(
        dt_bias[0].reshape(-1))
    g1 = norm1_g[0][None, :]
    cw = jnp.zeros((8, QKV_W), F32).at[:conv_w.shape[1]].set(conv_w[0])

    qkv_c, gb_c = _inproj_ctx_call(ctx, mod3, bsz, g1, w_lat, gp, cw)
    zero_state = jnp.zeros((bsz, N_HEADS, HEAD_DIM, HEAD_DIM), F32)
    _, _, s_f, s_b = _gdn_call(qkv_c, gb_c, zero_state, zero_state, ctx_len)

    qkv, gb, z, ya = _inproj_lat_call(x, mod3, g1, w_lat, gp, cw, gm_norm_g[0][None, :], gm_ws[0].astype(BF16),
                                      _pad_lanes(gm_bs[0].T), 1024)
    o_f, o_b, _, _ = _gdn_call(qkv, gb, s_f, s_b, 1024)
    wr = _pad_lanes(w_router[0])
    wr_hi = wr.astype(BF16)
    wr_lo = jnp.concatenate([wr_hi, (wr - wr_hi.astype(F32)).astype(BF16)], axis=1)
    br = jnp.full((1, LANES), -1e30, F32).at[0, :N_EXPERTS].set(b_router[0])
    h, fin, afft = _mixout_call(x, o_f, o_b, z, ya, mod3, gdn_norm_g[0][None, :], w_out[0].astype(BF16),
                                     norm2_g[0][None, :], wr_hi, wr_lo, br, 1024)

    npieces = t // LANES
    slott, off = _route_call(afft.reshape(bsz, npieces * N_EXPERTS, LANES), cap)
    slott = slott.reshape(bsz, npieces, N_EXPERTS, LANES)
    base = off[:, :, 0].reshape(bsz, npieces, N_EXPERTS)[:, ::ROUTE_CHUNK // LANES, :]
    base_flat = jnp.concatenate([base, jnp.full((bsz, 1, N_EXPERTS), cap, I32)], axis=1).reshape(-1)
    xe = _dispatch_call(base_flat, slott, fin, cap)
    y = _experts_call(xe, w_gate[0], w_up[0], w_down[0], cap)
    return _combine_call(base_flat, slott, afft, h, y, mod3, final_norm_g[None, :])
```

```python
import functools

import jax
import jax.numpy as jnp
from jax import lax
from jax.experimental import pallas as pl
from jax.experimental.pallas import tpu as pltpu

F32 = jnp.float32
BF16 = jnp.bfloat16
I32 = jnp.int32

D_MODEL = 1024
N_MOD = 6
N_HEADS = 4
HEAD_DIM = 128
B_W = N_HEADS * HEAD_DIM
QKV_W = 3 * B_W
A_W = 512
A_GROUPS = 4
A_CHUNK = 128
GDN_CHUNK = 64
N_EXPERTS = 16
EC_CAPACITY = 2
EXPERT_FF = 1024
NORM_EPS = 1e-6
LANES = 128
STATE_COLS = 4 * N_HEADS

ROUTE_CHUNK = 256
SLOT_WIN = 64
SLOT_ALIGN = 16
VMEM_LIMIT = 60 * 1024 * 1024


def _cparams(sem):
    return pltpu.CompilerParams(dimension_semantics=sem, vmem_limit_bytes=VMEM_LIMIT)


def _dot(a, b):
    return jnp.dot(a, b, preferred_element_type=F32)


def _dot_nt(a, b):
    return lax.dot_general(a, b, (((1,), (1,)), ((), ())), preferred_element_type=F32)


def _dot_tn(a, b):
    return lax.dot_general(a, b, (((0,), (0,)), ((), ())), preferred_element_type=F32)


def _silu(x):
    return x * jax.nn.sigmoid(x)


def _mod_body(c_ref, w_ref, b_ref, o_ref):
    s = _silu(c_ref[...])
    o_ref[...] = _dot(s.astype(BF16), w_ref[...].astype(BF16)) + b_ref[...]


def _mod_call(cs, w_mod, b_mod):
    n = w_mod.shape[1] // D_MODEL
    return pl.pallas_call(
        _mod_body,
        out_shape=jax.ShapeDtypeStruct((8, w_mod.shape[1]), F32),
        grid=(n,),
        in_specs=[pl.BlockSpec((8, D_MODEL), lambda j: (0, 0)),
                  pl.BlockSpec((D_MODEL, D_MODEL), lambda j: (0, j)),
                  pl.BlockSpec((1, D_MODEL), lambda j: (0, j))],
        out_specs=pl.BlockSpec((8, D_MODEL), lambda j: (0, j)),
        compiler_params=_cparams(("arbitrary",)),
        name="mod",
    )(cs, w_mod, b_mod)


def _norm_mod(x, g, shift, scale):
    ms = jnp.mean(x * x, axis=-1, keepdims=True)
    return (x * lax.rsqrt(ms + NORM_EPS) * g) * (1.0 + scale) + shift


def _gate_streams(st, gp_ref):
    lane = lax.broadcasted_iota(I32, st.shape, 1)
    g = -jnp.exp(gp_ref[0:1, :]) * jax.nn.softplus(st + gp_ref[1:2, :])
    beta = jax.nn.sigmoid(st)
    return jnp.where(lane < 2 * N_HEADS, g, jnp.where(lane < STATE_COLS, beta, 0.0))


def _conv_qkv(qkv, prev_row, next_row, cw_ref, out_ref, tm):
    cs = GDN_CHUNK
    w0, w1, w2 = cw_ref[0:1, :], cw_ref[1:2, :], cw_ref[2:3, :]
    row8 = lax.broadcasted_iota(I32, (8, 1), 0)
    down = pltpu.roll(qkv, 1, 0)
    up = pltpu.roll(qkv, tm - 1, 0)
    down = jnp.concatenate([jnp.where(row8 == 0, prev_row, down[0:8]), down[8:]], axis=0)
    up = jnp.concatenate([up[:tm - 8], jnp.where(row8 == 7, next_row, up[tm - 8:])], axis=0)
    for c in range(tm // cs):
        rows = slice(c * cs, (c + 1) * cs)
        y = _silu(down[rows] * w0 + qkv[rows] * w1 + up[rows] * w2)
        for h in range(N_HEADS):
            cq = slice(h * HEAD_DIM, (h + 1) * HEAD_DIM)
            ck = slice(B_W + h * HEAD_DIM, B_W + (h + 1) * HEAD_DIM)
            q = y[:, cq]
            k = y[:, ck]
            out_ref[0, rows, cq] = (q * (lax.rsqrt(jnp.sum(q * q, axis=-1, keepdims=True) + NORM_EPS)
                                         * (HEAD_DIM ** -0.5))).astype(BF16)
            out_ref[0, rows, ck] = (k * lax.rsqrt(jnp.sum(k * k, axis=-1, keepdims=True) + NORM_EPS)).astype(BF16)
        out_ref[0, rows, 2 * B_W:3 * B_W] = y[:, 2 * B_W:3 * B_W].astype(BF16)


def _inproj_lat_body(x_ref, xp_ref, xn_ref, mod_ref, g1_ref, w_ref, gp_ref, cw_ref, gmg_ref, ws_ref, bst_ref,
                     qkv_ref, gb_ref, z_ref, ya_ref, *, tm):
    i = pl.program_id(1)
    shift, scale = mod_ref[0, 0:1, :], mod_ref[0, 1:2, :]
    a = _norm_mod(x_ref[0], g1_ref[...], shift, scale).astype(BF16)
    xh = jnp.concatenate([xp_ref[0], xn_ref[0]], axis=0)
    halo = _dot(_norm_mod(xh, g1_ref[...], shift, scale).astype(BF16), w_ref[:, 0:QKV_W])
    prev_row = jnp.where(i == 0, 0.0, halo[7:8, :])
    next_row = jnp.where(i == pl.num_programs(1) - 1, 0.0, halo[8:9, :])
    _conv_qkv(_dot(a, w_ref[:, 0:QKV_W]), prev_row, next_row, cw_ref, qkv_ref, tm)
    z_ref[0] = _dot(a, w_ref[:, QKV_W:QKV_W + B_W]).astype(BF16)
    c_uv = QKV_W + B_W
    gb_ref[0] = _gate_streams(_dot(a, w_ref[:, c_uv + 2 * A_W:c_uv + 2 * A_W + LANES]), gp_ref)
    uv = _dot(a, w_ref[:, c_uv:c_uv + 2 * A_W])
    uv = 0.5 * uv * (1.0 + lax.erf(uv * 0.7071067811865476))
    gd = A_W // A_GROUPS
    for grp in range(A_GROUPS):
        v = uv[:, A_W + grp * gd:A_W + (grp + 1) * gd]
        vn = v * lax.rsqrt(jnp.mean(v * v, axis=-1, keepdims=True) + NORM_EPS) * gmg_ref[:, grp * gd:(grp + 1) * gd]
        vn = vn.astype(BF16)
        bias = bst_ref[:, grp:grp + 1]
        for c in range(tm // A_CHUNK):
            rows = slice(c * A_CHUNK, (c + 1) * A_CHUNK)
            s = _dot(ws_ref[grp], vn[rows]) + bias
            ya_ref[0, rows, grp * gd:(grp + 1) * gd] = (uv[rows, grp * gd:(grp + 1) * gd] * s).astype(BF16)


def _inproj_ctx_body(x_ref, mod_ref, g1_ref, w_ref, gp_ref, cw_ref, qkv_ref, gb_ref, *, tm):
    a = _norm_mod(x_ref[0], g1_ref[...], mod_ref[0, 0:1, :], mod_ref[0, 1:2, :]).astype(BF16)
    edge = jnp.zeros((1, QKV_W), F32)
    _conv_qkv(_dot(a, w_ref[:, 0:QKV_W]), edge, edge, cw_ref, qkv_ref, tm)
    c_state = QKV_W + B_W + 2 * A_W
    gb_ref[0] = _gate_streams(_dot(a, w_ref[:, c_state:c_state + LANES]), gp_ref)


def _inproj_lat_call(x, mod3, g1, w_lat, gp, cw, gmg, ws16, bst, tm):
    bsz, t, _ = x.shape
    hb = tm // 8
    last8 = t // 8 - 1
    full = lambda a: pl.BlockSpec(a.shape, lambda b, i: (0,) * a.ndim)
    tok = lambda w: pl.BlockSpec((1, tm, w), lambda b, i: (b, i, 0))
    return pl.pallas_call(
        functools.partial(_inproj_lat_body, tm=tm),
        out_shape=(jax.ShapeDtypeStruct((bsz, t, QKV_W), BF16),
                   jax.ShapeDtypeStruct((bsz, t, LANES), F32),
                   jax.ShapeDtypeStruct((bsz, t, B_W), BF16),
                   jax.ShapeDtypeStruct((bsz, t, A_W), BF16)),
        grid=(bsz, t // tm),
        in_specs=[tok(D_MODEL),
                  pl.BlockSpec((1, 8, D_MODEL), lambda b, i: (b, jnp.maximum(i * hb - 1, 0), 0)),
                  pl.BlockSpec((1, 8, D_MODEL), lambda b, i: (b, jnp.minimum((i + 1) * hb, last8), 0)),
                  pl.BlockSpec((1, N_MOD, D_MODEL), lambda b, i: (b, 0, 0)),
                  full(g1), full(w_lat), full(gp), full(cw), full(gmg), full(ws16), full(bst)],
        out_specs=(tok(QKV_W), tok(LANES), tok(B_W), tok(A_W)),
        compiler_params=_cparams(("parallel", "arbitrary")),
        name="inproj_lat",
    )(x, x, x, mod3, g1, w_lat, gp, cw, gmg, ws16, bst)


def _inproj_ctx_call(ctx, mod3, ctx_row, g1, w_ctx, gp, cw):
    bsz, t, _ = ctx.shape
    full = lambda a: pl.BlockSpec(a.shape, lambda b: (0,) * a.ndim)
    tok = lambda w: pl.BlockSpec((1, t, w), lambda b: (b, 0, 0))
    return pl.pallas_call(
        functools.partial(_inproj_ctx_body, tm=t),
        out_shape=(jax.ShapeDtypeStruct((bsz, t, QKV_W), BF16),
                   jax.ShapeDtypeStruct((bsz, t, LANES), F32)),
        grid=(bsz,),
        in_specs=[tok(D_MODEL),
                  pl.BlockSpec((1, N_MOD, D_MODEL), lambda b: (ctx_row, 0, 0)),
                  full(g1), full(w_ctx), full(gp), full(cw)],
        out_specs=(tok(QKV_W), tok(LANES)),
        compiler_params=_cparams(("parallel",)),
        name="inproj_ctx",
    )(ctx, mod3, g1, w_ctx, gp, cw)


GDN_CHUNKS_PER_STEP = 8
GDN_STAGE_SKEW = 2
PACK_W = N_HEADS * GDN_CHUNK


def _per_head(tile, lanes, width):
    rows = tile.shape[0]
    if width == HEAD_DIM:
        return jnp.concatenate([jnp.broadcast_to(tile[:, l:l + 1], (rows, width)) for l in lanes], axis=1)
    head = lax.broadcasted_iota(I32, (rows, N_HEADS * width), 1) // width
    out = jnp.broadcast_to(tile[:, lanes[0]:lanes[0] + 1], (rows, N_HEADS * width))
    for h in range(1, N_HEADS):
        out = jnp.where(head == h, jnp.broadcast_to(tile[:, lanes[h]:lanes[h] + 1], (rows, N_HEADS * width)), out)
    return out


def _block_rows(x16, half_masks=None):
    rows, width = x16.shape
    per_head = width // N_HEADS
    zero = jnp.zeros((rows, LANES), x16.dtype)
    blocks = []
    for h in range(N_HEADS):
        tile = h * per_head // LANES
        kept = x16[:, tile * LANES:(tile + 1) * LANES]
        if per_head < LANES:
            kept = kept * half_masks[h * per_head % LANES // per_head]
        blocks.append(jnp.concatenate([kept if t == tile else zero for t in range(width // LANES)], axis=1))
    return jnp.concatenate(blocks, axis=0)


def _gdn_group_program(g, half_masks, turn):
    cs = GDN_CHUNK
    hd = HEAD_DIM
    q16, k16, v16, beta16, egc16 = g["q"], g["k"], g["v"], g["beta"], g["egc"]
    kb16 = k16 * beta16
    decay = jnp.where(g["incl"], jnp.exp(jnp.where(g["incl"], g["gc_col"] - g["gc_row"], 0.0)), 0.0)
    kk = _dot_nt(jnp.concatenate([kb16, q16], axis=0), _block_rows(k16))
    yield
    a = jnp.where(g["strict"], kk[:cs] * decay, 0.0)
    attn = (kk[cs:] * decay).astype(BF16)
    m = -jnp.where(g["levels"][0], a, 0.0)
    for lm in g["levels"][1:]:
        m16 = m.astype(BF16)
        cm = jnp.where(lm, a, 0.0)
        x = cm + _dot(m16, _block_rows(cm.astype(BF16), half_masks))
        yield
        y = x + _dot(x.astype(BF16), _block_rows(m16, half_masks))
        yield
        m = m - y
    t16 = (m + g["eye"]).astype(BF16)
    u = _dot(t16, _block_rows(v16 * beta16))
    w = _dot(t16, _block_rows(kb16 * egc16))
    qg16 = q16 * egc16
    kg = k16 * g["kdec"]
    yield
    while turn[g["backward"]] != g["order"]:
        yield
    s_ref = g["s_ref"]
    s = [s_ref[h] for h in range(N_HEADS)]
    wq = [_dot(jnp.concatenate([w[:, h * hd:(h + 1) * hd].astype(BF16), qg16[:, h * hd:(h + 1) * hd]], axis=0),
               s[h].astype(BF16)) for h in range(N_HEADS)]
    yield
    v_new = (u - jnp.concatenate([r[:cs] for r in wq], axis=1)).astype(BF16)
    o = jnp.concatenate([r[cs:] for r in wq], axis=1) + _dot(attn, _block_rows(v_new))
    for h in range(N_HEADS):
        cols = slice(h * hd, (h + 1) * hd)
        s_ref[h] = s[h] * g["eg"][h] + _dot_tn(kg[:, cols], v_new[:, cols])
    g["o_ref"][0, pl.ds(g["r0"], cs), :] = o.astype(BF16)
    turn[g["backward"]] += 1


def _run_interleaved(programs, skew):
    live = {}
    tick = 0
    while live or tick <= skew * (len(programs) - 1):
        if tick % skew == 0 and tick // skew < len(programs):
            live[tick // skew] = programs[tick // skew]
        for key in sorted(live):
            for prog in live[key]:
                if next(prog, "done") == "done":
                    live[key] = [p for p in live[key] if p is not prog]
            if not live[key]:
                del live[key]
        tick += 1


def _gdn_body(qf_ref, qb_ref, gbf_ref, gbb_ref, s0f_ref, s0b_ref,
              of_ref, ob_ref, sff_ref, sfb_ref, sf_scr, sb_scr, *, tb, nt):
    t = pl.program_id(1)
    cs = GDN_CHUNK
    nch = tb // cs
    cps = min(GDN_CHUNKS_PER_STEP, nch)
    nh = N_HEADS

    @pl.when(t == 0)
    def _():
        sf_scr[...] = s0f_ref[0]
        sb_scr[...] = s0b_ref[0]

    ii = lax.broadcasted_iota(I32, (cs, PACK_W), 0)
    jj = lax.broadcasted_iota(I32, (cs, PACK_W), 1) & (cs - 1)
    incl_f, strict_f = jj <= ii, jj < ii
    incl_b, strict_b = jj >= ii, jj > ii
    levels = []
    sh = 0
    while (1 << sh) < cs:
        levels.append(((ii >> (sh + 1)) == (jj >> (sh + 1))) & ((ii >> sh) != (jj >> sh)))
        sh += 1
    lv_f = [lm & strict_f for lm in levels]
    lv_b = [lm & strict_b for lm in levels]
    eye = (ii == jj).astype(F32)
    row = lax.broadcasted_iota(I32, (cs, LANES), 0)
    lane = lax.broadcasted_iota(I32, (1, LANES), 1)
    half_of_tile = lax.broadcasted_iota(I32, (cs, LANES), 1) // cs
    half_masks = [(half_of_tile == i).astype(BF16) for i in range(LANES // cs)]

    def group(q_ref, r0, gate, csum, gt, backward, s_ref, o_ref, order):
        l0 = nh if backward else 0
        lanes = [l0 + h for h in range(nh)]
        last = 0 if backward else cs - 1
        ld = lambda c0: q_ref[0, pl.ds(r0, cs), c0:c0 + B_W]
        halves = [gt[l:l + 1, :] if (h % 2 == 0) != backward else pltpu.roll(gt[l:l + 1, :], cs, 1)
                  for h, l in enumerate(lanes)]
        gc_row = jnp.concatenate([jnp.where(lane < cs, halves[0], halves[1]),
                                  jnp.where(lane < cs, halves[2], halves[3])], axis=1)
        glast = csum[last:last + 1, :]
        return dict(
            q=ld(0), k=ld(B_W), v=ld(2 * B_W),
            beta=_per_head(gate, [2 * nh + l for l in lanes], HEAD_DIM).astype(BF16),
            egc=_per_head(jnp.exp(csum), lanes, HEAD_DIM).astype(BF16),
            kdec=_per_head(jnp.exp(glast - csum), lanes, HEAD_DIM).astype(BF16), eye=eye,
            eg=[jnp.exp(csum[last:last + 1, l:l + 1]) for l in lanes],
            gc_col=_per_head(csum, lanes, cs), gc_row=gc_row,
            incl=incl_b if backward else incl_f, strict=strict_b if backward else strict_f,
            levels=lv_b if backward else lv_f, s_ref=s_ref, o_ref=o_ref, r0=r0, backward=backward, order=order)

    def step(n, carry):
        per_chunk = []
        for j in range(cps):
            rf = pl.multiple_of((n * cps + j) * cs, cs)
            rb = pl.multiple_of((nch - 1 - n * cps - j) * cs, cs)
            gf = gbf_ref[0, pl.ds(rf, cs), :]
            gb = gbb_ref[0, pl.ds(rb, cs), :]
            cf, cb = gf, gb
            s = 1
            while s < cs:
                cf = cf + jnp.where(row >= s, pltpu.roll(cf, s, 0), 0.0)
                cb = cb + jnp.where(row < cs - s, pltpu.roll(cb, cs - s, 0), 0.0)
                s *= 2
            gt = jnp.concatenate([cf, cb], axis=0).T
            per_chunk.append([group(qf_ref, rf, gf, cf, gt, False, sf_scr, of_ref, j),
                              group(qb_ref, rb, gb, cb, gt, True, sb_scr, ob_ref, j)])
        turn = {False: 0, True: 0}
        _run_interleaved([[_gdn_group_program(g, half_masks, turn) for g in pair] for pair in per_chunk],
                         GDN_STAGE_SKEW)
        return carry

    lax.fori_loop(0, nch // cps, step, 0)

    @pl.when(t == nt - 1)
    def _():
        sff_ref[0] = sf_scr[...]
        sfb_ref[0] = sb_scr[...]


def _gdn_call(qkv, gb, s0f, s0b, tb):
    bsz, t, _ = qkv.shape
    nt = t // tb
    assert (tb // GDN_CHUNK) % min(GDN_CHUNKS_PER_STEP, tb // GDN_CHUNK) == 0
    st = pl.BlockSpec((1, N_HEADS, HEAD_DIM, HEAD_DIM), lambda b, i: (b, 0, 0, 0))
    fwd = lambda w: pl.BlockSpec((1, tb, w), lambda b, i: (b, i, 0))
    bwd = lambda w: pl.BlockSpec((1, tb, w), lambda b, i: (b, nt - 1 - i, 0))
    return pl.pallas_call(
        functools.partial(_gdn_body, tb=tb, nt=nt),
        out_shape=(jax.ShapeDtypeStruct((bsz, t, B_W), BF16), jax.ShapeDtypeStruct((bsz, t, B_W), BF16),
                   jax.ShapeDtypeStruct((bsz, N_HEADS, HEAD_DIM, HEAD_DIM), F32),
                   jax.ShapeDtypeStruct((bsz, N_HEADS, HEAD_DIM, HEAD_DIM), F32)),
        grid=(bsz, nt),
        in_specs=[fwd(QKV_W), bwd(QKV_W), fwd(LANES), bwd(LANES), st, st],
        out_specs=(fwd(B_W), bwd(B_W), st, st),
        scratch_shapes=[pltpu.VMEM((N_HEADS, HEAD_DIM, HEAD_DIM), F32), pltpu.VMEM((N_HEADS, HEAD_DIM, HEAD_DIM), F32)],
        compiler_params=_cparams(("parallel", "arbitrary")),
        name="gdn",
    )(qkv, qkv, gb, gb, s0f, s0b)


def _mixout_body(x_ref, of_ref, ob_ref, z_ref, ya_ref, mod_ref, gng_ref, wout_ref, n2g_ref, wrh_ref, wrl_ref, br_ref,
                 h_ref, fin_ref, afft_ref, *, tm):
    o = of_ref[0].astype(F32) + ob_ref[0].astype(F32)
    z = z_ref[0].astype(F32)
    parts = [ya_ref[0]]
    for h in range(N_HEADS):
        c = slice(h * HEAD_DIM, (h + 1) * HEAD_DIM)
        oh = o[:, c]
        y = oh * lax.rsqrt(jnp.mean(oh * oh, axis=-1, keepdims=True) + NORM_EPS)
        parts.append((y * gng_ref[...] * _silu(z[:, c])).astype(BF16))
    mix = _dot(jnp.concatenate(parts, axis=1), wout_ref[...])
    hl = x_ref[0] + mod_ref[0, 2:3, :] * mix
    h_ref[0] = hl
    fin = _norm_mod(hl, n2g_ref[...], mod_ref[0, 3:4, :], mod_ref[0, 4:5, :])
    f_hi = fin.astype(BF16)
    fin_ref[0] = f_hi
    f_lo = (fin - f_hi.astype(F32)).astype(BF16)
    both = _dot(f_hi, wrl_ref[...])
    logits = both[:, :LANES] + _dot(f_lo, wrh_ref[...]) + both[:, LANES:] + br_ref[...]
    e = jnp.exp(logits - jnp.max(logits, axis=-1, keepdims=True))
    aff = e / jnp.sum(e, axis=-1, keepdims=True)
    for j in range(tm // LANES):
        afft_ref[0, j] = aff[j * LANES:(j + 1) * LANES, :].T[0:N_EXPERTS, :]


def _mixout_call(x, o_f, o_b, z, ya, mod3, gng, wout16, n2g, wr_hi, wr_lo, br, tm):
    bsz, t, _ = x.shape
    full = lambda a: pl.BlockSpec(a.shape, lambda b, i: (0,) * a.ndim)
    tok = lambda w: pl.BlockSpec((1, tm, w), lambda b, i: (b, i, 0))
    return pl.pallas_call(
        functools.partial(_mixout_body, tm=tm),
        out_shape=(jax.ShapeDtypeStruct((bsz, t, D_MODEL), F32), jax.ShapeDtypeStruct((bsz, t, D_MODEL), BF16),
                   jax.ShapeDtypeStruct((bsz, t // LANES, N_EXPERTS, LANES), F32)),
        grid=(bsz, t // tm),
        in_specs=[tok(D_MODEL), tok(B_W), tok(B_W), tok(B_W), tok(A_W),
                  pl.BlockSpec((1, N_MOD, D_MODEL), lambda b, i: (b, 0, 0)),
                  full(gng), full(wout16), full(n2g), full(wr_hi), full(wr_lo), full(br)],
        out_specs=(tok(D_MODEL), tok(D_MODEL),
                   pl.BlockSpec((1, tm // LANES, N_EXPERTS, LANES), lambda b, i: (b, i, 0, 0))),
        compiler_params=_cparams(("parallel", "parallel")),
        name="mixout",
    )(x, o_f, o_b, z, ya, mod3, gng, wout16, n2g, wr_hi, wr_lo, br)


def _route_body(afft_ref, slott_ref, off_ref, *, t, cap):
    ne = N_EXPERTS
    npieces = t // LANES
    rows = npieces * ne

    def count(thr_col, strict):
        acc = jnp.zeros((ne, LANES), I32)
        for p in range(npieces):
            piece = afft_ref[0, p * ne:(p + 1) * ne, :]
            acc = acc + (piece > thr_col if strict else piece >= thr_col).astype(I32)
        return jnp.sum(acc, axis=1, keepdims=True)

    def search(i, thr):
        cand = thr | jnp.left_shift(jnp.int32(1), 30 - i)
        return jnp.where(count(pltpu.bitcast(cand, F32), False) >= cap, cand, thr)

    thr_bits = lax.fori_loop(0, 31, search, jnp.zeros((ne, 1), I32))
    thr = pltpu.bitcast(thr_bits, F32)
    need = (cap - count(thr, True)).astype(F32)

    x = afft_ref[0]
    thr_rows = jnp.concatenate([thr] * npieces, axis=0)
    need_rows = jnp.concatenate([need] * npieces, axis=0)
    gt = x > thr_rows
    eq = x == thr_rows
    ti = lax.broadcasted_iota(I32, (LANES, LANES), 0)
    tj = lax.broadcasted_iota(I32, (LANES, LANES), 1)
    triu = (ti <= tj).astype(BF16)
    ri = lax.broadcasted_iota(I32, (rows, rows), 0)
    rj = lax.broadcasted_iota(I32, (rows, rows), 1)
    earlier = (((ri & (ne - 1)) == (rj & (ne - 1))) & (rj < ri)).astype(BF16)

    def prefix(mask):
        inpiece = _dot(mask.astype(BF16), triu)
        total = jnp.broadcast_to(inpiece[:, LANES - 1:LANES], (rows, LANES)).astype(BF16)
        offset = _dot(earlier, total)
        return inpiece + offset, offset

    eq_rank, _ = prefix(eq)
    sel = gt | (eq & (eq_rank <= need_rows))
    sel_rank, sel_off = prefix(sel)
    slott_ref[0] = jnp.where(sel, sel_rank - 1.0, -1.0).astype(I32)
    off_ref[0] = sel_off.astype(I32)


def _route_call(afft, cap):
    bsz, rows, _ = afft.shape
    t = rows // N_EXPERTS * LANES
    spec = lambda r: pl.BlockSpec((1, r, LANES), lambda b: (b, 0, 0))
    return pl.pallas_call(
        functools.partial(_route_body, t=t, cap=cap),
        out_shape=(jax.ShapeDtypeStruct((bsz, rows, LANES), I32),
                   jax.ShapeDtypeStruct((bsz, rows, LANES), I32)),
        grid=(bsz,),
        in_specs=[spec(rows)],
        out_specs=(spec(rows), spec(rows)),
        compiler_params=_cparams(("parallel",)),
        name="route",
    )(afft)


def _window_plan(base_ref, flat0, experts):
    starts, rounds = [], jnp.int32(0)
    for e in experts:
        lo = base_ref[flat0 + e]
        hi = base_ref[flat0 + N_EXPERTS + e]
        lo_al = (lo >> 4) << 4
        starts.append(lo_al)
        rounds = jnp.maximum(rounds, (hi - lo_al + SLOT_WIN - 1) // SLOT_WIN)
    return starts, rounds


def _window_start(start, r, cap):
    return pl.multiple_of(jnp.minimum(start + r * SLOT_WIN, cap), SLOT_ALIGN)


def _dispatch_body(base_ref, slott_ref, fin_ref, xe_ref, *, nchunk, sub, eh_n, cap):
    b, eh, ci = pl.program_id(0), pl.program_id(1), pl.program_id(2)
    rc = ROUTE_CHUNK

    @pl.when(ci == 0)
    def _():
        xe_ref[...] = jnp.zeros_like(xe_ref)

    srow = lax.broadcasted_iota(I32, (SLOT_WIN, rc), 0)
    for sc in range(sub):
        cc = ci * sub + sc
        flat0 = (b * (nchunk + 1) + cc) * N_EXPERTS + eh * eh_n
        f = fin_ref[0, sc * rc:(sc + 1) * rc, :]
        experts = list(range(eh_n))
        starts, rounds = _window_plan(base_ref, flat0, experts)

        def one_round(r, carry, starts=starts, f=f, sc=sc):
            rows = []
            wstart = [_window_start(starts[e], r, cap) for e in experts]
            for e in experts:
                tok_slot = jnp.concatenate(
                    [slott_ref[0, sc * (rc // LANES) + j, e:e + 1, :] for j in range(rc // LANES)], axis=1)
                rows.append((tok_slot == srow + wstart[e]).astype(BF16))
            prod = _dot(jnp.concatenate(rows, axis=0), f)
            for e in experts:
                win = pl.ds(wstart[e], SLOT_WIN)
                xe_ref[0, e, win, :] = xe_ref[0, e, win, :] + prod[e * SLOT_WIN:(e + 1) * SLOT_WIN].astype(BF16)
            return carry

        one_round(jnp.int32(0), 0)
        lax.fori_loop(1, rounds, one_round, 0)


def _dispatch_call(base_flat, slott, fin, cap):
    bsz, t, _ = fin.shape
    nchunk = t // ROUTE_CHUNK
    sub = 8
    eh_n = N_EXPERTS // 2
    sp = cap + SLOT_WIN
    grid_spec = pltpu.PrefetchScalarGridSpec(
        num_scalar_prefetch=1,
        grid=(bsz, N_EXPERTS // eh_n, nchunk // sub),
        in_specs=[pl.BlockSpec((1, sub * ROUTE_CHUNK // LANES, eh_n, LANES), lambda b, eh, ci, base: (b, ci, eh, 0)),
                  pl.BlockSpec((1, sub * ROUTE_CHUNK, D_MODEL), lambda b, eh, ci, base: (b, ci, 0))],
        out_specs=pl.BlockSpec((1, eh_n, sp, D_MODEL), lambda b, eh, ci, base: (b, eh, 0, 0)))
    return pl.pallas_call(
        functools.partial(_dispatch_body, nchunk=nchunk, sub=sub, eh_n=eh_n, cap=cap),
        out_shape=jax.ShapeDtypeStruct((bsz, N_EXPERTS, sp, D_MODEL), BF16),
        grid_spec=grid_spec,
        compiler_params=_cparams(("parallel", "parallel", "arbitrary")),
        name="dispatch",
    )(base_flat, slott, fin)


def _experts_body(xe_ref, wg_ref, wu_ref, wd_ref, y_ref, *, cap):
    x = xe_ref[0, 0, 0:cap, :]
    ft = 256
    acc = None
    for f in range(EXPERT_FF // ft):
        cols = slice(f * ft, (f + 1) * ft)
        wg16 = wg_ref[0, :, cols].astype(BF16)
        wu16 = wu_ref[0, :, cols].astype(BF16)
        wd16 = wd_ref[0, cols, :].astype(BF16)
        hid = (_silu(_dot(x, wg16)) * _dot(x, wu16)).astype(BF16)
        part = _dot(hid, wd16)
        acc = part if acc is None else acc + part
    y_ref[0, 0, 0:cap, :] = acc.astype(BF16)
    y_ref[0, 0, cap:, :] = jnp.zeros((y_ref.shape[2] - cap, D_MODEL), BF16)


def _experts_call(xe, w_gate, w_up, w_down, cap):
    bsz, _, sp, _ = xe.shape
    wspec = lambda shape: pl.BlockSpec((1,) + shape, lambda e, b: (e, 0, 0))
    slots = pl.BlockSpec((1, 1, sp, D_MODEL), lambda e, b: (b, e, 0, 0))
    return pl.pallas_call(
        functools.partial(_experts_body, cap=cap),
        out_shape=jax.ShapeDtypeStruct(xe.shape, BF16),
        grid=(N_EXPERTS, bsz),
        in_specs=[slots, wspec((D_MODEL, EXPERT_FF)), wspec((D_MODEL, EXPERT_FF)), wspec((EXPERT_FF, D_MODEL))],
        out_specs=slots,
        compiler_params=_cparams(("parallel", "parallel")),
        name="experts",
    )(xe, w_gate, w_up, w_down)


def _combine_body(base_ref, slott_ref, afft_ref, h_ref, y_ref, mod_ref, fng_ref, o_ref, acc_ref, *, nchunk, sub, cap):
    b, ci = pl.program_id(0), pl.program_id(1)
    rc = ROUTE_CHUNK
    pieces = rc // LANES
    srow = lax.broadcasted_iota(I32, (SLOT_WIN, rc), 0)
    experts = list(range(N_EXPERTS))
    for sc in range(sub):
        rows = slice(sc * rc, (sc + 1) * rc)
        flat0 = (b * (nchunk + 1) + ci * sub + sc) * N_EXPERTS
        starts, rounds = _window_plan(base_ref, flat0, experts)
        tok_slot = [jnp.concatenate([slott_ref[0, sc * pieces + j, e:e + 1, :] for j in range(pieces)], axis=1)
                    for e in experts]
        tok_gate = [jnp.concatenate([afft_ref[0, sc * pieces + j, e:e + 1, :] for j in range(pieces)], axis=1)
                    for e in experts]

        def contribution(r, starts=starts, tok_slot=tok_slot, tok_gate=tok_gate):
            wstart = [_window_start(starts[e], r, cap) for e in experts]
            ywin = jnp.concatenate([y_ref[0, e, pl.ds(wstart[e], SLOT_WIN), :] for e in experts], axis=0)
            st = jnp.concatenate([jnp.where(tok_slot[e] == srow + wstart[e], tok_gate[e], 0.0).astype(BF16)
                                  for e in experts], axis=0)
            return _dot_tn(st, ywin)

        def extra_round(r, carry, contribution=contribution):
            acc_ref[...] += contribution(r)
            return carry

        acc_ref[...] = contribution(jnp.int32(0))
        lax.fori_loop(1, rounds, extra_round, 0)
        hl = h_ref[0, rows, :] + mod_ref[0, 5:6, :] * acc_ref[...]
        ms = jnp.mean(hl * hl, axis=-1, keepdims=True)
        o_ref[0, rows, :] = hl * lax.rsqrt(ms + NORM_EPS) * fng_ref[...]


def _combine_call(base_flat, slott, afft, h, y, mod3, fng):
    bsz, t, _ = h.shape
    nchunk = t // ROUTE_CHUNK
    sub = 4
    rc = ROUTE_CHUNK
    tok = lambda w: pl.BlockSpec((1, sub * rc, w), lambda b, i, base: (b, i, 0))
    piece = pl.BlockSpec((1, sub * rc // LANES, N_EXPERTS, LANES), lambda b, i, base: (b, i, 0, 0))
    grid_spec = pltpu.PrefetchScalarGridSpec(
        num_scalar_prefetch=1,
        grid=(bsz, nchunk // sub),
        in_specs=[piece, piece, tok(D_MODEL),
                  pl.BlockSpec((1,) + y.shape[1:], lambda b, i, base: (b, 0, 0, 0), pipeline_mode=pl.Buffered(1)),
                  pl.BlockSpec((1, N_MOD, D_MODEL), lambda b, i, base: (b, 0, 0)),
                  pl.BlockSpec(fng.shape, lambda b, i, base: (0, 0))],
        out_specs=tok(D_MODEL),
        scratch_shapes=[pltpu.VMEM((rc, D_MODEL), F32)])
    return pl.pallas_call(
        functools.partial(_combine_body, nchunk=nchunk, sub=sub, cap=y.shape[2] - SLOT_WIN),
        out_shape=jax.ShapeDtypeStruct(h.shape, F32),
        grid_spec=grid_spec,
        compiler_params=_cparams(("parallel", "arbitrary")),
        name="combine",
    )(base_flat, slott, afft, h, y, mod3, fng)


def _pad_lanes(a):
    return jnp.pad(a, ((0, 0), (0, LANES - a.shape[1])))


def kernel(x, c, ctx, c_ctx, w_mod, b_mod, norm1_g, norm2_g, w_in, conv_w, a_log, dt_bias, gdn_norm_g, gm_norm_g,
           gm_ws, gm_bs, w_out, w_router, b_router, w_gate, w_up, w_down, final_norm_g):
    bsz, t, _ = x.shape
    ctx_len = ctx.shape[1]
    assert w_mod.shape[0] == 1, "single-layer problem"
    assert t % 2048 == 0 and ctx_len % GDN_CHUNK == 0 and bsz < 8
    cap = EC_CAPACITY * t // N_EXPERTS

    cs = jnp.zeros((8, D_MODEL), F32).at[:bsz].set(c).at[bsz].set(c_ctx)
    mod3 = _mod_call(cs, w_mod[0], b_mod[0][None, :]).reshape(8, N_MOD, D_MODEL)

    wl = w_in[0]
    n_state = QKV_W + STATE_COLS
    w_state = _pad_lanes(wl[:, QKV_W:n_state])
    w_lat = jnp.concatenate([wl[:, :QKV_W], wl[:, n_state:n_state + B_W], wl[:, n_state + B_W:], w_state],
                            axis=1).astype(BF16)
    gp = jnp.zeros((8, LANES), F32).at[0, :2 * N_HEADS].set(a_log[0].reshape(-1)).at[1, :2 * N_HEADS].set(
        dt_bias[0].reshape(-1))
    g1 = norm1_g[0][None, :]
    cw = jnp.zeros((8, QKV_W), F32).at[:conv_w.shape[1]].set(conv_w[0])

    qkv_c, gb_c = _inproj_ctx_call(ctx, mod3, bsz, g1, w_lat, gp, cw)
    zero_state = jnp.zeros((bsz, N_HEADS, HEAD_DIM, HEAD_DIM), F32)
    _, _, s_f, s_b = _gdn_call(qkv_c, gb_c, zero_state, zero_state, ctx_len)

    qkv, gb, z, ya = _inproj_lat_call(x, mod3, g1, w_lat, gp, cw, gm_norm_g[0][None, :], gm_ws[0].astype(BF16),
                                      _pad_lanes(gm_bs[0].T), 1024)
    o_f, o_b, _, _ = _gdn_call(qkv, gb, s_f, s_b, 2048)
    wr = _pad_lanes(w_router[0])
    wr_hi = wr.astype(BF16)
    wr_lo = jnp.concatenate([wr_hi, (wr - wr_hi.astype(F32)).astype(BF16)], axis=1)
    br = jnp.full((1, LANES), -1e30, F32).at[0, :N_EXPERTS].set(b_router[0])
    h, fin, afft = _mixout_call(x, o_f, o_b, z, ya, mod3, gdn_norm_g[0][None, :], w_out[0].astype(BF16),
                                     norm2_g[0][None, :], wr_hi, wr_lo, br, 1024)

    npieces = t // LANES
    slott, off = _route_call(afft.reshape(bsz, npieces * N_EXPERTS, LANES), cap)
    slott = slott.reshape(bsz, npieces, N_EXPERTS, LANES)
    base = off[:, :, 0].reshape(bsz, npieces, N_EXPERTS)[:, ::ROUTE_CHUNK // LANES, :]
    base_flat = jnp.concatenate([base, jnp.full((bsz, 1, N_EXPERTS), cap, I32)], axis=1).reshape(-1)
    xe = _dispatch_call(base_flat, slott, fin, cap)
    y = _experts_call(xe, w_gate[0], w_up[0], w_down[0], cap)
    return _combine_call(base_flat, slott, afft, h, y, mod3, final_norm_g[None, :])
```

```python
import functools

import jax
import jax.numpy as jnp
from jax import lax
from jax.experimental import pallas as pl
from jax.experimental.pallas import tpu as pltpu

F32 = jnp.float32
BF16 = jnp.bfloat16
I32 = jnp.int32

D_MODEL = 1024
N_MOD = 6
N_HEADS = 4
HEAD_DIM = 128
B_W = N_HEADS * HEAD_DIM
QKV_W = 3 * B_W
A_W = 512
A_GROUPS = 4
A_CHUNK = 128
GDN_CHUNK = 64
N_EXPERTS = 16
EC_CAPACITY = 2
EXPERT_FF = 1024
NORM_EPS = 1e-6
LANES = 128
STATE_COLS = 4 * N_HEADS

ROUTE_CHUNK = 256
SLOT_WIN = 64
SLOT_ALIGN = 16
VMEM_LIMIT = 60 * 1024 * 1024


def _cparams(sem):
    return pltpu.CompilerParams(dimension_semantics=sem, vmem_limit_bytes=VMEM_LIMIT)


def _dot(a, b):
    return jnp.dot(a, b, preferred_element_type=F32)


def _dot_nt(a, b):
    return lax.dot_general(a, b, (((1,), (1,)), ((), ())), preferred_element_type=F32)


def _dot_tn(a, b):
    return lax.dot_general(a, b, (((0,), (0,)), ((), ())), preferred_element_type=F32)


def _silu(x):
    return x * jax.nn.sigmoid(x)


def _mod_body(c_ref, w_ref, b_ref, o_ref):
    s = _silu(c_ref[...])
    o_ref[...] = _dot(s.astype(BF16), w_ref[...].astype(BF16)) + b_ref[...]


def _mod_call(cs, w_mod, b_mod):
    n = w_mod.shape[1] // D_MODEL
    return pl.pallas_call(
        _mod_body,
        out_shape=jax.ShapeDtypeStruct((8, w_mod.shape[1]), F32),
        grid=(n,),
        in_specs=[pl.BlockSpec((8, D_MODEL), lambda j: (0, 0)),
                  pl.BlockSpec((D_MODEL, D_MODEL), lambda j: (0, j)),
                  pl.BlockSpec((1, D_MODEL), lambda j: (0, j))],
        out_specs=pl.BlockSpec((8, D_MODEL), lambda j: (0, j)),
        compiler_params=_cparams(("arbitrary",)),
        name="mod",
    )(cs, w_mod, b_mod)


def _norm_mod(x, g, shift, scale):
    ms = jnp.mean(x * x, axis=-1, keepdims=True)
    return (x * lax.rsqrt(ms + NORM_EPS) * g) * (1.0 + scale) + shift


def _gate_streams(st, gp_ref):
    lane = lax.broadcasted_iota(I32, st.shape, 1)
    g = -jnp.exp(gp_ref[0:1, :]) * jax.nn.softplus(st + gp_ref[1:2, :])
    beta = jax.nn.sigmoid(st)
    return jnp.where(lane < 2 * N_HEADS, g, jnp.where(lane < STATE_COLS, beta, 0.0))


def _conv_qkv(qkv, prev_row, next_row, cw_ref, out_ref, tm):
    cs = GDN_CHUNK
    nsub = tm // cs
    w0, w1, w2 = cw_ref[0:1, :], cw_ref[1:2, :], cw_ref[2:3, :]
    row = lax.broadcasted_iota(I32, (cs, 1), 0)
    for c in range(nsub):
        rows = slice(c * cs, (c + 1) * cs)
        x = qkv[rows]
        prow = prev_row if c == 0 else qkv[c * cs - 1:c * cs]
        nrow = next_row if c == nsub - 1 else qkv[(c + 1) * cs:(c + 1) * cs + 1]
        xp = jnp.where(row == 0, prow, pltpu.roll(x, 1, 0))
        xn = jnp.where(row == cs - 1, nrow, pltpu.roll(x, cs - 1, 0))
        y = _silu(xp * w0 + x * w1 + xn * w2)
        for h in range(N_HEADS):
            cq = slice(h * HEAD_DIM, (h + 1) * HEAD_DIM)
            ck = slice(B_W + h * HEAD_DIM, B_W + (h + 1) * HEAD_DIM)
            q = y[:, cq]
            k = y[:, ck]
            out_ref[0, rows, cq] = (q * (lax.rsqrt(jnp.sum(q * q, axis=-1, keepdims=True) + NORM_EPS)
                                         * (HEAD_DIM ** -0.5))).astype(BF16)
            out_ref[0, rows, ck] = (k * lax.rsqrt(jnp.sum(k * k, axis=-1, keepdims=True) + NORM_EPS)).astype(BF16)
        out_ref[0, rows, 2 * B_W:3 * B_W] = y[:, 2 * B_W:3 * B_W].astype(BF16)


def _inproj_lat_body(x_ref, xp_ref, xn_ref, mod_ref, g1_ref, w_ref, gp_ref, cw_ref, gmg_ref, ws_ref, bst_ref,
                     qkv_ref, gb_ref, z_ref, ya_ref, *, tm):
    i = pl.program_id(1)
    shift, scale = mod_ref[0, 0:1, :], mod_ref[0, 1:2, :]
    a = _norm_mod(x_ref[0], g1_ref[...], shift, scale).astype(BF16)
    xh = jnp.concatenate([xp_ref[0], xn_ref[0]], axis=0)
    halo = _dot(_norm_mod(xh, g1_ref[...], shift, scale).astype(BF16), w_ref[:, 0:QKV_W])
    prev_row = jnp.where(i == 0, 0.0, halo[7:8, :])
    next_row = jnp.where(i == pl.num_programs(1) - 1, 0.0, halo[8:9, :])
    _conv_qkv(_dot(a, w_ref[:, 0:QKV_W]), prev_row, next_row, cw_ref, qkv_ref, tm)
    z_ref[0] = _dot(a, w_ref[:, QKV_W:QKV_W + B_W]).astype(BF16)
    c_uv = QKV_W + B_W
    gb_ref[0] = _gate_streams(_dot(a, w_ref[:, c_uv + 2 * A_W:c_uv + 2 * A_W + LANES]), gp_ref)
    uv = _dot(a, w_ref[:, c_uv:c_uv + 2 * A_W])
    uv = 0.5 * uv * (1.0 + lax.erf(uv * 0.7071067811865476))
    gd = A_W // A_GROUPS
    for grp in range(A_GROUPS):
        v = uv[:, A_W + grp * gd:A_W + (grp + 1) * gd]
        vn = v * lax.rsqrt(jnp.mean(v * v, axis=-1, keepdims=True) + NORM_EPS) * gmg_ref[:, grp * gd:(grp + 1) * gd]
        vn = vn.astype(BF16)
        bias = bst_ref[:, grp:grp + 1]
        for c in range(tm // A_CHUNK):
            rows = slice(c * A_CHUNK, (c + 1) * A_CHUNK)
            s = _dot(ws_ref[grp], vn[rows]) + bias
            ya_ref[0, rows, grp * gd:(grp + 1) * gd] = (uv[rows, grp * gd:(grp + 1) * gd] * s).astype(BF16)


def _inproj_ctx_body(x_ref, mod_ref, g1_ref, w_ref, gp_ref, cw_ref, qkv_ref, gb_ref, *, tm):
    a = _norm_mod(x_ref[0], g1_ref[...], mod_ref[0, 0:1, :], mod_ref[0, 1:2, :]).astype(BF16)
    edge = jnp.zeros((1, QKV_W), F32)
    _conv_qkv(_dot(a, w_ref[:, 0:QKV_W]), edge, edge, cw_ref, qkv_ref, tm)
    c_state = QKV_W + B_W + 2 * A_W
    gb_ref[0] = _gate_streams(_dot(a, w_ref[:, c_state:c_state + LANES]), gp_ref)


def _inproj_lat_call(x, mod3, g1, w_lat, gp, cw, gmg, ws16, bst, tm):
    bsz, t, _ = x.shape
    hb = tm // 8
    last8 = t // 8 - 1
    full = lambda a: pl.BlockSpec(a.shape, lambda b, i: (0,) * a.ndim)
    tok = lambda w: pl.BlockSpec((1, tm, w), lambda b, i: (b, i, 0))
    return pl.pallas_call(
        functools.partial(_inproj_lat_body, tm=tm),
        out_shape=(jax.ShapeDtypeStruct((bsz, t, QKV_W), BF16),
                   jax.ShapeDtypeStruct((bsz, t, LANES), F32),
                   jax.ShapeDtypeStruct((bsz, t, B_W), BF16),
                   jax.ShapeDtypeStruct((bsz, t, A_W), BF16)),
        grid=(bsz, t // tm),
        in_specs=[tok(D_MODEL),
                  pl.BlockSpec((1, 8, D_MODEL), lambda b, i: (b, jnp.maximum(i * hb - 1, 0), 0)),
                  pl.BlockSpec((1, 8, D_MODEL), lambda b, i: (b, jnp.minimum((i + 1) * hb, last8), 0)),
                  pl.BlockSpec((1, N_MOD, D_MODEL), lambda b, i: (b, 0, 0)),
                  full(g1), full(w_lat), full(gp), full(cw), full(gmg), full(ws16), full(bst)],
        out_specs=(tok(QKV_W), tok(LANES), tok(B_W), tok(A_W)),
        compiler_params=_cparams(("parallel", "arbitrary")),
        name="inproj_lat",
    )(x, x, x, mod3, g1, w_lat, gp, cw, gmg, ws16, bst)


def _inproj_ctx_call(ctx, mod3, ctx_row, g1, w_ctx, gp, cw):
    bsz, t, _ = ctx.shape
    full = lambda a: pl.BlockSpec(a.shape, lambda b: (0,) * a.ndim)
    tok = lambda w: pl.BlockSpec((1, t, w), lambda b: (b, 0, 0))
    return pl.pallas_call(
        functools.partial(_inproj_ctx_body, tm=t),
        out_shape=(jax.ShapeDtypeStruct((bsz, t, QKV_W), BF16),
                   jax.ShapeDtypeStruct((bsz, t, LANES), F32)),
        grid=(bsz,),
        in_specs=[tok(D_MODEL),
                  pl.BlockSpec((1, N_MOD, D_MODEL), lambda b: (ctx_row, 0, 0)),
                  full(g1), full(w_ctx), full(gp), full(cw)],
        out_specs=(tok(QKV_W), tok(LANES)),
        compiler_params=_cparams(("parallel",)),
        name="inproj_ctx",
    )(ctx, mod3, g1, w_ctx, gp, cw)


GDN_CHUNKS_PER_STEP = 8
GDN_STAGE_SKEW = 2
PACK_W = N_HEADS * GDN_CHUNK


def _per_head(tile, lanes, width):
    rows = tile.shape[0]
    if width == HEAD_DIM:
        return jnp.concatenate([jnp.broadcast_to(tile[:, l:l + 1], (rows, width)) for l in lanes], axis=1)
    head = lax.broadcasted_iota(I32, (rows, N_HEADS * width), 1) // width
    out = jnp.broadcast_to(tile[:, lanes[0]:lanes[0] + 1], (rows, N_HEADS * width))
    for h in range(1, N_HEADS):
        out = jnp.where(head == h, jnp.broadcast_to(tile[:, lanes[h]:lanes[h] + 1], (rows, N_HEADS * width)), out)
    return out


def _block_rows(x16, half_masks=None):
    rows, width = x16.shape
    per_head = width // N_HEADS
    zero = jnp.zeros((rows, LANES), x16.dtype)
    blocks = []
    for h in range(N_HEADS):
        tile = h * per_head // LANES
        kept = x16[:, tile * LANES:(tile + 1) * LANES]
        if per_head < LANES:
            kept = kept * half_masks[h * per_head % LANES // per_head]
        blocks.append(jnp.concatenate([kept if t == tile else zero for t in range(width // LANES)], axis=1))
    return jnp.concatenate(blocks, axis=0)


def _gdn_group_program(g, half_masks, turn):
    cs = GDN_CHUNK
    hd = HEAD_DIM
    q16, k16, v16, beta16, egc16 = g["q"], g["k"], g["v"], g["beta"], g["egc"]
    kb16 = k16 * beta16
    decay = jnp.where(g["incl"], jnp.exp(g["gc_col"] - g["gc_row"]), 0.0)
    kk = _dot_nt(jnp.concatenate([kb16, q16], axis=0), _block_rows(k16))
    yield
    a = jnp.where(g["strict"], kk[:cs] * decay, 0.0)
    attn = (kk[cs:] * decay).astype(BF16)
    m = -jnp.where(g["levels"][0], a, 0.0)
    for lm in g["levels"][1:]:
        m16 = m.astype(BF16)
        cm = jnp.where(lm, a, 0.0)
        x = cm + _dot(m16, _block_rows(cm.astype(BF16), half_masks))
        yield
        y = x + _dot(x.astype(BF16), _block_rows(m16, half_masks))
        yield
        m = m - y
    t16 = (m + g["eye"]).astype(BF16)
    u = _dot(t16, _block_rows(v16 * beta16))
    w = _dot(t16, _block_rows(kb16 * egc16))
    qg16 = q16 * egc16
    kg = k16 * g["kdec"]
    yield
    while turn[g["backward"]] != g["order"]:
        yield
    s_ref = g["s_ref"]
    s = [s_ref[h] for h in range(N_HEADS)]
    wq = [_dot(jnp.concatenate([w[:, h * hd:(h + 1) * hd].astype(BF16), qg16[:, h * hd:(h + 1) * hd]], axis=0),
               s[h].astype(BF16)) for h in range(N_HEADS)]
    yield
    v_new = (u - jnp.concatenate([r[:cs] for r in wq], axis=1)).astype(BF16)
    o = jnp.concatenate([r[cs:] for r in wq], axis=1) + _dot(attn, _block_rows(v_new))
    for h in range(N_HEADS):
        cols = slice(h * hd, (h + 1) * hd)
        s_ref[h] = s[h] * g["eg"][h] + _dot_tn(kg[:, cols], v_new[:, cols])
    g["o_ref"][0, pl.ds(g["r0"], cs), :] = o.astype(BF16)
    turn[g["backward"]] += 1


def _run_interleaved(programs, skew):
    live = {}
    tick = 0
    while live or tick <= skew * (len(programs) - 1):
        if tick % skew == 0 and tick // skew < len(programs):
            live[tick // skew] = programs[tick // skew]
        for key in sorted(live):
            for prog in live[key]:
                if next(prog, "done") == "done":
                    live[key] = [p for p in live[key] if p is not prog]
            if not live[key]:
                del live[key]
        tick += 1


def _gdn_body(qf_ref, qb_ref, gbf_ref, gbb_ref, s0f_ref, s0b_ref,
              of_ref, ob_ref, sff_ref, sfb_ref, sf_scr, sb_scr, *, tb, nt):
    t = pl.program_id(1)
    cs = GDN_CHUNK
    nch = tb // cs
    cps = min(GDN_CHUNKS_PER_STEP, nch)
    nh = N_HEADS

    @pl.when(t == 0)
    def _():
        sf_scr[...] = s0f_ref[0]
        sb_scr[...] = s0b_ref[0]

    ii = lax.broadcasted_iota(I32, (cs, PACK_W), 0)
    jj = lax.broadcasted_iota(I32, (cs, PACK_W), 1) & (cs - 1)
    incl_f, strict_f = jj <= ii, jj < ii
    incl_b, strict_b = jj >= ii, jj > ii
    levels = []
    sh = 0
    while (1 << sh) < cs:
        levels.append(((ii >> (sh + 1)) == (jj >> (sh + 1))) & ((ii >> sh) != (jj >> sh)))
        sh += 1
    lv_f = [lm & strict_f for lm in levels]
    lv_b = [lm & strict_b for lm in levels]
    eye = (ii == jj).astype(F32)
    row = lax.broadcasted_iota(I32, (cs, LANES), 0)
    lane = lax.broadcasted_iota(I32, (1, LANES), 1)
    half_of_tile = lax.broadcasted_iota(I32, (cs, LANES), 1) // cs
    half_masks = [(half_of_tile == i).astype(BF16) for i in range(LANES // cs)]

    def group(q_ref, r0, gate, csum, gt, backward, s_ref, o_ref, order):
        l0 = nh if backward else 0
        lanes = [l0 + h for h in range(nh)]
        last = 0 if backward else cs - 1
        ld = lambda c0: q_ref[0, pl.ds(r0, cs), c0:c0 + B_W]
        halves = [gt[l:l + 1, :] if (h % 2 == 0) != backward else pltpu.roll(gt[l:l + 1, :], cs, 1)
                  for h, l in enumerate(lanes)]
        gc_row = jnp.concatenate([jnp.where(lane < cs, halves[0], halves[1]),
                                  jnp.where(lane < cs, halves[2], halves[3])], axis=1)
        glast = csum[last:last + 1, :]
        return dict(
            q=ld(0), k=ld(B_W), v=ld(2 * B_W),
            beta=_per_head(gate, [2 * nh + l for l in lanes], HEAD_DIM).astype(BF16),
            egc=_per_head(jnp.exp(csum), lanes, HEAD_DIM).astype(BF16),
            kdec=_per_head(jnp.exp(glast - csum), lanes, HEAD_DIM).astype(BF16), eye=eye,
            eg=[jnp.exp(csum[last:last + 1, l:l + 1]) for l in lanes],
            gc_col=_per_head(csum, lanes, cs), gc_row=gc_row,
            incl=incl_b if backward else incl_f, strict=strict_b if backward else strict_f,
            levels=lv_b if backward else lv_f, s_ref=s_ref, o_ref=o_ref, r0=r0, backward=backward, order=order)

    def step(n, carry):
        per_chunk = []
        for j in range(cps):
            rf = pl.multiple_of((n * cps + j) * cs, cs)
            rb = pl.multiple_of((nch - 1 - n * cps - j) * cs, cs)
            gf = gbf_ref[0, pl.ds(rf, cs), :]
            gb = gbb_ref[0, pl.ds(rb, cs), :]
            cf, cb = gf, gb
            s = 1
            while s < cs:
                cf = cf + jnp.where(row >= s, pltpu.roll(cf, s, 0), 0.0)
                cb = cb + jnp.where(row < cs - s, pltpu.roll(cb, cs - s, 0), 0.0)
                s *= 2
            gt = jnp.concatenate([cf, cb], axis=0).T
            per_chunk.append([group(qf_ref, rf, gf, cf, gt, False, sf_scr, of_ref, j),
                              group(qb_ref, rb, gb, cb, gt, True, sb_scr, ob_ref, j)])
        turn = {False: 0, True: 0}
        _run_interleaved([[_gdn_group_program(g, half_masks, turn) for g in pair] for pair in per_chunk],
                         GDN_STAGE_SKEW)
        return carry

    lax.fori_loop(0, nch // cps, step, 0)

    @pl.when(t == nt - 1)
    def _():
        sff_ref[0] = sf_scr[...]
        sfb_ref[0] = sb_scr[...]


def _gdn_call(qkv, gb, s0f, s0b, tb):
    bsz, t, _ = qkv.shape
    nt = t // tb
    assert (tb // GDN_CHUNK) % min(GDN_CHUNKS_PER_STEP, tb // GDN_CHUNK) == 0
    st = pl.BlockSpec((1, N_HEADS, HEAD_DIM, HEAD_DIM), lambda b, i: (b, 0, 0, 0))
    fwd = lambda w: pl.BlockSpec((1, tb, w), lambda b, i: (b, i, 0))
    bwd = lambda w: pl.BlockSpec((1, tb, w), lambda b, i: (b, nt - 1 - i, 0))
    return pl.pallas_call(
        functools.partial(_gdn_body, tb=tb, nt=nt),
        out_shape=(jax.ShapeDtypeStruct((bsz, t, B_W), BF16), jax.ShapeDtypeStruct((bsz, t, B_W), BF16),
                   jax.ShapeDtypeStruct((bsz, N_HEADS, HEAD_DIM, HEAD_DIM), F32),
                   jax.ShapeDtypeStruct((bsz, N_HEADS, HEAD_DIM, HEAD_DIM), F32)),
        grid=(bsz, nt),
        in_specs=[fwd(QKV_W), bwd(QKV_W), fwd(LANES), bwd(LANES), st, st],
        out_specs=(fwd(B_W), bwd(B_W), st, st),
        scratch_shapes=[pltpu.VMEM((N_HEADS, HEAD_DIM, HEAD_DIM), F32), pltpu.VMEM((N_HEADS, HEAD_DIM, HEAD_DIM), F32)],
        compiler_params=_cparams(("parallel", "arbitrary")),
        name="gdn",
    )(qkv, qkv, gb, gb, s0f, s0b)


def _mixout_body(x_ref, of_ref, ob_ref, z_ref, ya_ref, mod_ref, gng_ref, wout_ref, n2g_ref, wrh_ref, wrl_ref, br_ref,
                 h_ref, fin_ref, afft_ref, *, tm):
    o = of_ref[0].astype(F32) + ob_ref[0].astype(F32)
    z = z_ref[0].astype(F32)
    parts = [ya_ref[0]]
    for h in range(N_HEADS):
        c = slice(h * HEAD_DIM, (h + 1) * HEAD_DIM)
        oh = o[:, c]
        y = oh * lax.rsqrt(jnp.mean(oh * oh, axis=-1, keepdims=True) + NORM_EPS)
        parts.append((y * gng_ref[...] * _silu(z[:, c])).astype(BF16))
    mix = _dot(jnp.concatenate(parts, axis=1), wout_ref[...])
    hl = x_ref[0] + mod_ref[0, 2:3, :] * mix
    h_ref[0] = hl
    fin = _norm_mod(hl, n2g_ref[...], mod_ref[0, 3:4, :], mod_ref[0, 4:5, :])
    f_hi = fin.astype(BF16)
    fin_ref[0] = f_hi
    f_lo = (fin - f_hi.astype(F32)).astype(BF16)
    both = _dot(f_hi, wrl_ref[...])
    logits = both[:, :LANES] + _dot(f_lo, wrh_ref[...]) + both[:, LANES:] + br_ref[...]
    e = jnp.exp(logits - jnp.max(logits, axis=-1, keepdims=True))
    aff = e / jnp.sum(e, axis=-1, keepdims=True)
    for j in range(tm // LANES):
        afft_ref[0, j] = aff[j * LANES:(j + 1) * LANES, :].T[0:N_EXPERTS, :]


def _mixout_call(x, o_f, o_b, z, ya, mod3, gng, wout16, n2g, wr_hi, wr_lo, br, tm):
    bsz, t, _ = x.shape
    full = lambda a: pl.BlockSpec(a.shape, lambda b, i: (0,) * a.ndim)
    tok = lambda w: pl.BlockSpec((1, tm, w), lambda b, i: (b, i, 0))
    return pl.pallas_call(
        functools.partial(_mixout_body, tm=tm),
        out_shape=(jax.ShapeDtypeStruct((bsz, t, D_MODEL), F32), jax.ShapeDtypeStruct((bsz, t, D_MODEL), BF16),
                   jax.ShapeDtypeStruct((bsz, t // LANES, N_EXPERTS, LANES), F32)),
        grid=(bsz, t // tm),
        in_specs=[tok(D_MODEL), tok(B_W), tok(B_W), tok(B_W), tok(A_W),
                  pl.BlockSpec((1, N_MOD, D_MODEL), lambda b, i: (b, 0, 0)),
                  full(gng), full(wout16), full(n2g), full(wr_hi), full(wr_lo), full(br)],
        out_specs=(tok(D_MODEL), tok(D_MODEL),
                   pl.BlockSpec((1, tm // LANES, N_EXPERTS, LANES), lambda b, i: (b, i, 0, 0))),
        compiler_params=_cparams(("parallel", "parallel")),
        name="mixout",
    )(x, o_f, o_b, z, ya, mod3, gng, wout16, n2g, wr_hi, wr_lo, br)


def _route_body(afft_ref, slott_ref, off_ref, *, t, cap):
    ne = N_EXPERTS
    npieces = t // LANES
    rows = npieces * ne

    def count(thr_col, strict):
        acc = jnp.zeros((ne, LANES), I32)
        for p in range(npieces):
            piece = afft_ref[0, p * ne:(p + 1) * ne, :]
            acc = acc + (piece > thr_col if strict else piece >= thr_col).astype(I32)
        return jnp.sum(acc, axis=1, keepdims=True)

    def search(i, thr):
        cand = thr | jnp.left_shift(jnp.int32(1), 30 - i)
        return jnp.where(count(pltpu.bitcast(cand, F32), False) >= cap, cand, thr)

    thr_bits = lax.fori_loop(0, 31, search, jnp.zeros((ne, 1), I32))
    thr = pltpu.bitcast(thr_bits, F32)
    need = (cap - count(thr, True)).astype(F32)

    x = afft_ref[0]
    thr_rows = jnp.concatenate([thr] * npieces, axis=0)
    need_rows = jnp.concatenate([need] * npieces, axis=0)
    gt = x > thr_rows
    eq = x == thr_rows
    ti = lax.broadcasted_iota(I32, (LANES, LANES), 0)
    tj = lax.broadcasted_iota(I32, (LANES, LANES), 1)
    triu = (ti <= tj).astype(BF16)
    ri = lax.broadcasted_iota(I32, (rows, rows), 0)
    rj = lax.broadcasted_iota(I32, (rows, rows), 1)
    earlier = (((ri & (ne - 1)) == (rj & (ne - 1))) & (rj < ri)).astype(BF16)

    def prefix(mask):
        inpiece = _dot(mask.astype(BF16), triu)
        total = jnp.broadcast_to(inpiece[:, LANES - 1:LANES], (rows, LANES)).astype(BF16)
        offset = _dot(earlier, total)
        return inpiece + offset, offset

    eq_rank, _ = prefix(eq)
    sel = gt | (eq & (eq_rank <= need_rows))
    sel_rank, sel_off = prefix(sel)
    slott_ref[0] = jnp.where(sel, sel_rank - 1.0, -1.0).astype(I32)
    off_ref[0] = sel_off.astype(I32)


def _route_call(afft, cap):
    bsz, rows, _ = afft.shape
    t = rows // N_EXPERTS * LANES
    spec = lambda r: pl.BlockSpec((1, r, LANES), lambda b: (b, 0, 0))
    return pl.pallas_call(
        functools.partial(_route_body, t=t, cap=cap),
        out_shape=(jax.ShapeDtypeStruct((bsz, rows, LANES), I32),
                   jax.ShapeDtypeStruct((bsz, rows, LANES), I32)),
        grid=(bsz,),
        in_specs=[spec(rows)],
        out_specs=(spec(rows), spec(rows)),
        compiler_params=_cparams(("parallel",)),
        name="route",
    )(afft)


def _window_plan(base_ref, flat0, experts):
    starts, rounds = [], jnp.int32(0)
    for e in experts:
        lo = base_ref[flat0 + e]
        hi = base_ref[flat0 + N_EXPERTS + e]
        lo_al = lo & -SLOT_ALIGN
        starts.append(lo_al)
        rounds = jnp.maximum(rounds, (hi - lo_al + SLOT_WIN - 1) // SLOT_WIN)
    return starts, rounds


def _window_start(start, r, cap):
    return pl.multiple_of(jnp.minimum(start + r * SLOT_WIN, cap), SLOT_ALIGN)


def _dispatch_body(base_ref, slott_ref, fin_ref, xe_ref, *, nchunk, sub, eh_n, cap):
    b, eh, ci = pl.program_id(0), pl.program_id(1), pl.program_id(2)
    rc = ROUTE_CHUNK

    @pl.when(ci == 0)
    def _():
        xe_ref[...] = jnp.zeros_like(xe_ref)

    srow = lax.broadcasted_iota(I32, (SLOT_WIN, rc), 0)
    for sc in range(sub):
        cc = ci * sub + sc
        flat0 = (b * (nchunk + 1) + cc) * N_EXPERTS + eh * eh_n
        f = fin_ref[0, sc * rc:(sc + 1) * rc, :]
        experts = list(range(eh_n))
        starts, rounds = _window_plan(base_ref, flat0, experts)

        def one_round(r, carry, starts=starts, f=f, sc=sc):
            rows = []
            wstart = [_window_start(starts[e], r, cap) for e in experts]
            for e in experts:
                tok_slot = jnp.concatenate(
                    [slott_ref[0, sc * (rc // LANES) + j, e:e + 1, :] for j in range(rc // LANES)], axis=1)
                rows.append((tok_slot == srow + wstart[e]).astype(BF16))
            prod = _dot(jnp.concatenate(rows, axis=0), f)
            for e in experts:
                win = pl.ds(wstart[e], SLOT_WIN)
                xe_ref[0, e, win, :] = xe_ref[0, e, win, :] + prod[e * SLOT_WIN:(e + 1) * SLOT_WIN].astype(BF16)
            return carry

        one_round(jnp.int32(0), 0)
        lax.fori_loop(1, rounds, one_round, 0)


def _dispatch_call(base_flat, slott, fin, cap):
    bsz, t, _ = fin.shape
    nchunk = t // ROUTE_CHUNK
    sub = 8
    eh_n = N_EXPERTS // 2
    sp = cap + SLOT_WIN
    grid_spec = pltpu.PrefetchScalarGridSpec(
        num_scalar_prefetch=1,
        grid=(bsz, N_EXPERTS // eh_n, nchunk // sub),
        in_specs=[pl.BlockSpec((1, sub * ROUTE_CHUNK // LANES, eh_n, LANES), lambda b, eh, ci, base: (b, ci, eh, 0)),
                  pl.BlockSpec((1, sub * ROUTE_CHUNK, D_MODEL), lambda b, eh, ci, base: (b, ci, 0))],
        out_specs=pl.BlockSpec((1, eh_n, sp, D_MODEL), lambda b, eh, ci, base: (b, eh, 0, 0)))
    return pl.pallas_call(
        functools.partial(_dispatch_body, nchunk=nchunk, sub=sub, eh_n=eh_n, cap=cap),
        out_shape=jax.ShapeDtypeStruct((bsz, N_EXPERTS, sp, D_MODEL), BF16),
        grid_spec=grid_spec,
        compiler_params=_cparams(("parallel", "parallel", "arbitrary")),
        name="dispatch",
    )(base_flat, slott, fin)


def _experts_body(xe_ref, wg_ref, wu_ref, wd_ref, y_ref, *, cap):
    x = xe_ref[0, 0, 0:cap, :]
    ft = 256
    acc = None
    for f in range(EXPERT_FF // ft):
        cols = slice(f * ft, (f + 1) * ft)
        wg16 = wg_ref[0, :, cols].astype(BF16)
        wu16 = wu_ref[0, :, cols].astype(BF16)
        wd16 = wd_ref[0, cols, :].astype(BF16)
        hid = (_silu(_dot(x, wg16)) * _dot(x, wu16)).astype(BF16)
        part = _dot(hid, wd16)
        acc = part if acc is None else acc + part
    y_ref[0, 0, 0:cap, :] = acc.astype(BF16)
    y_ref[0, 0, cap:, :] = jnp.zeros((y_ref.shape[2] - cap, D_MODEL), BF16)


def _experts_call(xe, w_gate, w_up, w_down, cap):
    bsz, _, sp, _ = xe.shape
    wspec = lambda shape: pl.BlockSpec((1,) + shape, lambda e, b: (e, 0, 0))
    slots = pl.BlockSpec((1, 1, sp, D_MODEL), lambda e, b: (b, e, 0, 0))
    return pl.pallas_call(
        functools.partial(_experts_body, cap=cap),
        out_shape=jax.ShapeDtypeStruct(xe.shape, BF16),
        grid=(N_EXPERTS, bsz),
        in_specs=[slots, wspec((D_MODEL, EXPERT_FF)), wspec((D_MODEL, EXPERT_FF)), wspec((EXPERT_FF, D_MODEL))],
        out_specs=slots,
        compiler_params=_cparams(("parallel", "parallel")),
        name="experts",
    )(xe, w_gate, w_up, w_down)


def _combine_body(base_ref, slott_ref, afft_ref, h_ref, y_ref, mod_ref, fng_ref, o_ref, acc_ref, *, nchunk, sub, cap):
    b, ci = pl.program_id(0), pl.program_id(1)
    rc = ROUTE_CHUNK
    pieces = rc // LANES
    srow = lax.broadcasted_iota(I32, (SLOT_WIN, rc), 0)
    experts = list(range(N_EXPERTS))
    for sc in range(sub):
        rows = slice(sc * rc, (sc + 1) * rc)
        flat0 = (b * (nchunk + 1) + ci * sub + sc) * N_EXPERTS
        starts, rounds = _window_plan(base_ref, flat0, experts)
        tok_slot = [jnp.concatenate([slott_ref[0, sc * pieces + j, e:e + 1, :] for j in range(pieces)], axis=1)
                    for e in experts]
        tok_gate = [jnp.concatenate([afft_ref[0, sc * pieces + j, e:e + 1, :] for j in range(pieces)], axis=1)
                    for e in experts]

        def contribution(r, starts=starts, tok_slot=tok_slot, tok_gate=tok_gate):
            wstart = [_window_start(starts[e], r, cap) for e in experts]
            ywin = jnp.concatenate([y_ref[0, e, pl.ds(wstart[e], SLOT_WIN), :] for e in experts], axis=0)
            st = jnp.concatenate([jnp.where(tok_slot[e] == srow + wstart[e], tok_gate[e], 0.0).astype(BF16)
                                  for e in experts], axis=0)
            return _dot_tn(st, ywin)

        def extra_round(r, carry, contribution=contribution):
            acc_ref[...] += contribution(r)
            return carry

        acc_ref[...] = contribution(jnp.int32(0))
        lax.fori_loop(1, rounds, extra_round, 0)
        hl = h_ref[0, rows, :] + mod_ref[0, 5:6, :] * acc_ref[...]
        ms = jnp.mean(hl * hl, axis=-1, keepdims=True)
        o_ref[0, rows, :] = hl * lax.rsqrt(ms + NORM_EPS) * fng_ref[...]


def _combine_call(base_flat, slott, afft, h, y, mod3, fng):
    bsz, t, _ = h.shape
    nchunk = t // ROUTE_CHUNK
    sub = 4
    rc = ROUTE_CHUNK
    tok = lambda w: pl.BlockSpec((1, sub * rc, w), lambda b, i, base: (b, i, 0))
    piece = pl.BlockSpec((1, sub * rc // LANES, N_EXPERTS, LANES), lambda b, i, base: (b, i, 0, 0))
    grid_spec = pltpu.PrefetchScalarGridSpec(
        num_scalar_prefetch=1,
        grid=(bsz, nchunk // sub),
        in_specs=[piece, piece, tok(D_MODEL),
                  pl.BlockSpec((1,) + y.shape[1:], lambda b, i, base: (b, 0, 0, 0), pipeline_mode=pl.Buffered(1)),
                  pl.BlockSpec((1, N_MOD, D_MODEL), lambda b, i, base: (b, 0, 0)),
                  pl.BlockSpec(fng.shape, lambda b, i, base: (0, 0))],
        out_specs=tok(D_MODEL),
        scratch_shapes=[pltpu.VMEM((rc, D_MODEL), F32)])
    return pl.pallas_call(
        functools.partial(_combine_body, nchunk=nchunk, sub=sub, cap=y.shape[2] - SLOT_WIN),
        out_shape=jax.ShapeDtypeStruct(h.shape, F32),
        grid_spec=grid_spec,
        compiler_params=_cparams(("parallel", "arbitrary")),
        name="combine",
    )(base_flat, slott, afft, h, y, mod3, fng)


def _pad_lanes(a):
    return jnp.pad(a, ((0, 0), (0, LANES - a.shape[1])))


def kernel(x, c, ctx, c_ctx, w_mod, b_mod, norm1_g, norm2_g, w_in, conv_w, a_log, dt_bias, gdn_norm_g, gm_norm_g,
           gm_ws, gm_bs, w_out, w_router, b_router, w_gate, w_up, w_down, final_norm_g):
    bsz, t, _ = x.shape
    ctx_len = ctx.shape[1]
    assert w_mod.shape[0] == 1, "single-layer problem"
    assert t % 2048 == 0 and ctx_len % GDN_CHUNK == 0 and bsz < 8
    cap = EC_CAPACITY * t // N_EXPERTS

    cs = jnp.zeros((8, D_MODEL), F32).at[:bsz].set(c).at[bsz].set(c_ctx)
    mod3 = _mod_call(cs, w_mod[0], b_mod[0][None, :]).reshape(8, N_MOD, D_MODEL)

    wl = w_in[0]
    n_state = QKV_W + STATE_COLS
    w_state = _pad_lanes(wl[:, QKV_W:n_state])
    w_lat = jnp.concatenate([wl[:, :QKV_W], wl[:, n_state:n_state + B_W], wl[:, n_state + B_W:], w_state],
                            axis=1).astype(BF16)
    gp = jnp.zeros((8, LANES), F32).at[0, :2 * N_HEADS].set(a_log[0].reshape(-1)).at[1, :2 * N_HEADS].set(
        dt_bias[0].reshape(-1))
    g1 = norm1_g[0][None, :]
    cw = jnp.zeros((8, QKV_W), F32).at[:conv_w.shape[1]].set(conv_w[0])

    qkv_c, gb_c = _inproj_ctx_call(ctx, mod3, bsz, g1, w_lat, gp, cw)
    zero_state = jnp.zeros((bsz, N_HEADS, HEAD_DIM, HEAD_DIM), F32)
    _, _, s_f, s_b = _gdn_call(qkv_c, gb_c, zero_state, zero_state, ctx_len)

    qkv, gb, z, ya = _inproj_lat_call(x, mod3, g1, w_lat, gp, cw, gm_norm_g[0][None, :], gm_ws[0].astype(BF16),
                                      _pad_lanes(gm_bs[0].T), 1024)
    o_f, o_b, _, _ = _gdn_call(qkv, gb, s_f, s_b, 1024)
    wr = _pad_lanes(w_router[0])
    wr_hi = wr.astype(BF16)
    wr_lo = jnp.concatenate([wr_hi, (wr - wr_hi.astype(F32)).astype(BF16)], axis=1)
    br = jnp.full((1, LANES), -1e30, F32).at[0, :N_EXPERTS].set(b_router[0])
    h, fin, afft = _mixout_call(x, o_f, o_b, z, ya, mod3, gdn_norm_g[0][None, :], w_out[0].astype(BF16),
                                     norm2_g[0][None, :], wr_hi, wr_lo, br, 1024)

    npieces = t // LANES
    slott, off = _route_call(afft.reshape(bsz, npieces * N_EXPERTS, LANES), cap)
    slott = slott.reshape(bsz, npieces, N_EXPERTS, LANES)
    base = off[:, :, 0].reshape(bsz, npieces, N_EXPERTS)[:, ::ROUTE_CHUNK // LANES, :]
    base_flat = jnp.concatenate([base, jnp.full((bsz, 1, N_EXPERTS), cap, I32)], axis=1).reshape(-1)
    xe = _dispatch_call(base_flat, slott, fin, cap)
    y = _experts_call(xe, w_gate[0], w_up[0], w_down[0], cap)
    return _combine_call(base_flat, slott, afft, h, y, mod3, final_norm_g[None, :])
```

```python
import functools

import jax
import jax.numpy as jnp
from jax import lax
from jax.experimental import pallas as pl
from jax.experimental.pallas import tpu as pltpu

F32 = jnp.float32
BF16 = jnp.bfloat16
I32 = jnp.int32

D_MODEL = 1024
N_MOD = 6
N_HEADS = 4
HEAD_DIM = 128
B_W = N_HEADS * HEAD_DIM
QKV_W = 3 * B_W
A_W = 512
A_GROUPS = 4
A_CHUNK = 128
GDN_CHUNK = 64
N_EXPERTS = 16
EC_CAPACITY = 2
EXPERT_FF = 1024
NORM_EPS = 1e-6
LANES = 128
STATE_COLS = 4 * N_HEADS

ROUTE_CHUNK = 256
SLOT_WIN = 64
SLOT_ALIGN = 16
VMEM_LIMIT = 60 * 1024 * 1024


def _cparams(sem):
    return pltpu.CompilerParams(dimension_semantics=sem, vmem_limit_bytes=VMEM_LIMIT)


def _dot(a, b):
    return jnp.dot(a, b, preferred_element_type=F32)


def _dot_nt(a, b):
    return lax.dot_general(a, b, (((1,), (1,)), ((), ())), preferred_element_type=F32)


def _dot_tn(a, b):
    return lax.dot_general(a, b, (((0,), (0,)), ((), ())), preferred_element_type=F32)


def _silu(x):
    return x * jax.nn.sigmoid(x)


def _mod_body(c_ref, w_ref, b_ref, o_ref):
    s = _silu(c_ref[...])
    o_ref[...] = _dot(s.astype(BF16), w_ref[...].astype(BF16)) + b_ref[...]


def _mod_call(cs, w_mod, b_mod):
    n = w_mod.shape[1] // D_MODEL
    return pl.pallas_call(
        _mod_body,
        out_shape=jax.ShapeDtypeStruct((8, w_mod.shape[1]), F32),
        grid=(n,),
        in_specs=[pl.BlockSpec((8, D_MODEL), lambda j: (0, 0)),
                  pl.BlockSpec((D_MODEL, D_MODEL), lambda j: (0, j)),
                  pl.BlockSpec((1, D_MODEL), lambda j: (0, j))],
        out_specs=pl.BlockSpec((8, D_MODEL), lambda j: (0, j)),
        compiler_params=_cparams(("arbitrary",)),
        name="mod",
    )(cs, w_mod, b_mod)


def _norm_mod(x, g, shift, scale):
    ms = jnp.mean(x * x, axis=-1, keepdims=True)
    return (x * lax.rsqrt(ms + NORM_EPS) * g) * (1.0 + scale) + shift


def _gate_streams(st, gp_ref):
    lane = lax.broadcasted_iota(I32, st.shape, 1)
    g = -jnp.exp(gp_ref[0:1, :]) * jax.nn.softplus(st + gp_ref[1:2, :])
    beta = jax.nn.sigmoid(st)
    return jnp.where(lane < 2 * N_HEADS, g, jnp.where(lane < STATE_COLS, beta, 0.0))


def _conv_qkv(qkv, prev_row, next_row, cw_ref, out_ref, tm, row0=0):
    cs = GDN_CHUNK
    nsub = tm // cs
    w0, w1, w2 = cw_ref[0:1, :], cw_ref[1:2, :], cw_ref[2:3, :]
    row = lax.broadcasted_iota(I32, (cs, 1), 0)
    for c in range(nsub):
        rows = slice(c * cs, (c + 1) * cs)
        orows = slice(row0 + c * cs, row0 + (c + 1) * cs)
        x = qkv[rows]
        prow = prev_row if c == 0 else qkv[c * cs - 1:c * cs]
        nrow = next_row if c == nsub - 1 else qkv[(c + 1) * cs:(c + 1) * cs + 1]
        xp = jnp.where(row == 0, prow, pltpu.roll(x, 1, 0))
        xn = jnp.where(row == cs - 1, nrow, pltpu.roll(x, cs - 1, 0))
        y = _silu(xp * w0 + x * w1 + xn * w2)
        for h in range(N_HEADS):
            cq = slice(h * HEAD_DIM, (h + 1) * HEAD_DIM)
            ck = slice(B_W + h * HEAD_DIM, B_W + (h + 1) * HEAD_DIM)
            q = y[:, cq]
            k = y[:, ck]
            out_ref[0, orows, cq] = (q * (lax.rsqrt(jnp.sum(q * q, axis=-1, keepdims=True) + NORM_EPS)
                                         * (HEAD_DIM ** -0.5))).astype(BF16)
            out_ref[0, orows, ck] = (k * lax.rsqrt(jnp.sum(k * k, axis=-1, keepdims=True) + NORM_EPS)).astype(BF16)
        out_ref[0, orows, 2 * B_W:3 * B_W] = y[:, 2 * B_W:3 * B_W].astype(BF16)


def _inproj_lat_body(x_ref, xp_ref, xn_ref, mod_ref, g1_ref, w_ref, gp_ref, cw_ref, gmg_ref, ws_ref, bst_ref,
                     qkv_ref, gb_ref, z_ref, ya_ref, *, tm):
    i = pl.program_id(1)
    shift, scale = mod_ref[0, 0:1, :], mod_ref[0, 1:2, :]
    xh = jnp.concatenate([xp_ref[0], xn_ref[0]], axis=0)
    halo = _dot(_norm_mod(xh, g1_ref[...], shift, scale).astype(BF16), w_ref[:, 0:QKV_W])
    prev_row = jnp.where(i == 0, 0.0, halo[7:8, :])
    next_row = jnp.where(i == pl.num_programs(1) - 1, 0.0, halo[8:9, :])
    c_uv = QKV_W + B_W
    gd = A_W // A_GROUPS
    nsub = 2
    ts = tm // nsub
    raw = []
    for sidx in range(nsub):
        a = _norm_mod(x_ref[0, sidx * ts:(sidx + 1) * ts, :], g1_ref[...], shift, scale).astype(BF16)
        raw.append(dict(qkv=_dot(a, w_ref[:, 0:QKV_W]), z=_dot(a, w_ref[:, QKV_W:QKV_W + B_W]),
                        st=_dot(a, w_ref[:, c_uv + 2 * A_W:c_uv + 2 * A_W + LANES]),
                        uv=_dot(a, w_ref[:, c_uv:c_uv + 2 * A_W])))
    for sidx, r in enumerate(raw):
        r0 = sidx * ts
        before = prev_row if sidx == 0 else raw[sidx - 1]["qkv"][ts - 1:ts]
        after = next_row if sidx == nsub - 1 else raw[sidx + 1]["qkv"][0:1]
        _conv_qkv(r["qkv"], before, after, cw_ref, qkv_ref, ts, row0=r0)
        z_ref[0, r0:r0 + ts, :] = r["z"].astype(BF16)
        gb_ref[0, r0:r0 + ts, :] = _gate_streams(r["st"], gp_ref)
        uv = r["uv"]
        uv = 0.5 * uv * (1.0 + lax.erf(uv * 0.7071067811865476))
        for grp in range(A_GROUPS):
            v = uv[:, A_W + grp * gd:A_W + (grp + 1) * gd]
            vn = v * lax.rsqrt(jnp.mean(v * v, axis=-1, keepdims=True) + NORM_EPS) * gmg_ref[:, grp * gd:(grp + 1) * gd]
            vn = vn.astype(BF16)
            bias = bst_ref[:, grp:grp + 1]
            for c in range(ts // A_CHUNK):
                rows = slice(c * A_CHUNK, (c + 1) * A_CHUNK)
                orows = slice(r0 + c * A_CHUNK, r0 + (c + 1) * A_CHUNK)
                sg = _dot(ws_ref[grp], vn[rows]) + bias
                ya_ref[0, orows, grp * gd:(grp + 1) * gd] = (uv[rows, grp * gd:(grp + 1) * gd] * sg).astype(BF16)


def _inproj_ctx_body(x_ref, mod_ref, g1_ref, w_ref, gp_ref, cw_ref, qkv_ref, gb_ref, *, tm):
    a = _norm_mod(x_ref[0], g1_ref[...], mod_ref[0, 0:1, :], mod_ref[0, 1:2, :]).astype(BF16)
    edge = jnp.zeros((1, QKV_W), F32)
    _conv_qkv(_dot(a, w_ref[:, 0:QKV_W]), edge, edge, cw_ref, qkv_ref, tm)
    c_state = QKV_W + B_W + 2 * A_W
    gb_ref[0] = _gate_streams(_dot(a, w_ref[:, c_state:c_state + LANES]), gp_ref)


def _inproj_lat_call(x, mod3, g1, w_lat, gp, cw, gmg, ws16, bst, tm):
    bsz, t, _ = x.shape
    hb = tm // 8
    last8 = t // 8 - 1
    full = lambda a: pl.BlockSpec(a.shape, lambda b, i: (0,) * a.ndim)
    tok = lambda w: pl.BlockSpec((1, tm, w), lambda b, i: (b, i, 0))
    return pl.pallas_call(
        functools.partial(_inproj_lat_body, tm=tm),
        out_shape=(jax.ShapeDtypeStruct((bsz, t, QKV_W), BF16),
                   jax.ShapeDtypeStruct((bsz, t, LANES), F32),
                   jax.ShapeDtypeStruct((bsz, t, B_W), BF16),
                   jax.ShapeDtypeStruct((bsz, t, A_W), BF16)),
        grid=(bsz, t // tm),
        in_specs=[tok(D_MODEL),
                  pl.BlockSpec((1, 8, D_MODEL), lambda b, i: (b, jnp.maximum(i * hb - 1, 0), 0)),
                  pl.BlockSpec((1, 8, D_MODEL), lambda b, i: (b, jnp.minimum((i + 1) * hb, last8), 0)),
                  pl.BlockSpec((1, N_MOD, D_MODEL), lambda b, i: (b, 0, 0)),
                  full(g1), full(w_lat), full(gp), full(cw), full(gmg), full(ws16), full(bst)],
        out_specs=(tok(QKV_W), tok(LANES), tok(B_W), tok(A_W)),
        compiler_params=_cparams(("parallel", "arbitrary")),
        name="inproj_lat",
    )(x, x, x, mod3, g1, w_lat, gp, cw, gmg, ws16, bst)


def _inproj_ctx_call(ctx, mod3, ctx_row, g1, w_ctx, gp, cw):
    bsz, t, _ = ctx.shape
    full = lambda a: pl.BlockSpec(a.shape, lambda b: (0,) * a.ndim)
    tok = lambda w: pl.BlockSpec((1, t, w), lambda b: (b, 0, 0))
    return pl.pallas_call(
        functools.partial(_inproj_ctx_body, tm=t),
        out_shape=(jax.ShapeDtypeStruct((bsz, t, QKV_W), BF16),
                   jax.ShapeDtypeStruct((bsz, t, LANES), F32)),
        grid=(bsz,),
        in_specs=[tok(D_MODEL),
                  pl.BlockSpec((1, N_MOD, D_MODEL), lambda b: (ctx_row, 0, 0)),
                  full(g1), full(w_ctx), full(gp), full(cw)],
        out_specs=(tok(QKV_W), tok(LANES)),
        compiler_params=_cparams(("parallel",)),
        name="inproj_ctx",
    )(ctx, mod3, g1, w_ctx, gp, cw)


GDN_CHUNKS_PER_STEP = 8
GDN_STAGE_SKEW = 2
PACK_W = N_HEADS * GDN_CHUNK


def _per_head(tile, lanes, width):
    rows = tile.shape[0]
    if width == HEAD_DIM:
        return jnp.concatenate([jnp.broadcast_to(tile[:, l:l + 1], (rows, width)) for l in lanes], axis=1)
    head = lax.broadcasted_iota(I32, (rows, N_HEADS * width), 1) // width
    out = jnp.broadcast_to(tile[:, lanes[0]:lanes[0] + 1], (rows, N_HEADS * width))
    for h in range(1, N_HEADS):
        out = jnp.where(head == h, jnp.broadcast_to(tile[:, lanes[h]:lanes[h] + 1], (rows, N_HEADS * width)), out)
    return out


def _block_rows(x16, half_masks=None):
    rows, width = x16.shape
    per_head = width // N_HEADS
    zero = jnp.zeros((rows, LANES), x16.dtype)
    blocks = []
    for h in range(N_HEADS):
        tile = h * per_head // LANES
        kept = x16[:, tile * LANES:(tile + 1) * LANES]
        if per_head < LANES:
            kept = kept * half_masks[h * per_head % LANES // per_head]
        blocks.append(jnp.concatenate([kept if t == tile else zero for t in range(width // LANES)], axis=1))
    return jnp.concatenate(blocks, axis=0)


def _gdn_group_program(g, half_masks, turn):
    cs = GDN_CHUNK
    hd = HEAD_DIM
    q16, k16, v16, beta16, egc16 = g["q"], g["k"], g["v"], g["beta"], g["egc"]
    kb16 = k16 * beta16
    decay = jnp.where(g["incl"], jnp.exp(g["gc_col"] - g["gc_row"]), 0.0)
    kk = _dot_nt(jnp.concatenate([kb16, q16], axis=0), _block_rows(k16))
    yield
    a = jnp.where(g["strict"], kk[:cs] * decay, 0.0)
    attn = (kk[cs:] * decay).astype(BF16)
    m = -jnp.where(g["levels"][0], a, 0.0)
    for lm in g["levels"][1:]:
        m16 = m.astype(BF16)
        cm = jnp.where(lm, a, 0.0)
        x = cm + _dot(m16, _block_rows(cm.astype(BF16), half_masks))
        yield
        y = x + _dot(x.astype(BF16), _block_rows(m16, half_masks))
        yield
        m = m - y
    t16 = (m + g["eye"]).astype(BF16)
    u = _dot(t16, _block_rows(v16 * beta16))
    w = _dot(t16, _block_rows(kb16 * egc16))
    qg16 = q16 * egc16
    kg = k16 * g["kdec"]
    yield
    while turn[g["backward"]] != g["order"]:
        yield
    s_ref = g["s_ref"]
    s = [s_ref[h] for h in range(N_HEADS)]
    wq = [_dot(jnp.concatenate([w[:, h * hd:(h + 1) * hd].astype(BF16), qg16[:, h * hd:(h + 1) * hd]], axis=0),
               s[h].astype(BF16)) for h in range(N_HEADS)]
    yield
    v_new = (u - jnp.concatenate([r[:cs] for r in wq], axis=1)).astype(BF16)
    o = jnp.concatenate([r[cs:] for r in wq], axis=1) + _dot(attn, _block_rows(v_new))
    for h in range(N_HEADS):
        cols = slice(h * hd, (h + 1) * hd)
        s_ref[h] = s[h] * g["eg"][h] + _dot_tn(kg[:, cols], v_new[:, cols])
    g["o_ref"][0, pl.ds(g["r0"], cs), :] = o.astype(BF16)
    turn[g["backward"]] += 1


def _run_interleaved(programs, skew):
    live = {}
    tick = 0
    while live or tick <= skew * (len(programs) - 1):
        if tick % skew == 0 and tick // skew < len(programs):
            live[tick // skew] = programs[tick // skew]
        for key in sorted(live):
            for prog in live[key]:
                if next(prog, "done") == "done":
                    live[key] = [p for p in live[key] if p is not prog]
            if not live[key]:
                del live[key]
        tick += 1


def _gdn_body(qf_ref, qb_ref, gbf_ref, gbb_ref, s0f_ref, s0b_ref,
              of_ref, ob_ref, sff_ref, sfb_ref, sf_scr, sb_scr, *, tb, nt):
    t = pl.program_id(1)
    cs = GDN_CHUNK
    nch = tb // cs
    cps = min(GDN_CHUNKS_PER_STEP, nch)
    nh = N_HEADS

    @pl.when(t == 0)
    def _():
        sf_scr[...] = s0f_ref[0]
        sb_scr[...] = s0b_ref[0]

    ii = lax.broadcasted_iota(I32, (cs, PACK_W), 0)
    jj = lax.broadcasted_iota(I32, (cs, PACK_W), 1) & (cs - 1)
    incl_f, strict_f = jj <= ii, jj < ii
    incl_b, strict_b = jj >= ii, jj > ii
    levels = []
    sh = 0
    while (1 << sh) < cs:
        levels.append(((ii >> (sh + 1)) == (jj >> (sh + 1))) & ((ii >> sh) != (jj >> sh)))
        sh += 1
    lv_f = [lm & strict_f for lm in levels]
    lv_b = [lm & strict_b for lm in levels]
    eye = (ii == jj).astype(F32)
    row = lax.broadcasted_iota(I32, (cs, LANES), 0)
    lane = lax.broadcasted_iota(I32, (1, LANES), 1)
    half_of_tile = lax.broadcasted_iota(I32, (cs, LANES), 1) // cs
    half_masks = [(half_of_tile == i).astype(BF16) for i in range(LANES // cs)]

    def group(q_ref, r0, gate, csum, gt, backward, s_ref, o_ref, order):
        l0 = nh if backward else 0
        lanes = [l0 + h for h in range(nh)]
        last = 0 if backward else cs - 1
        ld = lambda c0: q_ref[0, pl.ds(r0, cs), c0:c0 + B_W]
        halves = [gt[l:l + 1, :] if (h % 2 == 0) != backward else pltpu.roll(gt[l:l + 1, :], cs, 1)
                  for h, l in enumerate(lanes)]
        gc_row = jnp.concatenate([jnp.where(lane < cs, halves[0], halves[1]),
                                  jnp.where(lane < cs, halves[2], halves[3])], axis=1)
        glast = csum[last:last + 1, :]
        return dict(
            q=ld(0), k=ld(B_W), v=ld(2 * B_W),
            beta=_per_head(gate, [2 * nh + l for l in lanes], HEAD_DIM).astype(BF16),
            egc=_per_head(jnp.exp(csum), lanes, HEAD_DIM).astype(BF16),
            kdec=_per_head(jnp.exp(glast - csum), lanes, HEAD_DIM).astype(BF16), eye=eye,
            eg=[jnp.exp(csum[last:last + 1, l:l + 1]) for l in lanes],
            gc_col=_per_head(csum, lanes, cs), gc_row=gc_row,
            incl=incl_b if backward else incl_f, strict=strict_b if backward else strict_f,
            levels=lv_b if backward else lv_f, s_ref=s_ref, o_ref=o_ref, r0=r0, backward=backward, order=order)

    def step(n, carry):
        per_chunk = []
        for j in range(cps):
            rf = pl.multiple_of((n * cps + j) * cs, cs)
            rb = pl.multiple_of((nch - 1 - n * cps - j) * cs, cs)
            gf = gbf_ref[0, pl.ds(rf, cs), :]
            gb = gbb_ref[0, pl.ds(rb, cs), :]
            cf, cb = gf, gb
            s = 1
            while s < cs:
                cf = cf + jnp.where(row >= s, pltpu.roll(cf, s, 0), 0.0)
                cb = cb + jnp.where(row < cs - s, pltpu.roll(cb, cs - s, 0), 0.0)
                s *= 2
            gt = jnp.concatenate([cf, cb], axis=0).T
            per_chunk.append([group(qf_ref, rf, gf, cf, gt, False, sf_scr, of_ref, j),
                              group(qb_ref, rb, gb, cb, gt, True, sb_scr, ob_ref, j)])
        turn = {False: 0, True: 0}
        _run_interleaved([[_gdn_group_program(g, half_masks, turn) for g in pair] for pair in per_chunk],
                         GDN_STAGE_SKEW)
        return carry

    lax.fori_loop(0, nch // cps, step, 0)

    @pl.when(t == nt - 1)
    def _():
        sff_ref[0] = sf_scr[...]
        sfb_ref[0] = sb_scr[...]


def _gdn_call(qkv, gb, s0f, s0b, tb):
    bsz, t, _ = qkv.shape
    nt = t // tb
    assert (tb // GDN_CHUNK) % min(GDN_CHUNKS_PER_STEP, tb // GDN_CHUNK) == 0
    st = pl.BlockSpec((1, N_HEADS, HEAD_DIM, HEAD_DIM), lambda b, i: (b, 0, 0, 0))
    fwd = lambda w: pl.BlockSpec((1, tb, w), lambda b, i: (b, i, 0))
    bwd = lambda w: pl.BlockSpec((1, tb, w), lambda b, i: (b, nt - 1 - i, 0))
    return pl.pallas_call(
        functools.partial(_gdn_body, tb=tb, nt=nt),
        out_shape=(jax.ShapeDtypeStruct((bsz, t, B_W), BF16), jax.ShapeDtypeStruct((bsz, t, B_W), BF16),
                   jax.ShapeDtypeStruct((bsz, N_HEADS, HEAD_DIM, HEAD_DIM), F32),
                   jax.ShapeDtypeStruct((bsz, N_HEADS, HEAD_DIM, HEAD_DIM), F32)),
        grid=(bsz, nt),
        in_specs=[fwd(QKV_W), bwd(QKV_W), fwd(LANES), bwd(LANES), st, st],
        out_specs=(fwd(B_W), bwd(B_W), st, st),
        scratch_shapes=[pltpu.VMEM((N_HEADS, HEAD_DIM, HEAD_DIM), F32), pltpu.VMEM((N_HEADS, HEAD_DIM, HEAD_DIM), F32)],
        compiler_params=_cparams(("parallel", "arbitrary")),
        name="gdn",
    )(qkv, qkv, gb, gb, s0f, s0b)


def _mixout_body(x_ref, of_ref, ob_ref, z_ref, ya_ref, mod_ref, gng_ref, wout_ref, n2g_ref, wrh_ref, wrl_ref, br_ref,
                 h_ref, fin_ref, afft_ref, *, tm):
    o = of_ref[0].astype(F32) + ob_ref[0].astype(F32)
    z = z_ref[0].astype(F32)
    parts = [ya_ref[0]]
    for h in range(N_HEADS):
        c = slice(h * HEAD_DIM, (h + 1) * HEAD_DIM)
        oh = o[:, c]
        y = oh * lax.rsqrt(jnp.mean(oh * oh, axis=-1, keepdims=True) + NORM_EPS)
        parts.append((y * gng_ref[...] * _silu(z[:, c])).astype(BF16))
    mix = _dot(jnp.concatenate(parts, axis=1), wout_ref[...])
    hl = x_ref[0] + mod_ref[0, 2:3, :] * mix
    h_ref[0] = hl
    fin = _norm_mod(hl, n2g_ref[...], mod_ref[0, 3:4, :], mod_ref[0, 4:5, :])
    f_hi = fin.astype(BF16)
    fin_ref[0] = f_hi
    f_lo = (fin - f_hi.astype(F32)).astype(BF16)
    both = _dot(f_hi, wrl_ref[...])
    logits = both[:, :LANES] + _dot(f_lo, wrh_ref[...]) + both[:, LANES:] + br_ref[...]
    e = jnp.exp(logits - jnp.max(logits, axis=-1, keepdims=True))
    aff = e / jnp.sum(e, axis=-1, keepdims=True)
    for j in range(tm // LANES):
        afft_ref[0, j] = aff[j * LANES:(j + 1) * LANES, :].T[0:N_EXPERTS, :]


def _mixout_call(x, o_f, o_b, z, ya, mod3, gng, wout16, n2g, wr_hi, wr_lo, br, tm):
    bsz, t, _ = x.shape
    full = lambda a: pl.BlockSpec(a.shape, lambda b, i: (0,) * a.ndim)
    tok = lambda w: pl.BlockSpec((1, tm, w), lambda b, i: (b, i, 0))
    return pl.pallas_call(
        functools.partial(_mixout_body, tm=tm),
        out_shape=(jax.ShapeDtypeStruct((bsz, t, D_MODEL), F32), jax.ShapeDtypeStruct((bsz, t, D_MODEL), BF16),
                   jax.ShapeDtypeStruct((bsz, t // LANES, N_EXPERTS, LANES), F32)),
        grid=(bsz, t // tm),
        in_specs=[tok(D_MODEL), tok(B_W), tok(B_W), tok(B_W), tok(A_W),
                  pl.BlockSpec((1, N_MOD, D_MODEL), lambda b, i: (b, 0, 0)),
                  full(gng), full(wout16), full(n2g), full(wr_hi), full(wr_lo), full(br)],
        out_specs=(tok(D_MODEL), tok(D_MODEL),
                   pl.BlockSpec((1, tm // LANES, N_EXPERTS, LANES), lambda b, i: (b, i, 0, 0))),
        compiler_params=_cparams(("parallel", "parallel")),
        name="mixout",
    )(x, o_f, o_b, z, ya, mod3, gng, wout16, n2g, wr_hi, wr_lo, br)


def _route_body(afft_ref, slott_ref, off_ref, *, t, cap):
    ne = N_EXPERTS
    npieces = t // LANES
    rows = npieces * ne

    def count(thr_col, strict):
        acc = jnp.zeros((ne, LANES), I32)
        for p in range(npieces):
            piece = afft_ref[0, p * ne:(p + 1) * ne, :]
            acc = acc + (piece > thr_col if strict else piece >= thr_col).astype(I32)
        return jnp.sum(acc, axis=1, keepdims=True)

    def search(i, thr):
        cand = thr | jnp.left_shift(jnp.int32(1), 30 - i)
        return jnp.where(count(pltpu.bitcast(cand, F32), False) >= cap, cand, thr)

    thr_bits = lax.fori_loop(0, 31, search, jnp.zeros((ne, 1), I32))
    thr = pltpu.bitcast(thr_bits, F32)
    need = (cap - count(thr, True)).astype(F32)

    x = afft_ref[0]
    thr_rows = jnp.concatenate([thr] * npieces, axis=0)
    need_rows = jnp.concatenate([need] * npieces, axis=0)
    gt = x > thr_rows
    eq = x == thr_rows
    ti = lax.broadcasted_iota(I32, (LANES, LANES), 0)
    tj = lax.broadcasted_iota(I32, (LANES, LANES), 1)
    triu = (ti <= tj).astype(BF16)
    ri = lax.broadcasted_iota(I32, (rows, rows), 0)
    rj = lax.broadcasted_iota(I32, (rows, rows), 1)
    earlier = (((ri & (ne - 1)) == (rj & (ne - 1))) & (rj < ri)).astype(BF16)

    def prefix(mask):
        inpiece = _dot(mask.astype(BF16), triu)
        total = jnp.broadcast_to(inpiece[:, LANES - 1:LANES], (rows, LANES)).astype(BF16)
        offset = _dot(earlier, total)
        return inpiece + offset, offset

    eq_rank, _ = prefix(eq)
    sel = gt | (eq & (eq_rank <= need_rows))
    sel_rank, sel_off = prefix(sel)
    slott_ref[0] = jnp.where(sel, sel_rank - 1.0, -1.0).astype(I32)
    off_ref[0] = sel_off.astype(I32)


def _route_call(afft, cap):
    bsz, rows, _ = afft.shape
    t = rows // N_EXPERTS * LANES
    spec = lambda r: pl.BlockSpec((1, r, LANES), lambda b: (b, 0, 0))
    return pl.pallas_call(
        functools.partial(_route_body, t=t, cap=cap),
        out_shape=(jax.ShapeDtypeStruct((bsz, rows, LANES), I32),
                   jax.ShapeDtypeStruct((bsz, rows, LANES), I32)),
        grid=(bsz,),
        in_specs=[spec(rows)],
        out_specs=(spec(rows), spec(rows)),
        compiler_params=_cparams(("parallel",)),
        name="route",
    )(afft)


def _window_plan(base_ref, flat0, experts):
    starts, rounds = [], jnp.int32(0)
    for e in experts:
        lo = base_ref[flat0 + e]
        hi = base_ref[flat0 + N_EXPERTS + e]
        lo_al = lo & -SLOT_ALIGN
        starts.append(lo_al)
        rounds = jnp.maximum(rounds, (hi - lo_al + SLOT_WIN - 1) // SLOT_WIN)
    return starts, rounds


def _window_start(start, r, cap):
    return pl.multiple_of(jnp.minimum(start + r * SLOT_WIN, cap), SLOT_ALIGN)


def _dispatch_body(base_ref, slott_ref, fin_ref, xe_ref, *, nchunk, sub, eh_n, cap):
    b, eh, ci = pl.program_id(0), pl.program_id(1), pl.program_id(2)
    rc = ROUTE_CHUNK

    @pl.when(ci == 0)
    def _():
        xe_ref[...] = jnp.zeros_like(xe_ref)

    srow = lax.broadcasted_iota(I32, (SLOT_WIN, rc), 0)
    for sc in range(sub):
        cc = ci * sub + sc
        flat0 = (b * (nchunk + 1) + cc) * N_EXPERTS + eh * eh_n
        f = fin_ref[0, sc * rc:(sc + 1) * rc, :]
        experts = list(range(eh_n))
        starts, rounds = _window_plan(base_ref, flat0, experts)

        def one_round(r, carry, starts=starts, f=f, sc=sc):
            rows = []
            wstart = [_window_start(starts[e], r, cap) for e in experts]
            for e in experts:
                tok_slot = jnp.concatenate(
                    [slott_ref[0, sc * (rc // LANES) + j, e:e + 1, :] for j in range(rc // LANES)], axis=1)
                rows.append((tok_slot == srow + wstart[e]).astype(BF16))
            prod = _dot(jnp.concatenate(rows, axis=0), f)
            for e in experts:
                win = pl.ds(wstart[e], SLOT_WIN)
                xe_ref[0, e, win, :] = xe_ref[0, e, win, :] + prod[e * SLOT_WIN:(e + 1) * SLOT_WIN].astype(BF16)
            return carry

        one_round(jnp.int32(0), 0)
        lax.fori_loop(1, rounds, one_round, 0)


def _dispatch_call(base_flat, slott, fin, cap):
    bsz, t, _ = fin.shape
    nchunk = t // ROUTE_CHUNK
    sub = 8
    eh_n = N_EXPERTS // 2
    sp = cap + SLOT_WIN
    grid_spec = pltpu.PrefetchScalarGridSpec(
        num_scalar_prefetch=1,
        grid=(bsz, N_EXPERTS // eh_n, nchunk // sub),
        in_specs=[pl.BlockSpec((1, sub * ROUTE_CHUNK // LANES, eh_n, LANES), lambda b, eh, ci, base: (b, ci, eh, 0)),
                  pl.BlockSpec((1, sub * ROUTE_CHUNK, D_MODEL), lambda b, eh, ci, base: (b, ci, 0))],
        out_specs=pl.BlockSpec((1, eh_n, sp, D_MODEL), lambda b, eh, ci, base: (b, eh, 0, 0)))
    return pl.pallas_call(
        functools.partial(_dispatch_body, nchunk=nchunk, sub=sub, eh_n=eh_n, cap=cap),
        out_shape=jax.ShapeDtypeStruct((bsz, N_EXPERTS, sp, D_MODEL), BF16),
        grid_spec=grid_spec,
        compiler_params=_cparams(("parallel", "parallel", "arbitrary")),
        name="dispatch",
    )(base_flat, slott, fin)


def _experts_body(xe_ref, wg_ref, wu_ref, wd_ref, y_ref, *, cap):
    x = xe_ref[0, 0, 0:cap, :]
    ft = 256
    hids, wds = [], []
    for f in range(EXPERT_FF // ft):
        cols = slice(f * ft, (f + 1) * ft)
        wg16 = wg_ref[0, :, cols].astype(BF16)
        wu16 = wu_ref[0, :, cols].astype(BF16)
        wds.append(wd_ref[0, cols, :].astype(BF16))
        hids.append((_silu(_dot(x, wg16)) * _dot(x, wu16)).astype(BF16))
    acc = _dot(jnp.concatenate(hids, axis=1), jnp.concatenate(wds, axis=0))
    y_ref[0, 0, 0:cap, :] = acc.astype(BF16)
    y_ref[0, 0, cap:, :] = jnp.zeros((y_ref.shape[2] - cap, D_MODEL), BF16)


def _experts_call(xe, w_gate, w_up, w_down, cap):
    bsz, _, sp, _ = xe.shape
    wspec = lambda shape: pl.BlockSpec((1,) + shape, lambda e, b: (e, 0, 0))
    slots = pl.BlockSpec((1, 1, sp, D_MODEL), lambda e, b: (b, e, 0, 0))
    return pl.pallas_call(
        functools.partial(_experts_body, cap=cap),
        out_shape=jax.ShapeDtypeStruct(xe.shape, BF16),
        grid=(N_EXPERTS, bsz),
        in_specs=[slots, wspec((D_MODEL, EXPERT_FF)), wspec((D_MODEL, EXPERT_FF)), wspec((EXPERT_FF, D_MODEL))],
        out_specs=slots,
        compiler_params=_cparams(("parallel", "parallel")),
        name="experts",
    )(xe, w_gate, w_up, w_down)


def _combine_body(base_ref, slott_ref, afft_ref, h_ref, y_ref, mod_ref, fng_ref, o_ref, acc_ref, *, nchunk, sub, cap):
    b, ci = pl.program_id(0), pl.program_id(1)
    rc = ROUTE_CHUNK
    pieces = rc // LANES
    srow = lax.broadcasted_iota(I32, (SLOT_WIN, rc), 0)
    experts = list(range(N_EXPERTS))
    for sc in range(sub):
        rows = slice(sc * rc, (sc + 1) * rc)
        flat0 = (b * (nchunk + 1) + ci * sub + sc) * N_EXPERTS
        starts, rounds = _window_plan(base_ref, flat0, experts)
        tok_slot = [jnp.concatenate([slott_ref[0, sc * pieces + j, e:e + 1, :] for j in range(pieces)], axis=1)
                    for e in experts]
        tok_gate = [jnp.concatenate([afft_ref[0, sc * pieces + j, e:e + 1, :] for j in range(pieces)], axis=1)
                    for e in experts]

        def contribution(r, starts=starts, tok_slot=tok_slot, tok_gate=tok_gate):
            wstart = [_window_start(starts[e], r, cap) for e in experts]
            ywin = jnp.concatenate([y_ref[0, e, pl.ds(wstart[e], SLOT_WIN), :] for e in experts], axis=0)
            st = jnp.concatenate([jnp.where(tok_slot[e] == srow + wstart[e], tok_gate[e], 0.0).astype(BF16)
                                  for e in experts], axis=0)
            return _dot_tn(st, ywin)

        def extra_round(r, carry, contribution=contribution):
            acc_ref[...] += contribution(r)
            return carry

        acc_ref[...] = contribution(jnp.int32(0))
        lax.fori_loop(1, rounds, extra_round, 0)
        hl = h_ref[0, rows, :] + mod_ref[0, 5:6, :] * acc_ref[...]
        ms = jnp.mean(hl * hl, axis=-1, keepdims=True)
        o_ref[0, rows, :] = hl * lax.rsqrt(ms + NORM_EPS) * fng_ref[...]


def _combine_call(base_flat, slott, afft, h, y, mod3, fng):
    bsz, t, _ = h.shape
    nchunk = t // ROUTE_CHUNK
    sub = 4
    rc = ROUTE_CHUNK
    tok = lambda w: pl.BlockSpec((1, sub * rc, w), lambda b, i, base: (b, i, 0))
    piece = pl.BlockSpec((1, sub * rc // LANES, N_EXPERTS, LANES), lambda b, i, base: (b, i, 0, 0))
    grid_spec = pltpu.PrefetchScalarGridSpec(
        num_scalar_prefetch=1,
        grid=(bsz, nchunk // sub),
        in_specs=[piece, piece, tok(D_MODEL),
                  pl.BlockSpec((1,) + y.shape[1:], lambda b, i, base: (b, 0, 0, 0), pipeline_mode=pl.Buffered(1)),
                  pl.BlockSpec((1, N_MOD, D_MODEL), lambda b, i, base: (b, 0, 0)),
                  pl.BlockSpec(fng.shape, lambda b, i, base: (0, 0))],
        out_specs=tok(D_MODEL),
        scratch_shapes=[pltpu.VMEM((rc, D_MODEL), F32)])
    return pl.pallas_call(
        functools.partial(_combine_body, nchunk=nchunk, sub=sub, cap=y.shape[2] - SLOT_WIN),
        out_shape=jax.ShapeDtypeStruct(h.shape, F32),
        grid_spec=grid_spec,
        compiler_params=_cparams(("parallel", "arbitrary")),
        name="combine",
    )(base_flat, slott, afft, h, y, mod3, fng)


def _pad_lanes(a):
    return jnp.pad(a, ((0, 0), (0, LANES - a.shape[1])))


def kernel(x, c, ctx, c_ctx, w_mod, b_mod, norm1_g, norm2_g, w_in, conv_w, a_log, dt_bias, gdn_norm_g, gm_norm_g,
           gm_ws, gm_bs, w_out, w_router, b_router, w_gate, w_up, w_down, final_norm_g):
    bsz, t, _ = x.shape
    ctx_len = ctx.shape[1]
    assert w_mod.shape[0] == 1, "single-layer problem"
    assert t % 2048 == 0 and ctx_len % GDN_CHUNK == 0 and bsz < 8
    cap = EC_CAPACITY * t // N_EXPERTS

    cs = jnp.zeros((8, D_MODEL), F32).at[:bsz].set(c).at[bsz].set(c_ctx)
    mod3 = _mod_call(cs, w_mod[0], b_mod[0][None, :]).reshape(8, N_MOD, D_MODEL)

    wl = w_in[0]
    n_state = QKV_W + STATE_COLS
    w_state = _pad_lanes(wl[:, QKV_W:n_state])
    w_lat = jnp.concatenate([wl[:, :QKV_W], wl[:, n_state:n_state + B_W], wl[:, n_state + B_W:], w_state],
                            axis=1).astype(BF16)
    gp = jnp.zeros((8, LANES), F32).at[0, :2 * N_HEADS].set(a_log[0].reshape(-1)).at[1, :2 * N_HEADS].set(
        dt_bias[0].reshape(-1))
    g1 = norm1_g[0][None, :]
    cw = jnp.zeros((8, QKV_W), F32).at[:conv_w.shape[1]].set(conv_w[0])

    qkv_c, gb_c = _inproj_ctx_call(ctx, mod3, bsz, g1, w_lat, gp, cw)
    zero_state = jnp.zeros((bsz, N_HEADS, HEAD_DIM, HEAD_DIM), F32)
    _, _, s_f, s_b = _gdn_call(qkv_c, gb_c, zero_state, zero_state, ctx_len)

    qkv, gb, z, ya = _inproj_lat_call(x, mod3, g1, w_lat, gp, cw, gm_norm_g[0][None, :], gm_ws[0].astype(BF16),
                                      _pad_lanes(gm_bs[0].T), 1024)
    o_f, o_b, _, _ = _gdn_call(qkv, gb, s_f, s_b, 1024)
    wr = _pad_lanes(w_router[0])
    wr_hi = wr.astype(BF16)
    wr_lo = jnp.concatenate([wr_hi, (wr - wr_hi.astype(F32)).astype(BF16)], axis=1)
    br = jnp.full((1, LANES), -1e30, F32).at[0, :N_EXPERTS].set(b_router[0])
    h, fin, afft = _mixout_call(x, o_f, o_b, z, ya, mod3, gdn_norm_g[0][None, :], w_out[0].astype(BF16),
                                     norm2_g[0][None, :], wr_hi, wr_lo, br, 1024)

    npieces = t // LANES
    slott, off = _route_call(afft.reshape(bsz, npieces * N_EXPERTS, LANES), cap)
    slott = slott.reshape(bsz, npieces, N_EXPERTS, LANES)
    base = off[:, :, 0].reshape(bsz, npieces, N_EXPERTS)[:, ::ROUTE_CHUNK // LANES, :]
    base_flat = jnp.concatenate([base, jnp.full((bsz, 1, N_EXPERTS), cap, I32)], axis=1).reshape(-1)
    xe = _dispatch_call(base_flat, slott, fin, cap)
    y = _experts_call(xe, w_gate[0], w_up[0], w_down[0], cap)
    return _combine_call(base_flat, slott, afft, h, y, mod3, final_norm_g[None, :])
```

```python
import functools

import jax
import jax.numpy as jnp
from jax import lax
from jax.experimental import pallas as pl
from jax.experimental.pallas import tpu as pltpu

F32 = jnp.float32
BF16 = jnp.bfloat16
I32 = jnp.int32

D_MODEL = 1024
N_MOD = 6
N_HEADS = 4
HEAD_DIM = 128
B_W = N_HEADS * HEAD_DIM
QKV_W = 3 * B_W
A_W = 512
A_GROUPS = 4
A_CHUNK = 128
GDN_CHUNK = 64
N_EXPERTS = 16
EC_CAPACITY = 2
EXPERT_FF = 1024
NORM_EPS = 1e-6
LANES = 128
STATE_COLS = 4 * N_HEADS

ROUTE_CHUNK = 256
SLOT_WIN = 64
SLOT_ALIGN = 16
VMEM_LIMIT = 60 * 1024 * 1024


def _cparams(sem):
    return pltpu.CompilerParams(dimension_semantics=sem, vmem_limit_bytes=VMEM_LIMIT)


def _dot(a, b):
    return jnp.dot(a, b, preferred_element_type=F32)


def _dot_nt(a, b):
    return lax.dot_general(a, b, (((1,), (1,)), ((), ())), preferred_element_type=F32)


def _dot_tn(a, b):
    return lax.dot_general(a, b, (((0,), (0,)), ((), ())), preferred_element_type=F32)


def _silu(x):
    return x * jax.nn.sigmoid(x)


def _mod_body(c_ref, w_ref, b_ref, o_ref):
    s = _silu(c_ref[...])
    o_ref[...] = _dot(s.astype(BF16), w_ref[...].astype(BF16)) + b_ref[...]


def _mod_call(cs, w_mod, b_mod):
    n = w_mod.shape[1] // D_MODEL
    return pl.pallas_call(
        _mod_body,
        out_shape=jax.ShapeDtypeStruct((8, w_mod.shape[1]), F32),
        grid=(n,),
        in_specs=[pl.BlockSpec((8, D_MODEL), lambda j: (0, 0)),
                  pl.BlockSpec((D_MODEL, D_MODEL), lambda j: (0, j)),
                  pl.BlockSpec((1, D_MODEL), lambda j: (0, j))],
        out_specs=pl.BlockSpec((8, D_MODEL), lambda j: (0, j)),
        compiler_params=_cparams(("arbitrary",)),
        name="mod",
    )(cs, w_mod, b_mod)


def _norm_mod(x, g, shift, scale):
    ms = jnp.mean(x * x, axis=-1, keepdims=True)
    return (x * lax.rsqrt(ms + NORM_EPS) * g) * (1.0 + scale) + shift


def _gate_streams(st, gp_ref):
    lane = lax.broadcasted_iota(I32, st.shape, 1)
    g = -jnp.exp(gp_ref[0:1, :]) * jax.nn.softplus(st + gp_ref[1:2, :])
    beta = jax.nn.sigmoid(st)
    return jnp.where(lane < 2 * N_HEADS, g, jnp.where(lane < STATE_COLS, beta, 0.0))


def _conv_qkv(qkv, prev_row, next_row, cw_ref, out_ref, tm):
    cs = GDN_CHUNK
    nsub = tm // cs
    w0, w1, w2 = cw_ref[0:1, :], cw_ref[1:2, :], cw_ref[2:3, :]
    row = lax.broadcasted_iota(I32, (cs, 1), 0)
    for c in range(nsub):
        rows = slice(c * cs, (c + 1) * cs)
        x = qkv[rows]
        prow = prev_row if c == 0 else qkv[c * cs - 1:c * cs]
        nrow = next_row if c == nsub - 1 else qkv[(c + 1) * cs:(c + 1) * cs + 1]
        xp = jnp.where(row == 0, prow, pltpu.roll(x, 1, 0))
        xn = jnp.where(row == cs - 1, nrow, pltpu.roll(x, cs - 1, 0))
        y = _silu(xp * w0 + x * w1 + xn * w2)
        for h in range(N_HEADS):
            cq = slice(h * HEAD_DIM, (h + 1) * HEAD_DIM)
            ck = slice(B_W + h * HEAD_DIM, B_W + (h + 1) * HEAD_DIM)
            q = y[:, cq]
            k = y[:, ck]
            out_ref[0, rows, cq] = (q * (lax.rsqrt(jnp.sum(q * q, axis=-1, keepdims=True) + NORM_EPS)
                                         * (HEAD_DIM ** -0.5))).astype(BF16)
            out_ref[0, rows, ck] = (k * lax.rsqrt(jnp.sum(k * k, axis=-1, keepdims=True) + NORM_EPS)).astype(BF16)
        out_ref[0, rows, 2 * B_W:3 * B_W] = y[:, 2 * B_W:3 * B_W].astype(BF16)


def _inproj_lat_body(x_ref, xp_ref, xn_ref, mod_ref, g1_ref, w_ref, gp_ref, cw_ref, gmg_ref, ws_ref, bst_ref,
                     qkv_ref, gb_ref, z_ref, ya_ref, *, tm):
    i = pl.program_id(1)
    shift, scale = mod_ref[0, 0:1, :], mod_ref[0, 1:2, :]
    a = _norm_mod(x_ref[0], g1_ref[...], shift, scale).astype(BF16)
    xh = jnp.concatenate([xp_ref[0], xn_ref[0]], axis=0)
    halo = _dot(_norm_mod(xh, g1_ref[...], shift, scale).astype(BF16), w_ref[:, 0:QKV_W])
    prev_row = jnp.where(i == 0, 0.0, halo[7:8, :])
    next_row = jnp.where(i == pl.num_programs(1) - 1, 0.0, halo[8:9, :])
    _conv_qkv(_dot(a, w_ref[:, 0:QKV_W]), prev_row, next_row, cw_ref, qkv_ref, tm)
    z_ref[0] = _dot(a, w_ref[:, QKV_W:QKV_W + B_W]).astype(BF16)
    c_uv = QKV_W + B_W
    gb_ref[0] = _gate_streams(_dot(a, w_ref[:, c_uv + 2 * A_W:c_uv + 2 * A_W + LANES]), gp_ref)
    uv = _dot(a, w_ref[:, c_uv:c_uv + 2 * A_W])
    uv = 0.5 * uv * (1.0 + lax.erf(uv * 0.7071067811865476))
    gd = A_W // A_GROUPS
    for grp in range(A_GROUPS):
        v = uv[:, A_W + grp * gd:A_W + (grp + 1) * gd]
        vn = v * lax.rsqrt(jnp.mean(v * v, axis=-1, keepdims=True) + NORM_EPS) * gmg_ref[:, grp * gd:(grp + 1) * gd]
        vn = vn.astype(BF16)
        bias = bst_ref[:, grp:grp + 1]
        for c in range(tm // A_CHUNK):
            rows = slice(c * A_CHUNK, (c + 1) * A_CHUNK)
            s = _dot(ws_ref[grp], vn[rows]) + bias
            ya_ref[0, rows, grp * gd:(grp + 1) * gd] = (uv[rows, grp * gd:(grp + 1) * gd] * s).astype(BF16)


def _inproj_ctx_body(x_ref, mod_ref, g1_ref, w_ref, gp_ref, cw_ref, qkv_ref, gb_ref, *, tm):
    a = _norm_mod(x_ref[0], g1_ref[...], mod_ref[0, 0:1, :], mod_ref[0, 1:2, :]).astype(BF16)
    edge = jnp.zeros((1, QKV_W), F32)
    _conv_qkv(_dot(a, w_ref[:, 0:QKV_W]), edge, edge, cw_ref, qkv_ref, tm)
    c_state = QKV_W + B_W + 2 * A_W
    gb_ref[0] = _gate_streams(_dot(a, w_ref[:, c_state:c_state + LANES]), gp_ref)


def _inproj_lat_call(x, mod3, g1, w_lat, gp, cw, gmg, ws16, bst, tm):
    bsz, t, _ = x.shape
    hb = tm // 8
    last8 = t // 8 - 1
    full = lambda a: pl.BlockSpec(a.shape, lambda b, i: (0,) * a.ndim)
    tok = lambda w: pl.BlockSpec((1, tm, w), lambda b, i: (b, i, 0))
    return pl.pallas_call(
        functools.partial(_inproj_lat_body, tm=tm),
        out_shape=(jax.ShapeDtypeStruct((bsz, t, QKV_W), BF16),
                   jax.ShapeDtypeStruct((bsz, t, LANES), F32),
                   jax.ShapeDtypeStruct((bsz, t, B_W), BF16),
                   jax.ShapeDtypeStruct((bsz, t, A_W), BF16)),
        grid=(bsz, t // tm),
        in_specs=[tok(D_MODEL),
                  pl.BlockSpec((1, 8, D_MODEL), lambda b, i: (b, jnp.maximum(i * hb - 1, 0), 0)),
                  pl.BlockSpec((1, 8, D_MODEL), lambda b, i: (b, jnp.minimum((i + 1) * hb, last8), 0)),
                  pl.BlockSpec((1, N_MOD, D_MODEL), lambda b, i: (b, 0, 0)),
                  full(g1), full(w_lat), full(gp), full(cw), full(gmg), full(ws16), full(bst)],
        out_specs=(tok(QKV_W), tok(LANES), tok(B_W), tok(A_W)),
        compiler_params=_cparams(("parallel", "arbitrary")),
        name="inproj_lat",
    )(x, x, x, mod3, g1, w_lat, gp, cw, gmg, ws16, bst)


def _inproj_ctx_call(ctx, mod3, ctx_row, g1, w_ctx, gp, cw):
    bsz, t, _ = ctx.shape
    full = lambda a: pl.BlockSpec(a.shape, lambda b: (0,) * a.ndim)
    tok = lambda w: pl.BlockSpec((1, t, w), lambda b: (b, 0, 0))
    return pl.pallas_call(
        functools.partial(_inproj_ctx_body, tm=t),
        out_shape=(jax.ShapeDtypeStruct((bsz, t, QKV_W), BF16),
                   jax.ShapeDtypeStruct((bsz, t, LANES), F32)),
        grid=(bsz,),
        in_specs=[tok(D_MODEL),
                  pl.BlockSpec((1, N_MOD, D_MODEL), lambda b: (ctx_row, 0, 0)),
                  full(g1), full(w_ctx), full(gp), full(cw)],
        out_specs=(tok(QKV_W), tok(LANES)),
        compiler_params=_cparams(("parallel",)),
        name="inproj_ctx",
    )(ctx, mod3, g1, w_ctx, gp, cw)


GDN_CHUNKS_PER_STEP = 8
GDN_STAGE_SKEW = 2
PACK_W = N_HEADS * GDN_CHUNK


def _per_head(tile, lanes, width):
    rows = tile.shape[0]
    if width == HEAD_DIM:
        return jnp.concatenate([jnp.broadcast_to(tile[:, l:l + 1], (rows, width)) for l in lanes], axis=1)
    head = lax.broadcasted_iota(I32, (rows, N_HEADS * width), 1) // width
    out = jnp.broadcast_to(tile[:, lanes[0]:lanes[0] + 1], (rows, N_HEADS * width))
    for h in range(1, N_HEADS):
        out = jnp.where(head == h, jnp.broadcast_to(tile[:, lanes[h]:lanes[h] + 1], (rows, N_HEADS * width)), out)
    return out


def _block_rows(x16, half_masks=None):
    rows, width = x16.shape
    per_head = width // N_HEADS
    zero = jnp.zeros((rows, LANES), x16.dtype)
    blocks = []
    for h in range(N_HEADS):
        tile = h * per_head // LANES
        kept = x16[:, tile * LANES:(tile + 1) * LANES]
        if per_head < LANES:
            kept = kept * half_masks[h * per_head % LANES // per_head]
        blocks.append(jnp.concatenate([kept if t == tile else zero for t in range(width // LANES)], axis=1))
    return jnp.concatenate(blocks, axis=0)


def _gdn_group_program(g, half_masks, turn):
    cs = GDN_CHUNK
    hd = HEAD_DIM
    q16, k16, v16, beta16, egc16 = g["q"], g["k"], g["v"], g["beta"], g["egc"]
    kb16 = k16 * beta16
    decay = jnp.where(g["incl"], jnp.exp(jnp.where(g["incl"], g["gc_col"] - g["gc_row"], 0.0)), 0.0)
    kk = _dot_nt(jnp.concatenate([kb16, q16], axis=0), _block_rows(k16))
    yield
    a = jnp.where(g["strict"], kk[:cs] * decay, 0.0)
    attn = (kk[cs:] * decay).astype(BF16)
    m = -jnp.where(g["levels"][0], a, 0.0)
    for lm in g["levels"][1:]:
        m16 = m.astype(BF16)
        cm = jnp.where(lm, a, 0.0)
        x = cm + _dot(m16, _block_rows(cm.astype(BF16), half_masks))
        yield
        y = x + _dot(x.astype(BF16), _block_rows(m16, half_masks))
        yield
        m = m - y
    t16 = (m + g["eye"]).astype(BF16)
    u = _dot(t16, _block_rows(v16 * beta16))
    w = _dot(t16, _block_rows(kb16 * egc16))
    qg16 = q16 * egc16
    kg = k16 * g["kdec"]
    yield
    while turn[g["backward"]] != g["order"]:
        yield
    s_ref = g["s_ref"]
    s = [s_ref[h] for h in range(N_HEADS)]
    wq = [_dot(jnp.concatenate([w[:, h * hd:(h + 1) * hd].astype(BF16), qg16[:, h * hd:(h + 1) * hd]], axis=0),
               s[h].astype(BF16)) for h in range(N_HEADS)]
    yield
    v_new = (u - jnp.concatenate([r[:cs] for r in wq], axis=1)).astype(BF16)
    o = jnp.concatenate([r[cs:] for r in wq], axis=1) + _dot(attn, _block_rows(v_new))
    for h in range(N_HEADS):
        cols = slice(h * hd, (h + 1) * hd)
        s_ref[h] = s[h] * g["eg"][h] + _dot_tn(kg[:, cols], v_new[:, cols])
    g["o_ref"][0, pl.ds(g["r0"], cs), :] = o.astype(BF16)
    turn[g["backward"]] += 1


def _run_interleaved(programs, skew):
    live = {}
    tick = 0
    while live or tick <= skew * (len(programs) - 1):
        if tick % skew == 0 and tick // skew < len(programs):
            live[tick // skew] = programs[tick // skew]
        for key in sorted(live):
            for prog in live[key]:
                if next(prog, "done") == "done":
                    live[key] = [p for p in live[key] if p is not prog]
            if not live[key]:
                del live[key]
        tick += 1


def _gdn_body(qf_ref, qb_ref, gbf_ref, gbb_ref, s0f_ref, s0b_ref,
              of_ref, ob_ref, sff_ref, sfb_ref, sf_scr, sb_scr, *, tb, nt):
    t = pl.program_id(1)
    cs = GDN_CHUNK
    nch = tb // cs
    cps = min(GDN_CHUNKS_PER_STEP, nch)
    nh = N_HEADS

    @pl.when(t == 0)
    def _():
        sf_scr[...] = s0f_ref[0]
        sb_scr[...] = s0b_ref[0]

    ii = lax.broadcasted_iota(I32, (cs, PACK_W), 0)
    jj = lax.broadcasted_iota(I32, (cs, PACK_W), 1) & (cs - 1)
    incl_f, strict_f = jj <= ii, jj < ii
    incl_b, strict_b = jj >= ii, jj > ii
    levels = []
    sh = 0
    while (1 << sh) < cs:
        levels.append(((ii >> (sh + 1)) == (jj >> (sh + 1))) & ((ii >> sh) != (jj >> sh)))
        sh += 1
    lv_f = [lm & strict_f for lm in levels]
    lv_b = [lm & strict_b for lm in levels]
    eye = (ii == jj).astype(F32)
    row = lax.broadcasted_iota(I32, (cs, LANES), 0)
    lane = lax.broadcasted_iota(I32, (1, LANES), 1)
    half_of_tile = lax.broadcasted_iota(I32, (cs, LANES), 1) // cs
    half_masks = [(half_of_tile == i).astype(BF16) for i in range(LANES // cs)]

    def group(q_ref, r0, gate, csum, gt, backward, s_ref, o_ref, order):
        l0 = nh if backward else 0
        lanes = [l0 + h for h in range(nh)]
        last = 0 if backward else cs - 1
        ld = lambda c0: q_ref[0, pl.ds(r0, cs), c0:c0 + B_W]
        halves = [gt[l:l + 1, :] if (h % 2 == 0) != backward else pltpu.roll(gt[l:l + 1, :], cs, 1)
                  for h, l in enumerate(lanes)]
        gc_row = jnp.concatenate([jnp.where(lane < cs, halves[0], halves[1]),
                                  jnp.where(lane < cs, halves[2], halves[3])], axis=1)
        glast = csum[last:last + 1, :]
        return dict(
            q=ld(0), k=ld(B_W), v=ld(2 * B_W),
            beta=_per_head(gate, [2 * nh + l for l in lanes], HEAD_DIM).astype(BF16),
            egc=_per_head(jnp.exp(csum), lanes, HEAD_DIM).astype(BF16),
            kdec=_per_head(jnp.exp(glast - csum), lanes, HEAD_DIM).astype(BF16), eye=eye,
            eg=[jnp.exp(csum[last:last + 1, l:l + 1]) for l in lanes],
            gc_col=_per_head(csum, lanes, cs), gc_row=gc_row,
            incl=incl_b if backward else incl_f, strict=strict_b if backward else strict_f,
            levels=lv_b if backward else lv_f, s_ref=s_ref, o_ref=o_ref, r0=r0, backward=backward, order=order)

    def step(n, carry):
        per_chunk = []
        for j in range(cps):
            rf = pl.multiple_of((n * cps + j) * cs, cs)
            rb = pl.multiple_of((nch - 1 - n * cps - j) * cs, cs)
            gf = gbf_ref[0, pl.ds(rf, cs), :]
            gb = gbb_ref[0, pl.ds(rb, cs), :]
            cf, cb = gf, gb
            s = 1
            while s < cs:
                cf = cf + jnp.where(row >= s, pltpu.roll(cf, s, 0), 0.0)
                cb = cb + jnp.where(row < cs - s, pltpu.roll(cb, cs - s, 0), 0.0)
                s *= 2
            gt = jnp.concatenate([cf, cb], axis=0).T
            per_chunk.append([group(qf_ref, rf, gf, cf, gt, False, sf_scr, of_ref, j),
                              group(qb_ref, rb, gb, cb, gt, True, sb_scr, ob_ref, j)])
        turn = {False: 0, True: 0}
        _run_interleaved([[_gdn_group_program(g, half_masks, turn) for g in pair] for pair in per_chunk],
                         GDN_STAGE_SKEW)
        return carry

    lax.fori_loop(0, nch // cps, step, 0)

    @pl.when(t == nt - 1)
    def _():
        sff_ref[0] = sf_scr[...]
        sfb_ref[0] = sb_scr[...]


def _gdn_call(qkv, gb, s0f, s0b, tb):
    bsz, t, _ = qkv.shape
    nt = t // tb
    assert (tb // GDN_CHUNK) % min(GDN_CHUNKS_PER_STEP, tb // GDN_CHUNK) == 0
    st = pl.BlockSpec((1, N_HEADS, HEAD_DIM, HEAD_DIM), lambda b, i: (b, 0, 0, 0))
    fwd = lambda w: pl.BlockSpec((1, tb, w), lambda b, i: (b, i, 0))
    bwd = lambda w: pl.BlockSpec((1, tb, w), lambda b, i: (b, nt - 1 - i, 0))
    return pl.pallas_call(
        functools.partial(_gdn_body, tb=tb, nt=nt),
        out_shape=(jax.ShapeDtypeStruct((bsz, t, B_W), BF16), jax.ShapeDtypeStruct((bsz, t, B_W), BF16),
                   jax.ShapeDtypeStruct((bsz, N_HEADS, HEAD_DIM, HEAD_DIM), F32),
                   jax.ShapeDtypeStruct((bsz, N_HEADS, HEAD_DIM, HEAD_DIM), F32)),
        grid=(bsz, nt),
        in_specs=[fwd(QKV_W), bwd(QKV_W), fwd(LANES), bwd(LANES), st, st],
        out_specs=(fwd(B_W), bwd(B_W), st, st),
        scratch_shapes=[pltpu.VMEM((N_HEADS, HEAD_DIM, HEAD_DIM), F32), pltpu.VMEM((N_HEADS, HEAD_DIM, HEAD_DIM), F32)],
        compiler_params=_cparams(("parallel", "arbitrary")),
        name="gdn",
    )(qkv, qkv, gb, gb, s0f, s0b)


def _mixout_body(x_ref, of_ref, ob_ref, z_ref, ya_ref, mod_ref, gng_ref, wout_ref, n2g_ref, wrh_ref, wrl_ref, br_ref,
                 h_ref, fin_ref, afft_ref, *, tm):
    o = of_ref[0].astype(F32) + ob_ref[0].astype(F32)
    z = z_ref[0].astype(F32)
    parts = [ya_ref[0]]
    for h in range(N_HEADS):
        c = slice(h * HEAD_DIM, (h + 1) * HEAD_DIM)
        oh = o[:, c]
        y = oh * lax.rsqrt(jnp.mean(oh * oh, axis=-1, keepdims=True) + NORM_EPS)
        parts.append((y * gng_ref[...] * _silu(z[:, c])).astype(BF16))
    mix = _dot(jnp.concatenate(parts, axis=1), wout_ref[...])
    hl = x_ref[0] + mod_ref[0, 2:3, :] * mix
    h_ref[0] = hl
    fin = _norm_mod(hl, n2g_ref[...], mod_ref[0, 3:4, :], mod_ref[0, 4:5, :])
    f_hi = fin.astype(BF16)
    fin_ref[0] = f_hi
    f_lo = (fin - f_hi.astype(F32)).astype(BF16)
    both = _dot(f_hi, wrl_ref[...])
    logits = both[:, :LANES] + _dot(f_lo, wrh_ref[...]) + both[:, LANES:] + br_ref[...]
    e = jnp.exp(logits - jnp.max(logits, axis=-1, keepdims=True))
    aff = e / jnp.sum(e, axis=-1, keepdims=True)
    for j in range(tm // LANES):
        afft_ref[0, j] = aff[j * LANES:(j + 1) * LANES, :].T[0:N_EXPERTS, :]


def _mixout_call(x, o_f, o_b, z, ya, mod3, gng, wout16, n2g, wr_hi, wr_lo, br, tm):
    bsz, t, _ = x.shape
    full = lambda a: pl.BlockSpec(a.shape, lambda b, i: (0,) * a.ndim)
    tok = lambda w: pl.BlockSpec((1, tm, w), lambda b, i: (b, i, 0))
    return pl.pallas_call(
        functools.partial(_mixout_body, tm=tm),
        out_shape=(jax.ShapeDtypeStruct((bsz, t, D_MODEL), F32), jax.ShapeDtypeStruct((bsz, t, D_MODEL), BF16),
                   jax.ShapeDtypeStruct((bsz, t // LANES, N_EXPERTS, LANES), F32)),
        grid=(bsz, t // tm),
        in_specs=[tok(D_MODEL), tok(B_W), tok(B_W), tok(B_W), tok(A_W),
                  pl.BlockSpec((1, N_MOD, D_MODEL), lambda b, i: (b, 0, 0)),
                  full(gng), full(wout16), full(n2g), full(wr_hi), full(wr_lo), full(br)],
        out_specs=(tok(D_MODEL), tok(D_MODEL),
                   pl.BlockSpec((1, tm // LANES, N_EXPERTS, LANES), lambda b, i: (b, i, 0, 0))),
        compiler_params=_cparams(("parallel", "parallel")),
        name="mixout",
    )(x, o_f, o_b, z, ya, mod3, gng, wout16, n2g, wr_hi, wr_lo, br)


def _route_body(afft_ref, slott_ref, off_ref, *, t, cap):
    ne = N_EXPERTS
    npieces = t // LANES
    rows = npieces * ne

    def count(thr_col, strict):
        acc = jnp.zeros((ne, LANES), I32)
        for p in range(npieces):
            piece = afft_ref[0, p * ne:(p + 1) * ne, :]
            acc = acc + (piece > thr_col if strict else piece >= thr_col).astype(I32)
        return jnp.sum(acc, axis=1, keepdims=True)

    def search(i, thr):
        cand = thr | jnp.left_shift(jnp.int32(1), 30 - i)
        return jnp.where(count(pltpu.bitcast(cand, F32), False) >= cap, cand, thr)

    thr_bits = lax.fori_loop(0, 31, search, jnp.zeros((ne, 1), I32))
    thr = pltpu.bitcast(thr_bits, F32)
    need = (cap - count(thr, True)).astype(F32)

    x = afft_ref[0]
    thr_rows = jnp.concatenate([thr] * npieces, axis=0)
    need_rows = jnp.concatenate([need] * npieces, axis=0)
    gt = x > thr_rows
    eq = x == thr_rows
    ti = lax.broadcasted_iota(I32, (LANES, LANES), 0)
    tj = lax.broadcasted_iota(I32, (LANES, LANES), 1)
    triu = (ti <= tj).astype(BF16)
    ri = lax.broadcasted_iota(I32, (rows, rows), 0)
    rj = lax.broadcasted_iota(I32, (rows, rows), 1)
    earlier = (((ri & (ne - 1)) == (rj & (ne - 1))) & (rj < ri)).astype(BF16)

    def prefix(mask):
        inpiece = _dot(mask.astype(BF16), triu)
        total = jnp.broadcast_to(inpiece[:, LANES - 1:LANES], (rows, LANES)).astype(BF16)
        offset = _dot(earlier, total)
        return inpiece + offset, offset

    eq_rank, _ = prefix(eq)
    sel = gt | (eq & (eq_rank <= need_rows))
    sel_rank, sel_off = prefix(sel)
    slott_ref[0] = jnp.where(sel, sel_rank - 1.0, -1.0).astype(I32)
    off_ref[0] = sel_off.astype(I32)


def _route_call(afft, cap):
    bsz, rows, _ = afft.shape
    t = rows // N_EXPERTS * LANES
    spec = lambda r: pl.BlockSpec((1, r, LANES), lambda b: (b, 0, 0))
    return pl.pallas_call(
        functools.partial(_route_body, t=t, cap=cap),
        out_shape=(jax.ShapeDtypeStruct((bsz, rows, LANES), I32),
                   jax.ShapeDtypeStruct((bsz, rows, LANES), I32)),
        grid=(bsz,),
        in_specs=[spec(rows)],
        out_specs=(spec(rows), spec(rows)),
        compiler_params=_cparams(("parallel",)),
        name="route",
    )(afft)


def _window_plan(base_ref, flat0, experts):
    starts, rounds = [], jnp.int32(0)
    for e in experts:
        lo = base_ref[flat0 + e]
        hi = base_ref[flat0 + N_EXPERTS + e]
        lo_al = (lo >> 4) << 4
        starts.append(lo_al)
        rounds = jnp.maximum(rounds, (hi - lo_al + SLOT_WIN - 1) // SLOT_WIN)
    return starts, rounds


def _window_start(start, r, cap):
    return pl.multiple_of(jnp.minimum(start + r * SLOT_WIN, cap), SLOT_ALIGN)


def _dispatch_body(base_ref, slott_ref, fin_ref, xe_ref, *, nchunk, sub, eh_n, cap):
    b, eh, ci = pl.program_id(0), pl.program_id(1), pl.program_id(2)
    rc = ROUTE_CHUNK

    @pl.when(ci == 0)
    def _():
        xe_ref[...] = jnp.zeros_like(xe_ref)

    srow = lax.broadcasted_iota(I32, (SLOT_WIN, rc), 0)
    for sc in range(sub):
        cc = ci * sub + sc
        flat0 = (b * (nchunk + 1) + cc) * N_EXPERTS + eh * eh_n
        f = fin_ref[0, sc * rc:(sc + 1) * rc, :]
        experts = list(range(eh_n))
        starts, rounds = _window_plan(base_ref, flat0, experts)

        def one_round(r, carry, starts=starts, f=f, sc=sc):
            rows = []
            wstart = [_window_start(starts[e], r, cap) for e in experts]
            for e in experts:
                tok_slot = jnp.concatenate(
                    [slott_ref[0, sc * (rc // LANES) + j, e:e + 1, :] for j in range(rc // LANES)], axis=1)
                rows.append((tok_slot == srow + wstart[e]).astype(BF16))
            prod = _dot(jnp.concatenate(rows, axis=0), f)
            for e in experts:
                win = pl.ds(wstart[e], SLOT_WIN)
                xe_ref[0, e, win, :] = xe_ref[0, e, win, :] + prod[e * SLOT_WIN:(e + 1) * SLOT_WIN].astype(BF16)
            return carry

        one_round(jnp.int32(0), 0)
        lax.fori_loop(1, rounds, one_round, 0)


def _dispatch_call(base_flat, slott, fin, cap):
    bsz, t, _ = fin.shape
    nchunk = t // ROUTE_CHUNK
    sub = 8
    eh_n = N_EXPERTS // 2
    sp = cap + SLOT_WIN
    grid_spec = pltpu.PrefetchScalarGridSpec(
        num_scalar_prefetch=1,
        grid=(bsz, N_EXPERTS // eh_n, nchunk // sub),
        in_specs=[pl.BlockSpec((1, sub * ROUTE_CHUNK // LANES, eh_n, LANES), lambda b, eh, ci, base: (b, ci, eh, 0)),
                  pl.BlockSpec((1, sub * ROUTE_CHUNK, D_MODEL), lambda b, eh, ci, base: (b, ci, 0))],
        out_specs=pl.BlockSpec((1, eh_n, sp, D_MODEL), lambda b, eh, ci, base: (b, eh, 0, 0)))
    return pl.pallas_call(
        functools.partial(_dispatch_body, nchunk=nchunk, sub=sub, eh_n=eh_n, cap=cap),
        out_shape=jax.ShapeDtypeStruct((bsz, N_EXPERTS, sp, D_MODEL), BF16),
        grid_spec=grid_spec,
        compiler_params=_cparams(("parallel", "parallel", "arbitrary")),
        name="dispatch",
    )(base_flat, slott, fin)


def _experts_body(xe_ref, wg_ref, wu_ref, wd_ref, y_ref, *, cap, bsz):
    ft = 256
    nf = EXPERT_FF // ft
    wg16 = [wg_ref[0, :, f * ft:(f + 1) * ft].astype(BF16) for f in range(nf)]
    wu16 = [wu_ref[0, :, f * ft:(f + 1) * ft].astype(BF16) for f in range(nf)]
    wd16 = [wd_ref[0, f * ft:(f + 1) * ft, :].astype(BF16) for f in range(nf)]
    for b in range(bsz):
        x = xe_ref[b, 0, 0:cap, :]
        acc = None
        for f in range(nf):
            hid = (_silu(_dot(x, wg16[f])) * _dot(x, wu16[f])).astype(BF16)
            part = _dot(hid, wd16[f])
            acc = part if acc is None else acc + part
        y_ref[b, 0, 0:cap, :] = acc.astype(BF16)
        y_ref[b, 0, cap:, :] = jnp.zeros((y_ref.shape[2] - cap, D_MODEL), BF16)


def _experts_call(xe, w_gate, w_up, w_down, cap):
    bsz, _, sp, _ = xe.shape
    wspec = lambda shape: pl.BlockSpec((1,) + shape, lambda e: (e, 0, 0))
    slots = pl.BlockSpec((bsz, 1, sp, D_MODEL), lambda e: (0, e, 0, 0))
    return pl.pallas_call(
        functools.partial(_experts_body, cap=cap, bsz=bsz),
        out_shape=jax.ShapeDtypeStruct(xe.shape, BF16),
        grid=(N_EXPERTS,),
        in_specs=[slots, wspec((D_MODEL, EXPERT_FF)), wspec((D_MODEL, EXPERT_FF)), wspec((EXPERT_FF, D_MODEL))],
        out_specs=slots,
        compiler_params=_cparams(("parallel",)),
        name="experts",
    )(xe, w_gate, w_up, w_down)


def _combine_body(base_ref, slott_ref, afft_ref, h_ref, y_ref, mod_ref, fng_ref, o_ref, acc_ref, *, nchunk, sub, cap):
    b, ci = pl.program_id(0), pl.program_id(1)
    rc = ROUTE_CHUNK
    pieces = rc // LANES
    srow = lax.broadcasted_iota(I32, (SLOT_WIN, rc), 0)
    experts = list(range(N_EXPERTS))
    for sc in range(sub):
        rows = slice(sc * rc, (sc + 1) * rc)
        flat0 = (b * (nchunk + 1) + ci * sub + sc) * N_EXPERTS
        starts, rounds = _window_plan(base_ref, flat0, experts)
        tok_slot = [jnp.concatenate([slott_ref[0, sc * pieces + j, e:e + 1, :] for j in range(pieces)], axis=1)
                    for e in experts]
        tok_gate = [jnp.concatenate([afft_ref[0, sc * pieces + j, e:e + 1, :] for j in range(pieces)], axis=1)
                    for e in experts]

        def contribution(r, starts=starts, tok_slot=tok_slot, tok_gate=tok_gate):
            wstart = [_window_start(starts[e], r, cap) for e in experts]
            ywin = jnp.concatenate([y_ref[0, e, pl.ds(wstart[e], SLOT_WIN), :] for e in experts], axis=0)
            st = jnp.concatenate([jnp.where(tok_slot[e] == srow + wstart[e], tok_gate[e], 0.0).astype(BF16)
                                  for e in experts], axis=0)
            return _dot_tn(st, ywin)

        def extra_round(r, carry, contribution=contribution):
            acc_ref[...] += contribution(r)
            return carry

        acc_ref[...] = contribution(jnp.int32(0))
        lax.fori_loop(1, rounds, extra_round, 0)
        hl = h_ref[0, rows, :] + mod_ref[0, 5:6, :] * acc_ref[...]
        ms = jnp.mean(hl * hl, axis=-1, keepdims=True)
        o_ref[0, rows, :] = hl * lax.rsqrt(ms + NORM_EPS) * fng_ref[...]


def _combine_call(base_flat, slott, afft, h, y, mod3, fng):
    bsz, t, _ = h.shape
    nchunk = t // ROUTE_CHUNK
    sub = 4
    rc = ROUTE_CHUNK
    tok = lambda w: pl.BlockSpec((1, sub * rc, w), lambda b, i, base: (b, i, 0))
    piece = pl.BlockSpec((1, sub * rc // LANES, N_EXPERTS, LANES), lambda b, i, base: (b, i, 0, 0))
    grid_spec = pltpu.PrefetchScalarGridSpec(
        num_scalar_prefetch=1,
        grid=(bsz, nchunk // sub),
        in_specs=[piece, piece, tok(D_MODEL),
                  pl.BlockSpec((1,) + y.shape[1:], lambda b, i, base: (b, 0, 0, 0), pipeline_mode=pl.Buffered(1)),
                  pl.BlockSpec((1, N_MOD, D_MODEL), lambda b, i, base: (b, 0, 0)),
                  pl.BlockSpec(fng.shape, lambda b, i, base: (0, 0))],
        out_specs=tok(D_MODEL),
        scratch_shapes=[pltpu.VMEM((rc, D_MODEL), F32)])
    return pl.pallas_call(
        functools.partial(_combine_body, nchunk=nchunk, sub=sub, cap=y.shape[2] - SLOT_WIN),
        out_shape=jax.ShapeDtypeStruct(h.shape, F32),
        grid_spec=grid_spec,
        compiler_params=_cparams(("parallel", "arbitrary")),
        name="combine",
    )(base_flat, slott, afft, h, y, mod3, fng)


def _pad_lanes(a):
    return jnp.pad(a, ((0, 0), (0, LANES - a.shape[1])))


def kernel(x, c, ctx, c_ctx, w_mod, b_mod, norm1_g, norm2_g, w_in, conv_w, a_log, dt_bias, gdn_norm_g, gm_norm_g,
           gm_ws, gm_bs, w_out, w_router, b_router, w_gate, w_up, w_down, final_norm_g):
    bsz, t, _ = x.shape
    ctx_len = ctx.shape[1]
    assert w_mod.shape[0] == 1, "single-layer problem"
    assert t % 2048 == 0 and ctx_len % GDN_CHUNK == 0 and bsz < 8
    cap = EC_CAPACITY * t // N_EXPERTS

    cs = jnp.zeros((8, D_MODEL), F32).at[:bsz].set(c).at[bsz].set(c_ctx)
    mod3 = _mod_call(cs, w_mod[0], b_mod[0][None, :]).reshape(8, N_MOD, D_MODEL)

    wl = w_in[0]
    n_state = QKV_W + STATE_COLS
    w_state = _pad_lanes(wl[:, QKV_W:n_state])
    w_lat = jnp.concatenate([wl[:, :QKV_W], wl[:, n_state:n_state + B_W], wl[:, n_state + B_W:], w_state],
                            axis=1).astype(BF16)
    gp = jnp.zeros((8, LANES), F32).at[0, :2 * N_HEADS].set(a_log[0].reshape(-1)).at[1, :2 * N_HEADS].set(
        dt_bias[0].reshape(-1))
    g1 = norm1_g[0][None, :]
    cw = jnp.zeros((8, QKV_W), F32).at[:conv_w.shape[1]].set(conv_w[0])

    qkv_c, gb_c = _inproj_ctx_call(ctx, mod3, bsz, g1, w_lat, gp, cw)
    zero_state = jnp.zeros((bsz, N_HEADS, HEAD_DIM, HEAD_DIM), F32)
    _, _, s_f, s_b = _gdn_call(qkv_c, gb_c, zero_state, zero_state, ctx_len)

    qkv, gb, z, ya = _inproj_lat_call(x, mod3, g1, w_lat, gp, cw, gm_norm_g[0][None, :], gm_ws[0].astype(BF16),
                                      _pad_lanes(gm_bs[0].T), 1024)
    o_f, o_b, _, _ = _gdn_call(qkv, gb, s_f, s_b, 1024)
    wr = _pad_lanes(w_router[0])
    wr_hi = wr.astype(BF16)
    wr_lo = jnp.concatenate([wr_hi, (wr - wr_hi.astype(F32)).astype(BF16)], axis=1)
    br = jnp.full((1, LANES), -1e30, F32).at[0, :N_EXPERTS].set(b_router[0])
    h, fin, afft = _mixout_call(x, o_f, o_b, z, ya, mod3, gdn_norm_g[0][None, :], w_out[0].astype(BF16),
                                     norm2_g[0][None, :], wr_hi, wr_lo, br, 1024)

    npieces = t // LANES
    slott, off = _route_call(afft.reshape(bsz, npieces * N_EXPERTS, LANES), cap)
    slott = slott.reshape(bsz, npieces, N_EXPERTS, LANES)
    base = off[:, :, 0].reshape(bsz, npieces, N_EXPERTS)[:, ::ROUTE_CHUNK // LANES, :]
    base_flat = jnp.concatenate([base, jnp.full((bsz, 1, N_EXPERTS), cap, I32)], axis=1).reshape(-1)
    xe = _dispatch_call(base_flat, slott, fin, cap)
    y = _experts_call(xe, w_gate[0], w_up[0], w_down[0], cap)
    return _combine_call(base_flat, slott, afft, h, y, mod3, final_norm_g[None, :])
```

```python
import functools

import jax
import jax.numpy as jnp
from jax import lax
from jax.experimental import pallas as pl
from jax.experimental.pallas import tpu as pltpu

F32 = jnp.float32
BF16 = jnp.bfloat16
I32 = jnp.int32

D_MODEL = 1024
N_MOD = 6
N_HEADS = 4
HEAD_DIM = 128
B_W = N_HEADS * HEAD_DIM
QKV_W = 3 * B_W
A_W = 512
A_GROUPS = 4
A_CHUNK = 128
GDN_CHUNK = 64
N_EXPERTS = 16
EC_CAPACITY = 2
EXPERT_FF = 1024
NORM_EPS = 1e-6
LANES = 128
STATE_COLS = 4 * N_HEADS

ROUTE_CHUNK = 256
SLOT_WIN = 64
SLOT_ALIGN = 16
VMEM_LIMIT = 60 * 1024 * 1024


def _cparams(sem):
    return pltpu.CompilerParams(dimension_semantics=sem, vmem_limit_bytes=VMEM_LIMIT)


def _dot(a, b):
    return jnp.dot(a, b, preferred_element_type=F32)


def _dot_nt(a, b):
    return lax.dot_general(a, b, (((1,), (1,)), ((), ())), preferred_element_type=F32)


def _dot_tn(a, b):
    return lax.dot_general(a, b, (((0,), (0,)), ((), ())), preferred_element_type=F32)


def _silu(x):
    return x * jax.nn.sigmoid(x)


def _mod_body(c_ref, w_ref, b_ref, o_ref):
    s = _silu(c_ref[...])
    o_ref[...] = _dot(s.astype(BF16), w_ref[...].astype(BF16)) + b_ref[...]


def _mod_call(cs, w_mod, b_mod):
    n = w_mod.shape[1] // D_MODEL
    return pl.pallas_call(
        _mod_body,
        out_shape=jax.ShapeDtypeStruct((8, w_mod.shape[1]), F32),
        grid=(n,),
        in_specs=[pl.BlockSpec((8, D_MODEL), lambda j: (0, 0)),
                  pl.BlockSpec((D_MODEL, D_MODEL), lambda j: (0, j)),
                  pl.BlockSpec((1, D_MODEL), lambda j: (0, j))],
        out_specs=pl.BlockSpec((8, D_MODEL), lambda j: (0, j)),
        compiler_params=_cparams(("arbitrary",)),
        name="mod",
    )(cs, w_mod, b_mod)


def _norm_mod(x, g, shift, scale):
    ms = jnp.mean(x * x, axis=-1, keepdims=True)
    return (x * lax.rsqrt(ms + NORM_EPS) * g) * (1.0 + scale) + shift


def _gate_streams(st, gp_ref):
    lane = lax.broadcasted_iota(I32, st.shape, 1)
    g = -jnp.exp(gp_ref[0:1, :]) * jax.nn.softplus(st + gp_ref[1:2, :])
    beta = jax.nn.sigmoid(st)
    return jnp.where(lane < 2 * N_HEADS, g, jnp.where(lane < STATE_COLS, beta, 0.0))


def _conv_qkv(qkv, prev_row, next_row, cw_ref, out_ref, tm, row0=0):
    cs = GDN_CHUNK
    nsub = tm // cs
    w0, w1, w2 = cw_ref[0:1, :], cw_ref[1:2, :], cw_ref[2:3, :]
    row = lax.broadcasted_iota(I32, (cs, 1), 0)
    for c in range(nsub):
        rows = slice(c * cs, (c + 1) * cs)
        orows = slice(row0 + c * cs, row0 + (c + 1) * cs)
        x = qkv[rows]
        prow = prev_row if c == 0 else qkv[c * cs - 1:c * cs]
        nrow = next_row if c == nsub - 1 else qkv[(c + 1) * cs:(c + 1) * cs + 1]
        xp = jnp.where(row == 0, prow, pltpu.roll(x, 1, 0))
        xn = jnp.where(row == cs - 1, nrow, pltpu.roll(x, cs - 1, 0))
        y = _silu(xp * w0 + x * w1 + xn * w2)
        for h in range(N_HEADS):
            cq = slice(h * HEAD_DIM, (h + 1) * HEAD_DIM)
            ck = slice(B_W + h * HEAD_DIM, B_W + (h + 1) * HEAD_DIM)
            q = y[:, cq]
            k = y[:, ck]
            out_ref[0, orows, cq] = (q * (lax.rsqrt(jnp.sum(q * q, axis=-1, keepdims=True) + NORM_EPS)
                                         * (HEAD_DIM ** -0.5))).astype(BF16)
            out_ref[0, orows, ck] = (k * lax.rsqrt(jnp.sum(k * k, axis=-1, keepdims=True) + NORM_EPS)).astype(BF16)
        out_ref[0, orows, 2 * B_W:3 * B_W] = y[:, 2 * B_W:3 * B_W].astype(BF16)


def _inproj_lat_body(x_ref, xp_ref, xn_ref, mod_ref, g1_ref, w_ref, gp_ref, cw_ref, gmg_ref, ws_ref, bst_ref,
                     qkv_ref, gb_ref, z_ref, ya_ref, *, tm):
    i = pl.program_id(1)
    shift, scale = mod_ref[0, 0:1, :], mod_ref[0, 1:2, :]
    xh = jnp.concatenate([xp_ref[0], xn_ref[0]], axis=0)
    halo = _dot(_norm_mod(xh, g1_ref[...], shift, scale).astype(BF16), w_ref[:, 0:QKV_W])
    prev_row = jnp.where(i == 0, 0.0, halo[7:8, :])
    next_row = jnp.where(i == pl.num_programs(1) - 1, 0.0, halo[8:9, :])
    c_uv = QKV_W + B_W
    gd = A_W // A_GROUPS
    nsub = 2
    ts = tm // nsub
    raw = []
    for sidx in range(nsub):
        a = _norm_mod(x_ref[0, sidx * ts:(sidx + 1) * ts, :], g1_ref[...], shift, scale).astype(BF16)
        raw.append(dict(qkv=_dot(a, w_ref[:, 0:QKV_W]), z=_dot(a, w_ref[:, QKV_W:QKV_W + B_W]),
                        st=_dot(a, w_ref[:, c_uv + 2 * A_W:c_uv + 2 * A_W + LANES]),
                        uv=_dot(a, w_ref[:, c_uv:c_uv + 2 * A_W])))
    for sidx, r in enumerate(raw):
        r0 = sidx * ts
        before = prev_row if sidx == 0 else raw[sidx - 1]["qkv"][ts - 1:ts]
        after = next_row if sidx == nsub - 1 else raw[sidx + 1]["qkv"][0:1]
        _conv_qkv(r["qkv"], before, after, cw_ref, qkv_ref, ts, row0=r0)
        z_ref[0, r0:r0 + ts, :] = r["z"].astype(BF16)
        gb_ref[0, r0:r0 + ts, :] = _gate_streams(r["st"], gp_ref)
        uv = r["uv"]
        uv = 0.5 * uv * (1.0 + lax.erf(uv * 0.7071067811865476))
        for grp in range(A_GROUPS):
            v = uv[:, A_W + grp * gd:A_W + (grp + 1) * gd]
            vn = v * lax.rsqrt(jnp.mean(v * v, axis=-1, keepdims=True) + NORM_EPS) * gmg_ref[:, grp * gd:(grp + 1) * gd]
            vn = vn.astype(BF16)
            bias = bst_ref[:, grp:grp + 1]
            for c in range(ts // A_CHUNK):
                rows = slice(c * A_CHUNK, (c + 1) * A_CHUNK)
                orows = slice(r0 + c * A_CHUNK, r0 + (c + 1) * A_CHUNK)
                sg = _dot(ws_ref[grp], vn[rows]) + bias
                ya_ref[0, orows, grp * gd:(grp + 1) * gd] = (uv[rows, grp * gd:(grp + 1) * gd] * sg).astype(BF16)


def _inproj_ctx_body(x_ref, mod_ref, g1_ref, w_ref, gp_ref, cw_ref, qkv_ref, gb_ref, *, tm):
    a = _norm_mod(x_ref[0], g1_ref[...], mod_ref[0, 0:1, :], mod_ref[0, 1:2, :]).astype(BF16)
    edge = jnp.zeros((1, QKV_W), F32)
    _conv_qkv(_dot(a, w_ref[:, 0:QKV_W]), edge, edge, cw_ref, qkv_ref, tm)
    c_state = QKV_W + B_W + 2 * A_W
    gb_ref[0] = _gate_streams(_dot(a, w_ref[:, c_state:c_state + LANES]), gp_ref)


def _inproj_lat_call(x, mod3, g1, w_lat, gp, cw, gmg, ws16, bst, tm):
    bsz, t, _ = x.shape
    hb = tm // 8
    last8 = t // 8 - 1
    full = lambda a: pl.BlockSpec(a.shape, lambda b, i: (0,) * a.ndim)
    tok = lambda w: pl.BlockSpec((1, tm, w), lambda b, i: (b, i, 0))
    return pl.pallas_call(
        functools.partial(_inproj_lat_body, tm=tm),
        out_shape=(jax.ShapeDtypeStruct((bsz, t, QKV_W), BF16),
                   jax.ShapeDtypeStruct((bsz, t, LANES), F32),
                   jax.ShapeDtypeStruct((bsz, t, B_W), BF16),
                   jax.ShapeDtypeStruct((bsz, t, A_W), BF16)),
        grid=(bsz, t // tm),
        in_specs=[tok(D_MODEL),
                  pl.BlockSpec((1, 8, D_MODEL), lambda b, i: (b, jnp.maximum(i * hb - 1, 0), 0)),
                  pl.BlockSpec((1, 8, D_MODEL), lambda b, i: (b, jnp.minimum((i + 1) * hb, last8), 0)),
                  pl.BlockSpec((1, N_MOD, D_MODEL), lambda b, i: (b, 0, 0)),
                  full(g1), full(w_lat), full(gp), full(cw), full(gmg), full(ws16), full(bst)],
        out_specs=(tok(QKV_W), tok(LANES), tok(B_W), tok(A_W)),
        compiler_params=_cparams(("parallel", "arbitrary")),
        name="inproj_lat",
    )(x, x, x, mod3, g1, w_lat, gp, cw, gmg, ws16, bst)


def _inproj_ctx_call(ctx, mod3, ctx_row, g1, w_ctx, gp, cw):
    bsz, t, _ = ctx.shape
    full = lambda a: pl.BlockSpec(a.shape, lambda b: (0,) * a.ndim)
    tok = lambda w: pl.BlockSpec((1, t, w), lambda b: (b, 0, 0))
    return pl.pallas_call(
        functools.partial(_inproj_ctx_body, tm=t),
        out_shape=(jax.ShapeDtypeStruct((bsz, t, QKV_W), BF16),
                   jax.ShapeDtypeStruct((bsz, t, LANES), F32)),
        grid=(bsz,),
        in_specs=[tok(D_MODEL),
                  pl.BlockSpec((1, N_MOD, D_MODEL), lambda b: (ctx_row, 0, 0)),
                  full(g1), full(w_ctx), full(gp), full(cw)],
        out_specs=(tok(QKV_W), tok(LANES)),
        compiler_params=_cparams(("parallel",)),
        name="inproj_ctx",
    )(ctx, mod3, g1, w_ctx, gp, cw)


GDN_CHUNKS_PER_STEP = 8
GDN_STAGE_SKEW = 2
PACK_W = N_HEADS * GDN_CHUNK


def _per_head(tile, lanes, width):
    rows = tile.shape[0]
    if width == HEAD_DIM:
        return jnp.concatenate([jnp.broadcast_to(tile[:, l:l + 1], (rows, width)) for l in lanes], axis=1)
    head = lax.broadcasted_iota(I32, (rows, N_HEADS * width), 1) // width
    out = jnp.broadcast_to(tile[:, lanes[0]:lanes[0] + 1], (rows, N_HEADS * width))
    for h in range(1, N_HEADS):
        out = jnp.where(head == h, jnp.broadcast_to(tile[:, lanes[h]:lanes[h] + 1], (rows, N_HEADS * width)), out)
    return out


def _block_rows(x16, half_masks=None):
    rows, width = x16.shape
    per_head = width // N_HEADS
    zero = jnp.zeros((rows, LANES), x16.dtype)
    blocks = []
    for h in range(N_HEADS):
        tile = h * per_head // LANES
        kept = x16[:, tile * LANES:(tile + 1) * LANES]
        if per_head < LANES:
            kept = kept * half_masks[h * per_head % LANES // per_head]
        blocks.append(jnp.concatenate([kept if t == tile else zero for t in range(width // LANES)], axis=1))
    return jnp.concatenate(blocks, axis=0)


def _gdn_group_program(g, half_masks, turn):
    cs = GDN_CHUNK
    hd = HEAD_DIM
    q16, k16, v16, beta16, egc16 = g["q"], g["k"], g["v"], g["beta"], g["egc"]
    kb16 = k16 * beta16
    decay = jnp.where(g["incl"], jnp.exp(g["gc_col"] - g["gc_row"]), 0.0)
    kk = _dot_nt(jnp.concatenate([kb16, q16], axis=0), _block_rows(k16))
    yield
    a = jnp.where(g["strict"], kk[:cs] * decay, 0.0)
    attn = (kk[cs:] * decay).astype(BF16)
    m = -jnp.where(g["levels"][0], a, 0.0)
    for lm in g["levels"][1:]:
        m16 = m.astype(BF16)
        cm = jnp.where(lm, a, 0.0)
        x = cm + _dot(m16, _block_rows(cm.astype(BF16), half_masks))
        yield
        y = x + _dot(x.astype(BF16), _block_rows(m16, half_masks))
        yield
        m = m - y
    t16 = (m + g["eye"]).astype(BF16)
    u = _dot(t16, _block_rows(v16 * beta16))
    w = _dot(t16, _block_rows(kb16 * egc16))
    qg16 = q16 * egc16
    kg = k16 * g["kdec"]
    yield
    while turn[g["backward"]] != g["order"]:
        yield
    s_ref = g["s_ref"]
    s = [s_ref[h] for h in range(N_HEADS)]
    wq = [_dot(jnp.concatenate([w[:, h * hd:(h + 1) * hd].astype(BF16), qg16[:, h * hd:(h + 1) * hd]], axis=0),
               s[h].astype(BF16)) for h in range(N_HEADS)]
    yield
    v_new = (u - jnp.concatenate([r[:cs] for r in wq], axis=1)).astype(BF16)
    o = jnp.concatenate([r[cs:] for r in wq], axis=1) + _dot(attn, _block_rows(v_new))
    for h in range(N_HEADS):
        cols = slice(h * hd, (h + 1) * hd)
        s_ref[h] = s[h] * g["eg"][h] + _dot_tn(kg[:, cols], v_new[:, cols])
    g["o_ref"][0, pl.ds(g["r0"], cs), :] = o.astype(BF16)
    turn[g["backward"]] += 1


def _run_interleaved(programs, skew):
    live = {}
    tick = 0
    while live or tick <= skew * (len(programs) - 1):
        if tick % skew == 0 and tick // skew < len(programs):
            live[tick // skew] = programs[tick // skew]
        for key in sorted(live):
            for prog in live[key]:
                if next(prog, "done") == "done":
                    live[key] = [p for p in live[key] if p is not prog]
            if not live[key]:
                del live[key]
        tick += 1


def _gdn_body(qf_ref, qb_ref, gbf_ref, gbb_ref, s0f_ref, s0b_ref,
              of_ref, ob_ref, sff_ref, sfb_ref, sf_scr, sb_scr, *, tb, nt):
    t = pl.program_id(1)
    cs = GDN_CHUNK
    nch = tb // cs
    cps = min(GDN_CHUNKS_PER_STEP, nch)
    nh = N_HEADS

    @pl.when(t == 0)
    def _():
        sf_scr[...] = s0f_ref[0]
        sb_scr[...] = s0b_ref[0]

    ii = lax.broadcasted_iota(I32, (cs, PACK_W), 0)
    jj = lax.broadcasted_iota(I32, (cs, PACK_W), 1) & (cs - 1)
    incl_f, strict_f = jj <= ii, jj < ii
    incl_b, strict_b = jj >= ii, jj > ii
    levels = []
    sh = 0
    while (1 << sh) < cs:
        levels.append(((ii >> (sh + 1)) == (jj >> (sh + 1))) & ((ii >> sh) != (jj >> sh)))
        sh += 1
    lv_f = [lm & strict_f for lm in levels]
    lv_b = [lm & strict_b for lm in levels]
    eye = (ii == jj).astype(F32)
    row = lax.broadcasted_iota(I32, (cs, LANES), 0)
    lane = lax.broadcasted_iota(I32, (1, LANES), 1)
    half_of_tile = lax.broadcasted_iota(I32, (cs, LANES), 1) // cs
    half_masks = [(half_of_tile == i).astype(BF16) for i in range(LANES // cs)]

    def group(q_ref, r0, gate, csum, gt, backward, s_ref, o_ref, order):
        l0 = nh if backward else 0
        lanes = [l0 + h for h in range(nh)]
        last = 0 if backward else cs - 1
        ld = lambda c0: q_ref[0, pl.ds(r0, cs), c0:c0 + B_W]
        halves = [gt[l:l + 1, :] if (h % 2 == 0) != backward else pltpu.roll(gt[l:l + 1, :], cs, 1)
                  for h, l in enumerate(lanes)]
        gc_row = jnp.concatenate([jnp.where(lane < cs, halves[0], halves[1]),
                                  jnp.where(lane < cs, halves[2], halves[3])], axis=1)
        glast = csum[last:last + 1, :]
        return dict(
            q=ld(0), k=ld(B_W), v=ld(2 * B_W),
            beta=_per_head(gate, [2 * nh + l for l in lanes], HEAD_DIM).astype(BF16),
            egc=_per_head(jnp.exp(csum), lanes, HEAD_DIM).astype(BF16),
            kdec=_per_head(jnp.exp(glast - csum), lanes, HEAD_DIM).astype(BF16), eye=eye,
            eg=[jnp.exp(csum[last:last + 1, l:l + 1]) for l in lanes],
            gc_col=_per_head(csum, lanes, cs), gc_row=gc_row,
            incl=incl_b if backward else incl_f, strict=strict_b if backward else strict_f,
            levels=lv_b if backward else lv_f, s_ref=s_ref, o_ref=o_ref, r0=r0, backward=backward, order=order)

    def step(n, carry):
        per_chunk = []
        for j in range(cps):
            rf = pl.multiple_of((n * cps + j) * cs, cs)
            rb = pl.multiple_of((nch - 1 - n * cps - j) * cs, cs)
            gf = gbf_ref[0, pl.ds(rf, cs), :]
            gb = gbb_ref[0, pl.ds(rb, cs), :]
            cf, cb = gf, gb
            s = 1
            while s < cs:
                cf = cf + jnp.where(row >= s, pltpu.roll(cf, s, 0), 0.0)
                cb = cb + jnp.where(row < cs - s, pltpu.roll(cb, cs - s, 0), 0.0)
                s *= 2
            gt = jnp.concatenate([cf, cb], axis=0).T
            per_chunk.append([group(qf_ref, rf, gf, cf, gt, False, sf_scr, of_ref, j),
                              group(qb_ref, rb, gb, cb, gt, True, sb_scr, ob_ref, j)])
        turn = {False: 0, True: 0}
        _run_interleaved([[_gdn_group_program(g, half_masks, turn) for g in pair] for pair in per_chunk],
                         GDN_STAGE_SKEW)
        return carry

    lax.fori_loop(0, nch // cps, step, 0)

    @pl.when(t == nt - 1)
    def _():
        sff_ref[0] = sf_scr[...]
        sfb_ref[0] = sb_scr[...]


def _gdn_call(qkv, gb, s0f, s0b, tb):
    bsz, t, _ = qkv.shape
    nt = t // tb
    assert (tb // GDN_CHUNK) % min(GDN_CHUNKS_PER_STEP, tb // GDN_CHUNK) == 0
    st = pl.BlockSpec((1, N_HEADS, HEAD_DIM, HEAD_DIM), lambda b, i: (b, 0, 0, 0))
    fwd = lambda w: pl.BlockSpec((1, tb, w), lambda b, i: (b, i, 0))
    bwd = lambda w: pl.BlockSpec((1, tb, w), lambda b, i: (b, nt - 1 - i, 0))
    return pl.pallas_call(
        functools.partial(_gdn_body, tb=tb, nt=nt),
        out_shape=(jax.ShapeDtypeStruct((bsz, t, B_W), BF16), jax.ShapeDtypeStruct((bsz, t, B_W), BF16),
                   jax.ShapeDtypeStruct((bsz, N_HEADS, HEAD_DIM, HEAD_DIM), F32),
                   jax.ShapeDtypeStruct((bsz, N_HEADS, HEAD_DIM, HEAD_DIM), F32)),
        grid=(bsz, nt),
        in_specs=[fwd(QKV_W), bwd(QKV_W), fwd(LANES), bwd(LANES), st, st],
        out_specs=(fwd(B_W), bwd(B_W), st, st),
        scratch_shapes=[pltpu.VMEM((N_HEADS, HEAD_DIM, HEAD_DIM), F32), pltpu.VMEM((N_HEADS, HEAD_DIM, HEAD_DIM), F32)],
        compiler_params=_cparams(("parallel", "arbitrary")),
        name="gdn",
    )(qkv, qkv, gb, gb, s0f, s0b)


def _mixout_body(x_ref, of_ref, ob_ref, z_ref, ya_ref, mod_ref, gng_ref, wout_ref, n2g_ref, wrh_ref, wrl_ref, br_ref,
                 h_ref, fin_ref, afft_ref, *, tm):
    o = of_ref[0].astype(F32) + ob_ref[0].astype(F32)
    z = z_ref[0].astype(F32)
    parts = [ya_ref[0]]
    for h in range(N_HEADS):
        c = slice(h * HEAD_DIM, (h + 1) * HEAD_DIM)
        oh = o[:, c]
        y = oh * lax.rsqrt(jnp.mean(oh * oh, axis=-1, keepdims=True) + NORM_EPS)
        parts.append((y * gng_ref[...] * _silu(z[:, c])).astype(BF16))
    mix = _dot(jnp.concatenate(parts, axis=1), wout_ref[...])
    hl = x_ref[0] + mod_ref[0, 2:3, :] * mix
    h_ref[0] = hl
    fin = _norm_mod(hl, n2g_ref[...], mod_ref[0, 3:4, :], mod_ref[0, 4:5, :])
    f_hi = fin.astype(BF16)
    fin_ref[0] = f_hi
    f_lo = (fin - f_hi.astype(F32)).astype(BF16)
    both = _dot(f_hi, wrl_ref[...])
    logits = both[:, :LANES] + _dot(f_lo, wrh_ref[...]) + both[:, LANES:] + br_ref[...]
    e = jnp.exp(logits - jnp.max(logits, axis=-1, keepdims=True))
    aff = e / jnp.sum(e, axis=-1, keepdims=True)
    for j in range(tm // LANES):
        afft_ref[0, j] = aff[j * LANES:(j + 1) * LANES, :].T[0:N_EXPERTS, :]


def _mixout_call(x, o_f, o_b, z, ya, mod3, gng, wout16, n2g, wr_hi, wr_lo, br, tm):
    bsz, t, _ = x.shape
    full = lambda a: pl.BlockSpec(a.shape, lambda b, i: (0,) * a.ndim)
    tok = lambda w: pl.BlockSpec((1, tm, w), lambda b, i: (b, i, 0))
    return pl.pallas_call(
        functools.partial(_mixout_body, tm=tm),
        out_shape=(jax.ShapeDtypeStruct((bsz, t, D_MODEL), F32), jax.ShapeDtypeStruct((bsz, t, D_MODEL), BF16),
                   jax.ShapeDtypeStruct((bsz, t // LANES, N_EXPERTS, LANES), F32)),
        grid=(bsz, t // tm),
        in_specs=[tok(D_MODEL), tok(B_W), tok(B_W), tok(B_W), tok(A_W),
                  pl.BlockSpec((1, N_MOD, D_MODEL), lambda b, i: (b, 0, 0)),
                  full(gng), full(wout16), full(n2g), full(wr_hi), full(wr_lo), full(br)],
        out_specs=(tok(D_MODEL), tok(D_MODEL),
                   pl.BlockSpec((1, tm // LANES, N_EXPERTS, LANES), lambda b, i: (b, i, 0, 0))),
        compiler_params=_cparams(("parallel", "parallel")),
        name="mixout",
    )(x, o_f, o_b, z, ya, mod3, gng, wout16, n2g, wr_hi, wr_lo, br)


def _route_body(afft_ref, slott_ref, off_ref, *, t, cap):
    ne = N_EXPERTS
    npieces = t // LANES
    rows = npieces * ne

    def count(thr_col, strict):
        acc = jnp.zeros((ne, LANES), I32)
        for p in range(npieces):
            piece = afft_ref[0, p * ne:(p + 1) * ne, :]
            acc = acc + (piece > thr_col if strict else piece >= thr_col).astype(I32)
        return jnp.sum(acc, axis=1, keepdims=True)

    def search(i, thr):
        cand = thr | jnp.left_shift(jnp.int32(1), 30 - i)
        return jnp.where(count(pltpu.bitcast(cand, F32), False) >= cap, cand, thr)

    thr_bits = lax.fori_loop(0, 31, search, jnp.zeros((ne, 1), I32))
    thr = pltpu.bitcast(thr_bits, F32)
    need = (cap - count(thr, True)).astype(F32)

    x = afft_ref[0]
    thr_rows = jnp.concatenate([thr] * npieces, axis=0)
    need_rows = jnp.concatenate([need] * npieces, axis=0)
    gt = x > thr_rows
    eq = x == thr_rows
    ti = lax.broadcasted_iota(I32, (LANES, LANES), 0)
    tj = lax.broadcasted_iota(I32, (LANES, LANES), 1)
    triu = (ti <= tj).astype(BF16)
    ri = lax.broadcasted_iota(I32, (rows, rows), 0)
    rj = lax.broadcasted_iota(I32, (rows, rows), 1)
    earlier = (((ri & (ne - 1)) == (rj & (ne - 1))) & (rj < ri)).astype(BF16)

    def prefix(mask):
        inpiece = _dot(mask.astype(BF16), triu)
        total = jnp.broadcast_to(inpiece[:, LANES - 1:LANES], (rows, LANES)).astype(BF16)
        offset = _dot(earlier, total)
        return inpiece + offset, offset

    eq_rank, _ = prefix(eq)
    sel = gt | (eq & (eq_rank <= need_rows))
    sel_rank, sel_off = prefix(sel)
    slott_ref[0] = jnp.where(sel, sel_rank - 1.0, -1.0).astype(I32)
    off_ref[0] = sel_off.astype(I32)


def _route_call(afft, cap):
    bsz, rows, _ = afft.shape
    t = rows // N_EXPERTS * LANES
    spec = lambda r: pl.BlockSpec((1, r, LANES), lambda b: (b, 0, 0))
    return pl.pallas_call(
        functools.partial(_route_body, t=t, cap=cap),
        out_shape=(jax.ShapeDtypeStruct((bsz, rows, LANES), I32),
                   jax.ShapeDtypeStruct((bsz, rows, LANES), I32)),
        grid=(bsz,),
        in_specs=[spec(rows)],
        out_specs=(spec(rows), spec(rows)),
        compiler_params=_cparams(("parallel",)),
        name="route",
    )(afft)


def _window_plan(base_ref, flat0, experts):
    starts, rounds = [], jnp.int32(0)
    for e in experts:
        lo = base_ref[flat0 + e]
        hi = base_ref[flat0 + N_EXPERTS + e]
        lo_al = lo & -SLOT_ALIGN
        starts.append(lo_al)
        rounds = jnp.maximum(rounds, (hi - lo_al + SLOT_WIN - 1) // SLOT_WIN)
    return starts, rounds


def _window_start(start, r, cap):
    return pl.multiple_of(jnp.minimum(start + r * SLOT_WIN, cap), SLOT_ALIGN)


def _dispatch_body(base_ref, slott_ref, fin_ref, xe_ref, *, nchunk, sub, eh_n, cap):
    b, eh, ci = pl.program_id(0), pl.program_id(1), pl.program_id(2)
    rc = ROUTE_CHUNK

    @pl.when(ci == 0)
    def _():
        xe_ref[...] = jnp.zeros_like(xe_ref)

    srow = lax.broadcasted_iota(I32, (SLOT_WIN, rc), 0)
    for sc in range(sub):
        cc = ci * sub + sc
        flat0 = (b * (nchunk + 1) + cc) * N_EXPERTS + eh * eh_n
        f = fin_ref[0, sc * rc:(sc + 1) * rc, :]
        experts = list(range(eh_n))
        starts, rounds = _window_plan(base_ref, flat0, experts)

        def one_round(r, carry, starts=starts, f=f, sc=sc):
            rows = []
            wstart = [_window_start(starts[e], r, cap) for e in experts]
            for e in experts:
                tok_slot = jnp.concatenate(
                    [slott_ref[0, sc * (rc // LANES) + j, e:e + 1, :] for j in range(rc // LANES)], axis=1)
                rows.append((tok_slot == srow + wstart[e]).astype(BF16))
            prod = _dot(jnp.concatenate(rows, axis=0), f)
            for e in experts:
                win = pl.ds(wstart[e], SLOT_WIN)
                xe_ref[0, e, win, :] = xe_ref[0, e, win, :] + prod[e * SLOT_WIN:(e + 1) * SLOT_WIN].astype(BF16)
            return carry

        one_round(jnp.int32(0), 0)
        lax.fori_loop(1, rounds, one_round, 0)


def _dispatch_call(base_flat, slott, fin, cap):
    bsz, t, _ = fin.shape
    nchunk = t // ROUTE_CHUNK
    sub = 8
    eh_n = N_EXPERTS // 2
    sp = cap + SLOT_WIN
    grid_spec = pltpu.PrefetchScalarGridSpec(
        num_scalar_prefetch=1,
        grid=(bsz, N_EXPERTS // eh_n, nchunk // sub),
        in_specs=[pl.BlockSpec((1, sub * ROUTE_CHUNK // LANES, eh_n, LANES), lambda b, eh, ci, base: (b, ci, eh, 0)),
                  pl.BlockSpec((1, sub * ROUTE_CHUNK, D_MODEL), lambda b, eh, ci, base: (b, ci, 0))],
        out_specs=pl.BlockSpec((1, eh_n, sp, D_MODEL), lambda b, eh, ci, base: (b, eh, 0, 0)))
    return pl.pallas_call(
        functools.partial(_dispatch_body, nchunk=nchunk, sub=sub, eh_n=eh_n, cap=cap),
        out_shape=jax.ShapeDtypeStruct((bsz, N_EXPERTS, sp, D_MODEL), BF16),
        grid_spec=grid_spec,
        compiler_params=_cparams(("parallel", "parallel", "arbitrary")),
        name="dispatch",
    )(base_flat, slott, fin)


def _experts_body(xe_ref, wg_ref, wu_ref, wd_ref, y_ref, *, cap, bsz):
    ft = 256
    nf = EXPERT_FF // ft
    wg16 = [wg_ref[0, :, f * ft:(f + 1) * ft].astype(BF16) for f in range(nf)]
    wu16 = [wu_ref[0, :, f * ft:(f + 1) * ft].astype(BF16) for f in range(nf)]
    wd16 = jnp.concatenate([wd_ref[0, f * ft:(f + 1) * ft, :].astype(BF16) for f in range(nf)], axis=0)
    for b in range(bsz):
        x = xe_ref[b, 0, 0:cap, :]
        hid = jnp.concatenate([(_silu(_dot(x, wg16[f])) * _dot(x, wu16[f])).astype(BF16) for f in range(nf)], axis=1)
        y_ref[b, 0, 0:cap, :] = _dot(hid, wd16).astype(BF16)
        y_ref[b, 0, cap:, :] = jnp.zeros((y_ref.shape[2] - cap, D_MODEL), BF16)


def _experts_call(xe, w_gate, w_up, w_down, cap):
    bsz, _, sp, _ = xe.shape
    wspec = lambda shape: pl.BlockSpec((1,) + shape, lambda e: (e, 0, 0))
    slots = pl.BlockSpec((bsz, 1, sp, D_MODEL), lambda e: (0, e, 0, 0))
    return pl.pallas_call(
        functools.partial(_experts_body, cap=cap, bsz=bsz),
        out_shape=jax.ShapeDtypeStruct(xe.shape, BF16),
        grid=(N_EXPERTS,),
        in_specs=[slots, wspec((D_MODEL, EXPERT_FF)), wspec((D_MODEL, EXPERT_FF)), wspec((EXPERT_FF, D_MODEL))],
        out_specs=slots,
        compiler_params=_cparams(("parallel",)),
        name="experts",
    )(xe, w_gate, w_up, w_down)


def _combine_body(base_ref, slott_ref, afft_ref, h_ref, y_ref, mod_ref, fng_ref, o_ref, acc_ref, *, nchunk, sub, cap):
    b, ci = pl.program_id(0), pl.program_id(1)
    rc = ROUTE_CHUNK
    pieces = rc // LANES
    srow = lax.broadcasted_iota(I32, (SLOT_WIN, rc), 0)
    experts = list(range(N_EXPERTS))
    for sc in range(sub):
        rows = slice(sc * rc, (sc + 1) * rc)
        flat0 = (b * (nchunk + 1) + ci * sub + sc) * N_EXPERTS
        starts, rounds = _window_plan(base_ref, flat0, experts)
        tok_slot = [jnp.concatenate([slott_ref[0, sc * pieces + j, e:e + 1, :] for j in range(pieces)], axis=1)
                    for e in experts]
        tok_gate = [jnp.concatenate([afft_ref[0, sc * pieces + j, e:e + 1, :] for j in range(pieces)], axis=1)
                    for e in experts]

        def contribution(r, starts=starts, tok_slot=tok_slot, tok_gate=tok_gate):
            wstart = [_window_start(starts[e], r, cap) for e in experts]
            ywin = jnp.concatenate([y_ref[0, e, pl.ds(wstart[e], SLOT_WIN), :] for e in experts], axis=0)
            st = jnp.concatenate([jnp.where(tok_slot[e] == srow + wstart[e], tok_gate[e], 0.0).astype(BF16)
                                  for e in experts], axis=0)
            return _dot_tn(st, ywin)

        def extra_round(r, carry, contribution=contribution):
            acc_ref[...] += contribution(r)
            return carry

        acc_ref[...] = contribution(jnp.int32(0))
        lax.fori_loop(1, rounds, extra_round, 0)
        hl = h_ref[0, rows, :] + mod_ref[0, 5:6, :] * acc_ref[...]
        ms = jnp.mean(hl * hl, axis=-1, keepdims=True)
        o_ref[0, rows, :] = hl * lax.rsqrt(ms + NORM_EPS) * fng_ref[...]


def _combine_call(base_flat, slott, afft, h, y, mod3, fng):
    bsz, t, _ = h.shape
    nchunk = t // ROUTE_CHUNK
    sub = 4
    rc = ROUTE_CHUNK
    tok = lambda w: pl.BlockSpec((1, sub * rc, w), lambda b, i, base: (b, i, 0))
    piece = pl.BlockSpec((1, sub * rc // LANES, N_EXPERTS, LANES), lambda b, i, base: (b, i, 0, 0))
    grid_spec = pltpu.PrefetchScalarGridSpec(
        num_scalar_prefetch=1,
        grid=(bsz, nchunk // sub),
        in_specs=[piece, piece, tok(D_MODEL),
                  pl.BlockSpec((1,) + y.shape[1:], lambda b, i, base: (b, 0, 0, 0), pipeline_mode=pl.Buffered(1)),
                  pl.BlockSpec((1, N_MOD, D_MODEL), lambda b, i, base: (b, 0, 0)),
                  pl.BlockSpec(fng.shape, lambda b, i, base: (0, 0))],
        out_specs=tok(D_MODEL),
        scratch_shapes=[pltpu.VMEM((rc, D_MODEL), F32)])
    return pl.pallas_call(
        functools.partial(_combine_body, nchunk=nchunk, sub=sub, cap=y.shape[2] - SLOT_WIN),
        out_shape=jax.ShapeDtypeStruct(h.shape, F32),
        grid_spec=grid_spec,
        compiler_params=_cparams(("parallel", "arbitrary")),
        name="combine",
    )(base_flat, slott, afft, h, y, mod3, fng)


def _pad_lanes(a):
    return jnp.pad(a, ((0, 0), (0, LANES - a.shape[1])))


def kernel(x, c, ctx, c_ctx, w_mod, b_mod, norm1_g, norm2_g, w_in, conv_w, a_log, dt_bias, gdn_norm_g, gm_norm_g,
           gm_ws, gm_bs, w_out, w_router, b_router, w_gate, w_up, w_down, final_norm_g):
    bsz, t, _ = x.shape
    ctx_len = ctx.shape[1]
    assert w_mod.shape[0] == 1, "single-layer problem"
    assert t % 2048 == 0 and ctx_len % GDN_CHUNK == 0 and bsz < 8
    cap = EC_CAPACITY * t // N_EXPERTS

    cs = jnp.zeros((8, D_MODEL), F32).at[:bsz].set(c).at[bsz].set(c_ctx)
    mod3 = _mod_call(cs, w_mod[0], b_mod[0][None, :]).reshape(8, N_MOD, D_MODEL)

    wl = w_in[0]
    n_state = QKV_W + STATE_COLS
    w_state = _pad_lanes(wl[:, QKV_W:n_state])
    w_lat = jnp.concatenate([wl[:, :QKV_W], wl[:, n_state:n_state + B_W], wl[:, n_state + B_W:], w_state],
                            axis=1).astype(BF16)
    gp = jnp.zeros((8, LANES), F32).at[0, :2 * N_HEADS].set(a_log[0].reshape(-1)).at[1, :2 * N_HEADS].set(
        dt_bias[0].reshape(-1))
    g1 = norm1_g[0][None, :]
    cw = jnp.zeros((8, QKV_W), F32).at[:conv_w.shape[1]].set(conv_w[0])

    qkv_c, gb_c = _inproj_ctx_call(ctx, mod3, bsz, g1, w_lat, gp, cw)
    zero_state = jnp.zeros((bsz, N_HEADS, HEAD_DIM, HEAD_DIM), F32)
    _, _, s_f, s_b = _gdn_call(qkv_c, gb_c, zero_state, zero_state, ctx_len)

    qkv, gb, z, ya = _inproj_lat_call(x, mod3, g1, w_lat, gp, cw, gm_norm_g[0][None, :], gm_ws[0].astype(BF16),
                                      _pad_lanes(gm_bs[0].T), 1024)
    o_f, o_b, _, _ = _gdn_call(qkv, gb, s_f, s_b, 1024)
    wr = _pad_lanes(w_router[0])
    wr_hi = wr.astype(BF16)
    wr_lo = jnp.concatenate([wr_hi, (wr - wr_hi.astype(F32)).astype(BF16)], axis=1)
    br = jnp.full((1, LANES), -1e30, F32).at[0, :N_EXPERTS].set(b_router[0])
    h, fin, afft = _mixout_call(x, o_f, o_b, z, ya, mod3, gdn_norm_g[0][None, :], w_out[0].astype(BF16),
                                     norm2_g[0][None, :], wr_hi, wr_lo, br, 1024)

    npieces = t // LANES
    slott, off = _route_call(afft.reshape(bsz, npieces * N_EXPERTS, LANES), cap)
    slott = slott.reshape(bsz, npieces, N_EXPERTS, LANES)
    base = off[:, :, 0].reshape(bsz, npieces, N_EXPERTS)[:, ::ROUTE_CHUNK // LANES, :]
    base_flat = jnp.concatenate([base, jnp.full((bsz, 1, N_EXPERTS), cap, I32)], axis=1).reshape(-1)
    xe = _dispatch_call(base_flat, slott, fin, cap)
    y = _experts_call(xe, w_gate[0], w_up[0], w_down[0], cap)
    return _combine_call(base_flat, slott, afft, h, y, mod3, final_norm_g[None, :])
```
